```python
import math
import jax, jax.numpy as jnp
from jax import lax
import numpy as np

D_MODEL = 1024
BATCH = 8
SEQ = 2048
DEPTH = 2
DEC_BATCH = 128
DEC_SEQ = 4
PAST_LEN = 2048
PAGE_SIZE = 128

HD = 64
A_HEADS = 4
A_KV = 2
B_HEADS = 8
B_KV = 2
CMP_BLOCK = 32
CMP_STRIDE = 16
CMP_HIDDEN = 128
SEL_BLOCK = 64
SEL_TOPK = 8
WINDOW = 512
C_HEADS = 8
C_KV = 4
MOBA_BLOCK = 256
MOBA_TOPK = 3
BR_WIDTH = 512
PEER_HEADS = 8
PEER_NKEYS = 128
PEER_DKEY = 256
PEER_TOPK = 16
N_EXPERTS = PEER_NKEYS * PEER_NKEYS
ROPE_THETA = 10000.0
EPS = 1e-6
Q_BLOCK = 128
MOBA_QBLOCK = 16
PEER_CHUNK = 256
IN_SIZES = (A_HEADS * 2 * HD, A_KV * 2 * HD, A_KV * 2 * HD,
            B_HEADS * HD, B_KV * HD, B_KV * HD, B_KV * HD, B_KV * HD, B_KV * HD, B_KV * HD, 3 * B_HEADS,
            C_HEADS * HD, C_KV * HD, C_KV * HD, 3 * D_MODEL)
IN_WIDTH = sum(IN_SIZES)

kernel_name = 'hybrid_diff_nsa_moba_peer_step'


def rms_norm(x, w):
    xf = x.astype(jnp.float32)
    y = xf * lax.rsqrt(jnp.mean(xf * xf, axis=-1, keepdims=True) + EPS)
    return (y * w.astype(jnp.float32)).astype(x.dtype)


def rope(x, pos):
    half = x.shape[-1] // 2
    freqs = jnp.power(ROPE_THETA, -jnp.arange(half, dtype=jnp.float32) / half)
    ang = pos.astype(jnp.float32)[:, None] * freqs[None, :]
    cos = jnp.cos(ang)[:, None, :]
    sin = jnp.sin(ang)[:, None, :]
    xf = x.astype(jnp.float32)
    x1, x2 = xf[..., :half], xf[..., half:]
    return jnp.concatenate([x1 * cos - x2 * sin, x1 * sin + x2 * cos], axis=-1).astype(x.dtype)


def masked_softmax(s, mask):
    s = jnp.where(mask, s.astype(jnp.float32), -jnp.inf)
    m = jnp.max(s, axis=-1, keepdims=True)
    m = jnp.where(jnp.isfinite(m), m, 0.0)
    p = jnp.exp(s - m)
    return p / jnp.maximum(jnp.sum(p, axis=-1, keepdims=True), 1e-30)


def map_query_blocks(fn, qb, q_pos, *qs):
    T = q_pos.shape[0]
    nb = T // qb
    blk = lambda a: jnp.moveaxis(a.reshape((a.shape[0], nb, qb) + a.shape[2:]), 1, 0)
    out = lax.map(lambda args: fn(args[0], *args[1:]), (q_pos.reshape(nb, qb),) + tuple(blk(a) for a in qs))
    out = jnp.moveaxis(out, 0, 1)
    return out.reshape((out.shape[0], T) + out.shape[3:])


def to_blocks(k, bs):
    Bn, L, G, D = k.shape
    nb = -(-L // bs)
    k = jnp.pad(k, ((0, 0), (0, nb * bs - L), (0, 0), (0, 0)))
    return jnp.transpose(k.reshape(Bn, nb, bs, G, D), (0, 3, 1, 2, 4))


def gather_pages(cache_l, page_table):
    pages = cache_l[page_table]
    return pages.reshape((page_table.shape[0], -1) + cache_l.shape[2:])


def project(h, w_in_l, pos):
    Bn, T, _ = h.shape
    z = jnp.einsum('btd,dn->btn', h, w_in_l)
    offs = np.cumsum((0,) + IN_SIZES)
    (aq, ak, av, bq, bkc, bvc, bks, bvs, bkw, bvw, bg, cq, ck, cv, gate) = [
        z[..., int(offs[i]):int(offs[i + 1])] for i in range(len(IN_SIZES))]
    heads = lambda a, n: a.reshape(Bn, T, n, HD)
    aq = rope(heads(aq, 2 * A_HEADS), pos).reshape(Bn, T, A_KV, A_HEADS // A_KV, 2, HD)
    ak = rope(heads(ak, 2 * A_KV), pos).reshape(Bn, T, A_KV, 2 * HD)
    av = av.reshape(Bn, T, A_KV, 2 * HD)
    bq = heads(bq, B_HEADS)
    bqc = bq.reshape(Bn, T, B_KV, B_HEADS // B_KV, HD)
    bqr = rope(bq, pos).reshape(Bn, T, B_KV, B_HEADS // B_KV, HD)
    bkc = heads(bkc, B_KV)
    bvc = heads(bvc, B_KV)
    bks = rope(heads(bks, B_KV), pos)
    bvs = heads(bvs, B_KV)
    bkw = rope(heads(bkw, B_KV), pos)
    bvw = heads(bvw, B_KV)
    bg = bg.reshape(Bn, T, B_HEADS, 3)
    cq = rope(heads(cq, C_HEADS), pos).reshape(Bn, T, C_KV, C_HEADS // C_KV, HD)
    ck = rope(heads(ck, C_KV), pos)
    cv = heads(cv, C_KV)
    gate = gate.reshape(Bn, T, 3, D_MODEL)
    return (aq, ak, av, bqc, bqr, bkc, bvc, bks, bvs, bkw, bvw, bg, cq, ck, cv, gate)


def diff_lambda(lam_p, lam_init):
    lp = lam_p.astype(jnp.float32)
    return jnp.exp(jnp.sum(lp[0] * lp[1])) - jnp.exp(jnp.sum(lp[2] * lp[3])) + lam_init


def diff_attend(q, q_pos, k, v, k_pos, lam, subln_w, lam_init):
    Bn, Tq = q.shape[:2]
    Tk = k.shape[1]
    k = k.reshape(Bn, Tk, A_KV, 2, HD)
    s = jnp.einsum('bqgrcd,bkgcd->bgrcqk', q, k).astype(jnp.float32) * (HD ** -0.5)
    p = masked_softmax(s, k_pos[None, :] <= q_pos[:, None])
    w = p[:, :, :, 0] - lam * p[:, :, :, 1]
    o = jnp.einsum('bgrqk,bkge->bqgre', w, v.astype(jnp.float32))
    o = rms_norm(o, subln_w) * (1.0 - lam_init)
    return o.reshape(Bn, Tq, A_HEADS * 2 * HD).astype(q.dtype)


def nsa_compress(k, pe, w1, w2):
    Bn, L, G, _ = k.shape
    nc = (L - CMP_BLOCK) // CMP_STRIDE + 1
    idx = np.arange(nc)[:, None] * CMP_STRIDE + np.arange(CMP_BLOCK)[None, :]
    kb = k[:, idx] + pe[:, None, :]
    kb = jnp.swapaxes(kb, 2, 3).reshape(Bn, nc, G, CMP_BLOCK * HD)
    hmid = jax.nn.gelu(jnp.einsum('bngf,fe->bnge', kb, w1))
    return jnp.einsum('bnge,ed->bngd', hmid, w2)


def nsa_attend(qc, qr, gates, q_pos, kcmp, vcmp, ks_blk, vs_blk, kw, vw, kw_pos):
    Bn, Tq, G, R, _ = qc.shape
    scale = HD ** -0.5
    nc = kcmp.shape[1]
    ns = ks_blk.shape[2]
    s = jnp.einsum('bqgrd,bngd->bqgrn', qc, kcmp).astype(jnp.float32) * scale
    cmp_end = jnp.arange(nc, dtype=jnp.int32) * CMP_STRIDE + CMP_BLOCK - 1
    p_cmp = masked_softmax(s, (cmp_end[None, :] <= q_pos[:, None])[None, :, None, None, :])
    o_cmp = jnp.einsum('bqgrn,bngd->bqgrd', p_cmp, vcmp.astype(jnp.float32))
    sel_of_cmp = (jnp.arange(nc, dtype=jnp.int32) * CMP_STRIDE) // SEL_BLOCK
    onehot = (sel_of_cmp[:, None] == jnp.arange(ns, dtype=jnp.int32)[None, :]).astype(jnp.float32)
    imp = jnp.einsum('bqgrn,nj->bqgj', p_cmp, onehot)
    cur = q_pos // SEL_BLOCK
    j = jnp.arange(ns, dtype=jnp.int32)
    past_m = (j[None, :] < cur[:, None])[None, :, None, :]
    own_m = (j[None, :] == cur[:, None])[None, :, None, :]
    imp = jnp.where(own_m, jnp.inf, jnp.where(past_m, imp, -jnp.inf))
    kk = min(SEL_TOPK, ns)
    _, idx = lax.top_k(imp, kk)
    valid = idx <= cur[None, :, None, None]
    bi = jnp.arange(Bn)[:, None, None, None]
    gi = jnp.arange(G)[None, None, :, None]
    ksel = ks_blk[bi, gi, idx]
    vsel = vs_blk[bi, gi, idx]
    s = jnp.einsum('bqgrd,bqgksd->bqgrks', qr, ksel).astype(jnp.float32) * scale
    kpos = idx[..., None] * SEL_BLOCK + jnp.arange(SEL_BLOCK, dtype=jnp.int32)
    m = valid[..., None] & (kpos <= q_pos[None, :, None, None, None])
    p = masked_softmax(s.reshape(Bn, Tq, G, R, kk * SEL_BLOCK), m.reshape(Bn, Tq, G, 1, kk * SEL_BLOCK))
    o_slc = jnp.einsum('bqgrn,bqgnd->bqgrd', p, vsel.reshape(Bn, Tq, G, kk * SEL_BLOCK, HD).astype(jnp.float32))
    s = jnp.einsum('bqgrd,bkgd->bqgrk', qr, kw).astype(jnp.float32) * scale
    dist = q_pos[:, None] - kw_pos[None, :]
    m = (dist >= 0) & (dist <= WINDOW) & (kw_pos[None, :] >= 0)
    p = masked_softmax(s, m[None, :, None, None, :])
    o_win = jnp.einsum('bqgrk,bkgd->bqgrd', p, vw.astype(jnp.float32))
    g = jax.nn.sigmoid(gates.astype(jnp.float32)).reshape(Bn, Tq, G, R, 3)
    o = g[..., 0:1] * o_cmp + g[..., 1:2] * o_slc + g[..., 2:3] * o_win
    return o.reshape(Bn, Tq, B_HEADS * HD).astype(qc.dtype)


def moba_attend(q, q_pos, k_blk, v_blk, k_mean):
    Bn, Tq, G, R, _ = q.shape
    nb = k_blk.shape[2]
    s_blk = jnp.einsum('bqgrd,bgnd->bqgrn', q, k_mean).astype(jnp.float32)
    cur = q_pos // MOBA_BLOCK
    past_m = (jnp.arange(nb, dtype=jnp.int32)[None, :] < cur[:, None])[None, :, None, None, :]
    s_blk = jnp.where(past_m, s_blk, -jnp.inf)
    kk = min(MOBA_TOPK, nb)
    _, idx = lax.top_k(s_blk, kk)
    valid = idx < cur[None, :, None, None, None]
    own = jnp.broadcast_to(cur[None, :, None, None, None], idx.shape[:-1] + (1,)).astype(idx.dtype)
    idx = jnp.concatenate([idx, own], axis=-1)
    valid = jnp.concatenate([valid, jnp.ones(own.shape, dtype=bool)], axis=-1)
    bi = jnp.arange(Bn)[:, None, None, None, None]
    gi = jnp.arange(G)[None, None, :, None, None]
    ksel = k_blk[bi, gi, idx]
    vsel = v_blk[bi, gi, idx]
    s = jnp.einsum('bqgrd,bqgrksd->bqgrks', q, ksel).astype(jnp.float32) * (HD ** -0.5)
    kpos = idx[..., None] * MOBA_BLOCK + jnp.arange(MOBA_BLOCK, dtype=jnp.int32)
    m = valid[..., None] & (kpos <= q_pos[None, :, None, None, None, None])
    n = (kk + 1) * MOBA_BLOCK
    p = masked_softmax(s.reshape(Bn, Tq, G, R, n), m.reshape(Bn, Tq, G, R, n))
    o = jnp.einsum('bqgrn,bqgrnd->bqgrd', p, vsel.reshape(Bn, Tq, G, R, n, HD).astype(jnp.float32))
    return o.reshape(Bn, Tq, C_HEADS * HD).astype(q.dtype)


def merge(gate, o_a, o_b, o_c, w_branch_l, w_out_l):
    o = jnp.stack([o_a, o_b, o_c], axis=2)
    br = jnp.einsum('btcw,cwd->btcd', o, w_branch_l)
    y = jnp.sum(jax.nn.sigmoid(gate) * br, axis=2)
    return jnp.einsum('btd,de->bte', y, w_out_l)


def peer(x, wq, subkeys, u, v):
    Bn, T, D = x.shape
    n = Bn * T
    npad = -(-n // PEER_CHUNK) * PEER_CHUNK
    xt = jnp.pad(x.reshape(n, D), ((0, npad - n), (0, 0)))

    def chunk(xc):
        q = (xc @ wq).reshape(-1, PEER_HEADS, 2, PEER_DKEY // 2)
        s = jnp.einsum('nhcd,hckd->nhck', q, subkeys).astype(jnp.float32)
        s1, i1 = lax.top_k(s[:, :, 0], PEER_TOPK)
        s2, i2 = lax.top_k(s[:, :, 1], PEER_TOPK)
        cs = (s1[..., :, None] + s2[..., None, :]).reshape(-1, PEER_HEADS, PEER_TOPK * PEER_TOPK)
        ci = (i1[..., :, None] * PEER_NKEYS + i2[..., None, :]).reshape(-1, PEER_HEADS, PEER_TOPK * PEER_TOPK)
        top, sel = lax.top_k(cs, PEER_TOPK)
        eidx = jnp.take_along_axis(ci, sel, axis=-1)
        g = jax.nn.softmax(top, axis=-1)
        a = jax.nn.gelu(jnp.einsum('nd,nhkd->nhk', xc, u[eidx]).astype(jnp.float32))
        return jnp.einsum('nhk,nhkd->nd', (g * a).astype(xc.dtype), v[eidx])

    out = lax.map(chunk, xt.reshape(-1, PEER_CHUNK, D)).reshape(npad, D)[:n]
    return out.reshape(Bn, T, D).astype(x.dtype)


def mix_prompt(h, l, w_in_l, a_lambda_l, a_subln_l, pe_l, w1_l, w2_l):
    Bn, T, _ = h.shape
    pos = jnp.arange(T, dtype=jnp.int32)
    (aq, ak, av, bqc, bqr, bkc, bvc, bks, bvs, bkw, bvw, bg, cq, ck, cv, gate) = project(h, w_in_l, pos)
    lam_init = 0.8 - 0.6 * math.exp(-0.3 * l)
    lam = diff_lambda(a_lambda_l, lam_init)
    o_a = map_query_blocks(lambda qp, q: diff_attend(q, qp, ak, av, pos, lam, a_subln_l, lam_init), Q_BLOCK, pos, aq)
    kcmp = nsa_compress(bkc, pe_l[0], w1_l[0], w2_l[0])
    vcmp = nsa_compress(bvc, pe_l[1], w1_l[1], w2_l[1])
    ks_blk = to_blocks(bks, SEL_BLOCK)
    vs_blk = to_blocks(bvs, SEL_BLOCK)
    pad = ((0, 0), (WINDOW, 0), (0, 0), (0, 0))
    kw_pad = jnp.pad(bkw, pad)
    vw_pad = jnp.pad(bvw, pad)

    def nsa_block(qp, qc, qr, g):
        start = qp[0]
        kw = lax.dynamic_slice_in_dim(kw_pad, start, WINDOW + Q_BLOCK, axis=1)
        vw = lax.dynamic_slice_in_dim(vw_pad, start, WINDOW + Q_BLOCK, axis=1)
        kw_pos = start - WINDOW + jnp.arange(WINDOW + Q_BLOCK, dtype=jnp.int32)
        return nsa_attend(qc, qr, g, qp, kcmp, vcmp, ks_blk, vs_blk, kw, vw, kw_pos)

    o_b = map_query_blocks(nsa_block, Q_BLOCK, pos, bqc, bqr, bg)
    kc_blk = to_blocks(ck, MOBA_BLOCK)
    vc_blk = to_blocks(cv, MOBA_BLOCK)
    kc_mean = jnp.mean(kc_blk.astype(jnp.float32), axis=3)
    o_c = map_query_blocks(lambda qp, q: moba_attend(q, qp, kc_blk, vc_blk, kc_mean), MOBA_QBLOCK, pos, cq)
    wp = min(WINDOW, T)
    rows = (ak, av, bkc, bvc, bks, bvs, ck, cv, bkw[:, T - wp:], bvw[:, T - wp:])
    return o_a, o_b, o_c, gate, rows


def mix_sample(h, l, w_in_l, a_lambda_l, a_subln_l, pe_l, w1_l, w2_l,
               c_ak, c_av, c_bkc, c_bvc, c_bks, c_bvs, c_ck, c_cv, s_wk, s_wv, page_table):
    Bn, T, _ = h.shape
    pos = PAST_LEN + jnp.arange(T, dtype=jnp.int32)
    (aq, ak, av, bqc, bqr, bkc, bvc, bks, bvs, bkw, bvw, bg, cq, ck, cv, gate) = project(h, w_in_l, pos)
    with_past = lambda c, new: jnp.concatenate([gather_pages(c, page_table), new], axis=1)
    kpos = jnp.arange(PAST_LEN + T, dtype=jnp.int32)
    lam_init = 0.8 - 0.6 * math.exp(-0.3 * l)
    lam = diff_lambda(a_lambda_l, lam_init)
    o_a = diff_attend(aq, pos, with_past(c_ak, ak), with_past(c_av, av), kpos, lam, a_subln_l, lam_init)
    kcmp = nsa_compress(with_past(c_bkc, bkc), pe_l[0], w1_l[0], w2_l[0])
    vcmp = nsa_compress(with_past(c_bvc, bvc), pe_l[1], w1_l[1], w2_l[1])
    ks_blk = to_blocks(with_past(c_bks, bks), SEL_BLOCK)
    vs_blk = to_blocks(with_past(c_bvs, bvs), SEL_BLOCK)
    kw = jnp.concatenate([s_wk, bkw], axis=1)
    vw = jnp.concatenate([s_wv, bvw], axis=1)
    wb = s_wk.shape[1]
    kw_pos = jnp.arange(PAST_LEN - wb, PAST_LEN + T, dtype=jnp.int32)
    o_b = nsa_attend(bqc, bqr, bg, pos, kcmp, vcmp, ks_blk, vs_blk, kw, vw, kw_pos)
    kc_blk = to_blocks(with_past(c_ck, ck), MOBA_BLOCK)
    vc_blk = to_blocks(with_past(c_cv, cv), MOBA_BLOCK)
    kc_mean = jnp.mean(kc_blk.astype(jnp.float32), axis=3)
    o_c = map_query_blocks(lambda qp, q: moba_attend(q, qp, kc_blk, vc_blk, kc_mean), 1, pos, cq)
    wn = min(WINDOW, PAST_LEN + T)
    lw = kw.shape[1]
    rows = (ak, av, bkc, bvc, bks, bvs, ck, cv, kw[:, lw - wn:], vw[:, lw - wn:])
    return o_a, o_b, o_c, gate, rows


def setup_inputs(seed: int = 0) -> dict:
    key = jax.random.key(seed)
    ks = jax.random.split(key, 32)
    n_pages = PAST_LEN // PAGE_SIZE
    used = DEC_BATCH * n_pages
    pool = used + max(1, used // 4)
    wb = min(WINDOW, PAST_LEN)
    f32 = jnp.float32
    nrm = lambda k, shape, s: jax.random.normal(k, shape, f32) * s
    page_table = jax.random.permutation(ks[0], pool)[:used].reshape(DEC_BATCH, n_pages).astype(jnp.int32)
    return {
        'x_prompt': nrm(ks[1], (BATCH, SEQ, D_MODEL), 1.0),
        'x_sample': nrm(ks[2], (DEC_BATCH, DEC_SEQ, D_MODEL), 1.0),
        'cache_a_k': nrm(ks[3], (DEPTH, pool, PAGE_SIZE, A_KV, 2 * HD), 1.0),
        'cache_a_v': nrm(ks[4], (DEPTH, pool, PAGE_SIZE, A_KV, 2 * HD), 1.0),
        'cache_b_kc': nrm(ks[5], (DEPTH, pool, PAGE_SIZE, B_KV, HD), 1.0),
        'cache_b_vc': nrm(ks[6], (DEPTH, pool, PAGE_SIZE, B_KV, HD), 1.0),
        'cache_b_ks': nrm(ks[7], (DEPTH, pool, PAGE_SIZE, B_KV, HD), 1.0),
        'cache_b_vs': nrm(ks[8], (DEPTH, pool, PAGE_SIZE, B_KV, HD), 1.0),
        'cache_c_k': nrm(ks[9], (DEPTH, pool, PAGE_SIZE, C_KV, HD), 1.0),
        'cache_c_v': nrm(ks[10], (DEPTH, pool, PAGE_SIZE, C_KV, HD), 1.0),
        'state_b_wk': nrm(ks[11], (DEPTH, DEC_BATCH, wb, B_KV, HD), 1.0),
        'state_b_wv': nrm(ks[12], (DEPTH, DEC_BATCH, wb, B_KV, HD), 1.0),
        'page_table': page_table,
        'norm_mix': 1.0 + nrm(ks[13], (DEPTH, D_MODEL), 0.02),
        'norm_ffn': 1.0 + nrm(ks[14], (DEPTH, D_MODEL), 0.02),
        'norm_final': 1.0 + nrm(ks[15], (D_MODEL,), 0.02),
        'w_in': nrm(ks[16], (DEPTH, D_MODEL, IN_WIDTH), D_MODEL ** -0.5),
        'a_lambda': nrm(ks[17], (DEPTH, 4, HD), 0.1),
        'a_subln': 1.0 + nrm(ks[18], (DEPTH, 2 * HD), 0.02),
        'b_cmp_pe': nrm(ks[19], (DEPTH, 2, CMP_BLOCK, HD), 0.1),
        'b_cmp_w1': nrm(ks[20], (DEPTH, 2, CMP_BLOCK * HD, CMP_HIDDEN), (CMP_BLOCK * HD) ** -0.5),
        'b_cmp_w2': nrm(ks[21], (DEPTH, 2, CMP_HIDDEN, HD), CMP_HIDDEN ** -0.5),
        'w_branch': nrm(ks[22], (DEPTH, 3, BR_WIDTH, D_MODEL), BR_WIDTH ** -0.5),
        'w_out': nrm(ks[23], (DEPTH, D_MODEL, D_MODEL), D_MODEL ** -0.5),
        'peer_wq': nrm(ks[24], (DEPTH, D_MODEL, PEER_HEADS * PEER_DKEY), D_MODEL ** -0.5),
        'peer_subkeys': nrm(ks[25], (DEPTH, PEER_HEADS, 2, PEER_NKEYS, PEER_DKEY // 2), (PEER_DKEY // 2) ** -0.5),
        'peer_u': nrm(ks[26], (DEPTH, N_EXPERTS, D_MODEL), D_MODEL ** -0.5),
        'peer_v': nrm(ks[27], (DEPTH, N_EXPERTS, D_MODEL), PEER_HEADS ** -0.5),
    }


def reference(x_prompt, x_sample, cache_a_k, cache_a_v, cache_b_kc, cache_b_vc, cache_b_ks, cache_b_vs,
              cache_c_k, cache_c_v, state_b_wk, state_b_wv, page_table, norm_mix, norm_ffn, norm_final,
              w_in, a_lambda, a_subln, b_cmp_pe, b_cmp_w1, b_cmp_w2, w_branch, w_out,
              peer_wq, peer_subkeys, peer_u, peer_v):
    xp = x_prompt
    xs = x_sample
    rows_p = []
    rows_s = []
    for l in range(DEPTH):
        o_a, o_b, o_c, gate, rows = mix_prompt(rms_norm(xp, norm_mix[l]), l, w_in[l], a_lambda[l], a_subln[l],
                                               b_cmp_pe[l], b_cmp_w1[l], b_cmp_w2[l])
        xp = xp + merge(gate, o_a, o_b, o_c, w_branch[l], w_out[l])
        xp = xp + peer(rms_norm(xp, norm_ffn[l]), peer_wq[l], peer_subkeys[l], peer_u[l], peer_v[l])
        rows_p.append(rows)
        o_a, o_b, o_c, gate, rows = mix_sample(rms_norm(xs, norm_mix[l]), l, w_in[l], a_lambda[l], a_subln[l],
                                               b_cmp_pe[l], b_cmp_w1[l], b_cmp_w2[l],
                                               cache_a_k[l], cache_a_v[l], cache_b_kc[l], cache_b_vc[l],
                                               cache_b_ks[l], cache_b_vs[l], cache_c_k[l], cache_c_v[l],
                                               state_b_wk[l], state_b_wv[l], page_table)
        xs = xs + merge(gate, o_a, o_b, o_c, w_branch[l], w_out[l])
        xs = xs + peer(rms_norm(xs, norm_ffn[l]), peer_wq[l], peer_subkeys[l], peer_u[l], peer_v[l])
        rows_s.append(rows)
    y_prompt = rms_norm(xp, norm_final)
    y_sample = rms_norm(xs, norm_final)
    (p_a_k, p_a_v, p_b_kc, p_b_vc, p_b_ks, p_b_vs, p_c_k, p_c_v, p_b_wk, p_b_wv) = [
        jnp.stack(r, axis=0) for r in zip(*rows_p)]
    (s_a_k, s_a_v, s_b_kc, s_b_vc, s_b_ks, s_b_vs, s_c_k, s_c_v, s_b_wk, s_b_wv) = [
        jnp.stack(r, axis=0) for r in zip(*rows_s)]
    return (y_prompt, y_sample,
            p_a_k, p_a_v, p_b_kc, p_b_vc, p_b_ks, p_b_vs, p_c_k, p_c_v, p_b_wk, p_b_wv,
            s_a_k, s_a_v, s_b_kc, s_b_vc, s_b_ks, s_b_vs, s_c_k, s_c_v, s_b_wk, s_b_wv)
```

```python
import functools
import math

import jax
import jax.numpy as jnp
import numpy as np
from jax import lax
from jax.experimental import pallas as pl
from jax.experimental.pallas import tpu as pltpu

F32 = jnp.float32
BF16 = jnp.bfloat16

D_MODEL = 1024
HD = 64
A_HEADS, A_KV = 4, 2
B_HEADS, B_KV = 8, 2
C_HEADS, C_KV = 8, 4
CMP_BLOCK, CMP_STRIDE, CMP_HIDDEN = 32, 16, 128
SEL_BLOCK, SEL_TOPK = 64, 8
CMP_PER_SEL_SHIFT = 2
WINDOW = 512
MOBA_BLOCK, MOBA_TOPK = 256, 3
PEER_HEADS, PEER_NKEYS, PEER_TOPK = 8, 128, 16
ROPE_THETA = 10000.0
EPS = 1e-6
PAGE_SIZE = 128

LANES = 128
NEG = -1e30
VMEM_LIMIT = 48 * 1024 * 1024
ROW_TILE = 256
KV_CHUNK = 256
EXPERT_TILE = 512


def _params(*sem):
    return pltpu.CompilerParams(dimension_semantics=sem, vmem_limit_bytes=VMEM_LIMIT)


def _dot(a, b):
    return jnp.dot(a, b, preferred_element_type=F32)


def _dot_nt(a, b):
    return lax.dot_general(a, b, (((1,), (1,)), ((), ())), preferred_element_type=F32)


def _split3(a):
    hi = a.astype(BF16)
    r1 = a - hi.astype(F32)
    mid = r1.astype(BF16)
    lo = (r1 - mid.astype(F32)).astype(BF16)
    return hi, mid, lo


def _rms(x, w):
    return x * lax.rsqrt(jnp.mean(x * x, axis=-1, keepdims=True) + EPS) * w


def _proj_body(pieces, chunk, x_ref, nw_ref, w_ref, cos_ref, sin_ref, *outs):
    h = _rms(x_ref[...], nw_ref[...]).astype(BF16)
    cos = cos_ref[...]
    sin = sin_ref[...]
    lane = lax.broadcasted_iota(jnp.int32, (1, LANES), 1)
    first = (lane & (HD - 1)) < (HD // 2)
    n_cols = len(pieces) * LANES
    for c0 in range(0, n_cols, chunk):
        w = min(chunk, n_cols - c0)
        z = _dot(h, w_ref[:, c0:c0 + w])
        for p in range(w // LANES):
            zp = z[:, p * LANES:(p + 1) * LANES]
            for (oi, oc, rope) in pieces[c0 // LANES + p]:
                if rope:
                    rot = jnp.where(first, pltpu.roll(zp, LANES - HD // 2, 1), pltpu.roll(zp, HD // 2, 1))
                    outs[oi][:, oc:oc + LANES] = zp * cos + rot * sin
                else:
                    outs[oi][:, oc:oc + LANES] = zp


def _project(x, nw, w, cos_t, sin_t, pieces, out_widths, name):
    n = x.shape[0]
    tm = min(ROW_TILE, n)
    ntab = cos_t.shape[0] // tm
    ncols = w.shape[1]
    body = functools.partial(_proj_body, pieces, 512)
    return pl.pallas_call(
        body,
        grid=(n // tm,),
        in_specs=[
            pl.BlockSpec((tm, D_MODEL), lambda i: (i, 0)),
            pl.BlockSpec((1, D_MODEL), lambda i: (0, 0)),
            pl.BlockSpec((D_MODEL, ncols), lambda i: (0, 0)),
            pl.BlockSpec((tm, LANES), lambda i: (i % ntab, 0)),
            pl.BlockSpec((tm, LANES), lambda i: (i % ntab, 0)),
        ],
        out_specs=[pl.BlockSpec((tm, ow), lambda i: (i, 0)) for ow in out_widths],
        out_shape=[jax.ShapeDtypeStruct((n, ow), F32) for ow in out_widths],
        compiler_params=_params("parallel"),
        name=name,
    )(x, nw, w, cos_t, sin_t)


def _plain(oi, width):
    return [[(oi, c, False)] for c in range(0, width, LANES)]


def _roped(oi, width):
    return [[(oi, c, True)] for c in range(0, width, LANES)]


PIECES_A = _roped(0, 512) + _roped(1, 256) + _plain(2, 256)
WIDTHS_A = (512, 256, 256)
PIECES_B = ([[(0, c, False), (1, c, True)] for c in range(0, 512, LANES)]
            + _plain(2, 128) + _plain(3, 128) + _roped(4, 128) + _plain(5, 128)
            + _roped(6, 128) + _plain(7, 128) + _plain(8, 128))
WIDTHS_B = (512, 512, 128, 128, 128, 128, 128, 128, 128)
PIECES_C = _roped(0, 512) + _roped(1, 256) + _plain(2, 256)
WIDTHS_C = (512, 256, 256)
PIECES_G = _plain(0, 3072)
WIDTHS_G = (3072,)


def _rope_tables(pos):
    half = HD // 2
    freqs = jnp.power(ROPE_THETA, -jnp.arange(half, dtype=F32) / half)
    ang = pos.astype(F32)[:, None] * freqs[None, :]
    cos = jnp.cos(ang)
    sin = jnp.sin(ang)
    cos_t = jnp.concatenate([cos, cos, cos, cos], axis=1)
    sin_t = jnp.concatenate([-sin, sin, -sin, sin], axis=1)
    return cos_t, sin_t


def _flash_init(m_ref, l_ref, acc_ref):
    m_ref[...] = jnp.full(m_ref.shape, NEG, F32)
    l_ref[...] = jnp.zeros(l_ref.shape, F32)
    acc_ref[...] = jnp.zeros(acc_ref.shape, F32)


def _flash_step(hh, q_emb, kc, vc, mask, m_ref, l_ref, acc_ref):
    s = jnp.where(mask, _dot_nt(q_emb, kc), NEG)
    m_old = m_ref[hh]
    m_new = jnp.maximum(m_old, jnp.max(s, axis=1, keepdims=True))
    alpha = jnp.exp(m_old - m_new)
    p = jnp.where(mask, jnp.exp(s - m_new), 0.0)
    l_ref[hh] = alpha * l_ref[hh] + jnp.sum(p, axis=1, keepdims=True)
    acc_ref[hh] = alpha * acc_ref[hh] + _dot(p.astype(BF16), vc)
    m_ref[hh] = m_new


def _flash_out(hh, l_ref, acc_ref):
    return acc_ref[hh] / jnp.maximum(l_ref[hh], 1e-30)


def _chunk(ref, j, tk):
    return ref[pl.ds(pl.multiple_of(j * tk, tk), tk), :].astype(BF16)


def _diff_body(tq, tk, q_pos0, lam_init, lam_ref, subln_ref, q_ref, k_ref, v_ref, o_ref,
               m_ref, l_ref, acc_ref):
    qi = pl.program_id(2)
    lane = lax.broadcasted_iota(jnp.int32, (1, LANES), 1)
    upper = lane >= HD
    q0 = q_pos0 + qi * tq
    qpos = q0 + lax.broadcasted_iota(jnp.int32, (tq, 1), 0)
    _flash_init(m_ref, l_ref, acc_ref)
    qs = []
    for r in range(2):
        qb = q_ref[:, r * LANES:(r + 1) * LANES] * (HD ** -0.5)
        qs.append(jnp.where(upper, 0.0, qb).astype(BF16))
        qs.append(jnp.where(upper, qb, 0.0).astype(BF16))
    n_chunks = (q0 + tq - 1) // tk + 1

    def body(j, carry):
        kc = _chunk(k_ref, j, tk)
        vc = _chunk(v_ref, j, tk)
        kpos = j * tk + lax.broadcasted_iota(jnp.int32, (1, tk), 1)
        mask = kpos <= qpos
        for hh in range(4):
            _flash_step(hh, qs[hh], kc, vc, mask, m_ref, l_ref, acc_ref)
        return carry

    lax.fori_loop(0, n_chunks, body, 0)
    lp = lam_ref[...]
    lam = (jnp.exp(jnp.sum(lp[0:1] * lp[1:2], axis=1, keepdims=True))
           - jnp.exp(jnp.sum(lp[2:3] * lp[3:4], axis=1, keepdims=True)) + lam_init)
    for r in range(2):
        o = _flash_out(2 * r, l_ref, acc_ref) - lam * _flash_out(2 * r + 1, l_ref, acc_ref)
        o_ref[:, r * LANES:(r + 1) * LANES] = _rms(o, subln_ref[...]) * (1.0 - lam_init)


def _diff_attention(aq, ak, av, lam_p, subln, lam_init, q_pos0, name):
    bn, tq_all, _ = aq.shape
    lk = ak.shape[1]
    tq = min(ROW_TILE, tq_all)
    tk = KV_CHUNK
    body = functools.partial(_diff_body, tq, tk, q_pos0, lam_init)
    return pl.pallas_call(
        body,
        grid=(bn, A_KV, tq_all // tq),
        in_specs=[
            pl.BlockSpec((4, HD), lambda b, g, i: (0, 0)),
            pl.BlockSpec((1, 2 * HD), lambda b, g, i: (0, 0)),
            pl.BlockSpec((None, tq, 256), lambda b, g, i: (b, i, g)),
            pl.BlockSpec((None, lk, LANES), lambda b, g, i: (b, 0, g)),
            pl.BlockSpec((None, lk, LANES), lambda b, g, i: (b, 0, g)),
        ],
        out_specs=pl.BlockSpec((None, tq, 256), lambda b, g, i: (b, i, g)),
        out_shape=jax.ShapeDtypeStruct((bn, tq_all, 512), F32),
        scratch_shapes=[pltpu.VMEM((4, tq, 1), F32), pltpu.VMEM((4, tq, 1), F32),
                        pltpu.VMEM((4, tq, LANES), F32)],
        compiler_params=_params("parallel", "parallel", "arbitrary"),
        name=name,
    )(lam_p, subln, aq, ak, av)


def _compress_one(x_ref, pe_ref, w1_ref, w2_ref, o_ref):
    n_half = CMP_STRIDE
    ha = jnp.zeros((LANES, 2 * CMP_HIDDEN), F32)
    hb = jnp.zeros((LANES, 2 * CMP_HIDDEN), F32)
    for p in range(n_half):
        x = x_ref[pl.ds(p, LANES, stride=CMP_STRIDE), :]
        ha = ha + _dot((x + pe_ref[p]).astype(BF16), w1_ref[p])
        hb = hb + _dot((x + pe_ref[n_half + p]).astype(BF16), w1_ref[n_half + p])
    h = ha + pltpu.roll(hb, LANES - 1, 0)
    y = _dot(jax.nn.gelu(h).astype(BF16), w2_ref[...])
    row = lax.broadcasted_iota(jnp.int32, (LANES, 1), 0)
    o_ref[...] = jnp.where(row < LANES - 1, y, 0.0)


def _compress_body(xk_ref, xv_ref, pek_ref, pev_ref, w1k_ref, w1v_ref, w2k_ref, w2v_ref, ok_ref, ov_ref):
    _compress_one(xk_ref, pek_ref, w1k_ref, w2k_ref, ok_ref)
    _compress_one(xv_ref, pev_ref, w1v_ref, w2v_ref, ov_ref)


def _compress(xk, xv, cw, name):
    bn = xk.shape[0]
    n_tok = LANES * CMP_STRIDE
    x_spec = pl.BlockSpec((None, n_tok, LANES), lambda b: (b, 0, 0))
    pe_spec = pl.BlockSpec((CMP_BLOCK, 1, LANES), lambda b: (0, 0, 0))
    w1_spec = pl.BlockSpec((CMP_BLOCK, LANES, 2 * CMP_HIDDEN), lambda b: (0, 0, 0))
    w2_spec = pl.BlockSpec((2 * CMP_HIDDEN, LANES), lambda b: (0, 0))
    o_spec = pl.BlockSpec((None, LANES, LANES), lambda b: (b, 0, 0))
    return pl.pallas_call(
        _compress_body,
        grid=(bn,),
        in_specs=[x_spec, x_spec, pe_spec, pe_spec, w1_spec, w1_spec, w2_spec, w2_spec],
        out_specs=[o_spec, o_spec],
        out_shape=[jax.ShapeDtypeStruct((bn, LANES, LANES), F32)] * 2,
        compiler_params=_params("parallel"),
        name=name,
    )(xk, xv, cw["pek"], cw["pev"], cw["w1k"], cw["w1v"], cw["w2k"], cw["w2v"])


def _compress_weights(pe, w1, w2):
    out = {}
    for idx, tag in ((0, "k"), (1, "v")):
        w1p = w1[idx].reshape(CMP_BLOCK, HD, CMP_HIDDEN)
        z = jnp.zeros_like(w1p)
        w1b = jnp.concatenate([jnp.concatenate([w1p, z], axis=2), jnp.concatenate([z, w1p], axis=2)], axis=1)
        z2 = jnp.zeros_like(w2[idx])
        w2b = jnp.concatenate([jnp.concatenate([w2[idx], z2], axis=1), jnp.concatenate([z2, w2[idx]], axis=1)], axis=0)
        out["w1" + tag] = w1b.astype(BF16)
        out["w2" + tag] = w2b.astype(BF16)
        out["pe" + tag] = jnp.concatenate([pe[idx], pe[idx]], axis=1)[:, None, :]
    return out


def _rank_desc(vals, n, lane):
    rank = jnp.zeros(vals.shape, F32)
    for i in range(n):
        col = vals[:, i:i + 1]
        tie = jnp.where(lane > i, 1.0, 0.0)
        rank = rank + jnp.where(col > vals, 1.0, 0.0) + jnp.where(col == vals, tie, 0.0)
    return rank


def _nsa_body(tq, tk, q_pos0, ns, w_pos0, w_valid, lw,
              qc_ref, qr_ref, bg_ref, kcmp_ref, vcmp_ref, ks_ref, vs_ref, kw_ref, vw_ref, o_ref,
              m_ref, l_ref, acc_ref):
    g = pl.program_id(1)
    qi = pl.program_id(2)
    lane = lax.broadcasted_iota(jnp.int32, (1, LANES), 1)
    halfi = lane >> 6
    q0 = q_pos0 + qi * tq
    qpos = q0 + lax.broadcasted_iota(jnp.int32, (tq, 1), 0)
    rep = B_HEADS // B_KV

    def embed(ref, r):
        v = ref[:, (r // 2) * LANES:(r // 2 + 1) * LANES] * (HD ** -0.5)
        own = jnp.where(halfi == (r % 2), v, 0.0)
        dup = own + pltpu.roll(own, HD, 1)
        return jnp.where(halfi == g, dup, 0.0).astype(BF16)

    def pick(o, r):
        return jnp.where(g == (r % 2), o, pltpu.roll(o, HD, 1))

    kcmp = kcmp_ref[...].astype(BF16)
    vcmp = vcmp_ref[...].astype(BF16)
    cmask = (lane * CMP_STRIDE + (CMP_BLOCK - 1)) <= qpos
    psum = jnp.zeros((tq, LANES), F32)
    o_cmp = []
    for r in range(rep):
        s = jnp.where(cmask, _dot_nt(embed(qc_ref, r), kcmp), NEG)
        m = jnp.max(s, axis=1, keepdims=True)
        p = jnp.where(cmask, jnp.exp(s - m), 0.0)
        p = p / jnp.maximum(jnp.sum(p, axis=1, keepdims=True), 1e-30)
        psum = psum + p
        o_cmp.append(_dot(p.astype(BF16), vcmp))

    ci = lax.broadcasted_iota(jnp.int32, (LANES, LANES), 0)
    cj = lax.broadcasted_iota(jnp.int32, (LANES, LANES), 1)
    onehot = jnp.where((ci >> CMP_PER_SEL_SHIFT) == cj, 1.0, 0.0).astype(BF16)
    hi, mid, lo = _split3(psum)
    imp = _dot(hi, onehot) + _dot(mid, onehot) + _dot(lo, onehot)
    cur = qpos >> 6
    impm = jnp.where(lane == cur, jnp.inf, jnp.where(lane < cur, imp, -jnp.inf))
    rank = _rank_desc(impm, ns, lane)
    sel = jnp.where(rank < float(SEL_TOPK), jnp.where(lane <= cur, 1.0, 0.0), 0.0).astype(BF16)

    qsr = [embed(qr_ref, r) for r in range(rep)]
    _flash_init(m_ref, l_ref, acc_ref)
    n_chunks = (q0 + tq - 1) // tk + 1
    bpc = tk // SEL_BLOCK

    def sel_body(j, carry):
        kc = _chunk(ks_ref, j, tk)
        vc = _chunk(vs_ref, j, tk)
        ei = lax.broadcasted_iota(jnp.int32, (LANES, tk), 0)
        el = lax.broadcasted_iota(jnp.int32, (LANES, tk), 1)
        expand = jnp.where(ei == j * bpc + (el >> 6), 1.0, 0.0).astype(BF16)
        allowed = _dot(sel, expand)
        kpos = j * tk + lax.broadcasted_iota(jnp.int32, (1, tk), 1)
        mask = jnp.where(kpos <= qpos, allowed, 0.0) > 0.5
        for r in range(rep):
            _flash_step(r, qsr[r], kc, vc, mask, m_ref, l_ref, acc_ref)
        return carry

    lax.fori_loop(0, n_chunks, sel_body, 0)
    o_slc = [_flash_out(r, l_ref, acc_ref) for r in range(rep)]

    _flash_init(m_ref, l_ref, acc_ref)
    w_lo = jnp.maximum(q0 - WINDOW - w_pos0, 0) // tk
    w_hi = jnp.minimum(q0 + tq - 1 - w_pos0, lw - 1) // tk + 1

    def win_body(j, carry):
        kc = _chunk(kw_ref, j, tk)
        vc = _chunk(vw_ref, j, tk)
        kidx = j * tk + lax.broadcasted_iota(jnp.int32, (1, tk), 1)
        dist = qpos - (kidx + w_pos0)
        inwin = jnp.where(dist >= 0, jnp.where(dist <= WINDOW, 1.0, 0.0), 0.0)
        mask = jnp.where(kidx < w_valid, inwin, 0.0) > 0.5
        for r in range(rep):
            _flash_step(r, qsr[r], kc, vc, mask, m_ref, l_ref, acc_ref)
        return carry

    lax.fori_loop(w_lo, w_hi, win_body, 0)

    sig = jax.nn.sigmoid(bg_ref[...])
    outs = []
    for r in range(rep):
        o_win = _flash_out(r, l_ref, acc_ref)
        base = (g * rep + r) * 3
        gates = [jnp.sum(jnp.where(lane == base + c, sig, 0.0), axis=1, keepdims=True) for c in range(3)]
        outs.append(pick(gates[0] * o_cmp[r] + gates[1] * o_slc[r] + gates[2] * o_win, r))
    for pr in range(rep // 2):
        o_ref[:, pr * LANES:(pr + 1) * LANES] = jnp.where(halfi == 0, outs[2 * pr], outs[2 * pr + 1])


def _nsa_attention(bqc, bqr, bg, kcmp, vcmp, ks, vs, kw, vw, q_pos0, k_valid, w_pos0, w_valid, name):
    bn, tq_all, _ = bqc.shape
    lk = ks.shape[1]
    lw = kw.shape[1]
    tq = min(ROW_TILE, tq_all)
    tk = KV_CHUNK
    ns = -(-k_valid // SEL_BLOCK)
    body = functools.partial(_nsa_body, tq, tk, q_pos0, ns, w_pos0, w_valid, lw)
    q_spec = pl.BlockSpec((None, tq, 256), lambda b, g, i: (b, i, g))
    full = lambda rows: pl.BlockSpec((None, rows, LANES), lambda b, g, i: (b, 0, 0))
    return pl.pallas_call(
        body,
        grid=(bn, B_KV, tq_all // tq),
        in_specs=[q_spec, q_spec, pl.BlockSpec((None, tq, LANES), lambda b, g, i: (b, i, 0)),
                  full(LANES), full(LANES), full(lk), full(lk), full(lw), full(lw)],
        out_specs=pl.BlockSpec((None, tq, 256), lambda b, g, i: (b, i, g)),
        out_shape=jax.ShapeDtypeStruct((bn, tq_all, 512), F32),
        scratch_shapes=[pltpu.VMEM((4, tq, 1), F32), pltpu.VMEM((4, tq, 1), F32),
                        pltpu.VMEM((4, tq, LANES), F32)],
        compiler_params=_params("parallel", "parallel", "arbitrary"),
        name=name,
    )(bqc, bqr, bg, kcmp, vcmp, ks, vs, kw, vw)


def _moba_body(tq, tk, q_pos0, nb, q_ref, k_ref, v_ref, o_ref, km_ref, m_ref, l_ref, acc_ref):
    g = pl.program_id(1)
    qi = pl.program_id(2)
    gh = g % 2
    lane = lax.broadcasted_iota(jnp.int32, (1, LANES), 1)
    halfi = lane >> 6
    q0 = q_pos0 + qi * tq
    qpos = q0 + lax.broadcasted_iota(jnp.int32, (tq, 1), 0)
    cur = qpos >> 8
    rep = C_HEADS // C_KV

    km_ref[...] = jnp.zeros(km_ref.shape, F32)
    km_ref[0:nb, :] = jnp.sum(k_ref[...].reshape(nb, MOBA_BLOCK, LANES), axis=1) * (1.0 / MOBA_BLOCK)
    km_hi, km_mid, _ = _split3(km_ref[...])

    qv = q_ref[...]
    qs = []
    sels = []
    for r in range(rep):
        own = jnp.where(halfi == r, qv, 0.0)
        dup = own + pltpu.roll(own, HD, 1)
        qe = jnp.where(halfi == gh, dup, 0.0)
        q_hi, q_mid, _ = _split3(qe)
        s_blk = _dot_nt(q_hi, km_hi) + _dot_nt(q_hi, km_mid) + _dot_nt(q_mid, km_hi)
        sm = jnp.where(lane < cur, s_blk, -jnp.inf)
        rank = _rank_desc(sm, nb, lane)
        past = jnp.where(rank < float(MOBA_TOPK), jnp.where(lane < cur, 1.0, 0.0), 0.0)
        sels.append(jnp.where(lane == cur, 1.0, past))
        qs.append((qe * (HD ** -0.5)).astype(BF16))

    _flash_init(m_ref, l_ref, acc_ref)
    n_chunks = (q0 + tq - 1) // tk + 1

    def body(j, carry):
        kc = _chunk(k_ref, j, tk)
        vc = _chunk(v_ref, j, tk)
        kpos = j * tk + lax.broadcasted_iota(jnp.int32, (1, tk), 1)
        causal = jnp.where(kpos <= qpos, 1.0, 0.0)
        for r in range(rep):
            col = jnp.sum(jnp.where(lane == j, sels[r], 0.0), axis=1, keepdims=True)
            mask = (causal * col) > 0.5
            _flash_step(r, qs[r], kc, vc, mask, m_ref, l_ref, acc_ref)
        return carry

    lax.fori_loop(0, n_chunks, body, 0)
    outs = []
    for r in range(rep):
        o = _flash_out(r, l_ref, acc_ref)
        outs.append(jnp.where(gh == r, o, pltpu.roll(o, HD, 1)))
    o_ref[...] = jnp.where(halfi == 0, outs[0], outs[1])


def _moba_attention(cq, ck, cv, q_pos0, name):
    bn, tq_all, _ = cq.shape
    lk = ck.shape[1]
    tq = min(ROW_TILE, tq_all)
    tk = MOBA_BLOCK
    nb = lk // MOBA_BLOCK
    body = functools.partial(_moba_body, tq, tk, q_pos0, nb)
    return pl.pallas_call(
        body,
        grid=(bn, C_KV, tq_all // tq),
        in_specs=[
            pl.BlockSpec((None, tq, LANES), lambda b, g, i: (b, i, g)),
            pl.BlockSpec((None, lk, LANES), lambda b, g, i: (b, 0, g // 2)),
            pl.BlockSpec((None, lk, LANES), lambda b, g, i: (b, 0, g // 2)),
        ],
        out_specs=pl.BlockSpec((None, tq, LANES), lambda b, g, i: (b, i, g)),
        out_shape=jax.ShapeDtypeStruct((bn, tq_all, 512), F32),
        scratch_shapes=[pltpu.VMEM((LANES, LANES), F32),
                        pltpu.VMEM((2, tq, 1), F32), pltpu.VMEM((2, tq, 1), F32),
                        pltpu.VMEM((2, tq, LANES), F32)],
        compiler_params=_params("parallel", "parallel", "arbitrary"),
        name=name,
    )(cq, ck, cv)


def _merge_body(x_ref, gate_ref, oa_ref, ob_ref, oc_ref, wb_ref, wo_ref, y_ref):
    y = jnp.zeros((x_ref.shape[0], D_MODEL), F32)
    for c, o_ref in enumerate((oa_ref, ob_ref, oc_ref)):
        br = _dot(o_ref[...].astype(BF16), wb_ref[c])
        y = y + jax.nn.sigmoid(gate_ref[:, c * D_MODEL:(c + 1) * D_MODEL]) * br
    y_ref[...] = x_ref[...] + _dot(y.astype(BF16), wo_ref[...])


def _merge(x, gate, o_a, o_b, o_c, wb, wo, name):
    n = x.shape[0]
    tm = min(ROW_TILE, n)
    row = lambda w: pl.BlockSpec((tm, w), lambda i: (i, 0))
    return pl.pallas_call(
        _merge_body,
        grid=(n // tm,),
        in_specs=[row(D_MODEL), row(3 * D_MODEL), row(512), row(512), row(512),
                  pl.BlockSpec((3, 512, D_MODEL), lambda i: (0, 0, 0)),
                  pl.BlockSpec((D_MODEL, D_MODEL), lambda i: (0, 0))],
        out_specs=row(D_MODEL),
        out_shape=jax.ShapeDtypeStruct((n, D_MODEL), F32),
        compiler_params=_params("parallel"),
        name=name,
    )(x, gate, o_a, o_b, o_c, wb, wo)


def _peer_stats_body(tm, x_ref, nw_ref, wq_ref, sk_ref, st_ref, stat_ref, top_ref):
    h = _rms(x_ref[...], nw_ref[...]).astype(BF16)
    q = _dot(h, wq_ref[...]).astype(BF16)
    n_hc = 2 * PEER_HEADS
    for hc in range(n_hc):
        st_ref[hc] = _dot_nt(sk_ref[hc], q[:, hc * LANES:(hc + 1) * LANES])

    def top_body(hc, carry):
        for hf in range(tm // LANES):
            s = st_ref[hc, :, hf * LANES:(hf + 1) * LANES]
            vals = []
            for _ in range(PEER_TOPK):
                m = jnp.max(s, axis=0, keepdims=True)
                vals.append(m)
                s = jnp.where(s == m, -jnp.inf, s)
            top_ref[hc, :, hf * LANES:(hf + 1) * LANES] = jnp.concatenate(vals, axis=0)
        return carry

    lax.fori_loop(0, n_hc, top_body, 0)

    row16 = lax.broadcasted_iota(jnp.int32, (PEER_TOPK, LANES), 0)

    def head_body(hd, carry):
        for hf in range(tm // LANES):
            sl = slice(hf * LANES, (hf + 1) * LANES)
            a = top_ref[2 * hd, :, sl]
            b = top_ref[2 * hd + 1, :, sl]
            parts = []
            for i in range(PEER_TOPK):
                lim = PEER_TOPK // (i + 1)
                rows = PEER_TOPK if lim > 8 else 8
                cs = a[i:i + 1] + b[:rows]
                parts.append(jnp.where(row16[:rows] < lim, cs, -jnp.inf))
            cand = jnp.concatenate(parts, axis=0)
            mx = a[0:1] + b[0:1]
            cum = jnp.zeros((1, LANES), F32)
            tau = mx
            work = cand
            for _ in range(PEER_TOPK):
                m = jnp.max(work, axis=0, keepdims=True)
                eq = work == m
                tau = jnp.where(cum < float(PEER_TOPK), m, tau)
                cum = cum + jnp.sum(jnp.where(eq, 1.0, 0.0), axis=0, keepdims=True)
                work = jnp.where(eq, -jnp.inf, work)
            z = jnp.sum(jnp.where(cand >= tau, jnp.exp(cand - mx), 0.0), axis=0, keepdims=True)
            stat_ref[hd, :, sl] = tau
            stat_ref[PEER_HEADS + hd, :, sl] = a[0:1]
            stat_ref[2 * PEER_HEADS + hd, :, sl] = b[0:1]
            stat_ref[3 * PEER_HEADS + hd, :, sl] = 1.0 / z
        return carry

    lax.fori_loop(0, PEER_HEADS, head_body, 0)


def _peer_stats(x, nw, wq, sk, name):
    n = x.shape[0]
    tm = min(ROW_TILE, n)
    n_hc = 2 * PEER_HEADS
    return pl.pallas_call(
        functools.partial(_peer_stats_body, tm),
        grid=(n // tm,),
        in_specs=[pl.BlockSpec((tm, D_MODEL), lambda i: (i, 0)),
                  pl.BlockSpec((1, D_MODEL), lambda i: (0, 0)),
                  pl.BlockSpec((D_MODEL, n_hc * LANES), lambda i: (0, 0)),
                  pl.BlockSpec((n_hc, LANES, LANES), lambda i: (0, 0, 0))],
        out_specs=[pl.BlockSpec((n_hc, LANES, tm), lambda i: (0, 0, i)),
                   pl.BlockSpec((4 * PEER_HEADS, 1, tm), lambda i: (0, 0, i))],
        out_shape=[jax.ShapeDtypeStruct((n_hc, LANES, n), F32),
                   jax.ShapeDtypeStruct((4 * PEER_HEADS, 1, n), F32)],
        scratch_shapes=[pltpu.VMEM((n_hc, PEER_TOPK, tm), F32)],
        compiler_params=_params("parallel"),
        name=name,
    )(x, nw, wq, sk)


def _peer_dense_body(tm, te, x_ref, nw_ref, st_ref, stat_ref, u_ref, vt_ref, o_ref,
                     h_ref, e1_ref, e2_ref, acc_ref):
    e = pl.program_id(1)

    @pl.when(e == 0)
    def _():
        h_ref[...] = _rms(x_ref[...], nw_ref[...]).astype(BF16)
        acc_ref[...] = jnp.zeros(acc_ref.shape, F32)
        for hd in range(PEER_HEADS):
            m1 = stat_ref[PEER_HEADS + hd]
            m2 = stat_ref[2 * PEER_HEADS + hd]
            inv_z = stat_ref[3 * PEER_HEADS + hd]
            e1_ref[hd] = jnp.exp(st_ref[2 * hd] - m1) * inv_z
            e2_ref[hd] = jnp.exp(st_ref[2 * hd + 1] - m2)

    act = jax.nn.gelu(_dot_nt(u_ref[...], h_ref[...]))
    n_i = te // PEER_NKEYS
    parts = []
    for ii in range(n_i):
        i = e * n_i + ii
        gmat = jnp.zeros((PEER_NKEYS, tm), F32)
        for hd in range(PEER_HEADS):
            s1 = st_ref[2 * hd, pl.ds(i, 1), :]
            w1 = e1_ref[hd, pl.ds(i, 1), :]
            tau = stat_ref[hd]
            keep = (st_ref[2 * hd + 1] + s1) >= tau
            gmat = gmat + jnp.where(keep, e2_ref[hd] * w1, 0.0)
        parts.append((gmat * act[ii * PEER_NKEYS:(ii + 1) * PEER_NKEYS]).astype(BF16))
    acc_ref[...] += _dot(vt_ref[...], jnp.concatenate(parts, axis=0))

    @pl.when(e == pl.num_programs(1) - 1)
    def _():
        o_ref[...] = x_ref[...] + acc_ref[...].T


def _peer_dense(x, nw, st, stat, u_bf, vt_bf, name):
    n = x.shape[0]
    tm = min(ROW_TILE, n)
    te = EXPERT_TILE
    n_exp = u_bf.shape[0]
    n_hc = 2 * PEER_HEADS
    return pl.pallas_call(
        functools.partial(_peer_dense_body, tm, te),
        grid=(n // tm, n_exp // te),
        in_specs=[pl.BlockSpec((tm, D_MODEL), lambda i, e: (i, 0)),
                  pl.BlockSpec((1, D_MODEL), lambda i, e: (0, 0)),
                  pl.BlockSpec((n_hc, LANES, tm), lambda i, e: (0, 0, i)),
                  pl.BlockSpec((4 * PEER_HEADS, 1, tm), lambda i, e: (0, 0, i)),
                  pl.BlockSpec((te, D_MODEL), lambda i, e: (e, 0)),
                  pl.BlockSpec((D_MODEL, te), lambda i, e: (0, e))],
        out_specs=pl.BlockSpec((tm, D_MODEL), lambda i, e: (i, 0)),
        out_shape=jax.ShapeDtypeStruct((n, D_MODEL), F32),
        scratch_shapes=[pltpu.VMEM((tm, D_MODEL), BF16),
                        pltpu.VMEM((PEER_HEADS, PEER_NKEYS, tm), F32),
                        pltpu.VMEM((PEER_HEADS, PEER_NKEYS, tm), F32),
                        pltpu.VMEM((D_MODEL, tm), F32)],
        compiler_params=_params("parallel", "arbitrary"),
        name=name,
    )(x, nw, st, stat, u_bf, vt_bf)


def _norm_body(x_ref, w_ref, o_ref):
    o_ref[...] = _rms(x_ref[...], w_ref[...])


def _final_norm(x, w, name):
    n = x.shape[0]
    tm = min(ROW_TILE, n)
    return pl.pallas_call(
        _norm_body,
        grid=(n // tm,),
        in_specs=[pl.BlockSpec((tm, D_MODEL), lambda i: (i, 0)), pl.BlockSpec((1, D_MODEL), lambda i: (0, 0))],
        out_specs=pl.BlockSpec((tm, D_MODEL), lambda i: (i, 0)),
        out_shape=jax.ShapeDtypeStruct((n, D_MODEL), F32),
        compiler_params=_params("parallel"),
        name=name,
    )(x, w)


def _gather_body(n_pages, n_arr, pt_ref, *refs):
    p = pl.program_id(1)
    caches = refs[:n_arr]
    news = refs[n_arr:2 * n_arr]
    outs = refs[2 * n_arr:]
    for c_ref, n_ref, o_ref in zip(caches, news, outs):
        @pl.when(p < n_pages)
        def _():
            o_ref[...] = c_ref[...]

        @pl.when(p == n_pages)
        def _():
            o_ref[...] = jnp.zeros(o_ref.shape, F32)
            o_ref[0:n_ref.shape[0], :] = n_ref[...]

        @pl.when(p > n_pages)
        def _():
            o_ref[...] = jnp.zeros(o_ref.shape, F32)


def _gather_pages(page_table, layer, caches, news, lk, name):
    db, n_pages = page_table.shape
    n_arr = len(caches)
    n_blk = lk // PAGE_SIZE
    in_specs = []
    for c in caches:
        in_specs.append(pl.BlockSpec((None, None, PAGE_SIZE, c.shape[-1]),
                                     lambda b, p, pt: (layer, pt[b, jnp.minimum(p, n_pages - 1)], 0, 0)))
    for nw in news:
        in_specs.append(pl.BlockSpec((None,) + nw.shape[1:], lambda b, p, pt: (b, 0, 0)))
    out_specs = [pl.BlockSpec((None, PAGE_SIZE, c.shape[-1]), lambda b, p, pt: (b, p, 0)) for c in caches]
    grid_spec = pltpu.PrefetchScalarGridSpec(
        num_scalar_prefetch=1, grid=(db, n_blk), in_specs=in_specs, out_specs=out_specs)
    return pl.pallas_call(
        functools.partial(_gather_body, n_pages, n_arr),
        grid_spec=grid_spec,
        out_shape=[jax.ShapeDtypeStruct((db, lk, c.shape[-1]), F32) for c in caches],
        compiler_params=_params("parallel", "arbitrary"),
        name=name,
    )(page_table, *caches, *news)


def _layer_weights(l, norm_mix, norm_ffn, w_in, a_lambda, a_subln, b_cmp_pe, b_cmp_w1, b_cmp_w2,
                   w_branch, w_out, peer_wq, peer_subkeys, peer_u, peer_v):
    w = w_in[l]
    bg_w = jnp.pad(w[:, 2304:2328], ((0, 0), (0, LANES - 3 * B_HEADS)))
    return {
        "norm_mix": norm_mix[l][None, :],
        "norm_ffn": norm_ffn[l][None, :],
        "w_a": w[:, 0:1024].astype(BF16),
        "w_b": jnp.concatenate([w[:, 1024:2304], bg_w], axis=1).astype(BF16),
        "w_c": w[:, 2328:3352].astype(BF16),
        "w_g": w[:, 3352:6424].astype(BF16),
        "lam": a_lambda[l],
        "subln": a_subln[l][None, :],
        "cmp": _compress_weights(b_cmp_pe[l], b_cmp_w1[l], b_cmp_w2[l]),
        "w_branch": w_branch[l].astype(BF16),
        "w_out": w_out[l].astype(BF16),
        "wq": peer_wq[l].astype(BF16),
        "sk": peer_subkeys[l].reshape(2 * PEER_HEADS, PEER_NKEYS, LANES).astype(BF16),
        "u": peer_u[l].astype(BF16),
        "vt": peer_v[l].T.astype(BF16),
        "lam_init": 0.8 - 0.6 * math.exp(-0.3 * l),
    }


def _project_all(x, lw, cos_t, sin_t, tag):
    aq, ak, av = _project(x, lw["norm_mix"], lw["w_a"], cos_t, sin_t, PIECES_A, WIDTHS_A, "proj_a_" + tag)
    b_out = _project(x, lw["norm_mix"], lw["w_b"], cos_t, sin_t, PIECES_B, WIDTHS_B, "proj_b_" + tag)
    cq, ck, cv = _project(x, lw["norm_mix"], lw["w_c"], cos_t, sin_t, PIECES_C, WIDTHS_C, "proj_c_" + tag)
    (gate,) = _project(x, lw["norm_mix"], lw["w_g"], cos_t, sin_t, PIECES_G, WIDTHS_G, "proj_g_" + tag)
    return (aq, ak, av), b_out, (cq, ck, cv), gate


def _ffn(x, lw, tag):
    st, stat = _peer_stats(x, lw["norm_ffn"], lw["wq"], lw["sk"], "peer_stats_" + tag)
    return _peer_dense(x, lw["norm_ffn"], st, stat, lw["u"], lw["vt"], "peer_dense_" + tag)


def _prompt_layer(x, lw, cos_t, sin_t, bn, t):
    (aq, ak, av), (bqc, bqr, bkc, bvc, bks, bvs, bkw, bvw, bg), (cq, ck, cv), gate = _project_all(
        x, lw, cos_t, sin_t, "p")
    r3 = lambda a: a.reshape(bn, t, a.shape[-1])
    o_a = _diff_attention(r3(aq), r3(ak), r3(av), lw["lam"], lw["subln"], lw["lam_init"], 0, "diff_p")
    kcmp, vcmp = _compress(r3(bkc), r3(bvc), lw["cmp"], "cmp_p")
    o_b = _nsa_attention(r3(bqc), r3(bqr), r3(bg), kcmp, vcmp, r3(bks), r3(bvs), r3(bkw), r3(bvw),
                         0, t, 0, t, "nsa_p")
    o_c = _moba_attention(r3(cq), r3(ck), r3(cv), 0, "moba_p")
    n = bn * t
    x = _merge(x, gate, o_a.reshape(n, 512), o_b.reshape(n, 512), o_c.reshape(n, 512),
               lw["w_branch"], lw["w_out"], "merge_p")
    x = _ffn(x, lw, "p")
    wp = min(WINDOW, t)
    rows = (r3(ak).reshape(bn, t, A_KV, 2 * HD), r3(av).reshape(bn, t, A_KV, 2 * HD),
            r3(bkc).reshape(bn, t, B_KV, HD), r3(bvc).reshape(bn, t, B_KV, HD),
            r3(bks).reshape(bn, t, B_KV, HD), r3(bvs).reshape(bn, t, B_KV, HD),
            r3(ck).reshape(bn, t, C_KV, HD), r3(cv).reshape(bn, t, C_KV, HD),
            r3(bkw)[:, t - wp:].reshape(bn, wp, B_KV, HD), r3(bvw)[:, t - wp:].reshape(bn, wp, B_KV, HD))
    return x, rows


def _sample_layer(x, lw, l, cos_t, sin_t, db, t, past, caches, state_wk, state_wv, page_table):
    (aq, ak, av), (bqc, bqr, bkc, bvc, bks, bvs, bkw, bvw, bg), (cq, ck, cv), gate = _project_all(
        x, lw, cos_t, sin_t, "s")
    r3 = lambda a: a.reshape(db, t, a.shape[-1])
    tq = 8
    padq = lambda a: jnp.pad(r3(a), ((0, 0), (0, tq - t), (0, 0)))
    lk = -(-(past + t) // KV_CHUNK) * KV_CHUNK
    news = [r3(a) for a in (ak, av, bkc, bvc, bks, bvs, ck, cv)]
    full = _gather_pages(page_table, l, caches, news, lk, "gather_s")
    f_ak, f_av, f_bkc, f_bvc, f_bks, f_bvs, f_ck, f_cv = full
    o_a = _diff_attention(padq(aq), f_ak, f_av, lw["lam"], lw["subln"], lw["lam_init"], past, "diff_s")
    kcmp, vcmp = _compress(f_bkc, f_bvc, lw["cmp"], "cmp_s")
    wb = state_wk.shape[2]
    lw_pad = -(-(wb + t) // KV_CHUNK) * KV_CHUNK
    kw = jnp.concatenate([state_wk[l], r3(bkw)], axis=1)
    vw = jnp.concatenate([state_wv[l], r3(bvw)], axis=1)
    padw = lambda a: jnp.pad(a, ((0, 0), (0, lw_pad - a.shape[1]), (0, 0)))
    o_b = _nsa_attention(padq(bqc), padq(bqr), padq(bg), kcmp, vcmp, f_bks, f_bvs, padw(kw), padw(vw),
                         past, past + t, past - wb, wb + t, "nsa_s")
    o_c = _moba_attention(padq(cq), f_ck, f_cv, past, "moba_s")
    n = db * t
    cut = lambda o: o[:, :t].reshape(n, 512)
    x = _merge(x, gate, cut(o_a), cut(o_b), cut(o_c), lw["w_branch"], lw["w_out"], "merge_s")
    x = _ffn(x, lw, "s")
    wn = min(WINDOW, past + t)
    lwn = kw.shape[1]
    rows = (r3(ak).reshape(db, t, A_KV, 2 * HD), r3(av).reshape(db, t, A_KV, 2 * HD),
            r3(bkc).reshape(db, t, B_KV, HD), r3(bvc).reshape(db, t, B_KV, HD),
            r3(bks).reshape(db, t, B_KV, HD), r3(bvs).reshape(db, t, B_KV, HD),
            r3(ck).reshape(db, t, C_KV, HD), r3(cv).reshape(db, t, C_KV, HD),
            kw[:, lwn - wn:].reshape(db, wn, B_KV, HD), vw[:, lwn - wn:].reshape(db, wn, B_KV, HD))
    return x, rows


def kernel(x_prompt, x_sample, cache_a_k, cache_a_v, cache_b_kc, cache_b_vc, cache_b_ks, cache_b_vs,
           cache_c_k, cache_c_v, state_b_wk, state_b_wv, page_table, norm_mix, norm_ffn, norm_final,
           w_in, a_lambda, a_subln, b_cmp_pe, b_cmp_w1, b_cmp_w2, w_branch, w_out,
           peer_wq, peer_subkeys, peer_u, peer_v):
    bn, t, _ = x_prompt.shape
    db, ts, _ = x_sample.shape
    depth = w_in.shape[0]
    past = page_table.shape[1] * PAGE_SIZE
    flat = lambda c: c.reshape(c.shape[0], c.shape[1], c.shape[2], c.shape[3] * c.shape[4])
    caches = [flat(c) for c in (cache_a_k, cache_a_v, cache_b_kc, cache_b_vc, cache_b_ks, cache_b_vs,
                                cache_c_k, cache_c_v)]
    swk = state_b_wk.reshape(state_b_wk.shape[:3] + (B_KV * HD,))
    swv = state_b_wv.reshape(state_b_wv.shape[:3] + (B_KV * HD,))

    cos_p, sin_p = _rope_tables(jnp.arange(t, dtype=jnp.int32))
    tm_s = min(ROW_TILE, db * ts)
    cos_s, sin_s = _rope_tables(past + (jnp.arange(tm_s, dtype=jnp.int32) % ts))

    xp = x_prompt.reshape(bn * t, D_MODEL)
    xs = x_sample.reshape(db * ts, D_MODEL)
    rows_p, rows_s = [], []
    for l in range(depth):
        lw = _layer_weights(l, norm_mix, norm_ffn, w_in, a_lambda, a_subln, b_cmp_pe, b_cmp_w1, b_cmp_w2,
                            w_branch, w_out, peer_wq, peer_subkeys, peer_u, peer_v)
        xp, rp = _prompt_layer(xp, lw, cos_p, sin_p, bn, t)
        xs, rs = _sample_layer(xs, lw, l, cos_s, sin_s, db, ts, past, caches, swk, swv, page_table)
        rows_p.append(rp)
        rows_s.append(rs)
    y_prompt = _final_norm(xp, norm_final[None, :], "final_p").reshape(bn, t, D_MODEL)
    y_sample = _final_norm(xs, norm_final[None, :], "final_s").reshape(db, ts, D_MODEL)
    outs_p = [jnp.stack(r, axis=0) for r in zip(*rows_p)]
    outs_s = [jnp.stack(r, axis=0) for r in zip(*rows_s)]
    return (y_prompt, y_sample, *outs_p, *outs_s)
```

```python
import functools
import math

import jax
import jax.numpy as jnp
import numpy as np
from jax import lax
from jax.experimental import pallas as pl
from jax.experimental.pallas import tpu as pltpu

F32 = jnp.float32
BF16 = jnp.bfloat16

D_MODEL = 1024
HD = 64
A_HEADS, A_KV = 4, 2
B_HEADS, B_KV = 8, 2
C_HEADS, C_KV = 8, 4
CMP_BLOCK, CMP_STRIDE, CMP_HIDDEN = 32, 16, 128
SEL_BLOCK, SEL_TOPK = 64, 8
CMP_PER_SEL_SHIFT = 2
WINDOW = 512
MOBA_BLOCK, MOBA_TOPK = 256, 3
PEER_HEADS, PEER_NKEYS, PEER_TOPK = 8, 128, 16
ROPE_THETA = 10000.0
EPS = 1e-6
PAGE_SIZE = 128

LANES = 128
NEG = -1e30
VMEM_LIMIT = 48 * 1024 * 1024
ROW_TILE = 256
KV_CHUNK = 256
EXPERT_TILE = 512


def _params(*sem):
    return pltpu.CompilerParams(dimension_semantics=sem, vmem_limit_bytes=VMEM_LIMIT)


def _dot(a, b):
    return jnp.dot(a, b, preferred_element_type=F32)


def _dot_nt(a, b):
    return lax.dot_general(a, b, (((1,), (1,)), ((), ())), preferred_element_type=F32)


def _split3(a):
    hi = a.astype(BF16)
    r1 = a - hi.astype(F32)
    mid = r1.astype(BF16)
    lo = (r1 - mid.astype(F32)).astype(BF16)
    return hi, mid, lo


def _rms(x, w):
    return x * lax.rsqrt(jnp.mean(x * x, axis=-1, keepdims=True) + EPS) * w


def _proj_body(pieces, chunk, x_ref, nw_ref, w_ref, cos_ref, sin_ref, *outs):
    h = _rms(x_ref[...], nw_ref[...]).astype(BF16)
    cos = cos_ref[...]
    sin = sin_ref[...]
    lane = lax.broadcasted_iota(jnp.int32, (1, LANES), 1)
    first = (lane & (HD - 1)) < (HD // 2)
    n_cols = len(pieces) * LANES
    for c0 in range(0, n_cols, chunk):
        w = min(chunk, n_cols - c0)
        z = _dot(h, w_ref[:, c0:c0 + w])
        for p in range(w // LANES):
            zp = z[:, p * LANES:(p + 1) * LANES]
            for (oi, oc, rope) in pieces[c0 // LANES + p]:
                if rope:
                    rot = jnp.where(first, pltpu.roll(zp, LANES - HD // 2, 1), pltpu.roll(zp, HD // 2, 1))
                    outs[oi][:, oc:oc + LANES] = zp * cos + rot * sin
                else:
                    outs[oi][:, oc:oc + LANES] = zp


def _project(x, nw, w, cos_t, sin_t, pieces, out_widths, name):
    n = x.shape[0]
    tm = min(ROW_TILE, n)
    ntab = cos_t.shape[0] // tm
    ncols = w.shape[1]
    body = functools.partial(_proj_body, pieces, 512)
    return pl.pallas_call(
        body,
        grid=(n // tm,),
        in_specs=[
            pl.BlockSpec((tm, D_MODEL), lambda i: (i, 0)),
            pl.BlockSpec((1, D_MODEL), lambda i: (0, 0)),
            pl.BlockSpec((D_MODEL, ncols), lambda i: (0, 0)),
            pl.BlockSpec((tm, LANES), lambda i: (i % ntab, 0)),
            pl.BlockSpec((tm, LANES), lambda i: (i % ntab, 0)),
        ],
        out_specs=[pl.BlockSpec((tm, ow), lambda i: (i, 0)) for ow in out_widths],
        out_shape=[jax.ShapeDtypeStruct((n, ow), F32) for ow in out_widths],
        compiler_params=_params("parallel"),
        name=name,
    )(x, nw, w, cos_t, sin_t)


def _plain(oi, width):
    return [[(oi, c, False)] for c in range(0, width, LANES)]


def _roped(oi, width):
    return [[(oi, c, True)] for c in range(0, width, LANES)]


PIECES_A = _roped(0, 512) + _roped(1, 256) + _plain(2, 256)
WIDTHS_A = (512, 256, 256)
PIECES_B = ([[(0, c, False), (1, c, True)] for c in range(0, 512, LANES)]
            + _plain(2, 128) + _plain(3, 128) + _roped(4, 128) + _plain(5, 128)
            + _roped(6, 128) + _plain(7, 128) + _plain(8, 128))
WIDTHS_B = (512, 512, 128, 128, 128, 128, 128, 128, 128)
PIECES_C = _roped(0, 512) + _roped(1, 256) + _plain(2, 256)
WIDTHS_C = (512, 256, 256)
PIECES_G = _plain(0, 3072)
WIDTHS_G = (3072,)


def _rope_tables(pos):
    half = HD // 2
    freqs = jnp.power(ROPE_THETA, -jnp.arange(half, dtype=F32) / half)
    ang = pos.astype(F32)[:, None] * freqs[None, :]
    cos = jnp.cos(ang)
    sin = jnp.sin(ang)
    cos_t = jnp.concatenate([cos, cos, cos, cos], axis=1)
    sin_t = jnp.concatenate([-sin, sin, -sin, sin], axis=1)
    return cos_t, sin_t


def _flash_init(m_ref, l_ref, acc_ref):
    m_ref[...] = jnp.full(m_ref.shape, NEG, F32)
    l_ref[...] = jnp.zeros(l_ref.shape, F32)
    acc_ref[...] = jnp.zeros(acc_ref.shape, F32)


def _lane_blocks(x):
    return [x[:, c:c + LANES] for c in range(0, x.shape[1], LANES)]


def _flash_step(hh, q_emb, kc, vc, mask, m_ref, l_ref, acc_ref):
    s = _dot_nt(q_emb, kc)
    if mask is not None:
        s = jnp.where(mask, s, NEG)
    blocks = _lane_blocks(s)
    mx = blocks[0]
    for b in blocks[1:]:
        mx = jnp.maximum(mx, b)
    m_old = m_ref[hh]
    m_new = jnp.maximum(m_old, jnp.max(mx, axis=1, keepdims=True))
    alpha = jnp.exp(m_old - m_new)
    p = jnp.exp(s - jnp.concatenate([m_new] * len(blocks), axis=1))
    if mask is not None:
        p = jnp.where(mask, p, 0.0)
    pblocks = _lane_blocks(p)
    tot = pblocks[0]
    for b in pblocks[1:]:
        tot = tot + b
    l_ref[hh] = alpha * l_ref[hh] + jnp.sum(tot, axis=1, keepdims=True)
    acc_ref[hh] = alpha * acc_ref[hh] + _dot(p.astype(BF16), vc)
    m_ref[hh] = m_new


def _flash_out(hh, l_ref, acc_ref):
    return acc_ref[hh] / jnp.maximum(l_ref[hh], 1e-30)


def _chunk(ref, j, tk):
    return ref[pl.ds(pl.multiple_of(j * tk, tk), tk), :].astype(BF16)


def _diff_body(tq, tk, q_pos0, lam_init, lam_ref, subln_ref, q_ref, k_ref, v_ref, o_ref,
               m_ref, l_ref, acc_ref):
    qi = pl.program_id(2)
    lane = lax.broadcasted_iota(jnp.int32, (1, LANES), 1)
    upper = lane >= HD
    q0 = q_pos0 + qi * tq
    qpos = q0 + lax.broadcasted_iota(jnp.int32, (tq, 1), 0)
    _flash_init(m_ref, l_ref, acc_ref)
    qs = []
    for r in range(2):
        qb = q_ref[:, r * LANES:(r + 1) * LANES] * (HD ** -0.5)
        qs.append(jnp.where(upper, 0.0, qb).astype(BF16))
        qs.append(jnp.where(upper, qb, 0.0).astype(BF16))
    n_full = (q0 + 1) // tk
    n_chunks = (q0 + tq - 1) // tk + 1

    def body(masked, j, carry):
        kc = _chunk(k_ref, j, tk)
        vc = _chunk(v_ref, j, tk)
        mask = None
        if masked:
            mask = (j * tk + lax.broadcasted_iota(jnp.int32, (1, tk), 1)) <= qpos
        for hh in range(4):
            _flash_step(hh, qs[hh], kc, vc, mask, m_ref, l_ref, acc_ref)
        return carry

    lax.fori_loop(0, n_full, functools.partial(body, False), 0)
    lax.fori_loop(n_full, n_chunks, functools.partial(body, True), 0)
    lp = lam_ref[...]
    lam = (jnp.exp(jnp.sum(lp[0:1] * lp[1:2], axis=1, keepdims=True))
           - jnp.exp(jnp.sum(lp[2:3] * lp[3:4], axis=1, keepdims=True)) + lam_init)
    for r in range(2):
        o = _flash_out(2 * r, l_ref, acc_ref) - lam * _flash_out(2 * r + 1, l_ref, acc_ref)
        o_ref[:, r * LANES:(r + 1) * LANES] = _rms(o, subln_ref[...]) * (1.0 - lam_init)


def _diff_attention(aq, ak, av, lam_p, subln, lam_init, q_pos0, name):
    bn, tq_all, _ = aq.shape
    lk = ak.shape[1]
    tq = min(ROW_TILE, tq_all)
    tk = KV_CHUNK
    body = functools.partial(_diff_body, tq, tk, q_pos0, lam_init)
    return pl.pallas_call(
        body,
        grid=(bn, A_KV, tq_all // tq),
        in_specs=[
            pl.BlockSpec((4, HD), lambda b, g, i: (0, 0)),
            pl.BlockSpec((1, 2 * HD), lambda b, g, i: (0, 0)),
            pl.BlockSpec((None, tq, 256), lambda b, g, i: (b, i, g)),
            pl.BlockSpec((None, lk, LANES), lambda b, g, i: (b, 0, g)),
            pl.BlockSpec((None, lk, LANES), lambda b, g, i: (b, 0, g)),
        ],
        out_specs=pl.BlockSpec((None, tq, 256), lambda b, g, i: (b, i, g)),
        out_shape=jax.ShapeDtypeStruct((bn, tq_all, 512), F32),
        scratch_shapes=[pltpu.VMEM((4, tq, LANES), F32), pltpu.VMEM((4, tq, LANES), F32),
                        pltpu.VMEM((4, tq, LANES), F32)],
        compiler_params=_params("parallel", "parallel", "arbitrary"),
        name=name,
    )(lam_p, subln, aq, ak, av)


def _compress_one(x_ref, pe_ref, w1_ref, w2_ref, o_ref):
    n_half = CMP_STRIDE
    ha = jnp.zeros((LANES, 2 * CMP_HIDDEN), F32)
    hb = jnp.zeros((LANES, 2 * CMP_HIDDEN), F32)
    for p in range(n_half):
        x = x_ref[pl.ds(p, LANES, stride=CMP_STRIDE), :]
        ha = ha + _dot((x + pe_ref[p]).astype(BF16), w1_ref[p])
        hb = hb + _dot((x + pe_ref[n_half + p]).astype(BF16), w1_ref[n_half + p])
    h = ha + pltpu.roll(hb, LANES - 1, 0)
    y = _dot(jax.nn.gelu(h).astype(BF16), w2_ref[...])
    row = lax.broadcasted_iota(jnp.int32, (LANES, 1), 0)
    o_ref[...] = jnp.where(row < LANES - 1, y, 0.0)


def _compress_body(xk_ref, xv_ref, pek_ref, pev_ref, w1k_ref, w1v_ref, w2k_ref, w2v_ref, ok_ref, ov_ref):
    _compress_one(xk_ref, pek_ref, w1k_ref, w2k_ref, ok_ref)
    _compress_one(xv_ref, pev_ref, w1v_ref, w2v_ref, ov_ref)


def _compress(xk, xv, cw, name):
    bn = xk.shape[0]
    n_tok = LANES * CMP_STRIDE
    x_spec = pl.BlockSpec((None, n_tok, LANES), lambda b: (b, 0, 0))
    pe_spec = pl.BlockSpec((CMP_BLOCK, 1, LANES), lambda b: (0, 0, 0))
    w1_spec = pl.BlockSpec((CMP_BLOCK, LANES, 2 * CMP_HIDDEN), lambda b: (0, 0, 0))
    w2_spec = pl.BlockSpec((2 * CMP_HIDDEN, LANES), lambda b: (0, 0))
    o_spec = pl.BlockSpec((None, LANES, LANES), lambda b: (b, 0, 0))
    return pl.pallas_call(
        _compress_body,
        grid=(bn,),
        in_specs=[x_spec, x_spec, pe_spec, pe_spec, w1_spec, w1_spec, w2_spec, w2_spec],
        out_specs=[o_spec, o_spec],
        out_shape=[jax.ShapeDtypeStruct((bn, LANES, LANES), F32)] * 2,
        compiler_params=_params("parallel"),
        name=name,
    )(xk, xv, cw["pek"], cw["pev"], cw["w1k"], cw["w1v"], cw["w2k"], cw["w2v"])


def _compress_weights(pe, w1, w2):
    out = {}
    for idx, tag in ((0, "k"), (1, "v")):
        w1p = w1[idx].reshape(CMP_BLOCK, HD, CMP_HIDDEN)
        z = jnp.zeros_like(w1p)
        w1b = jnp.concatenate([jnp.concatenate([w1p, z], axis=2), jnp.concatenate([z, w1p], axis=2)], axis=1)
        z2 = jnp.zeros_like(w2[idx])
        w2b = jnp.concatenate([jnp.concatenate([w2[idx], z2], axis=1), jnp.concatenate([z2, w2[idx]], axis=1)], axis=0)
        out["w1" + tag] = w1b.astype(BF16)
        out["w2" + tag] = w2b.astype(BF16)
        out["pe" + tag] = jnp.concatenate([pe[idx], pe[idx]], axis=1)[:, None, :]
    return out


def _rank_desc(vals, n, lane):
    rank = jnp.zeros(vals.shape, F32)
    for i in range(n):
        col = vals[:, i:i + 1]
        tie = jnp.where(lane > i, 1.0, 0.0)
        rank = rank + jnp.where(col > vals, 1.0, 0.0) + jnp.where(col == vals, tie, 0.0)
    return rank


def _nsa_body(tq, tk, q_pos0, ns, w_pos0, w_valid, lw,
              qc_ref, qr_ref, bg_ref, kcmp_ref, vcmp_ref, ks_ref, vs_ref, kw_ref, vw_ref, o_ref,
              m_ref, l_ref, acc_ref):
    g = pl.program_id(1)
    qi = pl.program_id(2)
    lane = lax.broadcasted_iota(jnp.int32, (1, LANES), 1)
    halfi = lane >> 6
    q0 = q_pos0 + qi * tq
    qpos = q0 + lax.broadcasted_iota(jnp.int32, (tq, 1), 0)
    rep = B_HEADS // B_KV

    def embed(ref, r):
        v = ref[:, (r // 2) * LANES:(r // 2 + 1) * LANES] * (HD ** -0.5)
        own = jnp.where(halfi == (r % 2), v, 0.0)
        dup = own + pltpu.roll(own, HD, 1)
        return jnp.where(halfi == g, dup, 0.0).astype(BF16)

    def pick(o, r):
        return jnp.where(g == (r % 2), o, pltpu.roll(o, HD, 1))

    kcmp = kcmp_ref[...].astype(BF16)
    vcmp = vcmp_ref[...].astype(BF16)
    cmask = (lane * CMP_STRIDE + (CMP_BLOCK - 1)) <= qpos
    psum = jnp.zeros((tq, LANES), F32)
    o_cmp = []
    for r in range(rep):
        s = jnp.where(cmask, _dot_nt(embed(qc_ref, r), kcmp), NEG)
        m = jnp.max(s, axis=1, keepdims=True)
        p = jnp.where(cmask, jnp.exp(s - m), 0.0)
        p = p / jnp.maximum(jnp.sum(p, axis=1, keepdims=True), 1e-30)
        psum = psum + p
        o_cmp.append(_dot(p.astype(BF16), vcmp))

    ci = lax.broadcasted_iota(jnp.int32, (LANES, LANES), 0)
    cj = lax.broadcasted_iota(jnp.int32, (LANES, LANES), 1)
    onehot = jnp.where((ci >> CMP_PER_SEL_SHIFT) == cj, 1.0, 0.0).astype(BF16)
    hi, mid, lo = _split3(psum)
    imp = _dot(hi, onehot) + _dot(mid, onehot) + _dot(lo, onehot)
    cur = qpos >> 6
    impm = jnp.where(lane == cur, jnp.inf, jnp.where(lane < cur, imp, -jnp.inf))
    rank = _rank_desc(impm, ns, lane)
    sel = jnp.where(rank < float(SEL_TOPK), jnp.where(lane <= cur, 1.0, 0.0), 0.0).astype(BF16)

    qsr = [embed(qr_ref, r) for r in range(rep)]
    _flash_init(m_ref, l_ref, acc_ref)
    n_chunks = (q0 + tq - 1) // tk + 1
    bpc = tk // SEL_BLOCK

    def sel_body(j, carry):
        kc = _chunk(ks_ref, j, tk)
        vc = _chunk(vs_ref, j, tk)
        ei = lax.broadcasted_iota(jnp.int32, (LANES, tk), 0)
        el = lax.broadcasted_iota(jnp.int32, (LANES, tk), 1)
        expand = jnp.where(ei == j * bpc + (el >> 6), 1.0, 0.0).astype(BF16)
        allowed = _dot(sel, expand)
        kpos = j * tk + lax.broadcasted_iota(jnp.int32, (1, tk), 1)
        mask = jnp.where(kpos <= qpos, allowed, 0.0) > 0.5
        for r in range(rep):
            _flash_step(r, qsr[r], kc, vc, mask, m_ref, l_ref, acc_ref)
        return carry

    lax.fori_loop(0, n_chunks, sel_body, 0)
    o_slc = [_flash_out(r, l_ref, acc_ref) for r in range(rep)]

    _flash_init(m_ref, l_ref, acc_ref)
    w_lo = jnp.maximum(q0 - WINDOW - w_pos0, 0) // tk
    w_hi = jnp.minimum(q0 + tq - 1 - w_pos0, lw - 1) // tk + 1

    def win_body(j, carry):
        kc = _chunk(kw_ref, j, tk)
        vc = _chunk(vw_ref, j, tk)
        kidx = j * tk + lax.broadcasted_iota(jnp.int32, (1, tk), 1)
        dist = qpos - (kidx + w_pos0)
        inwin = jnp.where(dist >= 0, jnp.where(dist <= WINDOW, 1.0, 0.0), 0.0)
        mask = jnp.where(kidx < w_valid, inwin, 0.0) > 0.5
        for r in range(rep):
            _flash_step(r, qsr[r], kc, vc, mask, m_ref, l_ref, acc_ref)
        return carry

    lax.fori_loop(w_lo, w_hi, win_body, 0)

    sig = jax.nn.sigmoid(bg_ref[...])
    outs = []
    for r in range(rep):
        o_win = _flash_out(r, l_ref, acc_ref)
        base = (g * rep + r) * 3
        gates = [jnp.sum(jnp.where(lane == base + c, sig, 0.0), axis=1, keepdims=True) for c in range(3)]
        outs.append(pick(gates[0] * o_cmp[r] + gates[1] * o_slc[r] + gates[2] * o_win, r))
    for pr in range(rep // 2):
        o_ref[:, pr * LANES:(pr + 1) * LANES] = jnp.where(halfi == 0, outs[2 * pr], outs[2 * pr + 1])


def _nsa_attention(bqc, bqr, bg, kcmp, vcmp, ks, vs, kw, vw, q_pos0, k_valid, w_pos0, w_valid, name):
    bn, tq_all, _ = bqc.shape
    lk = ks.shape[1]
    lw = kw.shape[1]
    tq = min(ROW_TILE, tq_all)
    tk = KV_CHUNK
    ns = -(-k_valid // SEL_BLOCK)
    body = functools.partial(_nsa_body, tq, tk, q_pos0, ns, w_pos0, w_valid, lw)
    q_spec = pl.BlockSpec((None, tq, 256), lambda b, g, i: (b, i, g))
    full = lambda rows: pl.BlockSpec((None, rows, LANES), lambda b, g, i: (b, 0, 0))
    return pl.pallas_call(
        body,
        grid=(bn, B_KV, tq_all // tq),
        in_specs=[q_spec, q_spec, pl.BlockSpec((None, tq, LANES), lambda b, g, i: (b, i, 0)),
                  full(LANES), full(LANES), full(lk), full(lk), full(lw), full(lw)],
        out_specs=pl.BlockSpec((None, tq, 256), lambda b, g, i: (b, i, g)),
        out_shape=jax.ShapeDtypeStruct((bn, tq_all, 512), F32),
        scratch_shapes=[pltpu.VMEM((4, tq, LANES), F32), pltpu.VMEM((4, tq, LANES), F32),
                        pltpu.VMEM((4, tq, LANES), F32)],
        compiler_params=_params("parallel", "parallel", "arbitrary"),
        name=name,
    )(bqc, bqr, bg, kcmp, vcmp, ks, vs, kw, vw)


def _moba_body(tq, tk, q_pos0, nb, q_ref, k_ref, v_ref, o_ref, km_ref, m_ref, l_ref, acc_ref):
    g = pl.program_id(1)
    qi = pl.program_id(2)
    gh = g % 2
    lane = lax.broadcasted_iota(jnp.int32, (1, LANES), 1)
    halfi = lane >> 6
    q0 = q_pos0 + qi * tq
    qpos = q0 + lax.broadcasted_iota(jnp.int32, (tq, 1), 0)
    cur = qpos >> 8
    rep = C_HEADS // C_KV

    km_ref[...] = jnp.zeros(km_ref.shape, F32)
    km_ref[0:nb, :] = jnp.sum(k_ref[...].reshape(nb, MOBA_BLOCK, LANES), axis=1) * (1.0 / MOBA_BLOCK)
    km_hi, km_mid, _ = _split3(km_ref[...])

    qv = q_ref[...]
    qs = []
    sels = []
    for r in range(rep):
        own = jnp.where(halfi == r, qv, 0.0)
        dup = own + pltpu.roll(own, HD, 1)
        qe = jnp.where(halfi == gh, dup, 0.0)
        q_hi, q_mid, _ = _split3(qe)
        s_blk = _dot_nt(q_hi, km_hi) + _dot_nt(q_hi, km_mid) + _dot_nt(q_mid, km_hi)
        sm = jnp.where(lane < cur, s_blk, -jnp.inf)
        rank = _rank_desc(sm, nb, lane)
        past = jnp.where(rank < float(MOBA_TOPK), jnp.where(lane < cur, 1.0, 0.0), 0.0)
        sels.append(jnp.where(lane == cur, 1.0, past))
        qs.append((qe * (HD ** -0.5)).astype(BF16))

    _flash_init(m_ref, l_ref, acc_ref)
    n_chunks = (q0 + tq - 1) // tk + 1

    def body(j, carry):
        kc = _chunk(k_ref, j, tk)
        vc = _chunk(v_ref, j, tk)
        kpos = j * tk + lax.broadcasted_iota(jnp.int32, (1, tk), 1)
        causal = jnp.where(kpos <= qpos, 1.0, 0.0)
        for r in range(rep):
            col = jnp.sum(jnp.where(lane == j, sels[r], 0.0), axis=1, keepdims=True)
            mask = (causal * col) > 0.5
            _flash_step(r, qs[r], kc, vc, mask, m_ref, l_ref, acc_ref)
        return carry

    lax.fori_loop(0, n_chunks, body, 0)
    outs = []
    for r in range(rep):
        o = _flash_out(r, l_ref, acc_ref)
        outs.append(jnp.where(gh == r, o, pltpu.roll(o, HD, 1)))
    o_ref[...] = jnp.where(halfi == 0, outs[0], outs[1])


def _moba_attention(cq, ck, cv, q_pos0, name):
    bn, tq_all, _ = cq.shape
    lk = ck.shape[1]
    tq = min(ROW_TILE, tq_all)
    tk = MOBA_BLOCK
    nb = lk // MOBA_BLOCK
    body = functools.partial(_moba_body, tq, tk, q_pos0, nb)
    return pl.pallas_call(
        body,
        grid=(bn, C_KV, tq_all // tq),
        in_specs=[
            pl.BlockSpec((None, tq, LANES), lambda b, g, i: (b, i, g)),
            pl.BlockSpec((None, lk, LANES), lambda b, g, i: (b, 0, g // 2)),
            pl.BlockSpec((None, lk, LANES), lambda b, g, i: (b, 0, g // 2)),
        ],
        out_specs=pl.BlockSpec((None, tq, LANES), lambda b, g, i: (b, i, g)),
        out_shape=jax.ShapeDtypeStruct((bn, tq_all, 512), F32),
        scratch_shapes=[pltpu.VMEM((LANES, LANES), F32),
                        pltpu.VMEM((2, tq, LANES), F32), pltpu.VMEM((2, tq, LANES), F32),
                        pltpu.VMEM((2, tq, LANES), F32)],
        compiler_params=_params("parallel", "parallel", "arbitrary"),
        name=name,
    )(cq, ck, cv)


def _merge_body(x_ref, gate_ref, oa_ref, ob_ref, oc_ref, wb_ref, wo_ref, y_ref):
    y = jnp.zeros((x_ref.shape[0], D_MODEL), F32)
    for c, o_ref in enumerate((oa_ref, ob_ref, oc_ref)):
        br = _dot(o_ref[...].astype(BF16), wb_ref[c])
        y = y + jax.nn.sigmoid(gate_ref[:, c * D_MODEL:(c + 1) * D_MODEL]) * br
    y_ref[...] = x_ref[...] + _dot(y.astype(BF16), wo_ref[...])


def _merge(x, gate, o_a, o_b, o_c, wb, wo, name):
    n = x.shape[0]
    tm = min(ROW_TILE, n)
    row = lambda w: pl.BlockSpec((tm, w), lambda i: (i, 0))
    return pl.pallas_call(
        _merge_body,
        grid=(n // tm,),
        in_specs=[row(D_MODEL), row(3 * D_MODEL), row(512), row(512), row(512),
                  pl.BlockSpec((3, 512, D_MODEL), lambda i: (0, 0, 0)),
                  pl.BlockSpec((D_MODEL, D_MODEL), lambda i: (0, 0))],
        out_specs=row(D_MODEL),
        out_shape=jax.ShapeDtypeStruct((n, D_MODEL), F32),
        compiler_params=_params("parallel"),
        name=name,
    )(x, gate, o_a, o_b, o_c, wb, wo)


def _peer_stats_body(tm, x_ref, nw_ref, wq_ref, sk_ref, st_ref, stat_ref, top_ref):
    h = _rms(x_ref[...], nw_ref[...]).astype(BF16)
    q = _dot(h, wq_ref[...]).astype(BF16)
    n_hc = 2 * PEER_HEADS
    for hc in range(n_hc):
        st_ref[hc] = _dot_nt(sk_ref[hc], q[:, hc * LANES:(hc + 1) * LANES])

    def top_body(hc, carry):
        for hf in range(tm // LANES):
            s = st_ref[hc, :, hf * LANES:(hf + 1) * LANES]
            vals = []
            for _ in range(PEER_TOPK):
                m = jnp.max(s, axis=0, keepdims=True)
                vals.append(m)
                s = jnp.where(s == m, -jnp.inf, s)
            top_ref[hc, :, hf * LANES:(hf + 1) * LANES] = jnp.concatenate(vals, axis=0)
        return carry

    lax.fori_loop(0, n_hc, top_body, 0)

    row16 = lax.broadcasted_iota(jnp.int32, (PEER_TOPK, LANES), 0)

    def head_body(hd, carry):
        for hf in range(tm // LANES):
            sl = slice(hf * LANES, (hf + 1) * LANES)
            a = top_ref[2 * hd, :, sl]
            b = top_ref[2 * hd + 1, :, sl]
            parts = []
            for i in range(PEER_TOPK):
                lim = PEER_TOPK // (i + 1)
                rows = PEER_TOPK if lim > 8 else 8
                cs = a[i:i + 1] + b[:rows]
                parts.append(jnp.where(row16[:rows] < lim, cs, -jnp.inf))
            cand = jnp.concatenate(parts, axis=0)
            mx = a[0:1] + b[0:1]
            cum = jnp.zeros((1, LANES), F32)
            tau = mx
            work = cand
            for _ in range(PEER_TOPK):
                m = jnp.max(work, axis=0, keepdims=True)
                eq = work == m
                tau = jnp.where(cum < float(PEER_TOPK), m, tau)
                cum = cum + jnp.sum(jnp.where(eq, 1.0, 0.0), axis=0, keepdims=True)
                work = jnp.where(eq, -jnp.inf, work)
            z = jnp.sum(jnp.where(cand >= tau, jnp.exp(cand - mx), 0.0), axis=0, keepdims=True)
            stat_ref[hd, :, sl] = tau
            stat_ref[PEER_HEADS + hd, :, sl] = a[0:1]
            stat_ref[2 * PEER_HEADS + hd, :, sl] = b[0:1]
            stat_ref[3 * PEER_HEADS + hd, :, sl] = 1.0 / z
        return carry

    lax.fori_loop(0, PEER_HEADS, head_body, 0)


def _peer_stats(x, nw, wq, sk, name):
    n = x.shape[0]
    tm = min(ROW_TILE, n)
    n_hc = 2 * PEER_HEADS
    return pl.pallas_call(
        functools.partial(_peer_stats_body, tm),
        grid=(n // tm,),
        in_specs=[pl.BlockSpec((tm, D_MODEL), lambda i: (i, 0)),
                  pl.BlockSpec((1, D_MODEL), lambda i: (0, 0)),
                  pl.BlockSpec((D_MODEL, n_hc * LANES), lambda i: (0, 0)),
                  pl.BlockSpec((n_hc, LANES, LANES), lambda i: (0, 0, 0))],
        out_specs=[pl.BlockSpec((n_hc, LANES, tm), lambda i: (0, 0, i)),
                   pl.BlockSpec((4 * PEER_HEADS, 1, tm), lambda i: (0, 0, i))],
        out_shape=[jax.ShapeDtypeStruct((n_hc, LANES, n), F32),
                   jax.ShapeDtypeStruct((4 * PEER_HEADS, 1, n), F32)],
        scratch_shapes=[pltpu.VMEM((n_hc, PEER_TOPK, tm), F32)],
        compiler_params=_params("parallel"),
        name=name,
    )(x, nw, wq, sk)


def _peer_dense_body(tm, te, x_ref, nw_ref, st_ref, stat_ref, u_ref, vt_ref, o_ref,
                     h_ref, e1_ref, e2_ref, acc_ref):
    e = pl.program_id(1)

    @pl.when(e == 0)
    def _():
        h_ref[...] = _rms(x_ref[...], nw_ref[...]).astype(BF16)
        acc_ref[...] = jnp.zeros(acc_ref.shape, F32)
        for hd in range(PEER_HEADS):
            m1 = stat_ref[PEER_HEADS + hd]
            m2 = stat_ref[2 * PEER_HEADS + hd]
            inv_z = stat_ref[3 * PEER_HEADS + hd]
            e1_ref[hd] = jnp.exp(st_ref[2 * hd] - m1) * inv_z
            e2_ref[hd] = jnp.exp(st_ref[2 * hd + 1] - m2)

    act = jax.nn.gelu(_dot_nt(u_ref[...], h_ref[...]))
    n_i = te // PEER_NKEYS
    parts = []
    for ii in range(n_i):
        i = e * n_i + ii
        gmat = jnp.zeros((PEER_NKEYS, tm), F32)
        for hd in range(PEER_HEADS):
            s1 = st_ref[2 * hd, pl.ds(i, 1), :]
            w1 = e1_ref[hd, pl.ds(i, 1), :]
            tau = stat_ref[hd]
            keep = (st_ref[2 * hd + 1] + s1) >= tau
            gmat = gmat + jnp.where(keep, e2_ref[hd] * w1, 0.0)
        parts.append((gmat * act[ii * PEER_NKEYS:(ii + 1) * PEER_NKEYS]).astype(BF16))
    acc_ref[...] += _dot(vt_ref[...], jnp.concatenate(parts, axis=0))

    @pl.when(e == pl.num_programs(1) - 1)
    def _():
        o_ref[...] = x_ref[...] + acc_ref[...].T


def _peer_dense(x, nw, st, stat, u_bf, vt_bf, name):
    n = x.shape[0]
    tm = min(ROW_TILE, n)
    te = EXPERT_TILE
    n_exp = u_bf.shape[0]
    n_hc = 2 * PEER_HEADS
    return pl.pallas_call(
        functools.partial(_peer_dense_body, tm, te),
        grid=(n // tm, n_exp // te),
        in_specs=[pl.BlockSpec((tm, D_MODEL), lambda i, e: (i, 0)),
                  pl.BlockSpec((1, D_MODEL), lambda i, e: (0, 0)),
                  pl.BlockSpec((n_hc, LANES, tm), lambda i, e: (0, 0, i)),
                  pl.BlockSpec((4 * PEER_HEADS, 1, tm), lambda i, e: (0, 0, i)),
                  pl.BlockSpec((te, D_MODEL), lambda i, e: (e, 0)),
                  pl.BlockSpec((D_MODEL, te), lambda i, e: (0, e))],
        out_specs=pl.BlockSpec((tm, D_MODEL), lambda i, e: (i, 0)),
        out_shape=jax.ShapeDtypeStruct((n, D_MODEL), F32),
        scratch_shapes=[pltpu.VMEM((tm, D_MODEL), BF16),
                        pltpu.VMEM((PEER_HEADS, PEER_NKEYS, tm), F32),
                        pltpu.VMEM((PEER_HEADS, PEER_NKEYS, tm), F32),
                        pltpu.VMEM((D_MODEL, tm), F32)],
        compiler_params=_params("parallel", "arbitrary"),
        name=name,
    )(x, nw, st, stat, u_bf, vt_bf)


def _norm_body(x_ref, w_ref, o_ref):
    o_ref[...] = _rms(x_ref[...], w_ref[...])


def _final_norm(x, w, name):
    n = x.shape[0]
    tm = min(ROW_TILE, n)
    return pl.pallas_call(
        _norm_body,
        grid=(n // tm,),
        in_specs=[pl.BlockSpec((tm, D_MODEL), lambda i: (i, 0)), pl.BlockSpec((1, D_MODEL), lambda i: (0, 0))],
        out_specs=pl.BlockSpec((tm, D_MODEL), lambda i: (i, 0)),
        out_shape=jax.ShapeDtypeStruct((n, D_MODEL), F32),
        compiler_params=_params("parallel"),
        name=name,
    )(x, w)


SROWS = 8


def _page_specs(n_pages, rows, layer):
    def spec(p):
        return pl.BlockSpec((None, None, rows, LANES), lambda b, pt: (layer, pt[b, p], 0, 0))
    return [spec(p) for p in range(n_pages)]


def _row_spec(width):
    return pl.BlockSpec((None, SROWS, width), lambda b, pt: (b, 0, 0))


def _dup(x):
    return jnp.concatenate([x, x], axis=0)


def _pad_rows(x):
    return jnp.concatenate([x, jnp.zeros((PAGE_SIZE - x.shape[0], x.shape[1]), F32)], axis=0)


def _new_mask(m_rows, n_new):
    lane = lax.broadcasted_iota(jnp.int32, (1, LANES), 1)
    trow = lax.broadcasted_iota(jnp.int32, (m_rows, 1), 0) & (SROWS - 1)
    return jnp.where(lane < n_new, jnp.where(lane <= trow, 1.0, 0.0), 0.0) > 0.5


def _softmax_pieces(pieces):
    mx = None
    for s, mk in pieces:
        sm = s if mk is None else jnp.where(mk, s, NEG)
        mx = sm if mx is None else jnp.maximum(mx, sm)
    m = jnp.max(mx, axis=1, keepdims=True)
    ps = []
    tot = None
    for s, mk in pieces:
        p = jnp.exp(s - m)
        if mk is not None:
            p = jnp.where(mk, p, 0.0)
        tot = p if tot is None else tot + p
        ps.append(p.astype(BF16))
    return ps, jnp.sum(tot, axis=1, keepdims=True)


def _scores_kt(q, kts):
    return [_dot(q, _dup(kt.astype(BF16))) for kt in kts]


def _values_kt(ps, vts):
    o = None
    for p, vt in zip(ps, vts):
        t = _dot_nt(p, _dup(vt.astype(BF16)))
        o = t if o is None else o + t
    return o


def _diff_s_body(n_pages, n_new, lam_init, pt_ref, lam_ref, subln_ref, q_ref, kn_ref, vn_ref, *rest):
    k_refs = rest[:n_pages]
    v_refs = rest[n_pages:2 * n_pages]
    o_ref = rest[2 * n_pages]
    lane = lax.broadcasted_iota(jnp.int32, (1, LANES), 1)
    upper = lane >= HD
    new_mask = _new_mask(4 * SROWS, n_new)
    lp = lam_ref[...]
    lam = (jnp.exp(jnp.sum(lp[0:1] * lp[1:2], axis=1, keepdims=True))
           - jnp.exp(jnp.sum(lp[2:3] * lp[3:4], axis=1, keepdims=True)) + lam_init)
    for g in range(A_KV):
        parts = []
        for r in range(A_HEADS // A_KV):
            qb = q_ref[:, (2 * g + r) * LANES:(2 * g + r + 1) * LANES] * (HD ** -0.5)
            parts += [jnp.where(upper, 0.0, qb), jnp.where(upper, qb, 0.0)]
        q = jnp.concatenate(parts, axis=0).astype(BF16)
        pieces = [(_dot_nt(q, k_refs[p][pl.ds(g, PAGE_SIZE, stride=A_KV), :].astype(BF16)), None)
                  for p in range(n_pages)]
        knew = _pad_rows(kn_ref[:, g * LANES:(g + 1) * LANES]).astype(BF16)
        vnew = _pad_rows(vn_ref[:, g * LANES:(g + 1) * LANES]).astype(BF16)
        pieces.append((_dot_nt(q, knew), new_mask))
        ps, tot = _softmax_pieces(pieces)
        o = _dot(ps[n_pages], vnew)
        for p in range(n_pages):
            o = o + _dot(ps[p], v_refs[p][pl.ds(g, PAGE_SIZE, stride=A_KV), :].astype(BF16))
        o = o / tot
        for r in range(A_HEADS // A_KV):
            d = o[2 * r * SROWS:(2 * r + 1) * SROWS] - lam * o[(2 * r + 1) * SROWS:(2 * r + 2) * SROWS]
            o_ref[:, (2 * g + r) * LANES:(2 * g + r + 1) * LANES] = _rms(d, subln_ref[...]) * (1.0 - lam_init)


def _diff_sample(page_table, layer, n_new, aq, ak_new, av_new, cache_k, cache_v, lam_p, subln, lam_init, name):
    db, n_pages = page_table.shape
    const = lambda shape: pl.BlockSpec(shape, lambda b, pt: (0,) * len(shape))
    in_specs = ([const((4, HD)), const((1, 2 * HD)), _row_spec(512), _row_spec(256), _row_spec(256)]
                + _page_specs(n_pages, PAGE_SIZE * A_KV, layer) + _page_specs(n_pages, PAGE_SIZE * A_KV, layer))
    grid_spec = pltpu.PrefetchScalarGridSpec(num_scalar_prefetch=1, grid=(db,), in_specs=in_specs,
                                             out_specs=_row_spec(512))
    return pl.pallas_call(
        functools.partial(_diff_s_body, n_pages, n_new, lam_init),
        grid_spec=grid_spec,
        out_shape=jax.ShapeDtypeStruct((db, SROWS, 512), F32),
        compiler_params=_params("parallel"),
        name=name,
    )(page_table, lam_p, subln, aq, ak_new, av_new, *([cache_k] * n_pages), *([cache_v] * n_pages))


def _moba_s_body(n_pages, n_new, pt_ref, q_ref, kn_ref, vn_ref, *rest):
    k_refs = rest[:n_pages]
    v_refs = rest[n_pages:2 * n_pages]
    o_ref = rest[2 * n_pages]
    lane = lax.broadcasted_iota(jnp.int32, (1, LANES), 1)
    halfi = lane >> 6
    rep = C_HEADS // C_KV
    new_mask = _new_mask(rep * SROWS, n_new)
    knt = _pad_rows(kn_ref[...]).T
    vnt = _pad_rows(vn_ref[...]).T
    ppb = MOBA_BLOCK // PAGE_SIZE
    n_blocks = n_pages // ppb
    for g in range(C_KV):
        rows = slice(g * HD, (g + 1) * HD)
        qb = q_ref[:, g * LANES:(g + 1) * LANES] * (HD ** -0.5)
        q = jnp.concatenate([jnp.where(halfi == r, qb, 0.0) for r in range(rep)], axis=0).astype(BF16)
        ss = _scores_kt(q, [k_refs[p][rows, :] for p in range(n_pages)] + [knt[rows, :]])
        bs = []
        for j in range(n_blocks):
            acc = ss[j * ppb]
            for u in range(1, ppb):
                acc = acc + ss[j * ppb + u]
            bs.append(jnp.sum(acc, axis=1, keepdims=True))
        pieces = []
        for j in range(n_blocks):
            rank = jnp.zeros(bs[j].shape, F32)
            for i in range(n_blocks):
                if i < j:
                    rank = rank + jnp.where(bs[i] >= bs[j], 1.0, 0.0)
                elif i > j:
                    rank = rank + jnp.where(bs[i] > bs[j], 1.0, 0.0)
            keep = rank < float(MOBA_TOPK)
            for u in range(ppb):
                pieces.append((ss[j * ppb + u], keep))
        pieces.append((ss[n_pages], new_mask))
        ps, tot = _softmax_pieces(pieces)
        o = _values_kt(ps, [v_refs[p][rows, :] for p in range(n_pages)] + [vnt[rows, :]]) / tot
        o_ref[:, g * LANES:(g + 1) * LANES] = jnp.where(halfi == 0, o[0:SROWS], o[SROWS:2 * SROWS])


def _moba_sample(page_table, layer, n_new, cq, ck_new, cv_new, cache_k, cache_v, name):
    db, n_pages = page_table.shape
    in_specs = ([_row_spec(512), _row_spec(256), _row_spec(256)]
                + _page_specs(n_pages, C_KV * HD, layer) + _page_specs(n_pages, C_KV * HD, layer))
    grid_spec = pltpu.PrefetchScalarGridSpec(num_scalar_prefetch=1, grid=(db,), in_specs=in_specs,
                                             out_specs=_row_spec(512))
    return pl.pallas_call(
        functools.partial(_moba_s_body, n_pages, n_new),
        grid_spec=grid_spec,
        out_shape=jax.ShapeDtypeStruct((db, SROWS, 512), F32),
        compiler_params=_params("parallel"),
        name=name,
    )(page_table, cq, ck_new, cv_new, *([cache_k] * n_pages), *([cache_v] * n_pages))


def _compress_s_body(n_pages, pt_ref, pek_ref, pev_ref, w1k_ref, w1v_ref, w2k_ref, w2v_ref, *rest):
    k_refs = rest[:n_pages]
    v_refs = rest[n_pages:2 * n_pages]
    ok_ref, ov_ref, xk_ref, xv_ref = rest[2 * n_pages:]
    for p in range(n_pages):
        xk_ref[p * PAGE_SIZE:(p + 1) * PAGE_SIZE, :] = k_refs[p][...].T
        xv_ref[p * PAGE_SIZE:(p + 1) * PAGE_SIZE, :] = v_refs[p][...].T
    _compress_one(xk_ref, pek_ref, w1k_ref, w2k_ref, ok_ref)
    _compress_one(xv_ref, pev_ref, w1v_ref, w2v_ref, ov_ref)


def _compress_sample(page_table, layer, cache_k, cache_v, cw, name):
    db, n_pages = page_table.shape
    const = lambda shape: pl.BlockSpec(shape, lambda b, pt: (0,) * len(shape))
    in_specs = ([const((CMP_BLOCK, 1, LANES))] * 2 + [const((CMP_BLOCK, LANES, 2 * CMP_HIDDEN))] * 2
                + [const((2 * CMP_HIDDEN, LANES))] * 2
                + _page_specs(n_pages, B_KV * HD, layer) + _page_specs(n_pages, B_KV * HD, layer))
    o_spec = pl.BlockSpec((None, LANES, LANES), lambda b, pt: (b, 0, 0))
    grid_spec = pltpu.PrefetchScalarGridSpec(
        num_scalar_prefetch=1, grid=(db,), in_specs=in_specs, out_specs=[o_spec, o_spec],
        scratch_shapes=[pltpu.VMEM((n_pages * PAGE_SIZE, LANES), F32)] * 2)
    return pl.pallas_call(
        functools.partial(_compress_s_body, n_pages),
        grid_spec=grid_spec,
        out_shape=[jax.ShapeDtypeStruct((db, LANES, LANES), F32)] * 2,
        compiler_params=_params("parallel"),
        name=name,
    )(page_table, cw["pek"], cw["pev"], cw["w1k"], cw["w1v"], cw["w2k"], cw["w2v"],
      *([cache_k] * n_pages), *([cache_v] * n_pages))


def _nsa_s_body(n_pages, n_new, q_pos0, n_win, pt_ref, qc_ref, qr_ref, bg_ref, kcmp_ref, vcmp_ref,
                ksn_ref, vsn_ref, kwn_ref, vwn_ref, wk_ref, wv_ref, *rest):
    k_refs = rest[:n_pages]
    v_refs = rest[n_pages:2 * n_pages]
    o_ref = rest[2 * n_pages]
    lane = lax.broadcasted_iota(jnp.int32, (1, LANES), 1)
    halfi = lane >> 6
    rep = B_HEADS // B_KV
    m_rows = rep * SROWS
    new_mask = _new_mask(m_rows, n_new)
    trow8 = lax.broadcasted_iota(jnp.int32, (SROWS, 1), 0)
    trow = lax.broadcasted_iota(jnp.int32, (m_rows, 1), 0) & (SROWS - 1)
    qpos8 = q_pos0 + trow8
    ns = -(-(q_pos0 + n_new) // SEL_BLOCK)
    bpp = PAGE_SIZE // SEL_BLOCK
    ksnt = _pad_rows(ksn_ref[...]).T
    vsnt = _pad_rows(vsn_ref[...]).T
    kwnt = _pad_rows(kwn_ref[...]).T
    vwnt = _pad_rows(vwn_ref[...]).T
    kcmp = kcmp_ref[...].astype(BF16)
    vcmp = vcmp_ref[...].astype(BF16)
    ci = lax.broadcasted_iota(jnp.int32, (LANES, LANES), 0)
    cj = lax.broadcasted_iota(jnp.int32, (LANES, LANES), 1)
    onehot = jnp.where((ci >> CMP_PER_SEL_SHIFT) == cj, 1.0, 0.0).astype(BF16)
    sig = jax.nn.sigmoid(bg_ref[...])
    tile_rows = lambda x: jnp.concatenate([x] * rep, axis=0)

    for g in range(B_KV):
        rows = slice(g * HD, (g + 1) * HD)
        own_c, own_r = [], []
        for r in range(rep):
            blk = slice((g * rep + r) // 2 * LANES, ((g * rep + r) // 2 + 1) * LANES)
            own_c.append(jnp.where(halfi == (r % 2), qc_ref[:, blk], 0.0) * (HD ** -0.5))
            own_r.append(jnp.where(halfi == (r % 2), qr_ref[:, blk], 0.0) * (HD ** -0.5))
        qc = jnp.concatenate(own_c, axis=0)
        qc = qc + pltpu.roll(qc, HD, 1)
        qc = jnp.where(halfi == g, qc, 0.0).astype(BF16)
        qr = jnp.concatenate(own_r, axis=0).astype(BF16)

        cmask = tile_rows(jnp.where((lane * CMP_STRIDE + (CMP_BLOCK - 1)) <= qpos8, 1.0, 0.0)) > 0.5
        s = jnp.where(cmask, _dot_nt(qc, kcmp), NEG)
        m = jnp.max(s, axis=1, keepdims=True)
        p = jnp.where(cmask, jnp.exp(s - m), 0.0)
        p = p / jnp.maximum(jnp.sum(p, axis=1, keepdims=True), 1e-30)
        o_cmp = jnp.where(halfi == g, _dot(p.astype(BF16), vcmp), 0.0)
        o_cmp = o_cmp + pltpu.roll(o_cmp, HD, 1)
        psum = p[0:SROWS]
        for r in range(1, rep):
            psum = psum + p[r * SROWS:(r + 1) * SROWS]
        hi, mid, lo = _split3(psum)
        imp = _dot(hi, onehot) + _dot(mid, onehot) + _dot(lo, onehot)
        cur = qpos8 >> 6
        impm = jnp.where(lane == cur, jnp.inf, jnp.where(lane < cur, imp, -jnp.inf))
        rank = _rank_desc(impm, ns, lane)
        sel = jnp.where(rank < float(SEL_TOPK), jnp.where(lane <= cur, 1.0, 0.0), 0.0)

        ss = _scores_kt(qr, [k_refs[pg][rows, :] for pg in range(n_pages)] + [ksnt[rows, :]])
        pieces = []
        for pg in range(n_pages):
            mk = sel[:, pg * bpp:pg * bpp + 1]
            for u in range(1, bpp):
                mk = jnp.where(halfi >= u, sel[:, pg * bpp + u:pg * bpp + u + 1], mk)
            pieces.append((ss[pg], tile_rows(mk) > 0.5))
        pieces.append((ss[n_pages], new_mask))
        ps, tot = _softmax_pieces(pieces)
        o_slc = _values_kt(ps, [v_refs[pg][rows, :] for pg in range(n_pages)] + [vsnt[rows, :]]) / tot

        wcols = [slice(u * LANES, (u + 1) * LANES) for u in range(n_win // LANES)]
        ss = _scores_kt(qr, [wk_ref[rows, c] for c in wcols] + [kwnt[rows, :]])
        pieces = []
        for u in range(len(wcols)):
            dist = (n_win - u * LANES) + trow - lane
            pieces.append((ss[u], dist <= WINDOW))
        pieces.append((ss[len(wcols)], new_mask))
        ps, tot = _softmax_pieces(pieces)
        o_win = _values_kt(ps, [wv_ref[rows, c] for c in wcols] + [vwnt[rows, :]]) / tot

        outs = []
        for r in range(rep):
            rs = slice(r * SROWS, (r + 1) * SROWS)
            base = (g * rep + r) * 3
            outs.append(sig[:, base:base + 1] * o_cmp[rs] + sig[:, base + 1:base + 2] * o_slc[rs]
                        + sig[:, base + 2:base + 3] * o_win[rs])
        for pr in range(rep // 2):
            col = (g * rep // 2 + pr) * LANES
            o_ref[:, col:col + LANES] = jnp.where(halfi == 0, outs[2 * pr], outs[2 * pr + 1])


def _nsa_sample(page_table, layer, n_new, q_pos0, bqc, bqr, bg, kcmp, vcmp, ks_new, vs_new, kw_new, vw_new,
                state_k, state_v, cache_k, cache_v, name):
    db, n_pages = page_table.shape
    n_win = state_k.shape[-1]
    w_spec = pl.BlockSpec((None, None, B_KV * HD, n_win), lambda b, pt: (layer, b, 0, 0))
    c_spec = pl.BlockSpec((None, LANES, LANES), lambda b, pt: (b, 0, 0))
    in_specs = ([_row_spec(512), _row_spec(512), _row_spec(LANES), c_spec, c_spec]
                + [_row_spec(LANES)] * 4 + [w_spec, w_spec]
                + _page_specs(n_pages, B_KV * HD, layer) + _page_specs(n_pages, B_KV * HD, layer))
    grid_spec = pltpu.PrefetchScalarGridSpec(num_scalar_prefetch=1, grid=(db,), in_specs=in_specs,
                                             out_specs=_row_spec(512))
    return pl.pallas_call(
        functools.partial(_nsa_s_body, n_pages, n_new, q_pos0, n_win),
        grid_spec=grid_spec,
        out_shape=jax.ShapeDtypeStruct((db, SROWS, 512), F32),
        compiler_params=_params("parallel"),
        name=name,
    )(page_table, bqc, bqr, bg, kcmp, vcmp, ks_new, vs_new, kw_new, vw_new, state_k, state_v,
      *([cache_k] * n_pages), *([cache_v] * n_pages))


def _layer_weights(l, norm_mix, norm_ffn, w_in, a_lambda, a_subln, b_cmp_pe, b_cmp_w1, b_cmp_w2,
                   w_branch, w_out, peer_wq, peer_subkeys, peer_u, peer_v):
    w = w_in[l]
    bg_w = jnp.pad(w[:, 2304:2328], ((0, 0), (0, LANES - 3 * B_HEADS)))
    return {
        "norm_mix": norm_mix[l][None, :],
        "norm_ffn": norm_ffn[l][None, :],
        "w_a": w[:, 0:1024].astype(BF16),
        "w_b": jnp.concatenate([w[:, 1024:2304], bg_w], axis=1).astype(BF16),
        "w_c": w[:, 2328:3352].astype(BF16),
        "w_g": w[:, 3352:6424].astype(BF16),
        "lam": a_lambda[l],
        "subln": a_subln[l][None, :],
        "cmp": _compress_weights(b_cmp_pe[l], b_cmp_w1[l], b_cmp_w2[l]),
        "w_branch": w_branch[l].astype(BF16),
        "w_out": w_out[l].astype(BF16),
        "wq": peer_wq[l].astype(BF16),
        "sk": peer_subkeys[l].reshape(2 * PEER_HEADS, PEER_NKEYS, LANES).astype(BF16),
        "u": peer_u[l].astype(BF16),
        "vt": peer_v[l].T.astype(BF16),
        "lam_init": 0.8 - 0.6 * math.exp(-0.3 * l),
    }


def _project_all(x, lw, cos_t, sin_t, tag):
    aq, ak, av = _project(x, lw["norm_mix"], lw["w_a"], cos_t, sin_t, PIECES_A, WIDTHS_A, "proj_a_" + tag)
    b_out = _project(x, lw["norm_mix"], lw["w_b"], cos_t, sin_t, PIECES_B, WIDTHS_B, "proj_b_" + tag)
    cq, ck, cv = _project(x, lw["norm_mix"], lw["w_c"], cos_t, sin_t, PIECES_C, WIDTHS_C, "proj_c_" + tag)
    (gate,) = _project(x, lw["norm_mix"], lw["w_g"], cos_t, sin_t, PIECES_G, WIDTHS_G, "proj_g_" + tag)
    return (aq, ak, av), b_out, (cq, ck, cv), gate


def _ffn(x, lw, tag):
    st, stat = _peer_stats(x, lw["norm_ffn"], lw["wq"], lw["sk"], "peer_stats_" + tag)
    return _peer_dense(x, lw["norm_ffn"], st, stat, lw["u"], lw["vt"], "peer_dense_" + tag)


def _prompt_layer(x, lw, cos_t, sin_t, bn, t):
    (aq, ak, av), (bqc, bqr, bkc, bvc, bks, bvs, bkw, bvw, bg), (cq, ck, cv), gate = _project_all(
        x, lw, cos_t, sin_t, "p")
    r3 = lambda a: a.reshape(bn, t, a.shape[-1])
    o_a = _diff_attention(r3(aq), r3(ak), r3(av), lw["lam"], lw["subln"], lw["lam_init"], 0, "diff_p")
    kcmp, vcmp = _compress(r3(bkc), r3(bvc), lw["cmp"], "cmp_p")
    o_b = _nsa_attention(r3(bqc), r3(bqr), r3(bg), kcmp, vcmp, r3(bks), r3(bvs), r3(bkw), r3(bvw),
                         0, t, 0, t, "nsa_p")
    o_c = _moba_attention(r3(cq), r3(ck), r3(cv), 0, "moba_p")
    n = bn * t
    x = _merge(x, gate, o_a.reshape(n, 512), o_b.reshape(n, 512), o_c.reshape(n, 512),
               lw["w_branch"], lw["w_out"], "merge_p")
    x = _ffn(x, lw, "p")
    wp = min(WINDOW, t)
    rows = (r3(ak).reshape(bn, t, A_KV, 2 * HD), r3(av).reshape(bn, t, A_KV, 2 * HD),
            r3(bkc).reshape(bn, t, B_KV, HD), r3(bvc).reshape(bn, t, B_KV, HD),
            r3(bks).reshape(bn, t, B_KV, HD), r3(bvs).reshape(bn, t, B_KV, HD),
            r3(ck).reshape(bn, t, C_KV, HD), r3(cv).reshape(bn, t, C_KV, HD),
            r3(bkw)[:, t - wp:].reshape(bn, wp, B_KV, HD), r3(bvw)[:, t - wp:].reshape(bn, wp, B_KV, HD))
    return x, rows


def _sample_layer(x, lw, l, cos_t, sin_t, db, t, past, caches, state_wk, state_wv, page_table):
    (aq, ak, av), (bqc, bqr, bkc, bvc, bks, bvs, bkw, bvw, bg), (cq, ck, cv), gate = _project_all(
        x, lw, cos_t, sin_t, "s")
    r3 = lambda a: a.reshape(db, SROWS, a.shape[-1])
    c_ak, c_av, c_bkc, c_bvc, c_bks, c_bvs, c_ck, c_cv = caches
    o_a = _diff_sample(page_table, l, t, r3(aq), r3(ak), r3(av), c_ak, c_av, lw["lam"], lw["subln"],
                       lw["lam_init"], "diff_s")
    kcmp, vcmp = _compress_sample(page_table, l, c_bkc, c_bvc, lw["cmp"], "cmp_s")
    o_b = _nsa_sample(page_table, l, t, past, r3(bqc), r3(bqr), r3(bg), kcmp, vcmp, r3(bks), r3(bvs),
                      r3(bkw), r3(bvw), state_wk, state_wv, c_bks, c_bvs, "nsa_s")
    o_c = _moba_sample(page_table, l, t, r3(cq), r3(ck), r3(cv), c_ck, c_cv, "moba_s")
    n = db * SROWS
    x = _merge(x, gate, o_a.reshape(n, 512), o_b.reshape(n, 512), o_c.reshape(n, 512),
               lw["w_branch"], lw["w_out"], "merge_s")
    x = _ffn(x, lw, "s")
    new = lambda a, kv, w: r3(a)[:, :t].reshape(db, t, kv, w)

    def rolled(state_t, a):
        new_t = jnp.transpose(r3(a)[:, :t], (0, 2, 1))
        out_t = jnp.concatenate([state_t[l][:, :, t:], new_t], axis=2)
        return jnp.transpose(out_t, (0, 2, 1)).reshape(db, out_t.shape[2], B_KV, HD)

    rows = (new(ak, A_KV, 2 * HD), new(av, A_KV, 2 * HD), new(bkc, B_KV, HD), new(bvc, B_KV, HD),
            new(bks, B_KV, HD), new(bvs, B_KV, HD), new(ck, C_KV, HD), new(cv, C_KV, HD),
            rolled(state_wk, bkw), rolled(state_wv, bvw))
    return x, rows


def kernel(x_prompt, x_sample, cache_a_k, cache_a_v, cache_b_kc, cache_b_vc, cache_b_ks, cache_b_vs,
           cache_c_k, cache_c_v, state_b_wk, state_b_wv, page_table, norm_mix, norm_ffn, norm_final,
           w_in, a_lambda, a_subln, b_cmp_pe, b_cmp_w1, b_cmp_w2, w_branch, w_out,
           peer_wq, peer_subkeys, peer_u, peer_v):
    bn, t, _ = x_prompt.shape
    db, ts, _ = x_sample.shape
    depth = w_in.shape[0]
    past = page_table.shape[1] * PAGE_SIZE
    wb = state_b_wk.shape[2]
    assert past % MOBA_BLOCK == 0 and ts <= SROWS and wb == WINDOW and wb % LANES == 0
    rows_view = lambda c: c.reshape(c.shape[0], c.shape[1], c.shape[2] * c.shape[3], c.shape[4])
    lanes_view = lambda c: jnp.transpose(c, (0, 1, 3, 4, 2)).reshape(
        c.shape[0], c.shape[1], c.shape[3] * c.shape[4], c.shape[2])
    caches = ([rows_view(c) for c in (cache_a_k, cache_a_v)]
              + [lanes_view(c) for c in (cache_b_kc, cache_b_vc, cache_b_ks, cache_b_vs, cache_c_k, cache_c_v)])
    swk = lanes_view(state_b_wk)
    swv = lanes_view(state_b_wv)

    cos_p, sin_p = _rope_tables(jnp.arange(t, dtype=jnp.int32))
    tm_s = min(ROW_TILE, db * SROWS)
    cos_s, sin_s = _rope_tables(past + (jnp.arange(tm_s, dtype=jnp.int32) % SROWS))

    xp = x_prompt.reshape(bn * t, D_MODEL)
    xs = jnp.pad(x_sample, ((0, 0), (0, SROWS - ts), (0, 0))).reshape(db * SROWS, D_MODEL)
    rows_p, rows_s = [], []
    for l in range(depth):
        lw = _layer_weights(l, norm_mix, norm_ffn, w_in, a_lambda, a_subln, b_cmp_pe, b_cmp_w1, b_cmp_w2,
                            w_branch, w_out, peer_wq, peer_subkeys, peer_u, peer_v)
        xp, rp = _prompt_layer(xp, lw, cos_p, sin_p, bn, t)
        xs, rs = _sample_layer(xs, lw, l, cos_s, sin_s, db, ts, past, caches, swk, swv, page_table)
        rows_p.append(rp)
        rows_s.append(rs)
    y_prompt = _final_norm(xp, norm_final[None, :], "final_p").reshape(bn, t, D_MODEL)
    y_sample = _final_norm(xs, norm_final[None, :], "final_s").reshape(db, SROWS, D_MODEL)[:, :ts]
    outs_p = [jnp.stack(r, axis=0) for r in zip(*rows_p)]
    outs_s = [jnp.stack(r, axis=0) for r in zip(*rows_s)]
    return (y_prompt, y_sample, *outs_p, *outs_s)
```

```python
import functools
import math

import jax
import jax.numpy as jnp
import numpy as np
from jax import lax
from jax.experimental import pallas as pl
from jax.experimental.pallas import tpu as pltpu

F32 = jnp.float32
BF16 = jnp.bfloat16

D_MODEL = 1024
HD = 64
A_HEADS, A_KV = 4, 2
B_HEADS, B_KV = 8, 2
C_HEADS, C_KV = 8, 4
CMP_BLOCK, CMP_STRIDE, CMP_HIDDEN = 32, 16, 128
SEL_BLOCK, SEL_TOPK = 64, 8
CMP_PER_SEL_SHIFT = 2
WINDOW = 512
MOBA_BLOCK, MOBA_TOPK = 256, 3
PEER_HEADS, PEER_NKEYS, PEER_TOPK = 8, 128, 16
ROPE_THETA = 10000.0
EPS = 1e-6
PAGE_SIZE = 128

LANES = 128
NEG = -1e30
VMEM_LIMIT = 48 * 1024 * 1024
ROW_TILE = 256
KV_CHUNK = 256
EXPERT_TILE = 512


def _params(*sem):
    return pltpu.CompilerParams(dimension_semantics=sem, vmem_limit_bytes=VMEM_LIMIT)


def _dot(a, b):
    return jnp.dot(a, b, preferred_element_type=F32)


def _dot_nt(a, b):
    return lax.dot_general(a, b, (((1,), (1,)), ((), ())), preferred_element_type=F32)


def _split3(a):
    hi = a.astype(BF16)
    r1 = a - hi.astype(F32)
    mid = r1.astype(BF16)
    lo = (r1 - mid.astype(F32)).astype(BF16)
    return hi, mid, lo


def _rms(x, w):
    return x * lax.rsqrt(jnp.mean(x * x, axis=-1, keepdims=True) + EPS) * w


def _proj_body(pieces, chunk, x_ref, nw_ref, w_ref, cos_ref, sin_ref, *outs):
    h = _rms(x_ref[...], nw_ref[...]).astype(BF16)
    cos = cos_ref[...]
    sin = sin_ref[...]
    lane = lax.broadcasted_iota(jnp.int32, (1, LANES), 1)
    first = (lane & (HD - 1)) < (HD // 2)
    n_cols = len(pieces) * LANES
    for c0 in range(0, n_cols, chunk):
        w = min(chunk, n_cols - c0)
        z = _dot(h, w_ref[:, c0:c0 + w])
        for p in range(w // LANES):
            zp = z[:, p * LANES:(p + 1) * LANES]
            for (oi, oc, rope) in pieces[c0 // LANES + p]:
                if rope:
                    rot = jnp.where(first, pltpu.roll(zp, LANES - HD // 2, 1), pltpu.roll(zp, HD // 2, 1))
                    outs[oi][:, oc:oc + LANES] = zp * cos + rot * sin
                else:
                    outs[oi][:, oc:oc + LANES] = zp


def _project(x, nw, w, cos_t, sin_t, pieces, out_widths, name):
    n = x.shape[0]
    tm = min(ROW_TILE, n)
    ntab = cos_t.shape[0] // tm
    ncols = w.shape[1]
    body = functools.partial(_proj_body, pieces, 512)
    return pl.pallas_call(
        body,
        grid=(n // tm,),
        in_specs=[
            pl.BlockSpec((tm, D_MODEL), lambda i: (i, 0)),
            pl.BlockSpec((1, D_MODEL), lambda i: (0, 0)),
            pl.BlockSpec((D_MODEL, ncols), lambda i: (0, 0)),
            pl.BlockSpec((tm, LANES), lambda i: (i % ntab, 0)),
            pl.BlockSpec((tm, LANES), lambda i: (i % ntab, 0)),
        ],
        out_specs=[pl.BlockSpec((tm, ow), lambda i: (i, 0)) for ow in out_widths],
        out_shape=[jax.ShapeDtypeStruct((n, ow), F32) for ow in out_widths],
        compiler_params=_params("parallel"),
        name=name,
    )(x, nw, w, cos_t, sin_t)


def _plain(oi, width):
    return [[(oi, c, False)] for c in range(0, width, LANES)]


def _roped(oi, width):
    return [[(oi, c, True)] for c in range(0, width, LANES)]


PIECES_A = _roped(0, 512) + _roped(1, 256) + _plain(2, 256)
WIDTHS_A = (512, 256, 256)
PIECES_B = ([[(0, c, False), (1, c, True)] for c in range(0, 512, LANES)]
            + _plain(2, 128) + _plain(3, 128) + _roped(4, 128) + _plain(5, 128)
            + _roped(6, 128) + _plain(7, 128) + _plain(8, 128))
WIDTHS_B = (512, 512, 128, 128, 128, 128, 128, 128, 128)
PIECES_C = _roped(0, 512) + _roped(1, 256) + _plain(2, 256)
WIDTHS_C = (512, 256, 256)
PIECES_G = _plain(0, 3072)
WIDTHS_G = (3072,)


def _rope_tables(pos):
    half = HD // 2
    freqs = jnp.power(ROPE_THETA, -jnp.arange(half, dtype=F32) / half)
    ang = pos.astype(F32)[:, None] * freqs[None, :]
    cos = jnp.cos(ang)
    sin = jnp.sin(ang)
    cos_t = jnp.concatenate([cos, cos, cos, cos], axis=1)
    sin_t = jnp.concatenate([-sin, sin, -sin, sin], axis=1)
    return cos_t, sin_t


def _flash_init(m_ref, l_ref, acc_ref):
    m_ref[...] = jnp.full(m_ref.shape, NEG, F32)
    l_ref[...] = jnp.zeros(l_ref.shape, F32)
    acc_ref[...] = jnp.zeros(acc_ref.shape, F32)


def _lane_blocks(x):
    return [x[:, c:c + LANES] for c in range(0, x.shape[1], LANES)]


def _flash_step(hh, q_emb, kc, vc, mask, m_ref, l_ref, acc_ref):
    s = _dot_nt(q_emb, kc)
    if mask is not None:
        s = jnp.where(mask, s, NEG)
    blocks = _lane_blocks(s)
    mx = blocks[0]
    for b in blocks[1:]:
        mx = jnp.maximum(mx, b)
    m_old = m_ref[hh]
    m_new = jnp.maximum(m_old, jnp.max(mx, axis=1, keepdims=True))
    alpha = jnp.exp(m_old - m_new)
    p = jnp.exp(s - jnp.concatenate([m_new] * len(blocks), axis=1))
    if mask is not None:
        p = jnp.where(mask, p, 0.0)
    pblocks = _lane_blocks(p)
    tot = pblocks[0]
    for b in pblocks[1:]:
        tot = tot + b
    l_ref[hh] = alpha * l_ref[hh] + jnp.sum(tot, axis=1, keepdims=True)
    acc_ref[hh] = alpha * acc_ref[hh] + _dot(p.astype(BF16), vc)
    m_ref[hh] = m_new


def _flash_out(hh, l_ref, acc_ref):
    return acc_ref[hh] / jnp.maximum(l_ref[hh], 1e-30)


def _chunk(ref, j, tk):
    return ref[pl.ds(pl.multiple_of(j * tk, tk), tk), :].astype(BF16)


def _diff_body(tq, tk, q_pos0, lam_init, lam_ref, subln_ref, q_ref, k_ref, v_ref, o_ref,
               m_ref, l_ref, acc_ref):
    qi = pl.program_id(2)
    lane = lax.broadcasted_iota(jnp.int32, (1, LANES), 1)
    upper = lane >= HD
    q0 = q_pos0 + qi * tq
    qpos = q0 + lax.broadcasted_iota(jnp.int32, (tq, 1), 0)
    _flash_init(m_ref, l_ref, acc_ref)
    qs = []
    for r in range(2):
        qb = q_ref[:, r * LANES:(r + 1) * LANES] * (HD ** -0.5)
        qs.append(jnp.where(upper, 0.0, qb).astype(BF16))
        qs.append(jnp.where(upper, qb, 0.0).astype(BF16))
    n_full = (q0 + 1) // tk
    n_chunks = (q0 + tq - 1) // tk + 1

    def body(masked, j, carry):
        kc = _chunk(k_ref, j, tk)
        vc = _chunk(v_ref, j, tk)
        mask = None
        if masked:
            mask = (j * tk + lax.broadcasted_iota(jnp.int32, (1, tk), 1)) <= qpos
        for hh in range(4):
            _flash_step(hh, qs[hh], kc, vc, mask, m_ref, l_ref, acc_ref)
        return carry

    lax.fori_loop(0, n_full, functools.partial(body, False), 0)
    lax.fori_loop(n_full, n_chunks, functools.partial(body, True), 0)
    lp = lam_ref[...]
    lam = (jnp.exp(jnp.sum(lp[0:1] * lp[1:2], axis=1, keepdims=True))
           - jnp.exp(jnp.sum(lp[2:3] * lp[3:4], axis=1, keepdims=True)) + lam_init)
    for r in range(2):
        o = _flash_out(2 * r, l_ref, acc_ref) - lam * _flash_out(2 * r + 1, l_ref, acc_ref)
        o_ref[:, r * LANES:(r + 1) * LANES] = _rms(o, subln_ref[...]) * (1.0 - lam_init)


def _diff_attention(aq, ak, av, lam_p, subln, lam_init, q_pos0, name):
    bn, tq_all, _ = aq.shape
    lk = ak.shape[1]
    tq = min(ROW_TILE, tq_all)
    tk = KV_CHUNK
    body = functools.partial(_diff_body, tq, tk, q_pos0, lam_init)
    return pl.pallas_call(
        body,
        grid=(bn, A_KV, tq_all // tq),
        in_specs=[
            pl.BlockSpec((4, HD), lambda b, g, i: (0, 0)),
            pl.BlockSpec((1, 2 * HD), lambda b, g, i: (0, 0)),
            pl.BlockSpec((None, tq, 256), lambda b, g, i: (b, i, g)),
            pl.BlockSpec((None, lk, LANES), lambda b, g, i: (b, 0, g)),
            pl.BlockSpec((None, lk, LANES), lambda b, g, i: (b, 0, g)),
        ],
        out_specs=pl.BlockSpec((None, tq, 256), lambda b, g, i: (b, i, g)),
        out_shape=jax.ShapeDtypeStruct((bn, tq_all, 512), F32),
        scratch_shapes=[pltpu.VMEM((4, tq, LANES), F32), pltpu.VMEM((4, tq, LANES), F32),
                        pltpu.VMEM((4, tq, LANES), F32)],
        compiler_params=_params("parallel", "parallel", "arbitrary"),
        name=name,
    )(lam_p, subln, aq, ak, av)


def _compress_one(x_ref, pe_ref, w1_ref, w2_ref, o_ref):
    n_half = CMP_STRIDE
    ha = jnp.zeros((LANES, 2 * CMP_HIDDEN), F32)
    hb = jnp.zeros((LANES, 2 * CMP_HIDDEN), F32)
    for p in range(n_half):
        x = x_ref[pl.ds(p, LANES, stride=CMP_STRIDE), :]
        ha = ha + _dot((x + pe_ref[p]).astype(BF16), w1_ref[p])
        hb = hb + _dot((x + pe_ref[n_half + p]).astype(BF16), w1_ref[n_half + p])
    h = ha + pltpu.roll(hb, LANES - 1, 0)
    y = _dot(jax.nn.gelu(h).astype(BF16), w2_ref[...])
    row = lax.broadcasted_iota(jnp.int32, (LANES, 1), 0)
    o_ref[...] = jnp.where(row < LANES - 1, y, 0.0)


def _compress_body(xk_ref, xv_ref, pek_ref, pev_ref, w1k_ref, w1v_ref, w2k_ref, w2v_ref, ok_ref, ov_ref):
    _compress_one(xk_ref, pek_ref, w1k_ref, w2k_ref, ok_ref)
    _compress_one(xv_ref, pev_ref, w1v_ref, w2v_ref, ov_ref)


def _compress(xk, xv, cw, name):
    bn = xk.shape[0]
    n_tok = LANES * CMP_STRIDE
    x_spec = pl.BlockSpec((None, n_tok, LANES), lambda b: (b, 0, 0))
    pe_spec = pl.BlockSpec((CMP_BLOCK, 1, LANES), lambda b: (0, 0, 0))
    w1_spec = pl.BlockSpec((CMP_BLOCK, LANES, 2 * CMP_HIDDEN), lambda b: (0, 0, 0))
    w2_spec = pl.BlockSpec((2 * CMP_HIDDEN, LANES), lambda b: (0, 0))
    o_spec = pl.BlockSpec((None, LANES, LANES), lambda b: (b, 0, 0))
    return pl.pallas_call(
        _compress_body,
        grid=(bn,),
        in_specs=[x_spec, x_spec, pe_spec, pe_spec, w1_spec, w1_spec, w2_spec, w2_spec],
        out_specs=[o_spec, o_spec],
        out_shape=[jax.ShapeDtypeStruct((bn, LANES, LANES), F32)] * 2,
        compiler_params=_params("parallel"),
        name=name,
    )(xk, xv, cw["pek"], cw["pev"], cw["w1k"], cw["w1v"], cw["w2k"], cw["w2v"])


def _compress_weights(pe, w1, w2):
    out = {}
    for idx, tag in ((0, "k"), (1, "v")):
        w1p = w1[idx].reshape(CMP_BLOCK, HD, CMP_HIDDEN)
        z = jnp.zeros_like(w1p)
        w1b = jnp.concatenate([jnp.concatenate([w1p, z], axis=2), jnp.concatenate([z, w1p], axis=2)], axis=1)
        z2 = jnp.zeros_like(w2[idx])
        w2b = jnp.concatenate([jnp.concatenate([w2[idx], z2], axis=1), jnp.concatenate([z2, w2[idx]], axis=1)], axis=0)
        out["w1" + tag] = w1b.astype(BF16)
        out["w2" + tag] = w2b.astype(BF16)
        out["pe" + tag] = jnp.concatenate([pe[idx], pe[idx]], axis=1)[:, None, :]
    return out


def _rank_desc(vals, n, lane):
    rank = jnp.zeros(vals.shape, F32)
    for i in range(n):
        col = vals[:, i:i + 1]
        tie = jnp.where(lane > i, 1.0, 0.0)
        rank = rank + jnp.where(col > vals, 1.0, 0.0) + jnp.where(col == vals, tie, 0.0)
    return rank


def _nsa_body(tq, tk, q_pos0, ns, w_pos0, w_valid, lw,
              qc_ref, qr_ref, bg_ref, kcmp_ref, vcmp_ref, ks_ref, vs_ref, kw_ref, vw_ref, o_ref,
              m_ref, l_ref, acc_ref):
    g = pl.program_id(1)
    qi = pl.program_id(2)
    lane = lax.broadcasted_iota(jnp.int32, (1, LANES), 1)
    halfi = lane >> 6
    q0 = q_pos0 + qi * tq
    qpos = q0 + lax.broadcasted_iota(jnp.int32, (tq, 1), 0)
    rep = B_HEADS // B_KV

    def embed(ref, r):
        v = ref[:, (r // 2) * LANES:(r // 2 + 1) * LANES] * (HD ** -0.5)
        own = jnp.where(halfi == (r % 2), v, 0.0)
        dup = own + pltpu.roll(own, HD, 1)
        return jnp.where(halfi == g, dup, 0.0).astype(BF16)

    def pick(o, r):
        return jnp.where(g == (r % 2), o, pltpu.roll(o, HD, 1))

    kcmp = kcmp_ref[...].astype(BF16)
    vcmp = vcmp_ref[...].astype(BF16)
    cmask = (lane * CMP_STRIDE + (CMP_BLOCK - 1)) <= qpos
    psum = jnp.zeros((tq, LANES), F32)
    o_cmp = []
    for r in range(rep):
        s = jnp.where(cmask, _dot_nt(embed(qc_ref, r), kcmp), NEG)
        m = jnp.max(s, axis=1, keepdims=True)
        p = jnp.where(cmask, jnp.exp(s - m), 0.0)
        p = p / jnp.maximum(jnp.sum(p, axis=1, keepdims=True), 1e-30)
        psum = psum + p
        o_cmp.append(_dot(p.astype(BF16), vcmp))

    ci = lax.broadcasted_iota(jnp.int32, (LANES, LANES), 0)
    cj = lax.broadcasted_iota(jnp.int32, (LANES, LANES), 1)
    onehot = jnp.where((ci >> CMP_PER_SEL_SHIFT) == cj, 1.0, 0.0).astype(BF16)
    hi, mid, lo = _split3(psum)
    imp = _dot(hi, onehot) + _dot(mid, onehot) + _dot(lo, onehot)
    cur = qpos >> 6
    impm = jnp.where(lane == cur, jnp.inf, jnp.where(lane < cur, imp, -jnp.inf))
    rank = _rank_desc(impm, ns, lane)
    sel = jnp.where(rank < float(SEL_TOPK), jnp.where(lane <= cur, 1.0, 0.0), 0.0).astype(BF16)

    qsr = [embed(qr_ref, r) for r in range(rep)]
    _flash_init(m_ref, l_ref, acc_ref)
    n_chunks = (q0 + tq - 1) // tk + 1
    bpc = tk // SEL_BLOCK

    def sel_body(j, carry):
        kc = _chunk(ks_ref, j, tk)
        vc = _chunk(vs_ref, j, tk)
        ei = lax.broadcasted_iota(jnp.int32, (LANES, tk), 0)
        el = lax.broadcasted_iota(jnp.int32, (LANES, tk), 1)
        expand = jnp.where(ei == j * bpc + (el >> 6), 1.0, 0.0).astype(BF16)
        allowed = _dot(sel, expand)
        kpos = j * tk + lax.broadcasted_iota(jnp.int32, (1, tk), 1)
        mask = jnp.where(kpos <= qpos, allowed, 0.0) > 0.5
        for r in range(rep):
            _flash_step(r, qsr[r], kc, vc, mask, m_ref, l_ref, acc_ref)
        return carry

    lax.fori_loop(0, n_chunks, sel_body, 0)
    o_slc = [_flash_out(r, l_ref, acc_ref) for r in range(rep)]

    _flash_init(m_ref, l_ref, acc_ref)
    w_lo = jnp.maximum(q0 - WINDOW - w_pos0, 0) // tk
    w_hi = jnp.minimum(q0 + tq - 1 - w_pos0, lw - 1) // tk + 1

    def win_body(j, carry):
        kc = _chunk(kw_ref, j, tk)
        vc = _chunk(vw_ref, j, tk)
        kidx = j * tk + lax.broadcasted_iota(jnp.int32, (1, tk), 1)
        dist = qpos - (kidx + w_pos0)
        inwin = jnp.where(dist >= 0, jnp.where(dist <= WINDOW, 1.0, 0.0), 0.0)
        mask = jnp.where(kidx < w_valid, inwin, 0.0) > 0.5
        for r in range(rep):
            _flash_step(r, qsr[r], kc, vc, mask, m_ref, l_ref, acc_ref)
        return carry

    lax.fori_loop(w_lo, w_hi, win_body, 0)

    sig = jax.nn.sigmoid(bg_ref[...])
    outs = []
    for r in range(rep):
        o_win = _flash_out(r, l_ref, acc_ref)
        base = (g * rep + r) * 3
        gates = [jnp.sum(jnp.where(lane == base + c, sig, 0.0), axis=1, keepdims=True) for c in range(3)]
        outs.append(pick(gates[0] * o_cmp[r] + gates[1] * o_slc[r] + gates[2] * o_win, r))
    for pr in range(rep // 2):
        o_ref[:, pr * LANES:(pr + 1) * LANES] = jnp.where(halfi == 0, outs[2 * pr], outs[2 * pr + 1])


def _nsa_attention(bqc, bqr, bg, kcmp, vcmp, ks, vs, kw, vw, q_pos0, k_valid, w_pos0, w_valid, name):
    bn, tq_all, _ = bqc.shape
    lk = ks.shape[1]
    lw = kw.shape[1]
    tq = min(ROW_TILE, tq_all)
    tk = KV_CHUNK
    ns = -(-k_valid // SEL_BLOCK)
    body = functools.partial(_nsa_body, tq, tk, q_pos0, ns, w_pos0, w_valid, lw)
    q_spec = pl.BlockSpec((None, tq, 256), lambda b, g, i: (b, i, g))
    full = lambda rows: pl.BlockSpec((None, rows, LANES), lambda b, g, i: (b, 0, 0))
    return pl.pallas_call(
        body,
        grid=(bn, B_KV, tq_all // tq),
        in_specs=[q_spec, q_spec, pl.BlockSpec((None, tq, LANES), lambda b, g, i: (b, i, 0)),
                  full(LANES), full(LANES), full(lk), full(lk), full(lw), full(lw)],
        out_specs=pl.BlockSpec((None, tq, 256), lambda b, g, i: (b, i, g)),
        out_shape=jax.ShapeDtypeStruct((bn, tq_all, 512), F32),
        scratch_shapes=[pltpu.VMEM((4, tq, LANES), F32), pltpu.VMEM((4, tq, LANES), F32),
                        pltpu.VMEM((4, tq, LANES), F32)],
        compiler_params=_params("parallel", "parallel", "arbitrary"),
        name=name,
    )(bqc, bqr, bg, kcmp, vcmp, ks, vs, kw, vw)


def _moba_body(tq, tk, q_pos0, nb, q_ref, k_ref, v_ref, o_ref, km_ref, m_ref, l_ref, acc_ref):
    g = pl.program_id(1)
    qi = pl.program_id(2)
    gh = g % 2
    lane = lax.broadcasted_iota(jnp.int32, (1, LANES), 1)
    halfi = lane >> 6
    q0 = q_pos0 + qi * tq
    qpos = q0 + lax.broadcasted_iota(jnp.int32, (tq, 1), 0)
    cur = qpos >> 8
    rep = C_HEADS // C_KV

    @pl.when(qi == 0)
    def _():
        km_ref[...] = jnp.zeros(km_ref.shape, F32)
        km_ref[0:nb, :] = jnp.sum(k_ref[...].reshape(nb, MOBA_BLOCK, LANES), axis=1) * (1.0 / MOBA_BLOCK)

    km_hi, km_mid, _ = _split3(km_ref[...])

    qv = q_ref[...]
    qs = []
    sels = []
    for r in range(rep):
        own = jnp.where(halfi == r, qv, 0.0)
        dup = own + pltpu.roll(own, HD, 1)
        qe = jnp.where(halfi == gh, dup, 0.0)
        q_hi, q_mid, _ = _split3(qe)
        s_blk = _dot_nt(q_hi, km_hi) + _dot_nt(q_hi, km_mid) + _dot_nt(q_mid, km_hi)
        sm = jnp.where(lane < cur, s_blk, -jnp.inf)
        rank = _rank_desc(sm, nb, lane)
        past = jnp.where(rank < float(MOBA_TOPK), jnp.where(lane < cur, 1.0, 0.0), 0.0)
        sels.append(jnp.where(lane == cur, 1.0, past))
        qs.append((qe * (HD ** -0.5)).astype(BF16))

    _flash_init(m_ref, l_ref, acc_ref)
    n_chunks = (q0 + tq - 1) // tk + 1

    def body(j, carry):
        kc = _chunk(k_ref, j, tk)
        vc = _chunk(v_ref, j, tk)
        kpos = j * tk + lax.broadcasted_iota(jnp.int32, (1, tk), 1)
        causal = jnp.where(kpos <= qpos, 1.0, 0.0)
        for r in range(rep):
            col = jnp.sum(jnp.where(lane == j, sels[r], 0.0), axis=1, keepdims=True)
            mask = (causal * col) > 0.5
            _flash_step(r, qs[r], kc, vc, mask, m_ref, l_ref, acc_ref)
        return carry

    lax.fori_loop(0, n_chunks, body, 0)
    outs = []
    for r in range(rep):
        o = _flash_out(r, l_ref, acc_ref)
        outs.append(jnp.where(gh == r, o, pltpu.roll(o, HD, 1)))
    o_ref[...] = jnp.where(halfi == 0, outs[0], outs[1])


def _moba_attention(cq, ck, cv, q_pos0, name):
    bn, tq_all, _ = cq.shape
    lk = ck.shape[1]
    tq = min(ROW_TILE, tq_all)
    tk = MOBA_BLOCK
    nb = lk // MOBA_BLOCK
    body = functools.partial(_moba_body, tq, tk, q_pos0, nb)
    return pl.pallas_call(
        body,
        grid=(bn, C_KV, tq_all // tq),
        in_specs=[
            pl.BlockSpec((None, tq, LANES), lambda b, g, i: (b, i, g)),
            pl.BlockSpec((None, lk, LANES), lambda b, g, i: (b, 0, g // 2)),
            pl.BlockSpec((None, lk, LANES), lambda b, g, i: (b, 0, g // 2)),
        ],
        out_specs=pl.BlockSpec((None, tq, LANES), lambda b, g, i: (b, i, g)),
        out_shape=jax.ShapeDtypeStruct((bn, tq_all, 512), F32),
        scratch_shapes=[pltpu.VMEM((LANES, LANES), F32),
                        pltpu.VMEM((2, tq, LANES), F32), pltpu.VMEM((2, tq, LANES), F32),
                        pltpu.VMEM((2, tq, LANES), F32)],
        compiler_params=_params("parallel", "parallel", "arbitrary"),
        name=name,
    )(cq, ck, cv)


def _merge_body(x_ref, gate_ref, oa_ref, ob_ref, oc_ref, wb_ref, wo_ref, y_ref):
    y = jnp.zeros((x_ref.shape[0], D_MODEL), F32)
    for c, o_ref in enumerate((oa_ref, ob_ref, oc_ref)):
        br = _dot(o_ref[...].astype(BF16), wb_ref[c])
        y = y + jax.nn.sigmoid(gate_ref[:, c * D_MODEL:(c + 1) * D_MODEL]) * br
    y_ref[...] = x_ref[...] + _dot(y.astype(BF16), wo_ref[...])


def _merge(x, gate, o_a, o_b, o_c, wb, wo, name):
    n = x.shape[0]
    tm = min(ROW_TILE, n)
    row = lambda w: pl.BlockSpec((tm, w), lambda i: (i, 0))
    return pl.pallas_call(
        _merge_body,
        grid=(n // tm,),
        in_specs=[row(D_MODEL), row(3 * D_MODEL), row(512), row(512), row(512),
                  pl.BlockSpec((3, 512, D_MODEL), lambda i: (0, 0, 0)),
                  pl.BlockSpec((D_MODEL, D_MODEL), lambda i: (0, 0))],
        out_specs=row(D_MODEL),
        out_shape=jax.ShapeDtypeStruct((n, D_MODEL), F32),
        compiler_params=_params("parallel"),
        name=name,
    )(x, gate, o_a, o_b, o_c, wb, wo)


def _peer_stats_body(tm, x_ref, nw_ref, wq_ref, sk_ref, st_ref, stat_ref, top_ref, nxt_ref):
    h = _rms(x_ref[...], nw_ref[...]).astype(BF16)
    q = _dot(h, wq_ref[...]).astype(BF16)
    n_hc = 2 * PEER_HEADS
    for hc in range(n_hc):
        st_ref[hc] = _dot_nt(sk_ref[hc], q[:, hc * LANES:(hc + 1) * LANES])

    def top_body(hc, carry):
        for hf in range(tm // LANES):
            s = st_ref[hc, :, hf * LANES:(hf + 1) * LANES]
            vals = []
            for _ in range(PEER_TOPK + 1):
                m = jnp.max(s, axis=0, keepdims=True)
                vals.append(m)
                s = jnp.where(s == m, -jnp.inf, s)
            top_ref[hc, :, hf * LANES:(hf + 1) * LANES] = jnp.concatenate(vals[:PEER_TOPK], axis=0)
            nxt_ref[hc, :, hf * LANES:(hf + 1) * LANES] = vals[PEER_TOPK]
        return carry

    lax.fori_loop(0, n_hc, top_body, 0)

    row16 = lax.broadcasted_iota(jnp.int32, (PEER_TOPK, LANES), 0)

    def head_body(hd, carry):
        for hf in range(tm // LANES):
            sl = slice(hf * LANES, (hf + 1) * LANES)
            a = top_ref[2 * hd, :, sl]
            b = top_ref[2 * hd + 1, :, sl]
            parts = []
            for i in range(PEER_TOPK):
                lim = PEER_TOPK // (i + 1)
                rows = PEER_TOPK if lim > 8 else 8
                cs = a[i:i + 1] + b[:rows]
                parts.append(jnp.where(row16[:rows] < lim, cs, -jnp.inf))
            cand = jnp.concatenate(parts, axis=0)
            mx = a[0:1] + b[0:1]
            cum = jnp.zeros((1, LANES), F32)
            tau = mx
            nxt = mx
            work = cand
            for _ in range(PEER_TOPK + 1):
                m = jnp.max(work, axis=0, keepdims=True)
                eq = work == m
                tau = jnp.where(cum < float(PEER_TOPK), m, tau)
                nxt = jnp.where(cum < float(PEER_TOPK + 1), m, nxt)
                cum = cum + jnp.sum(jnp.where(eq, 1.0, 0.0), axis=0, keepdims=True)
                work = jnp.where(eq, -jnp.inf, work)
            nxt = jnp.maximum(nxt, jnp.maximum(a[0:1] + nxt_ref[2 * hd + 1, :, sl], nxt_ref[2 * hd, :, sl] + b[0:1]))
            z = jnp.sum(jnp.where(cand >= tau, jnp.exp(cand - mx), 0.0), axis=0, keepdims=True)
            stat_ref[hd, :, sl] = 0.5 * (tau + nxt)
            stat_ref[PEER_HEADS + hd, :, sl] = a[0:1]
            stat_ref[2 * PEER_HEADS + hd, :, sl] = b[0:1]
            stat_ref[3 * PEER_HEADS + hd, :, sl] = 1.0 / z
        return carry

    lax.fori_loop(0, PEER_HEADS, head_body, 0)


def _peer_stats(x, nw, wq, sk, name):
    n = x.shape[0]
    tm = min(ROW_TILE, n)
    n_hc = 2 * PEER_HEADS
    return pl.pallas_call(
        functools.partial(_peer_stats_body, tm),
        grid=(n // tm,),
        in_specs=[pl.BlockSpec((tm, D_MODEL), lambda i: (i, 0)),
                  pl.BlockSpec((1, D_MODEL), lambda i: (0, 0)),
                  pl.BlockSpec((D_MODEL, n_hc * LANES), lambda i: (0, 0)),
                  pl.BlockSpec((n_hc, LANES, LANES), lambda i: (0, 0, 0))],
        out_specs=[pl.BlockSpec((n_hc, LANES, tm), lambda i: (0, 0, i)),
                   pl.BlockSpec((4 * PEER_HEADS, 1, tm), lambda i: (0, 0, i))],
        out_shape=[jax.ShapeDtypeStruct((n_hc, LANES, n), F32),
                   jax.ShapeDtypeStruct((4 * PEER_HEADS, 1, n), F32)],
        scratch_shapes=[pltpu.VMEM((n_hc, PEER_TOPK, tm), F32), pltpu.VMEM((n_hc, 1, tm), F32)],
        compiler_params=_params("parallel"),
        name=name,
    )(x, nw, wq, sk)


def _peer_dense_body(tm, te, x_ref, nw_ref, st_ref, stat_ref, u_ref, vt_ref, o_ref,
                     h_ref, thr_ref, w1_ref, s2c_ref, e2_ref, acc_ref):
    e = pl.program_id(1)

    @pl.when(e == 0)
    def _():
        h_ref[...] = _rms(x_ref[...], nw_ref[...]).astype(BF16)
        acc_ref[...] = jnp.zeros(acc_ref.shape, F32)
        for hd in range(PEER_HEADS):
            tau = stat_ref[hd]
            m1 = stat_ref[PEER_HEADS + hd]
            m2 = stat_ref[2 * PEER_HEADS + hd]
            inv_z = stat_ref[3 * PEER_HEADS + hd]
            thr_ref[hd] = m1 - st_ref[2 * hd]
            w1_ref[hd] = jnp.exp(st_ref[2 * hd] - m1) * inv_z
            s2c_ref[hd] = (st_ref[2 * hd + 1] - (tau - m1)).astype(BF16)
            e2_ref[hd] = jnp.exp(st_ref[2 * hd + 1] - m2).astype(BF16)

    act = jax.nn.gelu(_dot_nt(u_ref[...], h_ref[...])).astype(BF16)
    n_i = te // PEER_NKEYS
    zero = jnp.zeros((), BF16)
    parts = []
    for ii in range(n_i):
        i = e * n_i + ii
        gmat = jnp.zeros((PEER_NKEYS, tm), BF16)
        for hd in range(PEER_HEADS):
            thr = thr_ref[hd, pl.ds(i, 1), :].astype(BF16)
            w1 = w1_ref[hd, pl.ds(i, 1), :].astype(BF16)
            gmat = gmat + jnp.where(s2c_ref[hd] > thr, e2_ref[hd] * w1, zero)
        parts.append(gmat * act[ii * PEER_NKEYS:(ii + 1) * PEER_NKEYS])
    acc_ref[...] += _dot(vt_ref[...], jnp.concatenate(parts, axis=0))

    @pl.when(e == pl.num_programs(1) - 1)
    def _():
        o_ref[...] = x_ref[...] + acc_ref[...].T


def _peer_dense(x, nw, st, stat, u_bf, vt_bf, name):
    n = x.shape[0]
    tm = min(ROW_TILE, n)
    te = EXPERT_TILE
    n_exp = u_bf.shape[0]
    n_hc = 2 * PEER_HEADS
    return pl.pallas_call(
        functools.partial(_peer_dense_body, tm, te),
        grid=(n // tm, n_exp // te),
        in_specs=[pl.BlockSpec((tm, D_MODEL), lambda i, e: (i, 0)),
                  pl.BlockSpec((1, D_MODEL), lambda i, e: (0, 0)),
                  pl.BlockSpec((n_hc, LANES, tm), lambda i, e: (0, 0, i)),
                  pl.BlockSpec((4 * PEER_HEADS, 1, tm), lambda i, e: (0, 0, i)),
                  pl.BlockSpec((te, D_MODEL), lambda i, e: (e, 0)),
                  pl.BlockSpec((D_MODEL, te), lambda i, e: (0, e))],
        out_specs=pl.BlockSpec((tm, D_MODEL), lambda i, e: (i, 0)),
        out_shape=jax.ShapeDtypeStruct((n, D_MODEL), F32),
        scratch_shapes=[pltpu.VMEM((tm, D_MODEL), BF16),
                        pltpu.VMEM((PEER_HEADS, PEER_NKEYS, tm), F32),
                        pltpu.VMEM((PEER_HEADS, PEER_NKEYS, tm), F32),
                        pltpu.VMEM((PEER_HEADS, PEER_NKEYS, tm), BF16),
                        pltpu.VMEM((PEER_HEADS, PEER_NKEYS, tm), BF16),
                        pltpu.VMEM((D_MODEL, tm), F32)],
        compiler_params=_params("parallel", "arbitrary"),
        name=name,
    )(x, nw, st, stat, u_bf, vt_bf)


def _norm_body(x_ref, w_ref, o_ref):
    o_ref[...] = _rms(x_ref[...], w_ref[...])


def _final_norm(x, w, name):
    n = x.shape[0]
    tm = min(ROW_TILE, n)
    return pl.pallas_call(
        _norm_body,
        grid=(n // tm,),
        in_specs=[pl.BlockSpec((tm, D_MODEL), lambda i: (i, 0)), pl.BlockSpec((1, D_MODEL), lambda i: (0, 0))],
        out_specs=pl.BlockSpec((tm, D_MODEL), lambda i: (i, 0)),
        out_shape=jax.ShapeDtypeStruct((n, D_MODEL), F32),
        compiler_params=_params("parallel"),
        name=name,
    )(x, w)


SROWS = 8


def _page_specs(n_pages, rows, layer):
    def spec(p):
        return pl.BlockSpec((None, None, rows, LANES), lambda b, pt: (layer, pt[b, p], 0, 0))
    return [spec(p) for p in range(n_pages)]


def _row_spec(width):
    return pl.BlockSpec((None, SROWS, width), lambda b, pt: (b, 0, 0))


def _dup(x):
    return jnp.concatenate([x, x], axis=0)


def _pad_rows(x):
    return jnp.concatenate([x, jnp.zeros((PAGE_SIZE - x.shape[0], x.shape[1]), F32)], axis=0)


def _new_mask(m_rows, n_new):
    lane = lax.broadcasted_iota(jnp.int32, (1, LANES), 1)
    trow = lax.broadcasted_iota(jnp.int32, (m_rows, 1), 0) & (SROWS - 1)
    return jnp.where(lane < n_new, jnp.where(lane <= trow, 1.0, 0.0), 0.0) > 0.5


def _softmax_pieces(pieces):
    mx = None
    for s, mk in pieces:
        sm = s if mk is None else jnp.where(mk, s, NEG)
        mx = sm if mx is None else jnp.maximum(mx, sm)
    m = jnp.max(mx, axis=1, keepdims=True)
    ps = []
    tot = None
    for s, mk in pieces:
        p = jnp.exp(s - m)
        if mk is not None:
            p = jnp.where(mk, p, 0.0)
        tot = p if tot is None else tot + p
        ps.append(p.astype(BF16))
    return ps, jnp.sum(tot, axis=1, keepdims=True)


def _scores_kt(q, kts):
    return [_dot(q, _dup(kt.astype(BF16))) for kt in kts]


def _values_kt(ps, vts):
    o = None
    for p, vt in zip(ps, vts):
        t = _dot_nt(p, _dup(vt.astype(BF16)))
        o = t if o is None else o + t
    return o


def _diff_s_body(n_pages, n_new, lam_init, pt_ref, lam_ref, subln_ref, q_ref, kn_ref, vn_ref, *rest):
    k_refs = rest[:n_pages]
    v_refs = rest[n_pages:2 * n_pages]
    o_ref = rest[2 * n_pages]
    lane = lax.broadcasted_iota(jnp.int32, (1, LANES), 1)
    upper = lane >= HD
    new_mask = _new_mask(4 * SROWS, n_new)
    lp = lam_ref[...]
    lam = (jnp.exp(jnp.sum(lp[0:1] * lp[1:2], axis=1, keepdims=True))
           - jnp.exp(jnp.sum(lp[2:3] * lp[3:4], axis=1, keepdims=True)) + lam_init)
    for g in range(A_KV):
        parts = []
        for r in range(A_HEADS // A_KV):
            qb = q_ref[:, (2 * g + r) * LANES:(2 * g + r + 1) * LANES] * (HD ** -0.5)
            parts += [jnp.where(upper, 0.0, qb), jnp.where(upper, qb, 0.0)]
        q = jnp.concatenate(parts, axis=0).astype(BF16)
        pieces = [(_dot_nt(q, k_refs[p][pl.ds(g, PAGE_SIZE, stride=A_KV), :].astype(BF16)), None)
                  for p in range(n_pages)]
        knew = _pad_rows(kn_ref[:, g * LANES:(g + 1) * LANES]).astype(BF16)
        vnew = _pad_rows(vn_ref[:, g * LANES:(g + 1) * LANES]).astype(BF16)
        pieces.append((_dot_nt(q, knew), new_mask))
        ps, tot = _softmax_pieces(pieces)
        o = _dot(ps[n_pages], vnew)
        for p in range(n_pages):
            o = o + _dot(ps[p], v_refs[p][pl.ds(g, PAGE_SIZE, stride=A_KV), :].astype(BF16))
        o = o / tot
        for r in range(A_HEADS // A_KV):
            d = o[2 * r * SROWS:(2 * r + 1) * SROWS] - lam * o[(2 * r + 1) * SROWS:(2 * r + 2) * SROWS]
            o_ref[:, (2 * g + r) * LANES:(2 * g + r + 1) * LANES] = _rms(d, subln_ref[...]) * (1.0 - lam_init)


def _diff_sample(page_table, layer, n_new, aq, ak_new, av_new, cache_k, cache_v, lam_p, subln, lam_init, name):
    db, n_pages = page_table.shape
    const = lambda shape: pl.BlockSpec(shape, lambda b, pt: (0,) * len(shape))
    in_specs = ([const((4, HD)), const((1, 2 * HD)), _row_spec(512), _row_spec(256), _row_spec(256)]
                + _page_specs(n_pages, PAGE_SIZE * A_KV, layer) + _page_specs(n_pages, PAGE_SIZE * A_KV, layer))
    grid_spec = pltpu.PrefetchScalarGridSpec(num_scalar_prefetch=1, grid=(db,), in_specs=in_specs,
                                             out_specs=_row_spec(512))
    return pl.pallas_call(
        functools.partial(_diff_s_body, n_pages, n_new, lam_init),
        grid_spec=grid_spec,
        out_shape=jax.ShapeDtypeStruct((db, SROWS, 512), F32),
        compiler_params=_params("parallel"),
        name=name,
    )(page_table, lam_p, subln, aq, ak_new, av_new, *([cache_k] * n_pages), *([cache_v] * n_pages))


def _moba_s_body(n_pages, n_new, pt_ref, q_ref, kn_ref, vn_ref, *rest):
    k_refs = rest[:n_pages]
    v_refs = rest[n_pages:2 * n_pages]
    o_ref = rest[2 * n_pages]
    lane = lax.broadcasted_iota(jnp.int32, (1, LANES), 1)
    halfi = lane >> 6
    rep = C_HEADS // C_KV
    new_mask = _new_mask(rep * SROWS, n_new)
    knt = _pad_rows(kn_ref[...]).T
    vnt = _pad_rows(vn_ref[...]).T
    ppb = MOBA_BLOCK // PAGE_SIZE
    n_blocks = n_pages // ppb
    for g in range(C_KV):
        rows = slice(g * HD, (g + 1) * HD)
        qb = q_ref[:, g * LANES:(g + 1) * LANES] * (HD ** -0.5)
        q = jnp.concatenate([jnp.where(halfi == r, qb, 0.0) for r in range(rep)], axis=0).astype(BF16)
        ss = _scores_kt(q, [k_refs[p][rows, :] for p in range(n_pages)] + [knt[rows, :]])
        bs = []
        for j in range(n_blocks):
            acc = ss[j * ppb]
            for u in range(1, ppb):
                acc = acc + ss[j * ppb + u]
            bs.append(jnp.sum(acc, axis=1, keepdims=True))
        pieces = []
        for j in range(n_blocks):
            rank = jnp.zeros(bs[j].shape, F32)
            for i in range(n_blocks):
                if i < j:
                    rank = rank + jnp.where(bs[i] >= bs[j], 1.0, 0.0)
                elif i > j:
                    rank = rank + jnp.where(bs[i] > bs[j], 1.0, 0.0)
            keep = rank < float(MOBA_TOPK)
            for u in range(ppb):
                pieces.append((ss[j * ppb + u], keep))
        pieces.append((ss[n_pages], new_mask))
        ps, tot = _softmax_pieces(pieces)
        o = _values_kt(ps, [v_refs[p][rows, :] for p in range(n_pages)] + [vnt[rows, :]]) / tot
        o_ref[:, g * LANES:(g + 1) * LANES] = jnp.where(halfi == 0, o[0:SROWS], o[SROWS:2 * SROWS])


def _moba_sample(page_table, layer, n_new, cq, ck_new, cv_new, cache_k, cache_v, name):
    db, n_pages = page_table.shape
    in_specs = ([_row_spec(512), _row_spec(256), _row_spec(256)]
                + _page_specs(n_pages, C_KV * HD, layer) + _page_specs(n_pages, C_KV * HD, layer))
    grid_spec = pltpu.PrefetchScalarGridSpec(num_scalar_prefetch=1, grid=(db,), in_specs=in_specs,
                                             out_specs=_row_spec(512))
    return pl.pallas_call(
        functools.partial(_moba_s_body, n_pages, n_new),
        grid_spec=grid_spec,
        out_shape=jax.ShapeDtypeStruct((db, SROWS, 512), F32),
        compiler_params=_params("parallel"),
        name=name,
    )(page_table, cq, ck_new, cv_new, *([cache_k] * n_pages), *([cache_v] * n_pages))


def _compress_s_body(n_pages, pt_ref, pek_ref, pev_ref, w1k_ref, w1v_ref, w2k_ref, w2v_ref, *rest):
    k_refs = rest[:n_pages]
    v_refs = rest[n_pages:2 * n_pages]
    ok_ref, ov_ref, xk_ref, xv_ref = rest[2 * n_pages:]
    for p in range(n_pages):
        xk_ref[p * PAGE_SIZE:(p + 1) * PAGE_SIZE, :] = k_refs[p][...].T
        xv_ref[p * PAGE_SIZE:(p + 1) * PAGE_SIZE, :] = v_refs[p][...].T
    _compress_one(xk_ref, pek_ref, w1k_ref, w2k_ref, ok_ref)
    _compress_one(xv_ref, pev_ref, w1v_ref, w2v_ref, ov_ref)


def _compress_sample(page_table, layer, cache_k, cache_v, cw, name):
    db, n_pages = page_table.shape
    const = lambda shape: pl.BlockSpec(shape, lambda b, pt: (0,) * len(shape))
    in_specs = ([const((CMP_BLOCK, 1, LANES))] * 2 + [const((CMP_BLOCK, LANES, 2 * CMP_HIDDEN))] * 2
                + [const((2 * CMP_HIDDEN, LANES))] * 2
                + _page_specs(n_pages, B_KV * HD, layer) + _page_specs(n_pages, B_KV * HD, layer))
    o_spec = pl.BlockSpec((None, LANES, LANES), lambda b, pt: (b, 0, 0))
    grid_spec = pltpu.PrefetchScalarGridSpec(
        num_scalar_prefetch=1, grid=(db,), in_specs=in_specs, out_specs=[o_spec, o_spec],
        scratch_shapes=[pltpu.VMEM((n_pages * PAGE_SIZE, LANES), F32)] * 2)
    return pl.pallas_call(
        functools.partial(_compress_s_body, n_pages),
        grid_spec=grid_spec,
        out_shape=[jax.ShapeDtypeStruct((db, LANES, LANES), F32)] * 2,
        compiler_params=_params("parallel"),
        name=name,
    )(page_table, cw["pek"], cw["pev"], cw["w1k"], cw["w1v"], cw["w2k"], cw["w2v"],
      *([cache_k] * n_pages), *([cache_v] * n_pages))


def _nsa_s_body(n_pages, n_new, q_pos0, n_win, pt_ref, qc_ref, qr_ref, bg_ref, kcmp_ref, vcmp_ref,
                ksn_ref, vsn_ref, kwn_ref, vwn_ref, wk_ref, wv_ref, *rest):
    k_refs = rest[:n_pages]
    v_refs = rest[n_pages:2 * n_pages]
    o_ref = rest[2 * n_pages]
    lane = lax.broadcasted_iota(jnp.int32, (1, LANES), 1)
    halfi = lane >> 6
    rep = B_HEADS // B_KV
    m_rows = rep * SROWS
    new_mask = _new_mask(m_rows, n_new)
    trow8 = lax.broadcasted_iota(jnp.int32, (SROWS, 1), 0)
    trow = lax.broadcasted_iota(jnp.int32, (m_rows, 1), 0) & (SROWS - 1)
    qpos8 = q_pos0 + trow8
    ns = -(-(q_pos0 + n_new) // SEL_BLOCK)
    bpp = PAGE_SIZE // SEL_BLOCK
    ksnt = _pad_rows(ksn_ref[...]).T
    vsnt = _pad_rows(vsn_ref[...]).T
    kwnt = _pad_rows(kwn_ref[...]).T
    vwnt = _pad_rows(vwn_ref[...]).T
    kcmp = kcmp_ref[...].astype(BF16)
    vcmp = vcmp_ref[...].astype(BF16)
    ci = lax.broadcasted_iota(jnp.int32, (LANES, LANES), 0)
    cj = lax.broadcasted_iota(jnp.int32, (LANES, LANES), 1)
    onehot = jnp.where((ci >> CMP_PER_SEL_SHIFT) == cj, 1.0, 0.0).astype(BF16)
    sig = jax.nn.sigmoid(bg_ref[...])
    tile_rows = lambda x: jnp.concatenate([x] * rep, axis=0)

    for g in range(B_KV):
        rows = slice(g * HD, (g + 1) * HD)
        own_c, own_r = [], []
        for r in range(rep):
            blk = slice((g * rep + r) // 2 * LANES, ((g * rep + r) // 2 + 1) * LANES)
            own_c.append(jnp.where(halfi == (r % 2), qc_ref[:, blk], 0.0) * (HD ** -0.5))
            own_r.append(jnp.where(halfi == (r % 2), qr_ref[:, blk], 0.0) * (HD ** -0.5))
        qc = jnp.concatenate(own_c, axis=0)
        qc = qc + pltpu.roll(qc, HD, 1)
        qc = jnp.where(halfi == g, qc, 0.0).astype(BF16)
        qr = jnp.concatenate(own_r, axis=0).astype(BF16)

        cmask = tile_rows(jnp.where((lane * CMP_STRIDE + (CMP_BLOCK - 1)) <= qpos8, 1.0, 0.0)) > 0.5
        s = jnp.where(cmask, _dot_nt(qc, kcmp), NEG)
        m = jnp.max(s, axis=1, keepdims=True)
        p = jnp.where(cmask, jnp.exp(s - m), 0.0)
        p = p / jnp.maximum(jnp.sum(p, axis=1, keepdims=True), 1e-30)
        o_cmp = jnp.where(halfi == g, _dot(p.astype(BF16), vcmp), 0.0)
        o_cmp = o_cmp + pltpu.roll(o_cmp, HD, 1)
        psum = p[0:SROWS]
        for r in range(1, rep):
            psum = psum + p[r * SROWS:(r + 1) * SROWS]
        hi, mid, lo = _split3(psum)
        imp = _dot(hi, onehot) + _dot(mid, onehot) + _dot(lo, onehot)
        cur = qpos8 >> 6
        impm = jnp.where(lane == cur, jnp.inf, jnp.where(lane < cur, imp, -jnp.inf))
        rank = _rank_desc(impm, ns, lane)
        sel = jnp.where(rank < float(SEL_TOPK), jnp.where(lane <= cur, 1.0, 0.0), 0.0)

        ss = _scores_kt(qr, [k_refs[pg][rows, :] for pg in range(n_pages)] + [ksnt[rows, :]])
        pieces = []
        for pg in range(n_pages):
            mk = sel[:, pg * bpp:pg * bpp + 1]
            for u in range(1, bpp):
                mk = jnp.where(halfi >= u, sel[:, pg * bpp + u:pg * bpp + u + 1], mk)
            pieces.append((ss[pg], tile_rows(mk) > 0.5))
        pieces.append((ss[n_pages], new_mask))
        ps, tot = _softmax_pieces(pieces)
        o_slc = _values_kt(ps, [v_refs[pg][rows, :] for pg in range(n_pages)] + [vsnt[rows, :]]) / tot

        wcols = [slice(u * LANES, (u + 1) * LANES) for u in range(n_win // LANES)]
        ss = _scores_kt(qr, [wk_ref[rows, c] for c in wcols] + [kwnt[rows, :]])
        pieces = []
        for u in range(len(wcols)):
            dist = (n_win - u * LANES) + trow - lane
            pieces.append((ss[u], dist <= WINDOW))
        pieces.append((ss[len(wcols)], new_mask))
        ps, tot = _softmax_pieces(pieces)
        o_win = _values_kt(ps, [wv_ref[rows, c] for c in wcols] + [vwnt[rows, :]]) / tot

        outs = []
        for r in range(rep):
            rs = slice(r * SROWS, (r + 1) * SROWS)
            base = (g * rep + r) * 3
            outs.append(sig[:, base:base + 1] * o_cmp[rs] + sig[:, base + 1:base + 2] * o_slc[rs]
                        + sig[:, base + 2:base + 3] * o_win[rs])
        for pr in range(rep // 2):
            col = (g * rep // 2 + pr) * LANES
            o_ref[:, col:col + LANES] = jnp.where(halfi == 0, outs[2 * pr], outs[2 * pr + 1])


def _nsa_sample(page_table, layer, n_new, q_pos0, bqc, bqr, bg, kcmp, vcmp, ks_new, vs_new, kw_new, vw_new,
                state_k, state_v, cache_k, cache_v, name):
    db, n_pages = page_table.shape
    n_win = state_k.shape[-1]
    w_spec = pl.BlockSpec((None, None, B_KV * HD, n_win), lambda b, pt: (layer, b, 0, 0))
    c_spec = pl.BlockSpec((None, LANES, LANES), lambda b, pt: (b, 0, 0))
    in_specs = ([_row_spec(512), _row_spec(512), _row_spec(LANES), c_spec, c_spec]
                + [_row_spec(LANES)] * 4 + [w_spec, w_spec]
                + _page_specs(n_pages, B_KV * HD, layer) + _page_specs(n_pages, B_KV * HD, layer))
    grid_spec = pltpu.PrefetchScalarGridSpec(num_scalar_prefetch=1, grid=(db,), in_specs=in_specs,
                                             out_specs=_row_spec(512))
    return pl.pallas_call(
        functools.partial(_nsa_s_body, n_pages, n_new, q_pos0, n_win),
        grid_spec=grid_spec,
        out_shape=jax.ShapeDtypeStruct((db, SROWS, 512), F32),
        compiler_params=_params("parallel"),
        name=name,
    )(page_table, bqc, bqr, bg, kcmp, vcmp, ks_new, vs_new, kw_new, vw_new, state_k, state_v,
      *([cache_k] * n_pages), *([cache_v] * n_pages))


def _layer_weights(l, norm_mix, norm_ffn, w_in, a_lambda, a_subln, b_cmp_pe, b_cmp_w1, b_cmp_w2,
                   w_branch, w_out, peer_wq, peer_subkeys, peer_u, peer_v):
    w = w_in[l]
    bg_w = jnp.pad(w[:, 2304:2328], ((0, 0), (0, LANES - 3 * B_HEADS)))
    return {
        "norm_mix": norm_mix[l][None, :],
        "norm_ffn": norm_ffn[l][None, :],
        "w_a": w[:, 0:1024].astype(BF16),
        "w_b": jnp.concatenate([w[:, 1024:2304], bg_w], axis=1).astype(BF16),
        "w_c": w[:, 2328:3352].astype(BF16),
        "w_g": w[:, 3352:6424].astype(BF16),
        "lam": a_lambda[l],
        "subln": a_subln[l][None, :],
        "cmp": _compress_weights(b_cmp_pe[l], b_cmp_w1[l], b_cmp_w2[l]),
        "w_branch": w_branch[l].astype(BF16),
        "w_out": w_out[l].astype(BF16),
        "wq": peer_wq[l].astype(BF16),
        "sk": peer_subkeys[l].reshape(2 * PEER_HEADS, PEER_NKEYS, LANES).astype(BF16),
        "u": peer_u[l].astype(BF16),
        "vt": peer_v[l].T.astype(BF16),
        "lam_init": 0.8 - 0.6 * math.exp(-0.3 * l),
    }


def _project_all(x, lw, cos_t, sin_t, tag):
    aq, ak, av = _project(x, lw["norm_mix"], lw["w_a"], cos_t, sin_t, PIECES_A, WIDTHS_A, "proj_a_" + tag)
    b_out = _project(x, lw["norm_mix"], lw["w_b"], cos_t, sin_t, PIECES_B, WIDTHS_B, "proj_b_" + tag)
    cq, ck, cv = _project(x, lw["norm_mix"], lw["w_c"], cos_t, sin_t, PIECES_C, WIDTHS_C, "proj_c_" + tag)
    (gate,) = _project(x, lw["norm_mix"], lw["w_g"], cos_t, sin_t, PIECES_G, WIDTHS_G, "proj_g_" + tag)
    return (aq, ak, av), b_out, (cq, ck, cv), gate


def _ffn(x, lw, tag):
    st, stat = _peer_stats(x, lw["norm_ffn"], lw["wq"], lw["sk"], "peer_stats_" + tag)
    return _peer_dense(x, lw["norm_ffn"], st, stat, lw["u"], lw["vt"], "peer_dense_" + tag)


def _prompt_layer(x, lw, cos_t, sin_t, bn, t):
    (aq, ak, av), (bqc, bqr, bkc, bvc, bks, bvs, bkw, bvw, bg), (cq, ck, cv), gate = _project_all(
        x, lw, cos_t, sin_t, "p")
    r3 = lambda a: a.reshape(bn, t, a.shape[-1])
    o_a = _diff_attention(r3(aq), r3(ak), r3(av), lw["lam"], lw["subln"], lw["lam_init"], 0, "diff_p")
    kcmp, vcmp = _compress(r3(bkc), r3(bvc), lw["cmp"], "cmp_p")
    o_b = _nsa_attention(r3(bqc), r3(bqr), r3(bg), kcmp, vcmp, r3(bks), r3(bvs), r3(bkw), r3(bvw),
                         0, t, 0, t, "nsa_p")
    o_c = _moba_attention(r3(cq), r3(ck), r3(cv), 0, "moba_p")
    n = bn * t
    x = _merge(x, gate, o_a.reshape(n, 512), o_b.reshape(n, 512), o_c.reshape(n, 512),
               lw["w_branch"], lw["w_out"], "merge_p")
    x = _ffn(x, lw, "p")
    wp = min(WINDOW, t)
    rows = (r3(ak).reshape(bn, t, A_KV, 2 * HD), r3(av).reshape(bn, t, A_KV, 2 * HD),
            r3(bkc).reshape(bn, t, B_KV, HD), r3(bvc).reshape(bn, t, B_KV, HD),
            r3(bks).reshape(bn, t, B_KV, HD), r3(bvs).reshape(bn, t, B_KV, HD),
            r3(ck).reshape(bn, t, C_KV, HD), r3(cv).reshape(bn, t, C_KV, HD),
            r3(bkw)[:, t - wp:].reshape(bn, wp, B_KV, HD), r3(bvw)[:, t - wp:].reshape(bn, wp, B_KV, HD))
    return x, rows


def _sample_layer(x, lw, l, cos_t, sin_t, db, t, past, caches, state_wk, state_wv, page_table):
    (aq, ak, av), (bqc, bqr, bkc, bvc, bks, bvs, bkw, bvw, bg), (cq, ck, cv), gate = _project_all(
        x, lw, cos_t, sin_t, "s")
    r3 = lambda a: a.reshape(db, SROWS, a.shape[-1])
    c_ak, c_av, c_bkc, c_bvc, c_bks, c_bvs, c_ck, c_cv = caches
    o_a = _diff_sample(page_table, l, t, r3(aq), r3(ak), r3(av), c_ak, c_av, lw["lam"], lw["subln"],
                       lw["lam_init"], "diff_s")
    kcmp, vcmp = _compress_sample(page_table, l, c_bkc, c_bvc, lw["cmp"], "cmp_s")
    o_b = _nsa_sample(page_table, l, t, past, r3(bqc), r3(bqr), r3(bg), kcmp, vcmp, r3(bks), r3(bvs),
                      r3(bkw), r3(bvw), state_wk, state_wv, c_bks, c_bvs, "nsa_s")
    o_c = _moba_sample(page_table, l, t, r3(cq), r3(ck), r3(cv), c_ck, c_cv, "moba_s")
    n = db * SROWS
    x = _merge(x, gate, o_a.reshape(n, 512), o_b.reshape(n, 512), o_c.reshape(n, 512),
               lw["w_branch"], lw["w_out"], "merge_s")
    x = _ffn(x, lw, "s")
    new = lambda a, kv, w: r3(a)[:, :t].reshape(db, t, kv, w)

    def rolled(state_t, a):
        new_t = jnp.transpose(r3(a)[:, :t], (0, 2, 1))
        out_t = jnp.concatenate([state_t[l][:, :, t:], new_t], axis=2)
        return jnp.transpose(out_t, (0, 2, 1)).reshape(db, out_t.shape[2], B_KV, HD)

    rows = (new(ak, A_KV, 2 * HD), new(av, A_KV, 2 * HD), new(bkc, B_KV, HD), new(bvc, B_KV, HD),
            new(bks, B_KV, HD), new(bvs, B_KV, HD), new(ck, C_KV, HD), new(cv, C_KV, HD),
            rolled(state_wk, bkw), rolled(state_wv, bvw))
    return x, rows


def kernel(x_prompt, x_sample, cache_a_k, cache_a_v, cache_b_kc, cache_b_vc, cache_b_ks, cache_b_vs,
           cache_c_k, cache_c_v, state_b_wk, state_b_wv, page_table, norm_mix, norm_ffn, norm_final,
           w_in, a_lambda, a_subln, b_cmp_pe, b_cmp_w1, b_cmp_w2, w_branch, w_out,
           peer_wq, peer_subkeys, peer_u, peer_v):
    bn, t, _ = x_prompt.shape
    db, ts, _ = x_sample.shape
    depth = w_in.shape[0]
    past = page_table.shape[1] * PAGE_SIZE
    wb = state_b_wk.shape[2]
    assert past % MOBA_BLOCK == 0 and ts <= SROWS and wb == WINDOW and wb % LANES == 0
    rows_view = lambda c: c.reshape(c.shape[0], c.shape[1], c.shape[2] * c.shape[3], c.shape[4])
    lanes_view = lambda c: jnp.transpose(c, (0, 1, 3, 4, 2)).reshape(
        c.shape[0], c.shape[1], c.shape[3] * c.shape[4], c.shape[2])
    caches = ([rows_view(c) for c in (cache_a_k, cache_a_v)]
              + [lanes_view(c) for c in (cache_b_kc, cache_b_vc, cache_b_ks, cache_b_vs, cache_c_k, cache_c_v)])
    swk = lanes_view(state_b_wk)
    swv = lanes_view(state_b_wv)

    cos_p, sin_p = _rope_tables(jnp.arange(t, dtype=jnp.int32))
    tm_s = min(ROW_TILE, db * SROWS)
    cos_s, sin_s = _rope_tables(past + (jnp.arange(tm_s, dtype=jnp.int32) % SROWS))

    xp = x_prompt.reshape(bn * t, D_MODEL)
    xs = jnp.pad(x_sample, ((0, 0), (0, SROWS - ts), (0, 0))).reshape(db * SROWS, D_MODEL)
    rows_p, rows_s = [], []
    for l in range(depth):
        lw = _layer_weights(l, norm_mix, norm_ffn, w_in, a_lambda, a_subln, b_cmp_pe, b_cmp_w1, b_cmp_w2,
                            w_branch, w_out, peer_wq, peer_subkeys, peer_u, peer_v)
        xp, rp = _prompt_layer(xp, lw, cos_p, sin_p, bn, t)
        xs, rs = _sample_layer(xs, lw, l, cos_s, sin_s, db, ts, past, caches, swk, swv, page_table)
        rows_p.append(rp)
        rows_s.append(rs)
    y_prompt = _final_norm(xp, norm_final[None, :], "final_p").reshape(bn, t, D_MODEL)
    y_sample = _final_norm(xs, norm_final[None, :], "final_s").reshape(db, SROWS, D_MODEL)[:, :ts]
    outs_p = [jnp.stack(r, axis=0) for r in zip(*rows_p)]
    outs_s = [jnp.stack(r, axis=0) for r in zip(*rows_s)]
    return (y_prompt, y_sample, *outs_p, *outs_s)
```

```python
import functools
import math

import jax
import jax.numpy as jnp
import numpy as np
from jax import lax
from jax.experimental import pallas as pl
from jax.experimental.pallas import tpu as pltpu

F32 = jnp.float32
BF16 = jnp.bfloat16

D_MODEL = 1024
HD = 64
A_HEADS, A_KV = 4, 2
B_HEADS, B_KV = 8, 2
C_HEADS, C_KV = 8, 4
CMP_BLOCK, CMP_STRIDE, CMP_HIDDEN = 32, 16, 128
SEL_BLOCK, SEL_TOPK = 64, 8
CMP_PER_SEL_SHIFT = 2
WINDOW = 512
MOBA_BLOCK, MOBA_TOPK = 256, 3
PEER_HEADS, PEER_NKEYS, PEER_TOPK = 8, 128, 16
ROPE_THETA = 10000.0
EPS = 1e-6
PAGE_SIZE = 128

LANES = 128
NEG = -1e30
VMEM_LIMIT = 48 * 1024 * 1024
ROW_TILE = 256
KV_CHUNK = 256
EXPERT_TILE = 1024
EXPERT_SUB = 256


def _params(*sem):
    return pltpu.CompilerParams(dimension_semantics=sem, vmem_limit_bytes=VMEM_LIMIT)


def _dot(a, b):
    return jnp.dot(a, b, preferred_element_type=F32)


def _dot_nt(a, b):
    return lax.dot_general(a, b, (((1,), (1,)), ((), ())), preferred_element_type=F32)


def _split3(a):
    hi = a.astype(BF16)
    r1 = a - hi.astype(F32)
    mid = r1.astype(BF16)
    lo = (r1 - mid.astype(F32)).astype(BF16)
    return hi, mid, lo


def _rms(x, w):
    return x * lax.rsqrt(jnp.mean(x * x, axis=-1, keepdims=True) + EPS) * w


def _proj_body(pieces, chunk, x_ref, nw_ref, w_ref, cos_ref, sin_ref, *outs):
    h = _rms(x_ref[...], nw_ref[...]).astype(BF16)
    cos = cos_ref[...]
    sin = sin_ref[...]
    lane = lax.broadcasted_iota(jnp.int32, (1, LANES), 1)
    first = (lane & (HD - 1)) < (HD // 2)
    n_cols = len(pieces) * LANES
    for c0 in range(0, n_cols, chunk):
        w = min(chunk, n_cols - c0)
        z = _dot(h, w_ref[:, c0:c0 + w])
        for p in range(w // LANES):
            zp = z[:, p * LANES:(p + 1) * LANES]
            for (oi, oc, rope) in pieces[c0 // LANES + p]:
                if rope:
                    rot = jnp.where(first, pltpu.roll(zp, LANES - HD // 2, 1), pltpu.roll(zp, HD // 2, 1))
                    outs[oi][:, oc:oc + LANES] = zp * cos + rot * sin
                else:
                    outs[oi][:, oc:oc + LANES] = zp


def _project(x, nw, w, cos_t, sin_t, pieces, out_widths, name):
    n = x.shape[0]
    tm = min(ROW_TILE, n)
    ntab = cos_t.shape[0] // tm
    ncols = w.shape[1]
    body = functools.partial(_proj_body, pieces, 512)
    return pl.pallas_call(
        body,
        grid=(n // tm,),
        in_specs=[
            pl.BlockSpec((tm, D_MODEL), lambda i: (i, 0)),
            pl.BlockSpec((1, D_MODEL), lambda i: (0, 0)),
            pl.BlockSpec((D_MODEL, ncols), lambda i: (0, 0)),
            pl.BlockSpec((tm, LANES), lambda i: (i % ntab, 0)),
            pl.BlockSpec((tm, LANES), lambda i: (i % ntab, 0)),
        ],
        out_specs=[pl.BlockSpec((tm, ow), lambda i: (i, 0)) for ow in out_widths],
        out_shape=[jax.ShapeDtypeStruct((n, ow), F32) for ow in out_widths],
        compiler_params=_params("parallel"),
        name=name,
    )(x, nw, w, cos_t, sin_t)


def _plain(oi, width):
    return [[(oi, c, False)] for c in range(0, width, LANES)]


def _roped(oi, width):
    return [[(oi, c, True)] for c in range(0, width, LANES)]


PIECES_A = _roped(0, 512) + _roped(1, 256) + _plain(2, 256)
WIDTHS_A = (512, 256, 256)
PIECES_B = ([[(0, c, False), (1, c, True)] for c in range(0, 512, LANES)]
            + _plain(2, 128) + _plain(3, 128) + _roped(4, 128) + _plain(5, 128)
            + _roped(6, 128) + _plain(7, 128) + _plain(8, 128))
WIDTHS_B = (512, 512, 128, 128, 128, 128, 128, 128, 128)
PIECES_C = _roped(0, 512) + _roped(1, 256) + _plain(2, 256)
WIDTHS_C = (512, 256, 256)
PIECES_G = _plain(0, 3072)
WIDTHS_G = (3072,)


def _rope_tables(pos):
    half = HD // 2
    freqs = jnp.power(ROPE_THETA, -jnp.arange(half, dtype=F32) / half)
    ang = pos.astype(F32)[:, None] * freqs[None, :]
    cos = jnp.cos(ang)
    sin = jnp.sin(ang)
    cos_t = jnp.concatenate([cos, cos, cos, cos], axis=1)
    sin_t = jnp.concatenate([-sin, sin, -sin, sin], axis=1)
    return cos_t, sin_t


def _flash_init(m_ref, l_ref, acc_ref):
    m_ref[...] = jnp.full(m_ref.shape, NEG, F32)
    l_ref[...] = jnp.zeros(l_ref.shape, F32)
    acc_ref[...] = jnp.zeros(acc_ref.shape, F32)


def _lane_blocks(x):
    return [x[:, c:c + LANES] for c in range(0, x.shape[1], LANES)]


def _flash_step(hh, q_emb, kc, vc, biases, m_ref, l_ref, acc_ref):
    blocks = _lane_blocks(_dot_nt(q_emb, kc))
    if biases is not None:
        blocks = [b if bi is None else b + bi for b, bi in zip(blocks, biases)]
    mx = blocks[0]
    for b in blocks[1:]:
        mx = jnp.maximum(mx, b)
    m_old = m_ref[hh]
    m_new = jnp.maximum(m_old, jnp.max(mx, axis=1, keepdims=True))
    alpha = jnp.exp(m_old - m_new)
    ps = [jnp.exp(b - m_new) for b in blocks]
    tot = ps[0]
    for p in ps[1:]:
        tot = tot + p
    l_ref[hh] = alpha * l_ref[hh] + jnp.sum(tot, axis=1, keepdims=True)
    acc_ref[hh] = alpha * acc_ref[hh] + _dot(jnp.concatenate(ps, axis=1).astype(BF16), vc)
    m_ref[hh] = m_new


def _causal_sweep(q0, tq, body):
    n_full = (q0 + 1) // KV_CHUNK
    n_pair = n_full // 2
    lax.fori_loop(0, n_pair, body(2 * KV_CHUNK, False), 0)
    lax.fori_loop(2 * n_pair, n_full, body(KV_CHUNK, False), 0)
    lax.fori_loop(n_full, (q0 + tq - 1) // KV_CHUNK + 1, body(KV_CHUNK, True), 0)


def _causal_bias(j, tk, qpos, inside=0.0):
    kpos = j * tk + lax.broadcasted_iota(jnp.int32, (1, tk), 1)
    return jnp.where(kpos <= qpos, inside, NEG)


def _flash_out(hh, l_ref, acc_ref):
    return acc_ref[hh] / jnp.maximum(l_ref[hh], 1e-30)


def _chunk(ref, j, tk):
    return ref[pl.ds(pl.multiple_of(j * tk, tk), tk), :].astype(BF16)


def _diff_body(tq, tk, q_pos0, lam_init, lam_ref, subln_ref, q_ref, k_ref, v_ref, o_ref,
               m_ref, l_ref, acc_ref):
    qi = pl.program_id(2)
    lane = lax.broadcasted_iota(jnp.int32, (1, LANES), 1)
    upper = lane >= HD
    q0 = q_pos0 + qi * tq
    qpos = q0 + lax.broadcasted_iota(jnp.int32, (tq, 1), 0)
    _flash_init(m_ref, l_ref, acc_ref)
    qs = []
    for r in range(2):
        qb = q_ref[:, r * LANES:(r + 1) * LANES] * (HD ** -0.5)
        qs.append(jnp.where(upper, 0.0, qb).astype(BF16))
        qs.append(jnp.where(upper, qb, 0.0).astype(BF16))
    def sweep(tk, diag):
        def body(j, carry):
            kc = _chunk(k_ref, j, tk)
            vc = _chunk(v_ref, j, tk)
            biases = _lane_blocks(_causal_bias(j, tk, qpos)) if diag else None
            for hh in range(4):
                _flash_step(hh, qs[hh], kc, vc, biases, m_ref, l_ref, acc_ref)
            return carry
        return body

    _causal_sweep(q0, tq, sweep)
    lp = lam_ref[...]
    lam = (jnp.exp(jnp.sum(lp[0:1] * lp[1:2], axis=1, keepdims=True))
           - jnp.exp(jnp.sum(lp[2:3] * lp[3:4], axis=1, keepdims=True)) + lam_init)
    for r in range(2):
        o = _flash_out(2 * r, l_ref, acc_ref) - lam * _flash_out(2 * r + 1, l_ref, acc_ref)
        o_ref[:, r * LANES:(r + 1) * LANES] = _rms(o, subln_ref[...]) * (1.0 - lam_init)


def _diff_attention(aq, ak, av, lam_p, subln, lam_init, q_pos0, name):
    bn, tq_all, _ = aq.shape
    lk = ak.shape[1]
    tq = min(ROW_TILE, tq_all)
    tk = KV_CHUNK
    body = functools.partial(_diff_body, tq, tk, q_pos0, lam_init)
    return pl.pallas_call(
        body,
        grid=(bn, A_KV, tq_all // tq),
        in_specs=[
            pl.BlockSpec((4, HD), lambda b, g, i: (0, 0)),
            pl.BlockSpec((1, 2 * HD), lambda b, g, i: (0, 0)),
            pl.BlockSpec((None, tq, 256), lambda b, g, i: (b, i, g)),
            pl.BlockSpec((None, lk, LANES), lambda b, g, i: (b, 0, g)),
            pl.BlockSpec((None, lk, LANES), lambda b, g, i: (b, 0, g)),
        ],
        out_specs=pl.BlockSpec((None, tq, 256), lambda b, g, i: (b, i, g)),
        out_shape=jax.ShapeDtypeStruct((bn, tq_all, 512), F32),
        scratch_shapes=[pltpu.VMEM((4, tq, LANES), F32), pltpu.VMEM((4, tq, LANES), F32),
                        pltpu.VMEM((4, tq, LANES), F32)],
        compiler_params=_params("parallel", "parallel", "arbitrary"),
        name=name,
    )(lam_p, subln, aq, ak, av)


def _compress_one(x_ref, pe_ref, w1_ref, w2_ref, o_ref):
    n_half = CMP_STRIDE
    ha = jnp.zeros((LANES, 2 * CMP_HIDDEN), F32)
    hb = jnp.zeros((LANES, 2 * CMP_HIDDEN), F32)
    for p in range(n_half):
        x = x_ref[pl.ds(p, LANES, stride=CMP_STRIDE), :]
        ha = ha + _dot((x + pe_ref[p]).astype(BF16), w1_ref[p])
        hb = hb + _dot((x + pe_ref[n_half + p]).astype(BF16), w1_ref[n_half + p])
    h = ha + pltpu.roll(hb, LANES - 1, 0)
    y = _dot(jax.nn.gelu(h).astype(BF16), w2_ref[...])
    row = lax.broadcasted_iota(jnp.int32, (LANES, 1), 0)
    o_ref[...] = jnp.where(row < LANES - 1, y, 0.0)


def _compress_body(xk_ref, xv_ref, pek_ref, pev_ref, w1k_ref, w1v_ref, w2k_ref, w2v_ref, ok_ref, ov_ref):
    _compress_one(xk_ref, pek_ref, w1k_ref, w2k_ref, ok_ref)
    _compress_one(xv_ref, pev_ref, w1v_ref, w2v_ref, ov_ref)


def _compress(xk, xv, cw, name):
    bn = xk.shape[0]
    n_tok = LANES * CMP_STRIDE
    x_spec = pl.BlockSpec((None, n_tok, LANES), lambda b: (b, 0, 0))
    pe_spec = pl.BlockSpec((CMP_BLOCK, 1, LANES), lambda b: (0, 0, 0))
    w1_spec = pl.BlockSpec((CMP_BLOCK, LANES, 2 * CMP_HIDDEN), lambda b: (0, 0, 0))
    w2_spec = pl.BlockSpec((2 * CMP_HIDDEN, LANES), lambda b: (0, 0))
    o_spec = pl.BlockSpec((None, LANES, LANES), lambda b: (b, 0, 0))
    return pl.pallas_call(
        _compress_body,
        grid=(bn,),
        in_specs=[x_spec, x_spec, pe_spec, pe_spec, w1_spec, w1_spec, w2_spec, w2_spec],
        out_specs=[o_spec, o_spec],
        out_shape=[jax.ShapeDtypeStruct((bn, LANES, LANES), F32)] * 2,
        compiler_params=_params("parallel"),
        name=name,
    )(xk, xv, cw["pek"], cw["pev"], cw["w1k"], cw["w1v"], cw["w2k"], cw["w2v"])


def _compress_weights(pe, w1, w2):
    out = {}
    for idx, tag in ((0, "k"), (1, "v")):
        w1p = w1[idx].reshape(CMP_BLOCK, HD, CMP_HIDDEN)
        z = jnp.zeros_like(w1p)
        w1b = jnp.concatenate([jnp.concatenate([w1p, z], axis=2), jnp.concatenate([z, w1p], axis=2)], axis=1)
        z2 = jnp.zeros_like(w2[idx])
        w2b = jnp.concatenate([jnp.concatenate([w2[idx], z2], axis=1), jnp.concatenate([z2, w2[idx]], axis=1)], axis=0)
        out["w1" + tag] = w1b.astype(BF16)
        out["w2" + tag] = w2b.astype(BF16)
        out["pe" + tag] = jnp.concatenate([pe[idx], pe[idx]], axis=1)[:, None, :]
    return out


def _rank_desc(vals, n, lane):
    rank = jnp.zeros(vals.shape, F32)
    for i in range(n):
        col = vals[:, i:i + 1]
        tie = jnp.where(lane > i, 1.0, 0.0)
        rank = rank + jnp.where(col > vals, 1.0, 0.0) + jnp.where(col == vals, tie, 0.0)
    return rank


def _nsa_body(tq, tk, q_pos0, ns, w_pos0, w_valid, lw,
              qc_ref, qr_ref, bg_ref, kcmp_ref, vcmp_ref, ks_ref, vs_ref, kw_ref, vw_ref, o_ref,
              m_ref, l_ref, acc_ref):
    g = pl.program_id(1)
    qi = pl.program_id(2)
    lane = lax.broadcasted_iota(jnp.int32, (1, LANES), 1)
    halfi = lane >> 6
    q0 = q_pos0 + qi * tq
    qpos = q0 + lax.broadcasted_iota(jnp.int32, (tq, 1), 0)
    rep = B_HEADS // B_KV

    def embed(ref, r):
        v = ref[:, (r // 2) * LANES:(r // 2 + 1) * LANES] * (HD ** -0.5)
        own = jnp.where(halfi == (r % 2), v, 0.0)
        dup = own + pltpu.roll(own, HD, 1)
        return jnp.where(halfi == g, dup, 0.0).astype(BF16)

    def pick(o, r):
        return jnp.where(g == (r % 2), o, pltpu.roll(o, HD, 1))

    kcmp = kcmp_ref[...].astype(BF16)
    vcmp = vcmp_ref[...].astype(BF16)
    cmask = (lane * CMP_STRIDE + (CMP_BLOCK - 1)) <= qpos
    psum = jnp.zeros((tq, LANES), F32)
    o_cmp = []
    for r in range(rep):
        s = jnp.where(cmask, _dot_nt(embed(qc_ref, r), kcmp), NEG)
        m = jnp.max(s, axis=1, keepdims=True)
        p = jnp.where(cmask, jnp.exp(s - m), 0.0)
        p = p / jnp.maximum(jnp.sum(p, axis=1, keepdims=True), 1e-30)
        psum = psum + p
        o_cmp.append(_dot(p.astype(BF16), vcmp))

    ci = lax.broadcasted_iota(jnp.int32, (LANES, LANES), 0)
    cj = lax.broadcasted_iota(jnp.int32, (LANES, LANES), 1)
    onehot = jnp.where((ci >> CMP_PER_SEL_SHIFT) == cj, 1.0, 0.0).astype(BF16)
    hi, mid, lo = _split3(psum)
    imp = _dot(hi, onehot) + _dot(mid, onehot) + _dot(lo, onehot)
    cur = qpos >> 6
    impm = jnp.where(lane == cur, jnp.inf, jnp.where(lane < cur, imp, -jnp.inf))
    rank = _rank_desc(impm, ns, lane)
    sel = jnp.where(rank < float(SEL_TOPK), jnp.where(lane <= cur, 1.0, 0.0), 0.0).astype(BF16)

    qsr = [embed(qr_ref, r) for r in range(rep)]
    _flash_init(m_ref, l_ref, acc_ref)
    def sel_sweep(ck, diag):
        bpc = ck // SEL_BLOCK

        def body(j, carry):
            kc = _chunk(ks_ref, j, ck)
            vc = _chunk(vs_ref, j, ck)
            ei = lax.broadcasted_iota(jnp.int32, (LANES, ck), 0)
            el = lax.broadcasted_iota(jnp.int32, (LANES, ck), 1)
            expand = jnp.where(ei == j * bpc + (el >> 6), 1.0, 0.0).astype(BF16)
            bias = (_dot(sel, expand) - 1.0) * (-NEG)
            if diag:
                bias = _causal_bias(j, ck, qpos, bias)
            biases = _lane_blocks(bias)
            for r in range(rep):
                _flash_step(r, qsr[r], kc, vc, biases, m_ref, l_ref, acc_ref)
            return carry
        return body

    _causal_sweep(q0, tq, sel_sweep)
    o_slc = [_flash_out(r, l_ref, acc_ref) for r in range(rep)]

    _flash_init(m_ref, l_ref, acc_ref)
    w_lo = jnp.maximum(q0 - WINDOW - w_pos0, 0) // tk
    w_hi = jnp.minimum(q0 + tq - 1 - w_pos0, lw - 1) // tk + 1

    def win_body(j, carry):
        kc = _chunk(kw_ref, j, tk)
        vc = _chunk(vw_ref, j, tk)
        kidx = j * tk + lax.broadcasted_iota(jnp.int32, (1, tk), 1)
        dist = qpos - (kidx + w_pos0)
        inwin = jnp.where(dist >= 0, jnp.where(dist <= WINDOW, 0.0, NEG), NEG)
        biases = _lane_blocks(jnp.where(kidx < w_valid, inwin, NEG))
        for r in range(rep):
            _flash_step(r, qsr[r], kc, vc, biases, m_ref, l_ref, acc_ref)
        return carry

    lax.fori_loop(w_lo, w_hi, win_body, 0)

    sig = jax.nn.sigmoid(bg_ref[...])
    outs = []
    for r in range(rep):
        o_win = _flash_out(r, l_ref, acc_ref)
        base = (g * rep + r) * 3
        gates = [jnp.sum(jnp.where(lane == base + c, sig, 0.0), axis=1, keepdims=True) for c in range(3)]
        outs.append(pick(gates[0] * o_cmp[r] + gates[1] * o_slc[r] + gates[2] * o_win, r))
    for pr in range(rep // 2):
        o_ref[:, pr * LANES:(pr + 1) * LANES] = jnp.where(halfi == 0, outs[2 * pr], outs[2 * pr + 1])


def _nsa_attention(bqc, bqr, bg, kcmp, vcmp, ks, vs, kw, vw, q_pos0, k_valid, w_pos0, w_valid, name):
    bn, tq_all, _ = bqc.shape
    lk = ks.shape[1]
    lw = kw.shape[1]
    tq = min(ROW_TILE, tq_all)
    tk = KV_CHUNK
    ns = -(-k_valid // SEL_BLOCK)
    body = functools.partial(_nsa_body, tq, tk, q_pos0, ns, w_pos0, w_valid, lw)
    q_spec = pl.BlockSpec((None, tq, 256), lambda b, g, i: (b, i, g))
    full = lambda rows: pl.BlockSpec((None, rows, LANES), lambda b, g, i: (b, 0, 0))
    return pl.pallas_call(
        body,
        grid=(bn, B_KV, tq_all // tq),
        in_specs=[q_spec, q_spec, pl.BlockSpec((None, tq, LANES), lambda b, g, i: (b, i, 0)),
                  full(LANES), full(LANES), full(lk), full(lk), full(lw), full(lw)],
        out_specs=pl.BlockSpec((None, tq, 256), lambda b, g, i: (b, i, g)),
        out_shape=jax.ShapeDtypeStruct((bn, tq_all, 512), F32),
        scratch_shapes=[pltpu.VMEM((4, tq, LANES), F32), pltpu.VMEM((4, tq, LANES), F32),
                        pltpu.VMEM((4, tq, LANES), F32)],
        compiler_params=_params("parallel", "parallel", "arbitrary"),
        name=name,
    )(bqc, bqr, bg, kcmp, vcmp, ks, vs, kw, vw)


def _moba_body(tq, tk, q_pos0, nb, q_ref, k_ref, v_ref, o_ref, km_ref, m_ref, l_ref, acc_ref):
    g = pl.program_id(1)
    qi = pl.program_id(2)
    gh = g % 2
    lane = lax.broadcasted_iota(jnp.int32, (1, LANES), 1)
    halfi = lane >> 6
    q0 = q_pos0 + qi * tq
    qpos = q0 + lax.broadcasted_iota(jnp.int32, (tq, 1), 0)
    cur = qpos >> 8
    rep = C_HEADS // C_KV

    @pl.when(qi == 0)
    def _():
        km_ref[...] = jnp.zeros(km_ref.shape, F32)
        km_ref[0:nb, :] = jnp.sum(k_ref[...].reshape(nb, MOBA_BLOCK, LANES), axis=1) * (1.0 / MOBA_BLOCK)

    km_hi, km_mid, _ = _split3(km_ref[...])

    qv = q_ref[...]
    qs = []
    sels = []
    for r in range(rep):
        own = jnp.where(halfi == r, qv, 0.0)
        dup = own + pltpu.roll(own, HD, 1)
        qe = jnp.where(halfi == gh, dup, 0.0)
        q_hi, q_mid, _ = _split3(qe)
        s_blk = _dot_nt(q_hi, km_hi) + _dot_nt(q_hi, km_mid) + _dot_nt(q_mid, km_hi)
        sm = jnp.where(lane < cur, s_blk, -jnp.inf)
        rank = _rank_desc(sm, nb, lane)
        past = jnp.where(rank < float(MOBA_TOPK), jnp.where(lane < cur, 1.0, 0.0), 0.0)
        sels.append(jnp.where(lane == cur, 1.0, past))
        qs.append((qe * (HD ** -0.5)).astype(BF16))

    _flash_init(m_ref, l_ref, acc_ref)
    def sweep(ck, diag):
        bpc = ck // MOBA_BLOCK
        lpb = MOBA_BLOCK // LANES

        def body(j, carry):
            kc = _chunk(k_ref, j, ck)
            vc = _chunk(v_ref, j, ck)
            diag_biases = _lane_blocks(_causal_bias(j, ck, qpos)) if diag else None
            for r in range(rep):
                if diag:
                    biases = diag_biases
                else:
                    biases = []
                    for u in range(bpc):
                        col = jnp.sum(jnp.where(lane == j * bpc + u, sels[r], 0.0), axis=1, keepdims=True)
                        biases += [(col - 1.0) * (-NEG)] * lpb
                _flash_step(r, qs[r], kc, vc, biases, m_ref, l_ref, acc_ref)
            return carry
        return body

    _causal_sweep(q0, tq, sweep)
    outs = []
    for r in range(rep):
        o = _flash_out(r, l_ref, acc_ref)
        outs.append(jnp.where(gh == r, o, pltpu.roll(o, HD, 1)))
    o_ref[...] = jnp.where(halfi == 0, outs[0], outs[1])


def _moba_attention(cq, ck, cv, q_pos0, name):
    bn, tq_all, _ = cq.shape
    lk = ck.shape[1]
    tq = min(ROW_TILE, tq_all)
    tk = MOBA_BLOCK
    nb = lk // MOBA_BLOCK
    assert tq == MOBA_BLOCK == KV_CHUNK and q_pos0 % MOBA_BLOCK == 0
    body = functools.partial(_moba_body, tq, tk, q_pos0, nb)
    return pl.pallas_call(
        body,
        grid=(bn, C_KV, tq_all // tq),
        in_specs=[
            pl.BlockSpec((None, tq, LANES), lambda b, g, i: (b, i, g)),
            pl.BlockSpec((None, lk, LANES), lambda b, g, i: (b, 0, g // 2)),
            pl.BlockSpec((None, lk, LANES), lambda b, g, i: (b, 0, g // 2)),
        ],
        out_specs=pl.BlockSpec((None, tq, LANES), lambda b, g, i: (b, i, g)),
        out_shape=jax.ShapeDtypeStruct((bn, tq_all, 512), F32),
        scratch_shapes=[pltpu.VMEM((LANES, LANES), F32),
                        pltpu.VMEM((2, tq, LANES), F32), pltpu.VMEM((2, tq, LANES), F32),
                        pltpu.VMEM((2, tq, LANES), F32)],
        compiler_params=_params("parallel", "parallel", "arbitrary"),
        name=name,
    )(cq, ck, cv)


def _merge_body(x_ref, gate_ref, oa_ref, ob_ref, oc_ref, wb_ref, wo_ref, y_ref):
    y = jnp.zeros((x_ref.shape[0], D_MODEL), F32)
    for c, o_ref in enumerate((oa_ref, ob_ref, oc_ref)):
        br = _dot(o_ref[...].astype(BF16), wb_ref[c])
        y = y + jax.nn.sigmoid(gate_ref[:, c * D_MODEL:(c + 1) * D_MODEL]) * br
    y_ref[...] = x_ref[...] + _dot(y.astype(BF16), wo_ref[...])


def _merge(x, gate, o_a, o_b, o_c, wb, wo, name):
    n = x.shape[0]
    tm = min(ROW_TILE, n)
    row = lambda w: pl.BlockSpec((tm, w), lambda i: (i, 0))
    return pl.pallas_call(
        _merge_body,
        grid=(n // tm,),
        in_specs=[row(D_MODEL), row(3 * D_MODEL), row(512), row(512), row(512),
                  pl.BlockSpec((3, 512, D_MODEL), lambda i: (0, 0, 0)),
                  pl.BlockSpec((D_MODEL, D_MODEL), lambda i: (0, 0))],
        out_specs=row(D_MODEL),
        out_shape=jax.ShapeDtypeStruct((n, D_MODEL), F32),
        compiler_params=_params("parallel"),
        name=name,
    )(x, gate, o_a, o_b, o_c, wb, wo)


def _peer_stats_body(tm, x_ref, nw_ref, wq_ref, sk_ref, st_ref, stat_ref, top_ref, nxt_ref):
    h = _rms(x_ref[...], nw_ref[...]).astype(BF16)
    q = _dot(h, wq_ref[...]).astype(BF16)
    n_hc = 2 * PEER_HEADS
    for hc in range(n_hc):
        st_ref[hc] = _dot_nt(sk_ref[hc], q[:, hc * LANES:(hc + 1) * LANES])

    def top_body(hc, carry):
        for hf in range(tm // LANES):
            s = st_ref[hc, :, hf * LANES:(hf + 1) * LANES]
            vals = []
            for _ in range(PEER_TOPK + 1):
                m = jnp.max(s, axis=0, keepdims=True)
                vals.append(m)
                s = jnp.where(s == m, -jnp.inf, s)
            top_ref[hc, :, hf * LANES:(hf + 1) * LANES] = jnp.concatenate(vals[:PEER_TOPK], axis=0)
            nxt_ref[hc, :, hf * LANES:(hf + 1) * LANES] = vals[PEER_TOPK]
        return carry

    lax.fori_loop(0, n_hc, top_body, 0)

    row16 = lax.broadcasted_iota(jnp.int32, (PEER_TOPK, LANES), 0)

    def head_body(hd, carry):
        for hf in range(tm // LANES):
            sl = slice(hf * LANES, (hf + 1) * LANES)
            a = top_ref[2 * hd, :, sl]
            b = top_ref[2 * hd + 1, :, sl]
            parts = []
            for i in range(PEER_TOPK):
                lim = PEER_TOPK // (i + 1)
                rows = PEER_TOPK if lim > 8 else 8
                cs = a[i:i + 1] + b[:rows]
                parts.append(jnp.where(row16[:rows] < lim, cs, -jnp.inf))
            cand = jnp.concatenate(parts, axis=0)
            mx = a[0:1] + b[0:1]
            cum = jnp.zeros((1, LANES), F32)
            tau = mx
            nxt = mx
            work = cand
            for _ in range(PEER_TOPK + 1):
                m = jnp.max(work, axis=0, keepdims=True)
                eq = work == m
                tau = jnp.where(cum < float(PEER_TOPK), m, tau)
                nxt = jnp.where(cum < float(PEER_TOPK + 1), m, nxt)
                cum = cum + jnp.sum(jnp.where(eq, 1.0, 0.0), axis=0, keepdims=True)
                work = jnp.where(eq, -jnp.inf, work)
            nxt = jnp.maximum(nxt, jnp.maximum(a[0:1] + nxt_ref[2 * hd + 1, :, sl], nxt_ref[2 * hd, :, sl] + b[0:1]))
            z = jnp.sum(jnp.where(cand >= tau, jnp.exp(cand - mx), 0.0), axis=0, keepdims=True)
            stat_ref[hd, :, sl] = 0.5 * (tau + nxt)
            stat_ref[PEER_HEADS + hd, :, sl] = a[0:1]
            stat_ref[2 * PEER_HEADS + hd, :, sl] = b[0:1]
            stat_ref[3 * PEER_HEADS + hd, :, sl] = 1.0 / z
        return carry

    lax.fori_loop(0, PEER_HEADS, head_body, 0)


def _peer_stats(x, nw, wq, sk, name):
    n = x.shape[0]
    tm = min(ROW_TILE, n)
    n_hc = 2 * PEER_HEADS
    return pl.pallas_call(
        functools.partial(_peer_stats_body, tm),
        grid=(n // tm,),
        in_specs=[pl.BlockSpec((tm, D_MODEL), lambda i: (i, 0)),
                  pl.BlockSpec((1, D_MODEL), lambda i: (0, 0)),
                  pl.BlockSpec((D_MODEL, n_hc * LANES), lambda i: (0, 0)),
                  pl.BlockSpec((n_hc, LANES, LANES), lambda i: (0, 0, 0))],
        out_specs=[pl.BlockSpec((n_hc, LANES, tm), lambda i: (0, 0, i)),
                   pl.BlockSpec((4 * PEER_HEADS, 1, tm), lambda i: (0, 0, i))],
        out_shape=[jax.ShapeDtypeStruct((n_hc, LANES, n), F32),
                   jax.ShapeDtypeStruct((4 * PEER_HEADS, 1, n), F32)],
        scratch_shapes=[pltpu.VMEM((n_hc, PEER_TOPK, tm), F32), pltpu.VMEM((n_hc, 1, tm), F32)],
        compiler_params=_params("parallel"),
        name=name,
    )(x, nw, wq, sk)


def _peer_dense_body(tm, te, x_ref, nw_ref, st_ref, stat_ref, u_ref, vt_ref, o_ref,
                     h_ref, thr_ref, w1_ref, s2c_ref, e2_ref, acc_ref):
    e = pl.program_id(1)

    @pl.when(e == 0)
    def _():
        h_ref[...] = _rms(x_ref[...], nw_ref[...]).astype(BF16)
        acc_ref[...] = jnp.zeros(acc_ref.shape, F32)
        for hd in range(PEER_HEADS):
            tau = stat_ref[hd]
            m1 = stat_ref[PEER_HEADS + hd]
            m2 = stat_ref[2 * PEER_HEADS + hd]
            inv_z = stat_ref[3 * PEER_HEADS + hd]
            thr_ref[hd] = m1 - st_ref[2 * hd]
            w1_ref[hd] = jnp.exp(st_ref[2 * hd] - m1) * inv_z
            s2c_ref[hd] = (st_ref[2 * hd + 1] - (tau - m1)).astype(BF16)
            e2_ref[hd] = jnp.exp(st_ref[2 * hd + 1] - m2).astype(BF16)

    n_i = te // PEER_NKEYS
    zero = jnp.zeros((), BF16)
    h = h_ref[...]
    total = None
    for k in range(te // EXPERT_SUB):
        rows = slice(k * EXPERT_SUB, (k + 1) * EXPERT_SUB)
        act = jax.nn.gelu(_dot_nt(u_ref[rows, :], h)).astype(BF16)
        parts = []
        for ii in range(EXPERT_SUB // PEER_NKEYS):
            i = e * n_i + k * (EXPERT_SUB // PEER_NKEYS) + ii
            gmat = jnp.zeros((PEER_NKEYS, tm), BF16)
            for hd in range(PEER_HEADS):
                thr = thr_ref[hd, pl.ds(i, 1), :].astype(BF16)
                w1 = w1_ref[hd, pl.ds(i, 1), :].astype(BF16)
                gmat = gmat + jnp.where(s2c_ref[hd] > thr, e2_ref[hd] * w1, zero)
            parts.append(gmat * act[ii * PEER_NKEYS:(ii + 1) * PEER_NKEYS])
        t = _dot(vt_ref[:, rows], jnp.concatenate(parts, axis=0))
        total = t if total is None else total + t
    acc_ref[...] += total

    @pl.when(e == pl.num_programs(1) - 1)
    def _():
        o_ref[...] = x_ref[...] + acc_ref[...].T


def _peer_dense(x, nw, st, stat, u_bf, vt_bf, name):
    n = x.shape[0]
    tm = min(ROW_TILE, n)
    te = EXPERT_TILE
    n_exp = u_bf.shape[0]
    n_hc = 2 * PEER_HEADS
    return pl.pallas_call(
        functools.partial(_peer_dense_body, tm, te),
        grid=(n // tm, n_exp // te),
        in_specs=[pl.BlockSpec((tm, D_MODEL), lambda i, e: (i, 0)),
                  pl.BlockSpec((1, D_MODEL), lambda i, e: (0, 0)),
                  pl.BlockSpec((n_hc, LANES, tm), lambda i, e: (0, 0, i)),
                  pl.BlockSpec((4 * PEER_HEADS, 1, tm), lambda i, e: (0, 0, i)),
                  pl.BlockSpec((te, D_MODEL), lambda i, e: (e, 0)),
                  pl.BlockSpec((D_MODEL, te), lambda i, e: (0, e))],
        out_specs=pl.BlockSpec((tm, D_MODEL), lambda i, e: (i, 0)),
        out_shape=jax.ShapeDtypeStruct((n, D_MODEL), F32),
        scratch_shapes=[pltpu.VMEM((tm, D_MODEL), BF16),
                        pltpu.VMEM((PEER_HEADS, PEER_NKEYS, tm), F32),
                        pltpu.VMEM((PEER_HEADS, PEER_NKEYS, tm), F32),
                        pltpu.VMEM((PEER_HEADS, PEER_NKEYS, tm), BF16),
                        pltpu.VMEM((PEER_HEADS, PEER_NKEYS, tm), BF16),
                        pltpu.VMEM((D_MODEL, tm), F32)],
        compiler_params=_params("parallel", "arbitrary"),
        name=name,
    )(x, nw, st, stat, u_bf, vt_bf)


def _norm_body(x_ref, w_ref, o_ref):
    o_ref[...] = _rms(x_ref[...], w_ref[...])


def _final_norm(x, w, name):
    n = x.shape[0]
    tm = min(ROW_TILE, n)
    return pl.pallas_call(
        _norm_body,
        grid=(n // tm,),
        in_specs=[pl.BlockSpec((tm, D_MODEL), lambda i: (i, 0)), pl.BlockSpec((1, D_MODEL), lambda i: (0, 0))],
        out_specs=pl.BlockSpec((tm, D_MODEL), lambda i: (i, 0)),
        out_shape=jax.ShapeDtypeStruct((n, D_MODEL), F32),
        compiler_params=_params("parallel"),
        name=name,
    )(x, w)


SROWS = 8


def _page_specs(n_pages, rows, layer):
    def spec(p):
        return pl.BlockSpec((None, None, rows, LANES), lambda b, pt: (layer, pt[b, p], 0, 0))
    return [spec(p) for p in range(n_pages)]


def _row_spec(width):
    return pl.BlockSpec((None, SROWS, width), lambda b, pt: (b, 0, 0))


def _dup(x):
    return jnp.concatenate([x, x], axis=0)


def _pad_rows(x):
    return jnp.concatenate([x, jnp.zeros((PAGE_SIZE - x.shape[0], x.shape[1]), F32)], axis=0)


def _new_mask(m_rows, n_new):
    lane = lax.broadcasted_iota(jnp.int32, (1, LANES), 1)
    trow = lax.broadcasted_iota(jnp.int32, (m_rows, 1), 0) & (SROWS - 1)
    return jnp.where(lane < n_new, jnp.where(lane <= trow, 1.0, 0.0), 0.0) > 0.5


def _softmax_pieces(pieces):
    mx = None
    for s, mk in pieces:
        sm = s if mk is None else jnp.where(mk, s, NEG)
        mx = sm if mx is None else jnp.maximum(mx, sm)
    m = jnp.max(mx, axis=1, keepdims=True)
    ps = []
    tot = None
    for s, mk in pieces:
        p = jnp.exp(s - m)
        if mk is not None:
            p = jnp.where(mk, p, 0.0)
        tot = p if tot is None else tot + p
        ps.append(p.astype(BF16))
    return ps, jnp.sum(tot, axis=1, keepdims=True)


def _scores_kt(q, kts):
    return [_dot(q, _dup(kt.astype(BF16))) for kt in kts]


def _values_kt(ps, vts):
    o = None
    for p, vt in zip(ps, vts):
        t = _dot_nt(p, _dup(vt.astype(BF16)))
        o = t if o is None else o + t
    return o


def _diff_s_body(n_pages, n_new, lam_init, pt_ref, lam_ref, subln_ref, q_ref, kn_ref, vn_ref, *rest):
    k_refs = rest[:n_pages]
    v_refs = rest[n_pages:2 * n_pages]
    o_ref = rest[2 * n_pages]
    lane = lax.broadcasted_iota(jnp.int32, (1, LANES), 1)
    upper = lane >= HD
    new_mask = _new_mask(4 * SROWS, n_new)
    lp = lam_ref[...]
    lam = (jnp.exp(jnp.sum(lp[0:1] * lp[1:2], axis=1, keepdims=True))
           - jnp.exp(jnp.sum(lp[2:3] * lp[3:4], axis=1, keepdims=True)) + lam_init)
    for g in range(A_KV):
        parts = []
        for r in range(A_HEADS // A_KV):
            qb = q_ref[:, (2 * g + r) * LANES:(2 * g + r + 1) * LANES] * (HD ** -0.5)
            parts += [jnp.where(upper, 0.0, qb), jnp.where(upper, qb, 0.0)]
        q = jnp.concatenate(parts, axis=0).astype(BF16)
        pieces = [(_dot_nt(q, k_refs[p][pl.ds(g, PAGE_SIZE, stride=A_KV), :].astype(BF16)), None)
                  for p in range(n_pages)]
        knew = _pad_rows(kn_ref[:, g * LANES:(g + 1) * LANES]).astype(BF16)
        vnew = _pad_rows(vn_ref[:, g * LANES:(g + 1) * LANES]).astype(BF16)
        pieces.append((_dot_nt(q, knew), new_mask))
        ps, tot = _softmax_pieces(pieces)
        o = _dot(ps[n_pages], vnew)
        for p in range(n_pages):
            o = o + _dot(ps[p], v_refs[p][pl.ds(g, PAGE_SIZE, stride=A_KV), :].astype(BF16))
        o = o / tot
        for r in range(A_HEADS // A_KV):
            d = o[2 * r * SROWS:(2 * r + 1) * SROWS] - lam * o[(2 * r + 1) * SROWS:(2 * r + 2) * SROWS]
            o_ref[:, (2 * g + r) * LANES:(2 * g + r + 1) * LANES] = _rms(d, subln_ref[...]) * (1.0 - lam_init)


def _diff_sample(page_table, layer, n_new, aq, ak_new, av_new, cache_k, cache_v, lam_p, subln, lam_init, name):
    db, n_pages = page_table.shape
    const = lambda shape: pl.BlockSpec(shape, lambda b, pt: (0,) * len(shape))
    in_specs = ([const((4, HD)), const((1, 2 * HD)), _row_spec(512), _row_spec(256), _row_spec(256)]
                + _page_specs(n_pages, PAGE_SIZE * A_KV, layer) + _page_specs(n_pages, PAGE_SIZE * A_KV, layer))
    grid_spec = pltpu.PrefetchScalarGridSpec(num_scalar_prefetch=1, grid=(db,), in_specs=in_specs,
                                             out_specs=_row_spec(512))
    return pl.pallas_call(
        functools.partial(_diff_s_body, n_pages, n_new, lam_init),
        grid_spec=grid_spec,
        out_shape=jax.ShapeDtypeStruct((db, SROWS, 512), F32),
        compiler_params=_params("parallel"),
        name=name,
    )(page_table, lam_p, subln, aq, ak_new, av_new, *([cache_k] * n_pages), *([cache_v] * n_pages))


def _moba_s_body(n_pages, n_new, pt_ref, q_ref, kn_ref, vn_ref, *rest):
    k_refs = rest[:n_pages]
    v_refs = rest[n_pages:2 * n_pages]
    o_ref = rest[2 * n_pages]
    lane = lax.broadcasted_iota(jnp.int32, (1, LANES), 1)
    halfi = lane >> 6
    rep = C_HEADS // C_KV
    new_mask = _new_mask(rep * SROWS, n_new)
    knt = _pad_rows(kn_ref[...]).T
    vnt = _pad_rows(vn_ref[...]).T
    ppb = MOBA_BLOCK // PAGE_SIZE
    n_blocks = n_pages // ppb
    for g in range(C_KV):
        rows = slice(g * HD, (g + 1) * HD)
        qb = q_ref[:, g * LANES:(g + 1) * LANES] * (HD ** -0.5)
        q = jnp.concatenate([jnp.where(halfi == r, qb, 0.0) for r in range(rep)], axis=0).astype(BF16)
        ss = _scores_kt(q, [k_refs[p][rows, :] for p in range(n_pages)] + [knt[rows, :]])
        bs = []
        for j in range(n_blocks):
            acc = ss[j * ppb]
            for u in range(1, ppb):
                acc = acc + ss[j * ppb + u]
            bs.append(jnp.sum(acc, axis=1, keepdims=True))
        pieces = []
        for j in range(n_blocks):
            rank = jnp.zeros(bs[j].shape, F32)
            for i in range(n_blocks):
                if i < j:
                    rank = rank + jnp.where(bs[i] >= bs[j], 1.0, 0.0)
                elif i > j:
                    rank = rank + jnp.where(bs[i] > bs[j], 1.0, 0.0)
            keep = rank < float(MOBA_TOPK)
            for u in range(ppb):
                pieces.append((ss[j * ppb + u], keep))
        pieces.append((ss[n_pages], new_mask))
        ps, tot = _softmax_pieces(pieces)
        o = _values_kt(ps, [v_refs[p][rows, :] for p in range(n_pages)] + [vnt[rows, :]]) / tot
        o_ref[:, g * LANES:(g + 1) * LANES] = jnp.where(halfi == 0, o[0:SROWS], o[SROWS:2 * SROWS])


def _moba_sample(page_table, layer, n_new, cq, ck_new, cv_new, cache_k, cache_v, name):
    db, n_pages = page_table.shape
    in_specs = ([_row_spec(512), _row_spec(256), _row_spec(256)]
                + _page_specs(n_pages, C_KV * HD, layer) + _page_specs(n_pages, C_KV * HD, layer))
    grid_spec = pltpu.PrefetchScalarGridSpec(num_scalar_prefetch=1, grid=(db,), in_specs=in_specs,
                                             out_specs=_row_spec(512))
    return pl.pallas_call(
        functools.partial(_moba_s_body, n_pages, n_new),
        grid_spec=grid_spec,
        out_shape=jax.ShapeDtypeStruct((db, SROWS, 512), F32),
        compiler_params=_params("parallel"),
        name=name,
    )(page_table, cq, ck_new, cv_new, *([cache_k] * n_pages), *([cache_v] * n_pages))


def _compress_s_body(n_pages, pt_ref, pek_ref, pev_ref, w1k_ref, w1v_ref, w2k_ref, w2v_ref, *rest):
    k_refs = rest[:n_pages]
    v_refs = rest[n_pages:2 * n_pages]
    ok_ref, ov_ref, xk_ref, xv_ref = rest[2 * n_pages:]
    for p in range(n_pages):
        xk_ref[p * PAGE_SIZE:(p + 1) * PAGE_SIZE, :] = k_refs[p][...].T
        xv_ref[p * PAGE_SIZE:(p + 1) * PAGE_SIZE, :] = v_refs[p][...].T
    _compress_one(xk_ref, pek_ref, w1k_ref, w2k_ref, ok_ref)
    _compress_one(xv_ref, pev_ref, w1v_ref, w2v_ref, ov_ref)


def _compress_sample(page_table, layer, cache_k, cache_v, cw, name):
    db, n_pages = page_table.shape
    const = lambda shape: pl.BlockSpec(shape, lambda b, pt: (0,) * len(shape))
    in_specs = ([const((CMP_BLOCK, 1, LANES))] * 2 + [const((CMP_BLOCK, LANES, 2 * CMP_HIDDEN))] * 2
                + [const((2 * CMP_HIDDEN, LANES))] * 2
                + _page_specs(n_pages, B_KV * HD, layer) + _page_specs(n_pages, B_KV * HD, layer))
    o_spec = pl.BlockSpec((None, LANES, LANES), lambda b, pt: (b, 0, 0))
    grid_spec = pltpu.PrefetchScalarGridSpec(
        num_scalar_prefetch=1, grid=(db,), in_specs=in_specs, out_specs=[o_spec, o_spec],
        scratch_shapes=[pltpu.VMEM((n_pages * PAGE_SIZE, LANES), F32)] * 2)
    return pl.pallas_call(
        functools.partial(_compress_s_body, n_pages),
        grid_spec=grid_spec,
        out_shape=[jax.ShapeDtypeStruct((db, LANES, LANES), F32)] * 2,
        compiler_params=_params("parallel"),
        name=name,
    )(page_table, cw["pek"], cw["pev"], cw["w1k"], cw["w1v"], cw["w2k"], cw["w2v"],
      *([cache_k] * n_pages), *([cache_v] * n_pages))


def _nsa_s_body(n_pages, n_new, q_pos0, n_win, pt_ref, qc_ref, qr_ref, bg_ref, kcmp_ref, vcmp_ref,
                ksn_ref, vsn_ref, kwn_ref, vwn_ref, wk_ref, wv_ref, *rest):
    k_refs = rest[:n_pages]
    v_refs = rest[n_pages:2 * n_pages]
    o_ref = rest[2 * n_pages]
    lane = lax.broadcasted_iota(jnp.int32, (1, LANES), 1)
    halfi = lane >> 6
    rep = B_HEADS // B_KV
    m_rows = rep * SROWS
    new_mask = _new_mask(m_rows, n_new)
    trow8 = lax.broadcasted_iota(jnp.int32, (SROWS, 1), 0)
    trow = lax.broadcasted_iota(jnp.int32, (m_rows, 1), 0) & (SROWS - 1)
    qpos8 = q_pos0 + trow8
    ns = -(-(q_pos0 + n_new) // SEL_BLOCK)
    bpp = PAGE_SIZE // SEL_BLOCK
    ksnt = _pad_rows(ksn_ref[...]).T
    vsnt = _pad_rows(vsn_ref[...]).T
    kwnt = _pad_rows(kwn_ref[...]).T
    vwnt = _pad_rows(vwn_ref[...]).T
    kcmp = kcmp_ref[...].astype(BF16)
    vcmp = vcmp_ref[...].astype(BF16)
    ci = lax.broadcasted_iota(jnp.int32, (LANES, LANES), 0)
    cj = lax.broadcasted_iota(jnp.int32, (LANES, LANES), 1)
    onehot = jnp.where((ci >> CMP_PER_SEL_SHIFT) == cj, 1.0, 0.0).astype(BF16)
    sig = jax.nn.sigmoid(bg_ref[...])
    tile_rows = lambda x: jnp.concatenate([x] * rep, axis=0)

    for g in range(B_KV):
        rows = slice(g * HD, (g + 1) * HD)
        own_c, own_r = [], []
        for r in range(rep):
            blk = slice((g * rep + r) // 2 * LANES, ((g * rep + r) // 2 + 1) * LANES)
            own_c.append(jnp.where(halfi == (r % 2), qc_ref[:, blk], 0.0) * (HD ** -0.5))
            own_r.append(jnp.where(halfi == (r % 2), qr_ref[:, blk], 0.0) * (HD ** -0.5))
        qc = jnp.concatenate(own_c, axis=0)
        qc = qc + pltpu.roll(qc, HD, 1)
        qc = jnp.where(halfi == g, qc, 0.0).astype(BF16)
        qr = jnp.concatenate(own_r, axis=0).astype(BF16)

        cmask = tile_rows(jnp.where((lane * CMP_STRIDE + (CMP_BLOCK - 1)) <= qpos8, 1.0, 0.0)) > 0.5
        s = jnp.where(cmask, _dot_nt(qc, kcmp), NEG)
        m = jnp.max(s, axis=1, keepdims=True)
        p = jnp.where(cmask, jnp.exp(s - m), 0.0)
        p = p / jnp.maximum(jnp.sum(p, axis=1, keepdims=True), 1e-30)
        o_cmp = jnp.where(halfi == g, _dot(p.astype(BF16), vcmp), 0.0)
        o_cmp = o_cmp + pltpu.roll(o_cmp, HD, 1)
        psum = p[0:SROWS]
        for r in range(1, rep):
            psum = psum + p[r * SROWS:(r + 1) * SROWS]
        hi, mid, lo = _split3(psum)
        imp = _dot(hi, onehot) + _dot(mid, onehot) + _dot(lo, onehot)
        cur = qpos8 >> 6
        impm = jnp.where(lane == cur, jnp.inf, jnp.where(lane < cur, imp, -jnp.inf))
        rank = _rank_desc(impm, ns, lane)
        sel = jnp.where(rank < float(SEL_TOPK), jnp.where(lane <= cur, 1.0, 0.0), 0.0)

        ss = _scores_kt(qr, [k_refs[pg][rows, :] for pg in range(n_pages)] + [ksnt[rows, :]])
        pieces = []
        for pg in range(n_pages):
            mk = sel[:, pg * bpp:pg * bpp + 1]
            for u in range(1, bpp):
                mk = jnp.where(halfi >= u, sel[:, pg * bpp + u:pg * bpp + u + 1], mk)
            pieces.append((ss[pg], tile_rows(mk) > 0.5))
        pieces.append((ss[n_pages], new_mask))
        ps, tot = _softmax_pieces(pieces)
        o_slc = _values_kt(ps, [v_refs[pg][rows, :] for pg in range(n_pages)] + [vsnt[rows, :]]) / tot

        wcols = [slice(u * LANES, (u + 1) * LANES) for u in range(n_win // LANES)]
        ss = _scores_kt(qr, [wk_ref[rows, c] for c in wcols] + [kwnt[rows, :]])
        pieces = []
        for u in range(len(wcols)):
            dist = (n_win - u * LANES) + trow - lane
            pieces.append((ss[u], dist <= WINDOW))
        pieces.append((ss[len(wcols)], new_mask))
        ps, tot = _softmax_pieces(pieces)
        o_win = _values_kt(ps, [wv_ref[rows, c] for c in wcols] + [vwnt[rows, :]]) / tot

        outs = []
        for r in range(rep):
            rs = slice(r * SROWS, (r + 1) * SROWS)
            base = (g * rep + r) * 3
            outs.append(sig[:, base:base + 1] * o_cmp[rs] + sig[:, base + 1:base + 2] * o_slc[rs]
                        + sig[:, base + 2:base + 3] * o_win[rs])
        for pr in range(rep // 2):
            col = (g * rep // 2 + pr) * LANES
            o_ref[:, col:col + LANES] = jnp.where(halfi == 0, outs[2 * pr], outs[2 * pr + 1])


def _nsa_sample(page_table, layer, n_new, q_pos0, bqc, bqr, bg, kcmp, vcmp, ks_new, vs_new, kw_new, vw_new,
                state_k, state_v, cache_k, cache_v, name):
    db, n_pages = page_table.shape
    n_win = state_k.shape[-1]
    w_spec = pl.BlockSpec((None, None, B_KV * HD, n_win), lambda b, pt: (layer, b, 0, 0))
    c_spec = pl.BlockSpec((None, LANES, LANES), lambda b, pt: (b, 0, 0))
    in_specs = ([_row_spec(512), _row_spec(512), _row_spec(LANES), c_spec, c_spec]
                + [_row_spec(LANES)] * 4 + [w_spec, w_spec]
                + _page_specs(n_pages, B_KV * HD, layer) + _page_specs(n_pages, B_KV * HD, layer))
    grid_spec = pltpu.PrefetchScalarGridSpec(num_scalar_prefetch=1, grid=(db,), in_specs=in_specs,
                                             out_specs=_row_spec(512))
    return pl.pallas_call(
        functools.partial(_nsa_s_body, n_pages, n_new, q_pos0, n_win),
        grid_spec=grid_spec,
        out_shape=jax.ShapeDtypeStruct((db, SROWS, 512), F32),
        compiler_params=_params("parallel"),
        name=name,
    )(page_table, bqc, bqr, bg, kcmp, vcmp, ks_new, vs_new, kw_new, vw_new, state_k, state_v,
      *([cache_k] * n_pages), *([cache_v] * n_pages))


def _layer_weights(l, norm_mix, norm_ffn, w_in, a_lambda, a_subln, b_cmp_pe, b_cmp_w1, b_cmp_w2,
                   w_branch, w_out, peer_wq, peer_subkeys, peer_u, peer_v):
    w = w_in[l]
    bg_w = jnp.pad(w[:, 2304:2328], ((0, 0), (0, LANES - 3 * B_HEADS)))
    return {
        "norm_mix": norm_mix[l][None, :],
        "norm_ffn": norm_ffn[l][None, :],
        "w_a": w[:, 0:1024].astype(BF16),
        "w_b": jnp.concatenate([w[:, 1024:2304], bg_w], axis=1).astype(BF16),
        "w_c": w[:, 2328:3352].astype(BF16),
        "w_g": w[:, 3352:6424].astype(BF16),
        "lam": a_lambda[l],
        "subln": a_subln[l][None, :],
        "cmp": _compress_weights(b_cmp_pe[l], b_cmp_w1[l], b_cmp_w2[l]),
        "w_branch": w_branch[l].astype(BF16),
        "w_out": w_out[l].astype(BF16),
        "wq": peer_wq[l].astype(BF16),
        "sk": peer_subkeys[l].reshape(2 * PEER_HEADS, PEER_NKEYS, LANES).astype(BF16),
        "u": peer_u[l].astype(BF16),
        "vt": peer_v[l].T.astype(BF16),
        "lam_init": 0.8 - 0.6 * math.exp(-0.3 * l),
    }


def _project_all(x, lw, cos_t, sin_t, tag):
    aq, ak, av = _project(x, lw["norm_mix"], lw["w_a"], cos_t, sin_t, PIECES_A, WIDTHS_A, "proj_a_" + tag)
    b_out = _project(x, lw["norm_mix"], lw["w_b"], cos_t, sin_t, PIECES_B, WIDTHS_B, "proj_b_" + tag)
    cq, ck, cv = _project(x, lw["norm_mix"], lw["w_c"], cos_t, sin_t, PIECES_C, WIDTHS_C, "proj_c_" + tag)
    (gate,) = _project(x, lw["norm_mix"], lw["w_g"], cos_t, sin_t, PIECES_G, WIDTHS_G, "proj_g_" + tag)
    return (aq, ak, av), b_out, (cq, ck, cv), gate


def _ffn(x, lw, tag):
    st, stat = _peer_stats(x, lw["norm_ffn"], lw["wq"], lw["sk"], "peer_stats_" + tag)
    return _peer_dense(x, lw["norm_ffn"], st, stat, lw["u"], lw["vt"], "peer_dense_" + tag)


def _prompt_layer(x, lw, cos_t, sin_t, bn, t):
    (aq, ak, av), (bqc, bqr, bkc, bvc, bks, bvs, bkw, bvw, bg), (cq, ck, cv), gate = _project_all(
        x, lw, cos_t, sin_t, "p")
    r3 = lambda a: a.reshape(bn, t, a.shape[-1])
    o_a = _diff_attention(r3(aq), r3(ak), r3(av), lw["lam"], lw["subln"], lw["lam_init"], 0, "diff_p")
    kcmp, vcmp = _compress(r3(bkc), r3(bvc), lw["cmp"], "cmp_p")
    o_b = _nsa_attention(r3(bqc), r3(bqr), r3(bg), kcmp, vcmp, r3(bks), r3(bvs), r3(bkw), r3(bvw),
                         0, t, 0, t, "nsa_p")
    o_c = _moba_attention(r3(cq), r3(ck), r3(cv), 0, "moba_p")
    n = bn * t
    x = _merge(x, gate, o_a.reshape(n, 512), o_b.reshape(n, 512), o_c.reshape(n, 512),
               lw["w_branch"], lw["w_out"], "merge_p")
    x = _ffn(x, lw, "p")
    wp = min(WINDOW, t)
    rows = (r3(ak).reshape(bn, t, A_KV, 2 * HD), r3(av).reshape(bn, t, A_KV, 2 * HD),
            r3(bkc).reshape(bn, t, B_KV, HD), r3(bvc).reshape(bn, t, B_KV, HD),
            r3(bks).reshape(bn, t, B_KV, HD), r3(bvs).reshape(bn, t, B_KV, HD),
            r3(ck).reshape(bn, t, C_KV, HD), r3(cv).reshape(bn, t, C_KV, HD),
            r3(bkw)[:, t - wp:].reshape(bn, wp, B_KV, HD), r3(bvw)[:, t - wp:].reshape(bn, wp, B_KV, HD))
    return x, rows


def _sample_layer(x, lw, l, cos_t, sin_t, db, t, past, caches, state_wk, state_wv, page_table):
    (aq, ak, av), (bqc, bqr, bkc, bvc, bks, bvs, bkw, bvw, bg), (cq, ck, cv), gate = _project_all(
        x, lw, cos_t, sin_t, "s")
    r3 = lambda a: a.reshape(db, SROWS, a.shape[-1])
    c_ak, c_av, c_bkc, c_bvc, c_bks, c_bvs, c_ck, c_cv = caches
    o_a = _diff_sample(page_table, l, t, r3(aq), r3(ak), r3(av), c_ak, c_av, lw["lam"], lw["subln"],
                       lw["lam_init"], "diff_s")
    kcmp, vcmp = _compress_sample(page_table, l, c_bkc, c_bvc, lw["cmp"], "cmp_s")
    o_b = _nsa_sample(page_table, l, t, past, r3(bqc), r3(bqr), r3(bg), kcmp, vcmp, r3(bks), r3(bvs),
                      r3(bkw), r3(bvw), state_wk, state_wv, c_bks, c_bvs, "nsa_s")
    o_c = _moba_sample(page_table, l, t, r3(cq), r3(ck), r3(cv), c_ck, c_cv, "moba_s")
    n = db * SROWS
    x = _merge(x, gate, o_a.reshape(n, 512), o_b.reshape(n, 512), o_c.reshape(n, 512),
               lw["w_branch"], lw["w_out"], "merge_s")
    x = _ffn(x, lw, "s")
    new = lambda a, kv, w: r3(a)[:, :t].reshape(db, t, kv, w)

    def rolled(state_t, a):
        new_t = jnp.transpose(r3(a)[:, :t], (0, 2, 1))
        out_t = jnp.concatenate([state_t[l][:, :, t:], new_t], axis=2)
        return jnp.transpose(out_t, (0, 2, 1)).reshape(db, out_t.shape[2], B_KV, HD)

    rows = (new(ak, A_KV, 2 * HD), new(av, A_KV, 2 * HD), new(bkc, B_KV, HD), new(bvc, B_KV, HD),
            new(bks, B_KV, HD), new(bvs, B_KV, HD), new(ck, C_KV, HD), new(cv, C_KV, HD),
            rolled(state_wk, bkw), rolled(state_wv, bvw))
    return x, rows


def kernel(x_prompt, x_sample, cache_a_k, cache_a_v, cache_b_kc, cache_b_vc, cache_b_ks, cache_b_vs,
           cache_c_k, cache_c_v, state_b_wk, state_b_wv, page_table, norm_mix, norm_ffn, norm_final,
           w_in, a_lambda, a_subln, b_cmp_pe, b_cmp_w1, b_cmp_w2, w_branch, w_out,
           peer_wq, peer_subkeys, peer_u, peer_v):
    bn, t, _ = x_prompt.shape
    db, ts, _ = x_sample.shape
    depth = w_in.shape[0]
    past = page_table.shape[1] * PAGE_SIZE
    wb = state_b_wk.shape[2]
    assert past % MOBA_BLOCK == 0 and ts <= SROWS and wb == WINDOW and wb % LANES == 0
    rows_view = lambda c: c.reshape(c.shape[0], c.shape[1], c.shape[2] * c.shape[3], c.shape[4])
    lanes_view = lambda c: jnp.transpose(c, (0, 1, 3, 4, 2)).reshape(
        c.shape[0], c.shape[1], c.shape[3] * c.shape[4], c.shape[2])
    caches = ([rows_view(c) for c in (cache_a_k, cache_a_v)]
              + [lanes_view(c) for c in (cache_b_kc, cache_b_vc, cache_b_ks, cache_b_vs, cache_c_k, cache_c_v)])
    swk = lanes_view(state_b_wk)
    swv = lanes_view(state_b_wv)

    cos_p, sin_p = _rope_tables(jnp.arange(t, dtype=jnp.int32))
    tm_s = min(ROW_TILE, db * SROWS)
    cos_s, sin_s = _rope_tables(past + (jnp.arange(tm_s, dtype=jnp.int32) % SROWS))

    xp = x_prompt.reshape(bn * t, D_MODEL)
    xs = jnp.pad(x_sample, ((0, 0), (0, SROWS - ts), (0, 0))).reshape(db * SROWS, D_MODEL)
    rows_p, rows_s = [], []
    for l in range(depth):
        lw = _layer_weights(l, norm_mix, norm_ffn, w_in, a_lambda, a_subln, b_cmp_pe, b_cmp_w1, b_cmp_w2,
                            w_branch, w_out, peer_wq, peer_subkeys, peer_u, peer_v)
        xp, rp = _prompt_layer(xp, lw, cos_p, sin_p, bn, t)
        xs, rs = _sample_layer(xs, lw, l, cos_s, sin_s, db, ts, past, caches, swk, swv, page_table)
        rows_p.append(rp)
        rows_s.append(rs)
    y_prompt = _final_norm(xp, norm_final[None, :], "final_p").reshape(bn, t, D_MODEL)
    y_sample = _final_norm(xs, norm_final[None, :], "final_s").reshape(db, SROWS, D_MODEL)[:, :ts]
    outs_p = [jnp.stack(r, axis=0) for r in zip(*rows_p)]
    outs_s = [jnp.stack(r, axis=0) for r in zip(*rows_s)]
    return (y_prompt, y_sample, *outs_p, *outs_s)
```

```python
import functools
import math

import jax
import jax.numpy as jnp
import numpy as np
from jax import lax
from jax.experimental import pallas as pl
from jax.experimental.pallas import tpu as pltpu

F32 = jnp.float32
BF16 = jnp.bfloat16

D_MODEL = 1024
HD = 64
A_HEADS, A_KV = 4, 2
B_HEADS, B_KV = 8, 2
C_HEADS, C_KV = 8, 4
CMP_BLOCK, CMP_STRIDE, CMP_HIDDEN = 32, 16, 128
SEL_BLOCK, SEL_TOPK = 64, 8
CMP_PER_SEL_SHIFT = 2
WINDOW = 512
MOBA_BLOCK, MOBA_TOPK = 256, 3
PEER_HEADS, PEER_NKEYS, PEER_TOPK = 8, 128, 16
ROPE_THETA = 10000.0
EPS = 1e-6
PAGE_SIZE = 128

LANES = 128
NEG = -1e30
QK_SCALE_LOG2 = math.log2(math.e) * HD ** -0.5
VMEM_LIMIT = 48 * 1024 * 1024
ROW_TILE = 256
KV_CHUNK = 256
EXPERT_TILE = 2048
EXPERT_SUB = 256


def _params(*sem):
    return pltpu.CompilerParams(dimension_semantics=sem, vmem_limit_bytes=VMEM_LIMIT)


def _dot(a, b):
    return jnp.dot(a, b, preferred_element_type=F32)


def _dot_nt(a, b):
    return lax.dot_general(a, b, (((1,), (1,)), ((), ())), preferred_element_type=F32)


def _split3(a):
    hi = a.astype(BF16)
    r1 = a - hi.astype(F32)
    mid = r1.astype(BF16)
    lo = (r1 - mid.astype(F32)).astype(BF16)
    return hi, mid, lo


def _rms(x, w):
    return x * lax.rsqrt(jnp.mean(x * x, axis=-1, keepdims=True) + EPS) * w


def _proj_body(pieces, chunk, x_ref, nw_ref, w_ref, cos_ref, sin_ref, *outs):
    h = _rms(x_ref[...], nw_ref[...]).astype(BF16)
    cos = cos_ref[...]
    sin = sin_ref[...]
    lane = lax.broadcasted_iota(jnp.int32, (1, LANES), 1)
    first = (lane & (HD - 1)) < (HD // 2)
    n_cols = len(pieces) * LANES
    for c0 in range(0, n_cols, chunk):
        w = min(chunk, n_cols - c0)
        z = _dot(h, w_ref[:, c0:c0 + w])
        for p in range(w // LANES):
            zp = z[:, p * LANES:(p + 1) * LANES]
            for (oi, oc, rope) in pieces[c0 // LANES + p]:
                if rope:
                    rot = jnp.where(first, pltpu.roll(zp, LANES - HD // 2, 1), pltpu.roll(zp, HD // 2, 1))
                    outs[oi][:, oc:oc + LANES] = zp * cos + rot * sin
                else:
                    outs[oi][:, oc:oc + LANES] = zp


def _project(x, nw, w, cos_t, sin_t, pieces, out_widths, name):
    n = x.shape[0]
    tm = min(ROW_TILE, n)
    ntab = cos_t.shape[0] // tm
    ncols = w.shape[1]
    body = functools.partial(_proj_body, pieces, 512)
    return pl.pallas_call(
        body,
        grid=(n // tm,),
        in_specs=[
            pl.BlockSpec((tm, D_MODEL), lambda i: (i, 0)),
            pl.BlockSpec((1, D_MODEL), lambda i: (0, 0)),
            pl.BlockSpec((D_MODEL, ncols), lambda i: (0, 0)),
            pl.BlockSpec((tm, LANES), lambda i: (i % ntab, 0)),
            pl.BlockSpec((tm, LANES), lambda i: (i % ntab, 0)),
        ],
        out_specs=[pl.BlockSpec((tm, ow), lambda i: (i, 0)) for ow in out_widths],
        out_shape=[jax.ShapeDtypeStruct((n, ow), F32) for ow in out_widths],
        compiler_params=_params("parallel"),
        name=name,
    )(x, nw, w, cos_t, sin_t)


def _plain(oi, width):
    return [[(oi, c, False)] for c in range(0, width, LANES)]


def _roped(oi, width):
    return [[(oi, c, True)] for c in range(0, width, LANES)]


PIECES_A = _roped(0, 512) + _roped(1, 256) + _plain(2, 256)
WIDTHS_A = (512, 256, 256)
PIECES_B = ([[(0, c, False), (1, c, True)] for c in range(0, 512, LANES)]
            + _plain(2, 128) + _plain(3, 128) + _roped(4, 128) + _plain(5, 128)
            + _roped(6, 128) + _plain(7, 128) + _plain(8, 128))
WIDTHS_B = (512, 512, 128, 128, 128, 128, 128, 128, 128)
PIECES_C = _roped(0, 512) + _roped(1, 256) + _plain(2, 256)
WIDTHS_C = (512, 256, 256)
PIECES_G = _plain(0, 3072)
WIDTHS_G = (3072,)


def _rope_tables(pos):
    half = HD // 2
    freqs = jnp.power(ROPE_THETA, -jnp.arange(half, dtype=F32) / half)
    ang = pos.astype(F32)[:, None] * freqs[None, :]
    cos = jnp.cos(ang)
    sin = jnp.sin(ang)
    cos_t = jnp.concatenate([cos, cos, cos, cos], axis=1)
    sin_t = jnp.concatenate([-sin, sin, -sin, sin], axis=1)
    return cos_t, sin_t


def _flash_init(m_ref, l_ref, acc_ref):
    m_ref[...] = jnp.full(m_ref.shape, NEG, F32)
    l_ref[...] = jnp.zeros(l_ref.shape, F32)
    acc_ref[...] = jnp.zeros(acc_ref.shape, F32)


def _lane_blocks(x):
    return [x[:, c:c + LANES] for c in range(0, x.shape[1], LANES)]


def _flash_step(hh, q_emb, kc, vc, biases, m_ref, l_ref, acc_ref):
    blocks = _lane_blocks(_dot_nt(q_emb, kc))
    if biases is not None:
        blocks = [b if bi is None else b + bi for b, bi in zip(blocks, biases)]
    mx = blocks[0]
    for b in blocks[1:]:
        mx = jnp.maximum(mx, b)
    m_old = m_ref[hh]
    m_new = jnp.maximum(m_old, jnp.max(mx, axis=1, keepdims=True))
    alpha = jnp.exp2(m_old - m_new)
    ps = [jnp.exp2(b - m_new) for b in blocks]
    tot = ps[0]
    for p in ps[1:]:
        tot = tot + p
    l_ref[hh] = alpha * l_ref[hh] + jnp.sum(tot, axis=1, keepdims=True)
    acc_ref[hh] = alpha * acc_ref[hh] + _dot(jnp.concatenate(ps, axis=1).astype(BF16), vc)
    m_ref[hh] = m_new


def _causal_sweep(q0, tq, body):
    n_full = (q0 + 1) // KV_CHUNK
    n_pair = n_full // 2
    lax.fori_loop(0, n_pair, body(2 * KV_CHUNK, False), 0)
    lax.fori_loop(2 * n_pair, n_full, body(KV_CHUNK, False), 0)
    lax.fori_loop(n_full, (q0 + tq - 1) // KV_CHUNK + 1, body(KV_CHUNK, True), 0)


def _causal_bias(j, tk, qpos, inside=0.0):
    kpos = j * tk + lax.broadcasted_iota(jnp.int32, (1, tk), 1)
    return jnp.where(kpos <= qpos, inside, NEG)


def _flash_out(hh, l_ref, acc_ref):
    return acc_ref[hh] / jnp.maximum(l_ref[hh], 1e-30)


def _chunk(ref, j, tk):
    return ref[pl.ds(pl.multiple_of(j * tk, tk), tk), :].astype(BF16)


def _diff_body(tq, tk, q_pos0, lam_init, lam_ref, subln_ref, q_ref, k_ref, v_ref, o_ref,
               m_ref, l_ref, acc_ref):
    qi = pl.program_id(2)
    lane = lax.broadcasted_iota(jnp.int32, (1, LANES), 1)
    upper = lane >= HD
    q0 = q_pos0 + qi * tq
    qpos = q0 + lax.broadcasted_iota(jnp.int32, (tq, 1), 0)
    _flash_init(m_ref, l_ref, acc_ref)
    qs = []
    for r in range(2):
        qb = q_ref[:, r * LANES:(r + 1) * LANES] * QK_SCALE_LOG2
        qs.append(jnp.where(upper, 0.0, qb).astype(BF16))
        qs.append(jnp.where(upper, qb, 0.0).astype(BF16))
    def sweep(tk, diag):
        def body(j, carry):
            kc = _chunk(k_ref, j, tk)
            vc = _chunk(v_ref, j, tk)
            biases = _lane_blocks(_causal_bias(j, tk, qpos)) if diag else None
            for hh in range(4):
                _flash_step(hh, qs[hh], kc, vc, biases, m_ref, l_ref, acc_ref)
            return carry
        return body

    _causal_sweep(q0, tq, sweep)
    lp = lam_ref[...]
    lam = (jnp.exp(jnp.sum(lp[0:1] * lp[1:2], axis=1, keepdims=True))
           - jnp.exp(jnp.sum(lp[2:3] * lp[3:4], axis=1, keepdims=True)) + lam_init)
    for r in range(2):
        o = _flash_out(2 * r, l_ref, acc_ref) - lam * _flash_out(2 * r + 1, l_ref, acc_ref)
        o_ref[:, r * LANES:(r + 1) * LANES] = _rms(o, subln_ref[...]) * (1.0 - lam_init)


def _diff_attention(aq, ak, av, lam_p, subln, lam_init, q_pos0, name):
    bn, tq_all, _ = aq.shape
    lk = ak.shape[1]
    tq = min(ROW_TILE, tq_all)
    tk = KV_CHUNK
    body = functools.partial(_diff_body, tq, tk, q_pos0, lam_init)
    return pl.pallas_call(
        body,
        grid=(bn, A_KV, tq_all // tq),
        in_specs=[
            pl.BlockSpec((4, HD), lambda b, g, i: (0, 0)),
            pl.BlockSpec((1, 2 * HD), lambda b, g, i: (0, 0)),
            pl.BlockSpec((None, tq, 256), lambda b, g, i: (b, i, g)),
            pl.BlockSpec((None, lk, LANES), lambda b, g, i: (b, 0, g)),
            pl.BlockSpec((None, lk, LANES), lambda b, g, i: (b, 0, g)),
        ],
        out_specs=pl.BlockSpec((None, tq, 256), lambda b, g, i: (b, i, g)),
        out_shape=jax.ShapeDtypeStruct((bn, tq_all, 512), F32),
        scratch_shapes=[pltpu.VMEM((4, tq, LANES), F32), pltpu.VMEM((4, tq, LANES), F32),
                        pltpu.VMEM((4, tq, LANES), F32)],
        compiler_params=_params("parallel", "parallel", "arbitrary"),
        name=name,
    )(lam_p, subln, aq, ak, av)


def _compress_one(x_ref, pe_ref, w1_ref, w2_ref, o_ref):
    n_half = CMP_STRIDE
    ha = jnp.zeros((LANES, 2 * CMP_HIDDEN), F32)
    hb = jnp.zeros((LANES, 2 * CMP_HIDDEN), F32)
    for p in range(n_half):
        x = x_ref[pl.ds(p, LANES, stride=CMP_STRIDE), :]
        ha = ha + _dot((x + pe_ref[p]).astype(BF16), w1_ref[p])
        hb = hb + _dot((x + pe_ref[n_half + p]).astype(BF16), w1_ref[n_half + p])
    h = ha + pltpu.roll(hb, LANES - 1, 0)
    y = _dot(jax.nn.gelu(h).astype(BF16), w2_ref[...])
    row = lax.broadcasted_iota(jnp.int32, (LANES, 1), 0)
    o_ref[...] = jnp.where(row < LANES - 1, y, 0.0)


def _compress_body(xk_ref, xv_ref, pek_ref, pev_ref, w1k_ref, w1v_ref, w2k_ref, w2v_ref, ok_ref, ov_ref):
    _compress_one(xk_ref, pek_ref, w1k_ref, w2k_ref, ok_ref)
    _compress_one(xv_ref, pev_ref, w1v_ref, w2v_ref, ov_ref)


def _compress(xk, xv, cw, name):
    bn = xk.shape[0]
    n_tok = LANES * CMP_STRIDE
    x_spec = pl.BlockSpec((None, n_tok, LANES), lambda b: (b, 0, 0))
    pe_spec = pl.BlockSpec((CMP_BLOCK, 1, LANES), lambda b: (0, 0, 0))
    w1_spec = pl.BlockSpec((CMP_BLOCK, LANES, 2 * CMP_HIDDEN), lambda b: (0, 0, 0))
    w2_spec = pl.BlockSpec((2 * CMP_HIDDEN, LANES), lambda b: (0, 0))
    o_spec = pl.BlockSpec((None, LANES, LANES), lambda b: (b, 0, 0))
    return pl.pallas_call(
        _compress_body,
        grid=(bn,),
        in_specs=[x_spec, x_spec, pe_spec, pe_spec, w1_spec, w1_spec, w2_spec, w2_spec],
        out_specs=[o_spec, o_spec],
        out_shape=[jax.ShapeDtypeStruct((bn, LANES, LANES), F32)] * 2,
        compiler_params=_params("parallel"),
        name=name,
    )(xk, xv, cw["pek"], cw["pev"], cw["w1k"], cw["w1v"], cw["w2k"], cw["w2v"])


def _compress_weights(pe, w1, w2):
    out = {}
    for idx, tag in ((0, "k"), (1, "v")):
        w1p = w1[idx].reshape(CMP_BLOCK, HD, CMP_HIDDEN)
        z = jnp.zeros_like(w1p)
        w1b = jnp.concatenate([jnp.concatenate([w1p, z], axis=2), jnp.concatenate([z, w1p], axis=2)], axis=1)
        z2 = jnp.zeros_like(w2[idx])
        w2b = jnp.concatenate([jnp.concatenate([w2[idx], z2], axis=1), jnp.concatenate([z2, w2[idx]], axis=1)], axis=0)
        out["w1" + tag] = w1b.astype(BF16)
        out["w2" + tag] = w2b.astype(BF16)
        out["pe" + tag] = jnp.concatenate([pe[idx], pe[idx]], axis=1)[:, None, :]
    return out


def _rank_desc(vals, n, lane):
    rank = jnp.zeros(vals.shape, F32)
    for i in range(n):
        col = vals[:, i:i + 1]
        tie = jnp.where(lane > i, 1.0, 0.0)
        rank = rank + jnp.where(col > vals, 1.0, 0.0) + jnp.where(col == vals, tie, 0.0)
    return rank


def _nsa_body(tq, tk, q_pos0, ns, w_pos0, w_valid, lw,
              qc_ref, qr_ref, bg_ref, kcmp_ref, vcmp_ref, ks_ref, vs_ref, kw_ref, vw_ref, o_ref,
              m_ref, l_ref, acc_ref):
    g = pl.program_id(1)
    qi = pl.program_id(2)
    lane = lax.broadcasted_iota(jnp.int32, (1, LANES), 1)
    halfi = lane >> 6
    q0 = q_pos0 + qi * tq
    qpos = q0 + lax.broadcasted_iota(jnp.int32, (tq, 1), 0)
    rep = B_HEADS // B_KV

    def embed_all(ref, scale):
        out = []
        for pr in range(rep // 2):
            v = ref[:, pr * LANES:(pr + 1) * LANES] * scale
            swapped = pltpu.roll(v, HD, 1)
            for u in range(2):
                out.append(jnp.where(halfi == g, jnp.where(g == u, v, swapped), 0.0).astype(BF16))
        return out

    def pair_out(o_even, o_odd):
        aligned = jnp.where(g == 0, o_even, o_odd)
        other = jnp.where(g == 0, o_odd, o_even)
        return jnp.where(halfi == g, aligned, pltpu.roll(other, HD, 1))

    kcmp = kcmp_ref[...].astype(BF16)
    vcmp = vcmp_ref[...].astype(BF16)
    cmask = (lane * CMP_STRIDE + (CMP_BLOCK - 1)) <= qpos
    psum = jnp.zeros((tq, LANES), F32)
    o_cmp = []
    qsc = embed_all(qc_ref, HD ** -0.5)
    for r in range(rep):
        s = jnp.where(cmask, _dot_nt(qsc[r], kcmp), NEG)
        m = jnp.max(s, axis=1, keepdims=True)
        p = jnp.where(cmask, jnp.exp(s - m), 0.0)
        p = p / jnp.maximum(jnp.sum(p, axis=1, keepdims=True), 1e-30)
        psum = psum + p
        o_cmp.append(_dot(p.astype(BF16), vcmp))

    ci = lax.broadcasted_iota(jnp.int32, (LANES, LANES), 0)
    cj = lax.broadcasted_iota(jnp.int32, (LANES, LANES), 1)
    onehot = jnp.where((ci >> CMP_PER_SEL_SHIFT) == cj, 1.0, 0.0).astype(BF16)
    hi, mid, lo = _split3(psum)
    imp = _dot(hi, onehot) + _dot(mid, onehot) + _dot(lo, onehot)
    cur = qpos >> 6
    impm = jnp.where(lane == cur, jnp.inf, jnp.where(lane < cur, imp, -jnp.inf))
    rank = _rank_desc(impm, ns, lane)
    sel = jnp.where(rank < float(SEL_TOPK), jnp.where(lane <= cur, 1.0, 0.0), 0.0).astype(BF16)

    qsr = embed_all(qr_ref, QK_SCALE_LOG2)
    _flash_init(m_ref, l_ref, acc_ref)
    def sel_sweep(ck, diag):
        bpc = ck // SEL_BLOCK

        def body(j, carry):
            kc = _chunk(ks_ref, j, ck)
            vc = _chunk(vs_ref, j, ck)
            ei = lax.broadcasted_iota(jnp.int32, (LANES, ck), 0)
            el = lax.broadcasted_iota(jnp.int32, (LANES, ck), 1)
            expand = jnp.where(ei == j * bpc + (el >> 6), 1.0, 0.0).astype(BF16)
            bias = (_dot(sel, expand) - 1.0) * (-NEG)
            if diag:
                bias = _causal_bias(j, ck, qpos, bias)
            biases = _lane_blocks(bias)
            for r in range(rep):
                _flash_step(r, qsr[r], kc, vc, biases, m_ref, l_ref, acc_ref)
            return carry
        return body

    _causal_sweep(q0, tq, sel_sweep)
    o_slc = [_flash_out(r, l_ref, acc_ref) for r in range(rep)]

    _flash_init(m_ref, l_ref, acc_ref)
    w_lo = jnp.maximum(q0 - WINDOW - w_pos0, 0) // tk
    w_hi = jnp.minimum(q0 + tq - 1 - w_pos0, lw - 1) // tk + 1

    def win_body(j, carry):
        kc = _chunk(kw_ref, j, tk)
        vc = _chunk(vw_ref, j, tk)
        kidx = j * tk + lax.broadcasted_iota(jnp.int32, (1, tk), 1)
        dist = qpos - (kidx + w_pos0)
        inwin = jnp.where(dist >= 0, jnp.where(dist <= WINDOW, 0.0, NEG), NEG)
        biases = _lane_blocks(jnp.where(kidx < w_valid, inwin, NEG))
        for r in range(rep):
            _flash_step(r, qsr[r], kc, vc, biases, m_ref, l_ref, acc_ref)
        return carry

    lax.fori_loop(w_lo, w_hi, win_body, 0)

    sig = jax.nn.sigmoid(bg_ref[...])
    outs = []
    for r in range(rep):
        o_win = _flash_out(r, l_ref, acc_ref)
        base = (g * rep + r) * 3
        gates = [jnp.sum(jnp.where(lane == base + c, sig, 0.0), axis=1, keepdims=True) for c in range(3)]
        outs.append(gates[0] * o_cmp[r] + gates[1] * o_slc[r] + gates[2] * o_win)
    for pr in range(rep // 2):
        o_ref[:, pr * LANES:(pr + 1) * LANES] = pair_out(outs[2 * pr], outs[2 * pr + 1])


def _nsa_attention(bqc, bqr, bg, kcmp, vcmp, ks, vs, kw, vw, q_pos0, k_valid, w_pos0, w_valid, name):
    bn, tq_all, _ = bqc.shape
    lk = ks.shape[1]
    lw = kw.shape[1]
    tq = min(ROW_TILE, tq_all)
    tk = KV_CHUNK
    ns = -(-k_valid // SEL_BLOCK)
    body = functools.partial(_nsa_body, tq, tk, q_pos0, ns, w_pos0, w_valid, lw)
    q_spec = pl.BlockSpec((None, tq, 256), lambda b, g, i: (b, i, g))
    full = lambda rows: pl.BlockSpec((None, rows, LANES), lambda b, g, i: (b, 0, 0))
    return pl.pallas_call(
        body,
        grid=(bn, B_KV, tq_all // tq),
        in_specs=[q_spec, q_spec, pl.BlockSpec((None, tq, LANES), lambda b, g, i: (b, i, 0)),
                  full(LANES), full(LANES), full(lk), full(lk), full(lw), full(lw)],
        out_specs=pl.BlockSpec((None, tq, 256), lambda b, g, i: (b, i, g)),
        out_shape=jax.ShapeDtypeStruct((bn, tq_all, 512), F32),
        scratch_shapes=[pltpu.VMEM((4, tq, LANES), F32), pltpu.VMEM((4, tq, LANES), F32),
                        pltpu.VMEM((4, tq, LANES), F32)],
        compiler_params=_params("parallel", "parallel", "arbitrary"),
        name=name,
    )(bqc, bqr, bg, kcmp, vcmp, ks, vs, kw, vw)


def _moba_body(tq, tk, q_pos0, nb, q_ref, k_ref, v_ref, o_ref, km_ref, m_ref, l_ref, acc_ref):
    g = pl.program_id(1)
    qi = pl.program_id(2)
    gh = g % 2
    lane = lax.broadcasted_iota(jnp.int32, (1, LANES), 1)
    halfi = lane >> 6
    q0 = q_pos0 + qi * tq
    qpos = q0 + lax.broadcasted_iota(jnp.int32, (tq, 1), 0)
    cur = qpos >> 8
    rep = C_HEADS // C_KV

    @pl.when(qi == 0)
    def _():
        km_ref[...] = jnp.zeros(km_ref.shape, F32)
        km_ref[0:nb, :] = jnp.sum(k_ref[...].reshape(nb, MOBA_BLOCK, LANES), axis=1) * (1.0 / MOBA_BLOCK)

    km_hi, km_mid, _ = _split3(km_ref[...])

    qv = q_ref[...]
    qs = []
    sels = []
    swapped = pltpu.roll(qv, HD, 1)
    for r in range(rep):
        qe = jnp.where(halfi == gh, jnp.where(gh == r, qv, swapped), 0.0)
        q_hi, q_mid, _ = _split3(qe)
        s_blk = _dot_nt(q_hi, km_hi) + _dot_nt(q_hi, km_mid) + _dot_nt(q_mid, km_hi)
        sm = jnp.where(lane < cur, s_blk, -jnp.inf)
        rank = _rank_desc(sm, nb, lane)
        past = jnp.where(rank < float(MOBA_TOPK), jnp.where(lane < cur, 1.0, 0.0), 0.0)
        sels.append(jnp.where(lane == cur, 1.0, past))
        qs.append((qe * QK_SCALE_LOG2).astype(BF16))

    _flash_init(m_ref, l_ref, acc_ref)
    def sweep(ck, diag):
        bpc = ck // MOBA_BLOCK
        lpb = MOBA_BLOCK // LANES

        def body(j, carry):
            kc = _chunk(k_ref, j, ck)
            vc = _chunk(v_ref, j, ck)
            diag_biases = _lane_blocks(_causal_bias(j, ck, qpos)) if diag else None
            for r in range(rep):
                if diag:
                    biases = diag_biases
                else:
                    biases = []
                    for u in range(bpc):
                        col = jnp.sum(jnp.where(lane == j * bpc + u, sels[r], 0.0), axis=1, keepdims=True)
                        biases += [(col - 1.0) * (-NEG)] * lpb
                _flash_step(r, qs[r], kc, vc, biases, m_ref, l_ref, acc_ref)
            return carry
        return body

    _causal_sweep(q0, tq, sweep)
    o0 = _flash_out(0, l_ref, acc_ref)
    o1 = _flash_out(1, l_ref, acc_ref)
    aligned = jnp.where(gh == 0, o0, o1)
    other = jnp.where(gh == 0, o1, o0)
    o_ref[...] = jnp.where(halfi == gh, aligned, pltpu.roll(other, HD, 1))


def _moba_attention(cq, ck, cv, q_pos0, name):
    bn, tq_all, _ = cq.shape
    lk = ck.shape[1]
    tq = min(ROW_TILE, tq_all)
    tk = MOBA_BLOCK
    nb = lk // MOBA_BLOCK
    assert tq == MOBA_BLOCK == KV_CHUNK and q_pos0 % MOBA_BLOCK == 0
    body = functools.partial(_moba_body, tq, tk, q_pos0, nb)
    return pl.pallas_call(
        body,
        grid=(bn, C_KV, tq_all // tq),
        in_specs=[
            pl.BlockSpec((None, tq, LANES), lambda b, g, i: (b, i, g)),
            pl.BlockSpec((None, lk, LANES), lambda b, g, i: (b, 0, g // 2)),
            pl.BlockSpec((None, lk, LANES), lambda b, g, i: (b, 0, g // 2)),
        ],
        out_specs=pl.BlockSpec((None, tq, LANES), lambda b, g, i: (b, i, g)),
        out_shape=jax.ShapeDtypeStruct((bn, tq_all, 512), F32),
        scratch_shapes=[pltpu.VMEM((LANES, LANES), F32),
                        pltpu.VMEM((2, tq, LANES), F32), pltpu.VMEM((2, tq, LANES), F32),
                        pltpu.VMEM((2, tq, LANES), F32)],
        compiler_params=_params("parallel", "parallel", "arbitrary"),
        name=name,
    )(cq, ck, cv)


def _merge_body(x_ref, gate_ref, oa_ref, ob_ref, oc_ref, wb_ref, wo_ref, y_ref):
    y = jnp.zeros((x_ref.shape[0], D_MODEL), F32)
    for c, o_ref in enumerate((oa_ref, ob_ref, oc_ref)):
        br = _dot(o_ref[...].astype(BF16), wb_ref[c])
        y = y + jax.nn.sigmoid(gate_ref[:, c * D_MODEL:(c + 1) * D_MODEL]) * br
    y_ref[...] = x_ref[...] + _dot(y.astype(BF16), wo_ref[...])


def _merge(x, gate, o_a, o_b, o_c, wb, wo, name):
    n = x.shape[0]
    tm = min(ROW_TILE, n)
    row = lambda w: pl.BlockSpec((tm, w), lambda i: (i, 0))
    return pl.pallas_call(
        _merge_body,
        grid=(n // tm,),
        in_specs=[row(D_MODEL), row(3 * D_MODEL), row(512), row(512), row(512),
                  pl.BlockSpec((3, 512, D_MODEL), lambda i: (0, 0, 0)),
                  pl.BlockSpec((D_MODEL, D_MODEL), lambda i: (0, 0))],
        out_specs=row(D_MODEL),
        out_shape=jax.ShapeDtypeStruct((n, D_MODEL), F32),
        compiler_params=_params("parallel"),
        name=name,
    )(x, gate, o_a, o_b, o_c, wb, wo)


def _peer_stats_body(tm, x_ref, nw_ref, wq_ref, sk_ref, st_ref, stat_ref, top_ref, nxt_ref):
    h = _rms(x_ref[...], nw_ref[...]).astype(BF16)
    q = _dot(h, wq_ref[...]).astype(BF16)
    n_hc = 2 * PEER_HEADS
    for hc in range(n_hc):
        st_ref[hc] = _dot_nt(sk_ref[hc], q[:, hc * LANES:(hc + 1) * LANES])

    def top_body(hc, carry):
        for hf in range(tm // LANES):
            s = st_ref[hc, :, hf * LANES:(hf + 1) * LANES]
            vals = []
            for _ in range(PEER_TOPK + 1):
                m = jnp.max(s, axis=0, keepdims=True)
                vals.append(m)
                s = jnp.where(s == m, -jnp.inf, s)
            top_ref[hc, :, hf * LANES:(hf + 1) * LANES] = jnp.concatenate(vals[:PEER_TOPK], axis=0)
            nxt_ref[hc, :, hf * LANES:(hf + 1) * LANES] = vals[PEER_TOPK]
        return carry

    lax.fori_loop(0, n_hc, top_body, 0)

    row16 = lax.broadcasted_iota(jnp.int32, (PEER_TOPK, LANES), 0)

    def head_body(hd, carry):
        for hf in range(tm // LANES):
            sl = slice(hf * LANES, (hf + 1) * LANES)
            a = top_ref[2 * hd, :, sl]
            b = top_ref[2 * hd + 1, :, sl]
            parts = []
            for i in range(PEER_TOPK):
                lim = PEER_TOPK // (i + 1)
                rows = PEER_TOPK if lim > 8 else 8
                cs = a[i:i + 1] + b[:rows]
                parts.append(jnp.where(row16[:rows] < lim, cs, -jnp.inf))
            cand = jnp.concatenate(parts, axis=0)
            mx = a[0:1] + b[0:1]
            cum = jnp.zeros((1, LANES), F32)
            tau = mx
            nxt = mx
            work = cand
            for _ in range(PEER_TOPK + 1):
                m = jnp.max(work, axis=0, keepdims=True)
                eq = work == m
                tau = jnp.where(cum < float(PEER_TOPK), m, tau)
                nxt = jnp.where(cum < float(PEER_TOPK + 1), m, nxt)
                cum = cum + jnp.sum(jnp.where(eq, 1.0, 0.0), axis=0, keepdims=True)
                work = jnp.where(eq, -jnp.inf, work)
            nxt = jnp.maximum(nxt, jnp.maximum(a[0:1] + nxt_ref[2 * hd + 1, :, sl], nxt_ref[2 * hd, :, sl] + b[0:1]))
            z = jnp.sum(jnp.where(cand >= tau, jnp.exp(cand - mx), 0.0), axis=0, keepdims=True)
            stat_ref[hd, :, sl] = 0.5 * (tau + nxt)
            stat_ref[PEER_HEADS + hd, :, sl] = a[0:1]
            stat_ref[2 * PEER_HEADS + hd, :, sl] = b[0:1]
            stat_ref[3 * PEER_HEADS + hd, :, sl] = 1.0 / z
        return carry

    lax.fori_loop(0, PEER_HEADS, head_body, 0)


def _peer_stats(x, nw, wq, sk, name):
    n = x.shape[0]
    tm = min(ROW_TILE, n)
    n_hc = 2 * PEER_HEADS
    return pl.pallas_call(
        functools.partial(_peer_stats_body, tm),
        grid=(n // tm,),
        in_specs=[pl.BlockSpec((tm, D_MODEL), lambda i: (i, 0)),
                  pl.BlockSpec((1, D_MODEL), lambda i: (0, 0)),
                  pl.BlockSpec((D_MODEL, n_hc * LANES), lambda i: (0, 0)),
                  pl.BlockSpec((n_hc, LANES, LANES), lambda i: (0, 0, 0))],
        out_specs=[pl.BlockSpec((n_hc, LANES, tm), lambda i: (0, 0, i)),
                   pl.BlockSpec((4 * PEER_HEADS, 1, tm), lambda i: (0, 0, i))],
        out_shape=[jax.ShapeDtypeStruct((n_hc, LANES, n), F32),
                   jax.ShapeDtypeStruct((4 * PEER_HEADS, 1, n), F32)],
        scratch_shapes=[pltpu.VMEM((n_hc, PEER_TOPK, tm), F32), pltpu.VMEM((n_hc, 1, tm), F32)],
        compiler_params=_params("parallel"),
        name=name,
    )(x, nw, wq, sk)


def _peer_dense_body(tm, te, x_ref, nw_ref, st_ref, stat_ref, u_ref, vt_ref, o_ref,
                     h_ref, thr_ref, w1_ref, s2c_ref, e2_ref, acc_ref):
    e = pl.program_id(1)

    @pl.when(e == 0)
    def _():
        h_ref[...] = _rms(x_ref[...], nw_ref[...]).astype(BF16)
        acc_ref[...] = jnp.zeros(acc_ref.shape, F32)
        for hd in range(PEER_HEADS):
            tau = stat_ref[hd]
            m1 = stat_ref[PEER_HEADS + hd]
            m2 = stat_ref[2 * PEER_HEADS + hd]
            inv_z = stat_ref[3 * PEER_HEADS + hd]
            thr_ref[hd] = m1 - st_ref[2 * hd]
            w1_ref[hd] = jnp.exp(st_ref[2 * hd] - m1) * inv_z
            s2c_ref[hd] = (st_ref[2 * hd + 1] - (tau - m1)).astype(BF16)
            e2_ref[hd] = jnp.exp(st_ref[2 * hd + 1] - m2).astype(BF16)

    n_i = te // PEER_NKEYS
    zero = jnp.zeros((), BF16)
    h = h_ref[...]
    total = None
    for k in range(te // EXPERT_SUB):
        rows = slice(k * EXPERT_SUB, (k + 1) * EXPERT_SUB)
        act = jax.nn.gelu(_dot_nt(u_ref[rows, :], h)).astype(BF16)
        parts = []
        for ii in range(EXPERT_SUB // PEER_NKEYS):
            i = e * n_i + k * (EXPERT_SUB // PEER_NKEYS) + ii
            gmat = jnp.zeros((PEER_NKEYS, tm), BF16)
            for hd in range(PEER_HEADS):
                thr = thr_ref[hd, pl.ds(i, 1), :].astype(BF16)
                w1 = w1_ref[hd, pl.ds(i, 1), :].astype(BF16)
                gmat = gmat + jnp.where(s2c_ref[hd] > thr, e2_ref[hd] * w1, zero)
            parts.append(gmat * act[ii * PEER_NKEYS:(ii + 1) * PEER_NKEYS])
        t = _dot(vt_ref[:, rows], jnp.concatenate(parts, axis=0))
        total = t if total is None else total + t
    acc_ref[...] += total

    @pl.when(e == pl.num_programs(1) - 1)
    def _():
        o_ref[...] = x_ref[...] + acc_ref[...].T


def _peer_dense(x, nw, st, stat, u_bf, vt_bf, name):
    n = x.shape[0]
    tm = min(ROW_TILE, n)
    te = EXPERT_TILE
    n_exp = u_bf.shape[0]
    n_hc = 2 * PEER_HEADS
    return pl.pallas_call(
        functools.partial(_peer_dense_body, tm, te),
        grid=(n // tm, n_exp // te),
        in_specs=[pl.BlockSpec((tm, D_MODEL), lambda i, e: (i, 0)),
                  pl.BlockSpec((1, D_MODEL), lambda i, e: (0, 0)),
                  pl.BlockSpec((n_hc, LANES, tm), lambda i, e: (0, 0, i)),
                  pl.BlockSpec((4 * PEER_HEADS, 1, tm), lambda i, e: (0, 0, i)),
                  pl.BlockSpec((te, D_MODEL), lambda i, e: (e, 0)),
                  pl.BlockSpec((D_MODEL, te), lambda i, e: (0, e))],
        out_specs=pl.BlockSpec((tm, D_MODEL), lambda i, e: (i, 0)),
        out_shape=jax.ShapeDtypeStruct((n, D_MODEL), F32),
        scratch_shapes=[pltpu.VMEM((tm, D_MODEL), BF16),
                        pltpu.VMEM((PEER_HEADS, PEER_NKEYS, tm), F32),
                        pltpu.VMEM((PEER_HEADS, PEER_NKEYS, tm), F32),
                        pltpu.VMEM((PEER_HEADS, PEER_NKEYS, tm), BF16),
                        pltpu.VMEM((PEER_HEADS, PEER_NKEYS, tm), BF16),
                        pltpu.VMEM((D_MODEL, tm), F32)],
        compiler_params=_params("parallel", "arbitrary"),
        name=name,
    )(x, nw, st, stat, u_bf, vt_bf)


def _norm_body(x_ref, w_ref, o_ref):
    o_ref[...] = _rms(x_ref[...], w_ref[...])


def _final_norm(x, w, name):
    n = x.shape[0]
    tm = min(ROW_TILE, n)
    return pl.pallas_call(
        _norm_body,
        grid=(n // tm,),
        in_specs=[pl.BlockSpec((tm, D_MODEL), lambda i: (i, 0)), pl.BlockSpec((1, D_MODEL), lambda i: (0, 0))],
        out_specs=pl.BlockSpec((tm, D_MODEL), lambda i: (i, 0)),
        out_shape=jax.ShapeDtypeStruct((n, D_MODEL), F32),
        compiler_params=_params("parallel"),
        name=name,
    )(x, w)


SROWS = 8


def _page_specs(n_pages, rows, layer):
    def spec(p):
        return pl.BlockSpec((None, None, rows, LANES), lambda b, pt: (layer, pt[b, p], 0, 0))
    return [spec(p) for p in range(n_pages)]


def _row_spec(width):
    return pl.BlockSpec((None, SROWS, width), lambda b, pt: (b, 0, 0))


def _dup(x):
    return jnp.concatenate([x, x], axis=0)


def _pad_rows(x):
    return jnp.concatenate([x, jnp.zeros((PAGE_SIZE - x.shape[0], x.shape[1]), F32)], axis=0)


def _new_mask(m_rows, n_new):
    lane = lax.broadcasted_iota(jnp.int32, (1, LANES), 1)
    trow = lax.broadcasted_iota(jnp.int32, (m_rows, 1), 0) & (SROWS - 1)
    return jnp.where(lane < n_new, jnp.where(lane <= trow, 1.0, 0.0), 0.0) > 0.5


def _softmax_pieces(pieces):
    mx = None
    for s, mk in pieces:
        sm = s if mk is None else jnp.where(mk, s, NEG)
        mx = sm if mx is None else jnp.maximum(mx, sm)
    m = jnp.max(mx, axis=1, keepdims=True)
    ps = []
    tot = None
    for s, mk in pieces:
        p = jnp.exp(s - m)
        if mk is not None:
            p = jnp.where(mk, p, 0.0)
        tot = p if tot is None else tot + p
        ps.append(p.astype(BF16))
    return ps, jnp.sum(tot, axis=1, keepdims=True)


def _scores_kt(q, kts):
    return [_dot(q, _dup(kt.astype(BF16))) for kt in kts]


def _values_kt(ps, vts):
    o = None
    for p, vt in zip(ps, vts):
        t = _dot_nt(p, _dup(vt.astype(BF16)))
        o = t if o is None else o + t
    return o


def _diff_s_body(n_pages, n_new, lam_init, pt_ref, lam_ref, subln_ref, q_ref, kn_ref, vn_ref, *rest):
    k_refs = rest[:n_pages]
    v_refs = rest[n_pages:2 * n_pages]
    o_ref = rest[2 * n_pages]
    lane = lax.broadcasted_iota(jnp.int32, (1, LANES), 1)
    upper = lane >= HD
    new_mask = _new_mask(4 * SROWS, n_new)
    lp = lam_ref[...]
    lam = (jnp.exp(jnp.sum(lp[0:1] * lp[1:2], axis=1, keepdims=True))
           - jnp.exp(jnp.sum(lp[2:3] * lp[3:4], axis=1, keepdims=True)) + lam_init)
    for g in range(A_KV):
        parts = []
        for r in range(A_HEADS // A_KV):
            qb = q_ref[:, (2 * g + r) * LANES:(2 * g + r + 1) * LANES] * (HD ** -0.5)
            parts += [jnp.where(upper, 0.0, qb), jnp.where(upper, qb, 0.0)]
        q = jnp.concatenate(parts, axis=0).astype(BF16)
        pieces = [(_dot_nt(q, k_refs[p][pl.ds(g, PAGE_SIZE, stride=A_KV), :].astype(BF16)), None)
                  for p in range(n_pages)]
        knew = _pad_rows(kn_ref[:, g * LANES:(g + 1) * LANES]).astype(BF16)
        vnew = _pad_rows(vn_ref[:, g * LANES:(g + 1) * LANES]).astype(BF16)
        pieces.append((_dot_nt(q, knew), new_mask))
        ps, tot = _softmax_pieces(pieces)
        o = _dot(ps[n_pages], vnew)
        for p in range(n_pages):
            o = o + _dot(ps[p], v_refs[p][pl.ds(g, PAGE_SIZE, stride=A_KV), :].astype(BF16))
        o = o / tot
        for r in range(A_HEADS // A_KV):
            d = o[2 * r * SROWS:(2 * r + 1) * SROWS] - lam * o[(2 * r + 1) * SROWS:(2 * r + 2) * SROWS]
            o_ref[:, (2 * g + r) * LANES:(2 * g + r + 1) * LANES] = _rms(d, subln_ref[...]) * (1.0 - lam_init)


def _diff_sample(page_table, layer, n_new, aq, ak_new, av_new, cache_k, cache_v, lam_p, subln, lam_init, name):
    db, n_pages = page_table.shape
    const = lambda shape: pl.BlockSpec(shape, lambda b, pt: (0,) * len(shape))
    in_specs = ([const((4, HD)), const((1, 2 * HD)), _row_spec(512), _row_spec(256), _row_spec(256)]
                + _page_specs(n_pages, PAGE_SIZE * A_KV, layer) + _page_specs(n_pages, PAGE_SIZE * A_KV, layer))
    grid_spec = pltpu.PrefetchScalarGridSpec(num_scalar_prefetch=1, grid=(db,), in_specs=in_specs,
                                             out_specs=_row_spec(512))
    return pl.pallas_call(
        functools.partial(_diff_s_body, n_pages, n_new, lam_init),
        grid_spec=grid_spec,
        out_shape=jax.ShapeDtypeStruct((db, SROWS, 512), F32),
        compiler_params=_params("parallel"),
        name=name,
    )(page_table, lam_p, subln, aq, ak_new, av_new, *([cache_k] * n_pages), *([cache_v] * n_pages))


def _moba_s_body(n_pages, n_new, pt_ref, q_ref, kn_ref, vn_ref, *rest):
    k_refs = rest[:n_pages]
    v_refs = rest[n_pages:2 * n_pages]
    o_ref = rest[2 * n_pages]
    lane = lax.broadcasted_iota(jnp.int32, (1, LANES), 1)
    halfi = lane >> 6
    rep = C_HEADS // C_KV
    new_mask = _new_mask(rep * SROWS, n_new)
    knt = _pad_rows(kn_ref[...]).T
    vnt = _pad_rows(vn_ref[...]).T
    ppb = MOBA_BLOCK // PAGE_SIZE
    n_blocks = n_pages // ppb
    for g in range(C_KV):
        rows = slice(g * HD, (g + 1) * HD)
        qb = q_ref[:, g * LANES:(g + 1) * LANES] * (HD ** -0.5)
        q = jnp.concatenate([jnp.where(halfi == r, qb, 0.0) for r in range(rep)], axis=0).astype(BF16)
        ss = _scores_kt(q, [k_refs[p][rows, :] for p in range(n_pages)] + [knt[rows, :]])
        bs = []
        for j in range(n_blocks):
            acc = ss[j * ppb]
            for u in range(1, ppb):
                acc = acc + ss[j * ppb + u]
            bs.append(jnp.sum(acc, axis=1, keepdims=True))
        pieces = []
        for j in range(n_blocks):
            rank = jnp.zeros(bs[j].shape, F32)
            for i in range(n_blocks):
                if i < j:
                    rank = rank + jnp.where(bs[i] >= bs[j], 1.0, 0.0)
                elif i > j:
                    rank = rank + jnp.where(bs[i] > bs[j], 1.0, 0.0)
            keep = rank < float(MOBA_TOPK)
            for u in range(ppb):
                pieces.append((ss[j * ppb + u], keep))
        pieces.append((ss[n_pages], new_mask))
        ps, tot = _softmax_pieces(pieces)
        o = _values_kt(ps, [v_refs[p][rows, :] for p in range(n_pages)] + [vnt[rows, :]]) / tot
        o_ref[:, g * LANES:(g + 1) * LANES] = jnp.where(halfi == 0, o[0:SROWS], o[SROWS:2 * SROWS])


def _moba_sample(page_table, layer, n_new, cq, ck_new, cv_new, cache_k, cache_v, name):
    db, n_pages = page_table.shape
    in_specs = ([_row_spec(512), _row_spec(256), _row_spec(256)]
                + _page_specs(n_pages, C_KV * HD, layer) + _page_specs(n_pages, C_KV * HD, layer))
    grid_spec = pltpu.PrefetchScalarGridSpec(num_scalar_prefetch=1, grid=(db,), in_specs=in_specs,
                                             out_specs=_row_spec(512))
    return pl.pallas_call(
        functools.partial(_moba_s_body, n_pages, n_new),
        grid_spec=grid_spec,
        out_shape=jax.ShapeDtypeStruct((db, SROWS, 512), F32),
        compiler_params=_params("parallel"),
        name=name,
    )(page_table, cq, ck_new, cv_new, *([cache_k] * n_pages), *([cache_v] * n_pages))


def _compress_s_body(n_pages, pt_ref, pek_ref, pev_ref, w1k_ref, w1v_ref, w2k_ref, w2v_ref, *rest):
    k_refs = rest[:n_pages]
    v_refs = rest[n_pages:2 * n_pages]
    ok_ref, ov_ref, xk_ref, xv_ref = rest[2 * n_pages:]
    for p in range(n_pages):
        xk_ref[p * PAGE_SIZE:(p + 1) * PAGE_SIZE, :] = k_refs[p][...].T
        xv_ref[p * PAGE_SIZE:(p + 1) * PAGE_SIZE, :] = v_refs[p][...].T
    _compress_one(xk_ref, pek_ref, w1k_ref, w2k_ref, ok_ref)
    _compress_one(xv_ref, pev_ref, w1v_ref, w2v_ref, ov_ref)


def _compress_sample(page_table, layer, cache_k, cache_v, cw, name):
    db, n_pages = page_table.shape
    const = lambda shape: pl.BlockSpec(shape, lambda b, pt: (0,) * len(shape))
    in_specs = ([const((CMP_BLOCK, 1, LANES))] * 2 + [const((CMP_BLOCK, LANES, 2 * CMP_HIDDEN))] * 2
                + [const((2 * CMP_HIDDEN, LANES))] * 2
                + _page_specs(n_pages, B_KV * HD, layer) + _page_specs(n_pages, B_KV * HD, layer))
    o_spec = pl.BlockSpec((None, LANES, LANES), lambda b, pt: (b, 0, 0))
    grid_spec = pltpu.PrefetchScalarGridSpec(
        num_scalar_prefetch=1, grid=(db,), in_specs=in_specs, out_specs=[o_spec, o_spec],
        scratch_shapes=[pltpu.VMEM((n_pages * PAGE_SIZE, LANES), F32)] * 2)
    return pl.pallas_call(
        functools.partial(_compress_s_body, n_pages),
        grid_spec=grid_spec,
        out_shape=[jax.ShapeDtypeStruct((db, LANES, LANES), F32)] * 2,
        compiler_params=_params("parallel"),
        name=name,
    )(page_table, cw["pek"], cw["pev"], cw["w1k"], cw["w1v"], cw["w2k"], cw["w2v"],
      *([cache_k] * n_pages), *([cache_v] * n_pages))


def _nsa_s_body(n_pages, n_new, q_pos0, n_win, pt_ref, qc_ref, qr_ref, bg_ref, kcmp_ref, vcmp_ref,
                ksn_ref, vsn_ref, kwn_ref, vwn_ref, wk_ref, wv_ref, *rest):
    k_refs = rest[:n_pages]
    v_refs = rest[n_pages:2 * n_pages]
    o_ref = rest[2 * n_pages]
    lane = lax.broadcasted_iota(jnp.int32, (1, LANES), 1)
    halfi = lane >> 6
    rep = B_HEADS // B_KV
    m_rows = rep * SROWS
    new_mask = _new_mask(m_rows, n_new)
    trow8 = lax.broadcasted_iota(jnp.int32, (SROWS, 1), 0)
    trow = lax.broadcasted_iota(jnp.int32, (m_rows, 1), 0) & (SROWS - 1)
    qpos8 = q_pos0 + trow8
    ns = -(-(q_pos0 + n_new) // SEL_BLOCK)
    bpp = PAGE_SIZE // SEL_BLOCK
    ksnt = _pad_rows(ksn_ref[...]).T
    vsnt = _pad_rows(vsn_ref[...]).T
    kwnt = _pad_rows(kwn_ref[...]).T
    vwnt = _pad_rows(vwn_ref[...]).T
    kcmp = kcmp_ref[...].astype(BF16)
    vcmp = vcmp_ref[...].astype(BF16)
    ci = lax.broadcasted_iota(jnp.int32, (LANES, LANES), 0)
    cj = lax.broadcasted_iota(jnp.int32, (LANES, LANES), 1)
    onehot = jnp.where((ci >> CMP_PER_SEL_SHIFT) == cj, 1.0, 0.0).astype(BF16)
    sig = jax.nn.sigmoid(bg_ref[...])
    tile_rows = lambda x: jnp.concatenate([x] * rep, axis=0)

    for g in range(B_KV):
        rows = slice(g * HD, (g + 1) * HD)
        own_c, own_r = [], []
        for r in range(rep):
            blk = slice((g * rep + r) // 2 * LANES, ((g * rep + r) // 2 + 1) * LANES)
            own_c.append(jnp.where(halfi == (r % 2), qc_ref[:, blk], 0.0) * (HD ** -0.5))
            own_r.append(jnp.where(halfi == (r % 2), qr_ref[:, blk], 0.0) * (HD ** -0.5))
        qc = jnp.concatenate(own_c, axis=0)
        qc = qc + pltpu.roll(qc, HD, 1)
        qc = jnp.where(halfi == g, qc, 0.0).astype(BF16)
        qr = jnp.concatenate(own_r, axis=0).astype(BF16)

        cmask = tile_rows(jnp.where((lane * CMP_STRIDE + (CMP_BLOCK - 1)) <= qpos8, 1.0, 0.0)) > 0.5
        s = jnp.where(cmask, _dot_nt(qc, kcmp), NEG)
        m = jnp.max(s, axis=1, keepdims=True)
        p = jnp.where(cmask, jnp.exp(s - m), 0.0)
        p = p / jnp.maximum(jnp.sum(p, axis=1, keepdims=True), 1e-30)
        o_cmp = jnp.where(halfi == g, _dot(p.astype(BF16), vcmp), 0.0)
        o_cmp = o_cmp + pltpu.roll(o_cmp, HD, 1)
        psum = p[0:SROWS]
        for r in range(1, rep):
            psum = psum + p[r * SROWS:(r + 1) * SROWS]
        hi, mid, lo = _split3(psum)
        imp = _dot(hi, onehot) + _dot(mid, onehot) + _dot(lo, onehot)
        cur = qpos8 >> 6
        impm = jnp.where(lane == cur, jnp.inf, jnp.where(lane < cur, imp, -jnp.inf))
        rank = _rank_desc(impm, ns, lane)
        sel = jnp.where(rank < float(SEL_TOPK), jnp.where(lane <= cur, 1.0, 0.0), 0.0)

        ss = _scores_kt(qr, [k_refs[pg][rows, :] for pg in range(n_pages)] + [ksnt[rows, :]])
        pieces = []
        for pg in range(n_pages):
            mk = sel[:, pg * bpp:pg * bpp + 1]
            for u in range(1, bpp):
                mk = jnp.where(halfi >= u, sel[:, pg * bpp + u:pg * bpp + u + 1], mk)
            pieces.append((ss[pg], tile_rows(mk) > 0.5))
        pieces.append((ss[n_pages], new_mask))
        ps, tot = _softmax_pieces(pieces)
        o_slc = _values_kt(ps, [v_refs[pg][rows, :] for pg in range(n_pages)] + [vsnt[rows, :]]) / tot

        wcols = [slice(u * LANES, (u + 1) * LANES) for u in range(n_win // LANES)]
        ss = _scores_kt(qr, [wk_ref[rows, c] for c in wcols] + [kwnt[rows, :]])
        pieces = []
        for u in range(len(wcols)):
            dist = (n_win - u * LANES) + trow - lane
            pieces.append((ss[u], dist <= WINDOW))
        pieces.append((ss[len(wcols)], new_mask))
        ps, tot = _softmax_pieces(pieces)
        o_win = _values_kt(ps, [wv_ref[rows, c] for c in wcols] + [vwnt[rows, :]]) / tot

        outs = []
        for r in range(rep):
            rs = slice(r * SROWS, (r + 1) * SROWS)
            base = (g * rep + r) * 3
            outs.append(sig[:, base:base + 1] * o_cmp[rs] + sig[:, base + 1:base + 2] * o_slc[rs]
                        + sig[:, base + 2:base + 3] * o_win[rs])
        for pr in range(rep // 2):
            col = (g * rep // 2 + pr) * LANES
            o_ref[:, col:col + LANES] = jnp.where(halfi == 0, outs[2 * pr], outs[2 * pr + 1])


def _nsa_sample(page_table, layer, n_new, q_pos0, bqc, bqr, bg, kcmp, vcmp, ks_new, vs_new, kw_new, vw_new,
                state_k, state_v, cache_k, cache_v, name):
    db, n_pages = page_table.shape
    n_win = state_k.shape[-1]
    w_spec = pl.BlockSpec((None, None, B_KV * HD, n_win), lambda b, pt: (layer, b, 0, 0))
    c_spec = pl.BlockSpec((None, LANES, LANES), lambda b, pt: (b, 0, 0))
    in_specs = ([_row_spec(512), _row_spec(512), _row_spec(LANES), c_spec, c_spec]
                + [_row_spec(LANES)] * 4 + [w_spec, w_spec]
                + _page_specs(n_pages, B_KV * HD, layer) + _page_specs(n_pages, B_KV * HD, layer))
    grid_spec = pltpu.PrefetchScalarGridSpec(num_scalar_prefetch=1, grid=(db,), in_specs=in_specs,
                                             out_specs=_row_spec(512))
    return pl.pallas_call(
        functools.partial(_nsa_s_body, n_pages, n_new, q_pos0, n_win),
        grid_spec=grid_spec,
        out_shape=jax.ShapeDtypeStruct((db, SROWS, 512), F32),
        compiler_params=_params("parallel"),
        name=name,
    )(page_table, bqc, bqr, bg, kcmp, vcmp, ks_new, vs_new, kw_new, vw_new, state_k, state_v,
      *([cache_k] * n_pages), *([cache_v] * n_pages))


def _layer_weights(l, norm_mix, norm_ffn, w_in, a_lambda, a_subln, b_cmp_pe, b_cmp_w1, b_cmp_w2,
                   w_branch, w_out, peer_wq, peer_subkeys, peer_u, peer_v):
    w = w_in[l]
    bg_w = jnp.pad(w[:, 2304:2328], ((0, 0), (0, LANES - 3 * B_HEADS)))
    return {
        "norm_mix": norm_mix[l][None, :],
        "norm_ffn": norm_ffn[l][None, :],
        "w_a": w[:, 0:1024].astype(BF16),
        "w_b": jnp.concatenate([w[:, 1024:2304], bg_w], axis=1).astype(BF16),
        "w_c": w[:, 2328:3352].astype(BF16),
        "w_g": w[:, 3352:6424].astype(BF16),
        "lam": a_lambda[l],
        "subln": a_subln[l][None, :],
        "cmp": _compress_weights(b_cmp_pe[l], b_cmp_w1[l], b_cmp_w2[l]),
        "w_branch": w_branch[l].astype(BF16),
        "w_out": w_out[l].astype(BF16),
        "wq": peer_wq[l].astype(BF16),
        "sk": peer_subkeys[l].reshape(2 * PEER_HEADS, PEER_NKEYS, LANES).astype(BF16),
        "u": peer_u[l].astype(BF16),
        "vt": peer_v[l].T.astype(BF16),
        "lam_init": 0.8 - 0.6 * math.exp(-0.3 * l),
    }


def _project_all(x, lw, cos_t, sin_t, tag):
    aq, ak, av = _project(x, lw["norm_mix"], lw["w_a"], cos_t, sin_t, PIECES_A, WIDTHS_A, "proj_a_" + tag)
    b_out = _project(x, lw["norm_mix"], lw["w_b"], cos_t, sin_t, PIECES_B, WIDTHS_B, "proj_b_" + tag)
    cq, ck, cv = _project(x, lw["norm_mix"], lw["w_c"], cos_t, sin_t, PIECES_C, WIDTHS_C, "proj_c_" + tag)
    (gate,) = _project(x, lw["norm_mix"], lw["w_g"], cos_t, sin_t, PIECES_G, WIDTHS_G, "proj_g_" + tag)
    return (aq, ak, av), b_out, (cq, ck, cv), gate


def _ffn(x, lw, tag):
    st, stat = _peer_stats(x, lw["norm_ffn"], lw["wq"], lw["sk"], "peer_stats_" + tag)
    return _peer_dense(x, lw["norm_ffn"], st, stat, lw["u"], lw["vt"], "peer_dense_" + tag)


def _prompt_layer(x, lw, cos_t, sin_t, bn, t):
    (aq, ak, av), (bqc, bqr, bkc, bvc, bks, bvs, bkw, bvw, bg), (cq, ck, cv), gate = _project_all(
        x, lw, cos_t, sin_t, "p")
    r3 = lambda a: a.reshape(bn, t, a.shape[-1])
    o_a = _diff_attention(r3(aq), r3(ak), r3(av), lw["lam"], lw["subln"], lw["lam_init"], 0, "diff_p")
    kcmp, vcmp = _compress(r3(bkc), r3(bvc), lw["cmp"], "cmp_p")
    o_b = _nsa_attention(r3(bqc), r3(bqr), r3(bg), kcmp, vcmp, r3(bks), r3(bvs), r3(bkw), r3(bvw),
                         0, t, 0, t, "nsa_p")
    o_c = _moba_attention(r3(cq), r3(ck), r3(cv), 0, "moba_p")
    n = bn * t
    x = _merge(x, gate, o_a.reshape(n, 512), o_b.reshape(n, 512), o_c.reshape(n, 512),
               lw["w_branch"], lw["w_out"], "merge_p")
    x = _ffn(x, lw, "p")
    wp = min(WINDOW, t)
    rows = (r3(ak).reshape(bn, t, A_KV, 2 * HD), r3(av).reshape(bn, t, A_KV, 2 * HD),
            r3(bkc).reshape(bn, t, B_KV, HD), r3(bvc).reshape(bn, t, B_KV, HD),
            r3(bks).reshape(bn, t, B_KV, HD), r3(bvs).reshape(bn, t, B_KV, HD),
            r3(ck).reshape(bn, t, C_KV, HD), r3(cv).reshape(bn, t, C_KV, HD),
            r3(bkw)[:, t - wp:].reshape(bn, wp, B_KV, HD), r3(bvw)[:, t - wp:].reshape(bn, wp, B_KV, HD))
    return x, rows


def _sample_layer(x, lw, l, cos_t, sin_t, db, t, past, caches, state_wk, state_wv, page_table):
    (aq, ak, av), (bqc, bqr, bkc, bvc, bks, bvs, bkw, bvw, bg), (cq, ck, cv), gate = _project_all(
        x, lw, cos_t, sin_t, "s")
    r3 = lambda a: a.reshape(db, SROWS, a.shape[-1])
    c_ak, c_av, c_bkc, c_bvc, c_bks, c_bvs, c_ck, c_cv = caches
    o_a = _diff_sample(page_table, l, t, r3(aq), r3(ak), r3(av), c_ak, c_av, lw["lam"], lw["subln"],
                       lw["lam_init"], "diff_s")
    kcmp, vcmp = _compress_sample(page_table, l, c_bkc, c_bvc, lw["cmp"], "cmp_s")
    o_b = _nsa_sample(page_table, l, t, past, r3(bqc), r3(bqr), r3(bg), kcmp, vcmp, r3(bks), r3(bvs),
                      r3(bkw), r3(bvw), state_wk, state_wv, c_bks, c_bvs, "nsa_s")
    o_c = _moba_sample(page_table, l, t, r3(cq), r3(ck), r3(cv), c_ck, c_cv, "moba_s")
    n = db * SROWS
    x = _merge(x, gate, o_a.reshape(n, 512), o_b.reshape(n, 512), o_c.reshape(n, 512),
               lw["w_branch"], lw["w_out"], "merge_s")
    x = _ffn(x, lw, "s")
    new = lambda a, kv, w: r3(a)[:, :t].reshape(db, t, kv, w)

    def rolled(state_t, a):
        new_t = jnp.transpose(r3(a)[:, :t], (0, 2, 1))
        out_t = jnp.concatenate([state_t[l][:, :, t:], new_t], axis=2)
        return jnp.transpose(out_t, (0, 2, 1)).reshape(db, out_t.shape[2], B_KV, HD)

    rows = (new(ak, A_KV, 2 * HD), new(av, A_KV, 2 * HD), new(bkc, B_KV, HD), new(bvc, B_KV, HD),
            new(bks, B_KV, HD), new(bvs, B_KV, HD), new(ck, C_KV, HD), new(cv, C_KV, HD),
            rolled(state_wk, bkw), rolled(state_wv, bvw))
    return x, rows


def kernel(x_prompt, x_sample, cache_a_k, cache_a_v, cache_b_kc, cache_b_vc, cache_b_ks, cache_b_vs,
           cache_c_k, cache_c_v, state_b_wk, state_b_wv, page_table, norm_mix, norm_ffn, norm_final,
           w_in, a_lambda, a_subln, b_cmp_pe, b_cmp_w1, b_cmp_w2, w_branch, w_out,
           peer_wq, peer_subkeys, peer_u, peer_v):
    bn, t, _ = x_prompt.shape
    db, ts, _ = x_sample.shape
    depth = w_in.shape[0]
    past = page_table.shape[1] * PAGE_SIZE
    wb = state_b_wk.shape[2]
    assert past % MOBA_BLOCK == 0 and ts <= SROWS and wb == WINDOW and wb % LANES == 0
    rows_view = lambda c: c.reshape(c.shape[0], c.shape[1], c.shape[2] * c.shape[3], c.shape[4])
    lanes_view = lambda c: jnp.transpose(c, (0, 1, 3, 4, 2)).reshape(
        c.shape[0], c.shape[1], c.shape[3] * c.shape[4], c.shape[2])
    caches = ([rows_view(c) for c in (cache_a_k, cache_a_v)]
              + [lanes_view(c) for c in (cache_b_kc, cache_b_vc, cache_b_ks, cache_b_vs, cache_c_k, cache_c_v)])
    swk = lanes_view(state_b_wk)
    swv = lanes_view(state_b_wv)

    cos_p, sin_p = _rope_tables(jnp.arange(t, dtype=jnp.int32))
    tm_s = min(ROW_TILE, db * SROWS)
    cos_s, sin_s = _rope_tables(past + (jnp.arange(tm_s, dtype=jnp.int32) % SROWS))

    xp = x_prompt.reshape(bn * t, D_MODEL)
    xs = jnp.pad(x_sample, ((0, 0), (0, SROWS - ts), (0, 0))).reshape(db * SROWS, D_MODEL)
    rows_p, rows_s = [], []
    for l in range(depth):
        lw = _layer_weights(l, norm_mix, norm_ffn, w_in, a_lambda, a_subln, b_cmp_pe, b_cmp_w1, b_cmp_w2,
                            w_branch, w_out, peer_wq, peer_subkeys, peer_u, peer_v)
        xp, rp = _prompt_layer(xp, lw, cos_p, sin_p, bn, t)
        xs, rs = _sample_layer(xs, lw, l, cos_s, sin_s, db, ts, past, caches, swk, swv, page_table)
        rows_p.append(rp)
        rows_s.append(rs)
    y_prompt = _final_norm(xp, norm_final[None, :], "final_p").reshape(bn, t, D_MODEL)
    y_sample = _final_norm(xs, norm_final[None, :], "final_s").reshape(db, SROWS, D_MODEL)[:, :ts]
    outs_p = [jnp.stack(r, axis=0) for r in zip(*rows_p)]
    outs_s = [jnp.stack(r, axis=0) for r in zip(*rows_s)]
    return (y_prompt, y_sample, *outs_p, *outs_s)
```

```python
import functools
import math

import jax
import jax.numpy as jnp
import numpy as np
from jax import lax
from jax.experimental import pallas as pl
from jax.experimental.pallas import tpu as pltpu

F32 = jnp.float32
BF16 = jnp.bfloat16

D_MODEL = 1024
HD = 64
A_HEADS, A_KV = 4, 2
B_HEADS, B_KV = 8, 2
C_HEADS, C_KV = 8, 4
CMP_BLOCK, CMP_STRIDE, CMP_HIDDEN = 32, 16, 128
SEL_BLOCK, SEL_TOPK = 64, 8
CMP_PER_SEL_SHIFT = 2
WINDOW = 512
MOBA_BLOCK, MOBA_TOPK = 256, 3
PEER_HEADS, PEER_NKEYS, PEER_TOPK = 8, 128, 16
ROPE_THETA = 10000.0
EPS = 1e-6
PAGE_SIZE = 128

LANES = 128
NEG = -1e30
QK_SCALE_LOG2 = math.log2(math.e) * HD ** -0.5
VMEM_LIMIT = 48 * 1024 * 1024
ROW_TILE = 256
KV_CHUNK = 256
EXPERT_TILE = 2048
EXPERT_SUB = 256


def _params(*sem):
    return pltpu.CompilerParams(dimension_semantics=sem, vmem_limit_bytes=VMEM_LIMIT)


def _dot(a, b):
    return jnp.dot(a, b, preferred_element_type=F32)


def _dot_nt(a, b):
    return lax.dot_general(a, b, (((1,), (1,)), ((), ())), preferred_element_type=F32)


def _split3(a):
    hi = a.astype(BF16)
    r1 = a - hi.astype(F32)
    mid = r1.astype(BF16)
    lo = (r1 - mid.astype(F32)).astype(BF16)
    return hi, mid, lo


def _rms(x, w):
    return x * lax.rsqrt(jnp.mean(x * x, axis=-1, keepdims=True) + EPS) * w


def _proj_body(pieces, chunk, x_ref, nw_ref, w_ref, cos_ref, sin_ref, *outs):
    h = _rms(x_ref[...], nw_ref[...]).astype(BF16)
    cos = cos_ref[...]
    sin = sin_ref[...]
    lane = lax.broadcasted_iota(jnp.int32, (1, LANES), 1)
    first = (lane & (HD - 1)) < (HD // 2)
    n_cols = len(pieces) * LANES
    for c0 in range(0, n_cols, chunk):
        w = min(chunk, n_cols - c0)
        z = _dot(h, w_ref[:, c0:c0 + w])
        for p in range(w // LANES):
            zp = z[:, p * LANES:(p + 1) * LANES]
            for (oi, oc, rope) in pieces[c0 // LANES + p]:
                if rope:
                    rot = jnp.where(first, pltpu.roll(zp, LANES - HD // 2, 1), pltpu.roll(zp, HD // 2, 1))
                    outs[oi][:, oc:oc + LANES] = zp * cos + rot * sin
                else:
                    outs[oi][:, oc:oc + LANES] = zp


def _project(x, nw, w, cos_t, sin_t, pieces, out_widths, name):
    n = x.shape[0]
    tm = min(ROW_TILE, n)
    ntab = cos_t.shape[0] // tm
    ncols = w.shape[1]
    body = functools.partial(_proj_body, pieces, 512)
    return pl.pallas_call(
        body,
        grid=(n // tm,),
        in_specs=[
            pl.BlockSpec((tm, D_MODEL), lambda i: (i, 0)),
            pl.BlockSpec((1, D_MODEL), lambda i: (0, 0)),
            pl.BlockSpec((D_MODEL, ncols), lambda i: (0, 0)),
            pl.BlockSpec((tm, LANES), lambda i: (i % ntab, 0)),
            pl.BlockSpec((tm, LANES), lambda i: (i % ntab, 0)),
        ],
        out_specs=[pl.BlockSpec((tm, ow), lambda i: (i, 0)) for ow in out_widths],
        out_shape=[jax.ShapeDtypeStruct((n, ow), F32) for ow in out_widths],
        compiler_params=_params("parallel"),
        name=name,
    )(x, nw, w, cos_t, sin_t)


def _plain(oi, width):
    return [[(oi, c, False)] for c in range(0, width, LANES)]


def _roped(oi, width):
    return [[(oi, c, True)] for c in range(0, width, LANES)]


PIECES_A = _roped(0, 512) + _roped(1, 256) + _plain(2, 256)
WIDTHS_A = (512, 256, 256)
PIECES_B = ([[(0, c, False), (1, c, True)] for c in range(0, 512, LANES)]
            + _plain(2, 128) + _plain(3, 128) + _roped(4, 128) + _plain(5, 128)
            + _roped(6, 128) + _plain(7, 128) + _plain(8, 128))
WIDTHS_B = (512, 512, 128, 128, 128, 128, 128, 128, 128)
PIECES_C = _roped(0, 512) + _roped(1, 256) + _plain(2, 256)
WIDTHS_C = (512, 256, 256)
PIECES_G = _plain(0, 3072)
WIDTHS_G = (3072,)


def _rope_tables(pos):
    half = HD // 2
    freqs = jnp.power(ROPE_THETA, -jnp.arange(half, dtype=F32) / half)
    ang = pos.astype(F32)[:, None] * freqs[None, :]
    cos = jnp.cos(ang)
    sin = jnp.sin(ang)
    cos_t = jnp.concatenate([cos, cos, cos, cos], axis=1)
    sin_t = jnp.concatenate([-sin, sin, -sin, sin], axis=1)
    return cos_t, sin_t


def _flash_init(m_ref, l_ref, acc_ref):
    m_ref[...] = jnp.full(m_ref.shape, NEG, F32)
    l_ref[...] = jnp.zeros(l_ref.shape, F32)
    acc_ref[...] = jnp.zeros(acc_ref.shape, F32)


def _lane_blocks(x):
    return [x[:, c:c + LANES] for c in range(0, x.shape[1], LANES)]


def _flash_step(hh, q_emb, kc, vc, biases, m_ref, l_ref, acc_ref):
    blocks = _lane_blocks(_dot_nt(q_emb, kc))
    if biases is not None:
        blocks = [b if bi is None else b + bi for b, bi in zip(blocks, biases)]
    mx = blocks[0]
    for b in blocks[1:]:
        mx = jnp.maximum(mx, b)
    m_old = m_ref[hh]
    m_new = jnp.maximum(m_old, jnp.max(mx, axis=1, keepdims=True))
    alpha = jnp.exp2(m_old - m_new)
    ps = [jnp.exp2(b - m_new) for b in blocks]
    tot = ps[0]
    for p in ps[1:]:
        tot = tot + p
    l_ref[hh] = alpha * l_ref[hh] + jnp.sum(tot, axis=1, keepdims=True)
    acc_ref[hh] = alpha * acc_ref[hh] + _dot(jnp.concatenate(ps, axis=1).astype(BF16), vc)
    m_ref[hh] = m_new


def _causal_sweep(q0, tq, body):
    n_full = (q0 + 1) // KV_CHUNK
    n_pair = n_full // 2
    lax.fori_loop(0, n_pair, body(2 * KV_CHUNK, False), 0)
    lax.fori_loop(2 * n_pair, n_full, body(KV_CHUNK, False), 0)
    lax.fori_loop(n_full, (q0 + tq - 1) // KV_CHUNK + 1, body(KV_CHUNK, True), 0)


def _causal_bias(j, tk, qpos, inside=0.0):
    kpos = j * tk + lax.broadcasted_iota(jnp.int32, (1, tk), 1)
    return jnp.where(kpos <= qpos, inside, NEG)


def _flash_out(hh, l_ref, acc_ref):
    return acc_ref[hh] / jnp.maximum(l_ref[hh], 1e-30)


def _chunk(ref, j, tk):
    return ref[pl.ds(pl.multiple_of(j * tk, tk), tk), :].astype(BF16)


def _diff_body(tq, tk, q_pos0, lam_init, lam_ref, subln_ref, q_ref, k_ref, v_ref, o_ref,
               m_ref, l_ref, acc_ref):
    qi = pl.program_id(2)
    lane = lax.broadcasted_iota(jnp.int32, (1, LANES), 1)
    upper = lane >= HD
    q0 = q_pos0 + qi * tq
    qpos = q0 + lax.broadcasted_iota(jnp.int32, (tq, 1), 0)
    _flash_init(m_ref, l_ref, acc_ref)
    qs = []
    for r in range(2):
        qb = q_ref[:, r * LANES:(r + 1) * LANES] * QK_SCALE_LOG2
        qs.append(jnp.where(upper, 0.0, qb).astype(BF16))
        qs.append(jnp.where(upper, qb, 0.0).astype(BF16))
    def sweep(tk, diag):
        def body(j, carry):
            kc = _chunk(k_ref, j, tk)
            vc = _chunk(v_ref, j, tk)
            biases = _lane_blocks(_causal_bias(j, tk, qpos)) if diag else None
            for hh in range(4):
                _flash_step(hh, qs[hh], kc, vc, biases, m_ref, l_ref, acc_ref)
            return carry
        return body

    _causal_sweep(q0, tq, sweep)
    lp = lam_ref[...]
    lam = (jnp.exp(jnp.sum(lp[0:1] * lp[1:2], axis=1, keepdims=True))
           - jnp.exp(jnp.sum(lp[2:3] * lp[3:4], axis=1, keepdims=True)) + lam_init)
    for r in range(2):
        o = _flash_out(2 * r, l_ref, acc_ref) - lam * _flash_out(2 * r + 1, l_ref, acc_ref)
        o_ref[:, r * LANES:(r + 1) * LANES] = _rms(o, subln_ref[...]) * (1.0 - lam_init)


def _diff_attention(aq, ak, av, lam_p, subln, lam_init, q_pos0, name):
    bn, tq_all, _ = aq.shape
    lk = ak.shape[1]
    tq = min(ROW_TILE, tq_all)
    tk = KV_CHUNK
    body = functools.partial(_diff_body, tq, tk, q_pos0, lam_init)
    return pl.pallas_call(
        body,
        grid=(bn, A_KV, tq_all // tq),
        in_specs=[
            pl.BlockSpec((4, HD), lambda b, g, i: (0, 0)),
            pl.BlockSpec((1, 2 * HD), lambda b, g, i: (0, 0)),
            pl.BlockSpec((None, tq, 256), lambda b, g, i: (b, i, g)),
            pl.BlockSpec((None, lk, LANES), lambda b, g, i: (b, 0, g)),
            pl.BlockSpec((None, lk, LANES), lambda b, g, i: (b, 0, g)),
        ],
        out_specs=pl.BlockSpec((None, tq, 256), lambda b, g, i: (b, i, g)),
        out_shape=jax.ShapeDtypeStruct((bn, tq_all, 512), F32),
        scratch_shapes=[pltpu.VMEM((4, tq, LANES), F32), pltpu.VMEM((4, tq, LANES), F32),
                        pltpu.VMEM((4, tq, LANES), F32)],
        compiler_params=_params("parallel", "parallel", "arbitrary"),
        name=name,
    )(lam_p, subln, aq, ak, av)


def _compress_one(x_ref, pe_ref, w1_ref, w2_ref, o_ref):
    n_half = CMP_STRIDE
    xs = [x_ref[pl.ds(p, LANES, stride=CMP_STRIDE), :] for p in range(n_half)]
    xa = jnp.concatenate([(x + pe_ref[p]).astype(BF16) for p, x in enumerate(xs)], axis=1)
    xb = jnp.concatenate([(x + pe_ref[n_half + p]).astype(BF16) for p, x in enumerate(xs)], axis=1)
    w1 = w1_ref[...].reshape(2, n_half * LANES, 2 * CMP_HIDDEN)
    h = _dot(xa, w1[0]) + pltpu.roll(_dot(xb, w1[1]), LANES - 1, 0)
    y = _dot(jax.nn.gelu(h).astype(BF16), w2_ref[...])
    row = lax.broadcasted_iota(jnp.int32, (LANES, 1), 0)
    o_ref[...] = jnp.where(row < LANES - 1, y, 0.0)


def _compress_body(xk_ref, xv_ref, pek_ref, pev_ref, w1k_ref, w1v_ref, w2k_ref, w2v_ref, ok_ref, ov_ref):
    _compress_one(xk_ref, pek_ref, w1k_ref, w2k_ref, ok_ref)
    _compress_one(xv_ref, pev_ref, w1v_ref, w2v_ref, ov_ref)


def _compress(xk, xv, cw, name):
    bn = xk.shape[0]
    n_tok = LANES * CMP_STRIDE
    x_spec = pl.BlockSpec((None, n_tok, LANES), lambda b: (b, 0, 0))
    pe_spec = pl.BlockSpec((CMP_BLOCK, 1, LANES), lambda b: (0, 0, 0))
    w1_spec = pl.BlockSpec((CMP_BLOCK, LANES, 2 * CMP_HIDDEN), lambda b: (0, 0, 0))
    w2_spec = pl.BlockSpec((2 * CMP_HIDDEN, LANES), lambda b: (0, 0))
    o_spec = pl.BlockSpec((None, LANES, LANES), lambda b: (b, 0, 0))
    return pl.pallas_call(
        _compress_body,
        grid=(bn,),
        in_specs=[x_spec, x_spec, pe_spec, pe_spec, w1_spec, w1_spec, w2_spec, w2_spec],
        out_specs=[o_spec, o_spec],
        out_shape=[jax.ShapeDtypeStruct((bn, LANES, LANES), F32)] * 2,
        compiler_params=_params("parallel"),
        name=name,
    )(xk, xv, cw["pek"], cw["pev"], cw["w1k"], cw["w1v"], cw["w2k"], cw["w2v"])


def _compress_weights(pe, w1, w2):
    out = {}
    for idx, tag in ((0, "k"), (1, "v")):
        w1p = w1[idx].reshape(CMP_BLOCK, HD, CMP_HIDDEN)
        z = jnp.zeros_like(w1p)
        w1b = jnp.concatenate([jnp.concatenate([w1p, z], axis=2), jnp.concatenate([z, w1p], axis=2)], axis=1)
        z2 = jnp.zeros_like(w2[idx])
        w2b = jnp.concatenate([jnp.concatenate([w2[idx], z2], axis=1), jnp.concatenate([z2, w2[idx]], axis=1)], axis=0)
        out["w1" + tag] = w1b.astype(BF16)
        out["w2" + tag] = w2b.astype(BF16)
        out["pe" + tag] = jnp.concatenate([pe[idx], pe[idx]], axis=1)[:, None, :]
    return out


def _rank_desc(vals, n, lane):
    rank = jnp.zeros(vals.shape, F32)
    for i in range(n):
        col = vals[:, i:i + 1]
        tie = jnp.where(lane > i, 1.0, 0.0)
        rank = rank + jnp.where(col > vals, 1.0, 0.0) + jnp.where(col == vals, tie, 0.0)
    return rank


def _nsa_body(tq, tk, q_pos0, ns, w_pos0, w_valid, lw,
              qc_ref, qr_ref, bg_ref, kcmp_ref, vcmp_ref, ks_ref, vs_ref, kw_ref, vw_ref, o_ref,
              m_ref, l_ref, acc_ref):
    g = pl.program_id(1)
    qi = pl.program_id(2)
    lane = lax.broadcasted_iota(jnp.int32, (1, LANES), 1)
    halfi = lane >> 6
    q0 = q_pos0 + qi * tq
    qpos = q0 + lax.broadcasted_iota(jnp.int32, (tq, 1), 0)
    rep = B_HEADS // B_KV

    def embed_all(ref, scale):
        out = []
        for pr in range(rep // 2):
            v = ref[:, pr * LANES:(pr + 1) * LANES] * scale
            swapped = pltpu.roll(v, HD, 1)
            for u in range(2):
                out.append(jnp.where(halfi == g, jnp.where(g == u, v, swapped), 0.0).astype(BF16))
        return out

    def pair_out(o_even, o_odd):
        aligned = jnp.where(g == 0, o_even, o_odd)
        other = jnp.where(g == 0, o_odd, o_even)
        return jnp.where(halfi == g, aligned, pltpu.roll(other, HD, 1))

    kcmp = kcmp_ref[...].astype(BF16)
    vcmp = vcmp_ref[...].astype(BF16)
    cmask = (lane * CMP_STRIDE + (CMP_BLOCK - 1)) <= qpos
    psum = jnp.zeros((tq, LANES), F32)
    o_cmp = []
    qsc = embed_all(qc_ref, HD ** -0.5)
    for r in range(rep):
        s = jnp.where(cmask, _dot_nt(qsc[r], kcmp), NEG)
        m = jnp.max(s, axis=1, keepdims=True)
        p = jnp.where(cmask, jnp.exp(s - m), 0.0)
        p = p / jnp.maximum(jnp.sum(p, axis=1, keepdims=True), 1e-30)
        psum = psum + p
        o_cmp.append(_dot(p.astype(BF16), vcmp))

    ci = lax.broadcasted_iota(jnp.int32, (LANES, LANES), 0)
    cj = lax.broadcasted_iota(jnp.int32, (LANES, LANES), 1)
    onehot = jnp.where((ci >> CMP_PER_SEL_SHIFT) == cj, 1.0, 0.0).astype(BF16)
    hi, mid, lo = _split3(psum)
    imp = _dot(hi, onehot) + _dot(mid, onehot) + _dot(lo, onehot)
    cur = qpos >> 6
    impm = jnp.where(lane == cur, jnp.inf, jnp.where(lane < cur, imp, -jnp.inf))
    rank = _rank_desc(impm, ns, lane)
    sel = jnp.where(rank < float(SEL_TOPK), jnp.where(lane <= cur, 1.0, 0.0), 0.0).astype(BF16)

    qsr = embed_all(qr_ref, QK_SCALE_LOG2)
    _flash_init(m_ref, l_ref, acc_ref)
    def sel_sweep(ck, diag):
        bpc = ck // SEL_BLOCK

        def body(j, carry):
            kc = _chunk(ks_ref, j, ck)
            vc = _chunk(vs_ref, j, ck)
            ei = lax.broadcasted_iota(jnp.int32, (LANES, ck), 0)
            el = lax.broadcasted_iota(jnp.int32, (LANES, ck), 1)
            expand = jnp.where(ei == j * bpc + (el >> 6), 1.0, 0.0).astype(BF16)
            bias = (_dot(sel, expand) - 1.0) * (-NEG)
            if diag:
                bias = _causal_bias(j, ck, qpos, bias)
            biases = _lane_blocks(bias)
            for r in range(rep):
                _flash_step(r, qsr[r], kc, vc, biases, m_ref, l_ref, acc_ref)
            return carry
        return body

    _causal_sweep(q0, tq, sel_sweep)
    o_slc = [_flash_out(r, l_ref, acc_ref) for r in range(rep)]

    _flash_init(m_ref, l_ref, acc_ref)
    w_lo = jnp.maximum(q0 - WINDOW - w_pos0, 0) // tk
    w_hi = jnp.minimum(q0 + tq - 1 - w_pos0, lw - 1) // tk + 1

    def win_body(j, carry):
        kc = _chunk(kw_ref, j, tk)
        vc = _chunk(vw_ref, j, tk)
        kidx = j * tk + lax.broadcasted_iota(jnp.int32, (1, tk), 1)
        dist = qpos - (kidx + w_pos0)
        inwin = jnp.where(dist >= 0, jnp.where(dist <= WINDOW, 0.0, NEG), NEG)
        biases = _lane_blocks(jnp.where(kidx < w_valid, inwin, NEG))
        for r in range(rep):
            _flash_step(r, qsr[r], kc, vc, biases, m_ref, l_ref, acc_ref)
        return carry

    lax.fori_loop(w_lo, w_hi, win_body, 0)

    sig = jax.nn.sigmoid(bg_ref[...])
    outs = []
    for r in range(rep):
        o_win = _flash_out(r, l_ref, acc_ref)
        base = (g * rep + r) * 3
        gates = [jnp.sum(jnp.where(lane == base + c, sig, 0.0), axis=1, keepdims=True) for c in range(3)]
        outs.append(gates[0] * o_cmp[r] + gates[1] * o_slc[r] + gates[2] * o_win)
    for pr in range(rep // 2):
        o_ref[:, pr * LANES:(pr + 1) * LANES] = pair_out(outs[2 * pr], outs[2 * pr + 1])


def _nsa_attention(bqc, bqr, bg, kcmp, vcmp, ks, vs, kw, vw, q_pos0, k_valid, w_pos0, w_valid, name):
    bn, tq_all, _ = bqc.shape
    lk = ks.shape[1]
    lw = kw.shape[1]
    tq = min(ROW_TILE, tq_all)
    tk = KV_CHUNK
    ns = -(-k_valid // SEL_BLOCK)
    body = functools.partial(_nsa_body, tq, tk, q_pos0, ns, w_pos0, w_valid, lw)
    q_spec = pl.BlockSpec((None, tq, 256), lambda b, g, i: (b, i, g))
    full = lambda rows: pl.BlockSpec((None, rows, LANES), lambda b, g, i: (b, 0, 0))
    return pl.pallas_call(
        body,
        grid=(bn, B_KV, tq_all // tq),
        in_specs=[q_spec, q_spec, pl.BlockSpec((None, tq, LANES), lambda b, g, i: (b, i, 0)),
                  full(LANES), full(LANES), full(lk), full(lk), full(lw), full(lw)],
        out_specs=pl.BlockSpec((None, tq, 256), lambda b, g, i: (b, i, g)),
        out_shape=jax.ShapeDtypeStruct((bn, tq_all, 512), F32),
        scratch_shapes=[pltpu.VMEM((4, tq, LANES), F32), pltpu.VMEM((4, tq, LANES), F32),
                        pltpu.VMEM((4, tq, LANES), F32)],
        compiler_params=_params("parallel", "parallel", "arbitrary"),
        name=name,
    )(bqc, bqr, bg, kcmp, vcmp, ks, vs, kw, vw)


def _moba_body(tq, tk, q_pos0, nb, q_ref, k_ref, v_ref, o_ref, km_ref, m_ref, l_ref, acc_ref):
    g = pl.program_id(1)
    qi = pl.program_id(2)
    gh = g % 2
    lane = lax.broadcasted_iota(jnp.int32, (1, LANES), 1)
    halfi = lane >> 6
    q0 = q_pos0 + qi * tq
    qpos = q0 + lax.broadcasted_iota(jnp.int32, (tq, 1), 0)
    cur = qpos >> 8
    rep = C_HEADS // C_KV

    @pl.when(qi == 0)
    def _():
        km_ref[...] = jnp.zeros(km_ref.shape, F32)
        km_ref[0:nb, :] = jnp.sum(k_ref[...].reshape(nb, MOBA_BLOCK, LANES), axis=1) * (1.0 / MOBA_BLOCK)

    km_hi, km_mid, _ = _split3(km_ref[...])

    qv = q_ref[...]
    qs = []
    sels = []
    swapped = pltpu.roll(qv, HD, 1)
    for r in range(rep):
        qe = jnp.where(halfi == gh, jnp.where(gh == r, qv, swapped), 0.0)
        q_hi, q_mid, _ = _split3(qe)
        s_blk = _dot_nt(q_hi, km_hi) + _dot_nt(q_hi, km_mid) + _dot_nt(q_mid, km_hi)
        sm = jnp.where(lane < cur, s_blk, -jnp.inf)
        rank = _rank_desc(sm, nb, lane)
        past = jnp.where(rank < float(MOBA_TOPK), jnp.where(lane < cur, 1.0, 0.0), 0.0)
        sels.append(jnp.where(lane == cur, 1.0, past))
        qs.append((qe * QK_SCALE_LOG2).astype(BF16))

    _flash_init(m_ref, l_ref, acc_ref)
    def sweep(ck, diag):
        bpc = ck // MOBA_BLOCK
        lpb = MOBA_BLOCK // LANES

        def body(j, carry):
            kc = _chunk(k_ref, j, ck)
            vc = _chunk(v_ref, j, ck)
            diag_biases = _lane_blocks(_causal_bias(j, ck, qpos)) if diag else None
            for r in range(rep):
                if diag:
                    biases = diag_biases
                else:
                    biases = []
                    for u in range(bpc):
                        col = jnp.sum(jnp.where(lane == j * bpc + u, sels[r], 0.0), axis=1, keepdims=True)
                        biases += [(col - 1.0) * (-NEG)] * lpb
                _flash_step(r, qs[r], kc, vc, biases, m_ref, l_ref, acc_ref)
            return carry
        return body

    _causal_sweep(q0, tq, sweep)
    o0 = _flash_out(0, l_ref, acc_ref)
    o1 = _flash_out(1, l_ref, acc_ref)
    aligned = jnp.where(gh == 0, o0, o1)
    other = jnp.where(gh == 0, o1, o0)
    o_ref[...] = jnp.where(halfi == gh, aligned, pltpu.roll(other, HD, 1))


def _moba_attention(cq, ck, cv, q_pos0, name):
    bn, tq_all, _ = cq.shape
    lk = ck.shape[1]
    tq = min(ROW_TILE, tq_all)
    tk = MOBA_BLOCK
    nb = lk // MOBA_BLOCK
    assert tq == MOBA_BLOCK == KV_CHUNK and q_pos0 % MOBA_BLOCK == 0
    body = functools.partial(_moba_body, tq, tk, q_pos0, nb)
    return pl.pallas_call(
        body,
        grid=(bn, C_KV, tq_all // tq),
        in_specs=[
            pl.BlockSpec((None, tq, LANES), lambda b, g, i: (b, i, g)),
            pl.BlockSpec((None, lk, LANES), lambda b, g, i: (b, 0, g // 2)),
            pl.BlockSpec((None, lk, LANES), lambda b, g, i: (b, 0, g // 2)),
        ],
        out_specs=pl.BlockSpec((None, tq, LANES), lambda b, g, i: (b, i, g)),
        out_shape=jax.ShapeDtypeStruct((bn, tq_all, 512), F32),
        scratch_shapes=[pltpu.VMEM((LANES, LANES), F32),
                        pltpu.VMEM((2, tq, LANES), F32), pltpu.VMEM((2, tq, LANES), F32),
                        pltpu.VMEM((2, tq, LANES), F32)],
        compiler_params=_params("parallel", "parallel", "arbitrary"),
        name=name,
    )(cq, ck, cv)


def _merge_body(x_ref, gate_ref, oa_ref, ob_ref, oc_ref, wb_ref, wo_ref, y_ref):
    y = jnp.zeros((x_ref.shape[0], D_MODEL), F32)
    for c, o_ref in enumerate((oa_ref, ob_ref, oc_ref)):
        br = _dot(o_ref[...].astype(BF16), wb_ref[c])
        y = y + jax.nn.sigmoid(gate_ref[:, c * D_MODEL:(c + 1) * D_MODEL]) * br
    y_ref[...] = x_ref[...] + _dot(y.astype(BF16), wo_ref[...])


def _merge(x, gate, o_a, o_b, o_c, wb, wo, name):
    n = x.shape[0]
    tm = min(ROW_TILE, n)
    row = lambda w: pl.BlockSpec((tm, w), lambda i: (i, 0))
    return pl.pallas_call(
        _merge_body,
        grid=(n // tm,),
        in_specs=[row(D_MODEL), row(3 * D_MODEL), row(512), row(512), row(512),
                  pl.BlockSpec((3, 512, D_MODEL), lambda i: (0, 0, 0)),
                  pl.BlockSpec((D_MODEL, D_MODEL), lambda i: (0, 0))],
        out_specs=row(D_MODEL),
        out_shape=jax.ShapeDtypeStruct((n, D_MODEL), F32),
        compiler_params=_params("parallel"),
        name=name,
    )(x, gate, o_a, o_b, o_c, wb, wo)


def _bitonic_desc(vs):
    vs = list(vs)
    n = len(vs)
    k = 2
    while k <= n:
        j = k // 2
        while j >= 1:
            for i in range(n):
                partner = i ^ j
                if partner > i:
                    hi = jnp.maximum(vs[i], vs[partner])
                    lo = jnp.minimum(vs[i], vs[partner])
                    vs[i], vs[partner] = (hi, lo) if (i & k) == 0 else (lo, hi)
            j //= 2
        k *= 2
    return vs


def _pop_heads(cols, n_out):
    cols = list(cols)
    for k in range(n_out):
        m = jnp.max(cols[0], axis=0, keepdims=True)
        eq = cols[0] == m
        yield m, eq
        depth = min(len(cols), n_out - k)
        for d in range(depth - 1):
            cols[d] = jnp.where(eq, cols[d + 1], cols[d])
        cols[depth - 1] = jnp.where(eq, -jnp.inf, cols[depth - 1])


def _peer_stats_body(tm, x_ref, nw_ref, wq_ref, sk_ref, st_ref, stat_ref, top_ref, nxt_ref):
    h = _rms(x_ref[...], nw_ref[...]).astype(BF16)
    q = _dot(h, wq_ref[...]).astype(BF16)
    n_hc = 2 * PEER_HEADS
    for hc in range(n_hc):
        st_ref[hc] = _dot_nt(sk_ref[hc], q[:, hc * LANES:(hc + 1) * LANES])

    sub = 8
    n_out = PEER_TOPK + 1

    def top_body(hc, carry):
        for hf in range(tm // LANES):
            s = st_ref[hc, :, hf * LANES:(hf + 1) * LANES]
            cols = _bitonic_desc([s[d * sub:(d + 1) * sub] for d in range(PEER_NKEYS // sub)])
            vals = [m for m, _ in _pop_heads(cols, n_out)]
            top_ref[hc, :, hf * LANES:(hf + 1) * LANES] = jnp.concatenate(vals[:PEER_TOPK], axis=0)
            nxt_ref[hc, :, hf * LANES:(hf + 1) * LANES] = vals[PEER_TOPK]
        return carry

    lax.fori_loop(0, n_hc, top_body, 0)

    row8 = lax.broadcasted_iota(jnp.int32, (sub, LANES), 0)
    fill = jnp.full((sub, LANES), -jnp.inf, F32)

    def head_body(hd, carry):
        for hf in range(tm // LANES):
            sl = slice(hf * LANES, (hf + 1) * LANES)
            a = top_ref[2 * hd, :, sl]
            b = top_ref[2 * hd + 1, :, sl]
            cands = [a[0:1] + b[0:sub], a[0:1] + b[sub:2 * sub]]
            for i in range(1, sub):
                cands.append(jnp.where(row8 < PEER_TOPK // (i + 1), a[i:i + 1] + b[0:sub], -jnp.inf))
            cands.append(a[sub:2 * sub] + b[0:1])
            mx = a[0:1] + b[0:1]
            cum = jnp.zeros((1, LANES), F32)
            tau = mx
            nxt = mx
            cols = _bitonic_desc(cands + [fill] * (16 - len(cands)))
            for m, eq in _pop_heads(cols, n_out):
                tau = jnp.where(cum < float(PEER_TOPK), m, tau)
                nxt = jnp.where(cum < float(n_out), m, nxt)
                cum = cum + jnp.sum(jnp.where(eq, 1.0, 0.0), axis=0, keepdims=True)
            nxt = jnp.maximum(nxt, jnp.maximum(a[0:1] + nxt_ref[2 * hd + 1, :, sl], nxt_ref[2 * hd, :, sl] + b[0:1]))
            z = None
            for c in cands:
                t = jnp.sum(jnp.where(c >= tau, jnp.exp(c - mx), 0.0), axis=0, keepdims=True)
                z = t if z is None else z + t
            stat_ref[hd, :, sl] = 0.5 * (tau + nxt)
            stat_ref[PEER_HEADS + hd, :, sl] = a[0:1]
            stat_ref[2 * PEER_HEADS + hd, :, sl] = b[0:1]
            stat_ref[3 * PEER_HEADS + hd, :, sl] = 1.0 / z
        return carry

    lax.fori_loop(0, PEER_HEADS, head_body, 0)


def _peer_stats(x, nw, wq, sk, name):
    n = x.shape[0]
    tm = min(ROW_TILE, n)
    n_hc = 2 * PEER_HEADS
    return pl.pallas_call(
        functools.partial(_peer_stats_body, tm),
        grid=(n // tm,),
        in_specs=[pl.BlockSpec((tm, D_MODEL), lambda i: (i, 0)),
                  pl.BlockSpec((1, D_MODEL), lambda i: (0, 0)),
                  pl.BlockSpec((D_MODEL, n_hc * LANES), lambda i: (0, 0)),
                  pl.BlockSpec((n_hc, LANES, LANES), lambda i: (0, 0, 0))],
        out_specs=[pl.BlockSpec((n_hc, LANES, tm), lambda i: (0, 0, i)),
                   pl.BlockSpec((4 * PEER_HEADS, 1, tm), lambda i: (0, 0, i))],
        out_shape=[jax.ShapeDtypeStruct((n_hc, LANES, n), F32),
                   jax.ShapeDtypeStruct((4 * PEER_HEADS, 1, n), F32)],
        scratch_shapes=[pltpu.VMEM((n_hc, PEER_TOPK, tm), F32), pltpu.VMEM((n_hc, 1, tm), F32)],
        compiler_params=_params("parallel"),
        name=name,
    )(x, nw, wq, sk)


def _peer_dense_body(tm, te, x_ref, nw_ref, st_ref, stat_ref, u_ref, vt_ref, o_ref,
                     h_ref, thr_ref, w1_ref, s2c_ref, e2_ref, acc_ref):
    e = pl.program_id(1)

    @pl.when(e == 0)
    def _():
        h_ref[...] = _rms(x_ref[...], nw_ref[...]).astype(BF16)
        acc_ref[...] = jnp.zeros(acc_ref.shape, F32)
        for hd in range(PEER_HEADS):
            tau = stat_ref[hd]
            m1 = stat_ref[PEER_HEADS + hd]
            m2 = stat_ref[2 * PEER_HEADS + hd]
            inv_z = stat_ref[3 * PEER_HEADS + hd]
            thr_ref[hd] = m1 - st_ref[2 * hd]
            w1_ref[hd] = jnp.exp(st_ref[2 * hd] - m1) * inv_z
            s2c_ref[hd] = (st_ref[2 * hd + 1] - (tau - m1)).astype(BF16)
            e2_ref[hd] = jnp.exp(st_ref[2 * hd + 1] - m2).astype(BF16)

    n_i = te // PEER_NKEYS
    zero = jnp.zeros((), BF16)
    h = h_ref[...]
    total = None
    for k in range(te // EXPERT_SUB):
        rows = slice(k * EXPERT_SUB, (k + 1) * EXPERT_SUB)
        act = jax.nn.gelu(_dot_nt(u_ref[rows, :], h)).astype(BF16)
        parts = []
        for ii in range(EXPERT_SUB // PEER_NKEYS):
            i = e * n_i + k * (EXPERT_SUB // PEER_NKEYS) + ii
            gmat = jnp.zeros((PEER_NKEYS, tm), BF16)
            for hd in range(PEER_HEADS):
                thr = thr_ref[hd, pl.ds(i, 1), :].astype(BF16)
                w1 = w1_ref[hd, pl.ds(i, 1), :].astype(BF16)
                gmat = gmat + jnp.where(s2c_ref[hd] > thr, e2_ref[hd] * w1, zero)
            parts.append(gmat * act[ii * PEER_NKEYS:(ii + 1) * PEER_NKEYS])
        t = _dot(vt_ref[:, rows], jnp.concatenate(parts, axis=0))
        total = t if total is None else total + t
    acc_ref[...] += total

    @pl.when(e == pl.num_programs(1) - 1)
    def _():
        o_ref[...] = x_ref[...] + acc_ref[...].T


def _peer_dense(x, nw, st, stat, u_bf, vt_bf, name):
    n = x.shape[0]
    tm = min(ROW_TILE, n)
    te = EXPERT_TILE
    n_exp = u_bf.shape[0]
    n_hc = 2 * PEER_HEADS
    return pl.pallas_call(
        functools.partial(_peer_dense_body, tm, te),
        grid=(n // tm, n_exp // te),
        in_specs=[pl.BlockSpec((tm, D_MODEL), lambda i, e: (i, 0)),
                  pl.BlockSpec((1, D_MODEL), lambda i, e: (0, 0)),
                  pl.BlockSpec((n_hc, LANES, tm), lambda i, e: (0, 0, i)),
                  pl.BlockSpec((4 * PEER_HEADS, 1, tm), lambda i, e: (0, 0, i)),
                  pl.BlockSpec((te, D_MODEL), lambda i, e: (e, 0)),
                  pl.BlockSpec((D_MODEL, te), lambda i, e: (0, e))],
        out_specs=pl.BlockSpec((tm, D_MODEL), lambda i, e: (i, 0)),
        out_shape=jax.ShapeDtypeStruct((n, D_MODEL), F32),
        scratch_shapes=[pltpu.VMEM((tm, D_MODEL), BF16),
                        pltpu.VMEM((PEER_HEADS, PEER_NKEYS, tm), F32),
                        pltpu.VMEM((PEER_HEADS, PEER_NKEYS, tm), F32),
                        pltpu.VMEM((PEER_HEADS, PEER_NKEYS, tm), BF16),
                        pltpu.VMEM((PEER_HEADS, PEER_NKEYS, tm), BF16),
                        pltpu.VMEM((D_MODEL, tm), F32)],
        compiler_params=_params("parallel", "arbitrary"),
        name=name,
    )(x, nw, st, stat, u_bf, vt_bf)


def _norm_body(x_ref, w_ref, o_ref):
    o_ref[...] = _rms(x_ref[...], w_ref[...])


def _final_norm(x, w, name):
    n = x.shape[0]
    tm = min(ROW_TILE, n)
    return pl.pallas_call(
        _norm_body,
        grid=(n // tm,),
        in_specs=[pl.BlockSpec((tm, D_MODEL), lambda i: (i, 0)), pl.BlockSpec((1, D_MODEL), lambda i: (0, 0))],
        out_specs=pl.BlockSpec((tm, D_MODEL), lambda i: (i, 0)),
        out_shape=jax.ShapeDtypeStruct((n, D_MODEL), F32),
        compiler_params=_params("parallel"),
        name=name,
    )(x, w)


SROWS = 8


def _page_specs(n_pages, rows, layer):
    def spec(p):
        return pl.BlockSpec((None, None, rows, LANES), lambda b, pt: (layer, pt[b, p], 0, 0))
    return [spec(p) for p in range(n_pages)]


def _row_spec(width):
    return pl.BlockSpec((None, SROWS, width), lambda b, pt: (b, 0, 0))


def _dup(x):
    return jnp.concatenate([x, x], axis=0)


def _pad_rows(x):
    return jnp.concatenate([x, jnp.zeros((PAGE_SIZE - x.shape[0], x.shape[1]), F32)], axis=0)


def _new_mask(m_rows, n_new):
    lane = lax.broadcasted_iota(jnp.int32, (1, LANES), 1)
    trow = lax.broadcasted_iota(jnp.int32, (m_rows, 1), 0) & (SROWS - 1)
    return jnp.where(lane < n_new, jnp.where(lane <= trow, 1.0, 0.0), 0.0) > 0.5


def _softmax_pieces(pieces):
    mx = None
    for s, mk in pieces:
        sm = s if mk is None else jnp.where(mk, s, NEG)
        mx = sm if mx is None else jnp.maximum(mx, sm)
    m = jnp.max(mx, axis=1, keepdims=True)
    ps = []
    tot = None
    for s, mk in pieces:
        p = jnp.exp(s - m)
        if mk is not None:
            p = jnp.where(mk, p, 0.0)
        tot = p if tot is None else tot + p
        ps.append(p.astype(BF16))
    return ps, jnp.sum(tot, axis=1, keepdims=True)


def _scores_kt(q, kts):
    return [_dot(q, _dup(kt.astype(BF16))) for kt in kts]


def _values_kt(ps, vts):
    o = None
    for p, vt in zip(ps, vts):
        t = _dot_nt(p, _dup(vt.astype(BF16)))
        o = t if o is None else o + t
    return o


def _diff_s_body(n_pages, n_new, lam_init, pt_ref, lam_ref, subln_ref, q_ref, kn_ref, vn_ref, *rest):
    k_refs = rest[:n_pages]
    v_refs = rest[n_pages:2 * n_pages]
    o_ref = rest[2 * n_pages]
    lane = lax.broadcasted_iota(jnp.int32, (1, LANES), 1)
    upper = lane >= HD
    new_mask = _new_mask(4 * SROWS, n_new)
    lp = lam_ref[...]
    lam = (jnp.exp(jnp.sum(lp[0:1] * lp[1:2], axis=1, keepdims=True))
           - jnp.exp(jnp.sum(lp[2:3] * lp[3:4], axis=1, keepdims=True)) + lam_init)
    for g in range(A_KV):
        parts = []
        for r in range(A_HEADS // A_KV):
            qb = q_ref[:, (2 * g + r) * LANES:(2 * g + r + 1) * LANES] * (HD ** -0.5)
            parts += [jnp.where(upper, 0.0, qb), jnp.where(upper, qb, 0.0)]
        q = jnp.concatenate(parts, axis=0).astype(BF16)
        pieces = [(_dot_nt(q, k_refs[p][pl.ds(g, PAGE_SIZE, stride=A_KV), :].astype(BF16)), None)
                  for p in range(n_pages)]
        knew = _pad_rows(kn_ref[:, g * LANES:(g + 1) * LANES]).astype(BF16)
        vnew = _pad_rows(vn_ref[:, g * LANES:(g + 1) * LANES]).astype(BF16)
        pieces.append((_dot_nt(q, knew), new_mask))
        ps, tot = _softmax_pieces(pieces)
        o = _dot(ps[n_pages], vnew)
        for p in range(n_pages):
            o = o + _dot(ps[p], v_refs[p][pl.ds(g, PAGE_SIZE, stride=A_KV), :].astype(BF16))
        o = o / tot
        for r in range(A_HEADS // A_KV):
            d = o[2 * r * SROWS:(2 * r + 1) * SROWS] - lam * o[(2 * r + 1) * SROWS:(2 * r + 2) * SROWS]
            o_ref[:, (2 * g + r) * LANES:(2 * g + r + 1) * LANES] = _rms(d, subln_ref[...]) * (1.0 - lam_init)


def _diff_sample(page_table, layer, n_new, aq, ak_new, av_new, cache_k, cache_v, lam_p, subln, lam_init, name):
    db, n_pages = page_table.shape
    const = lambda shape: pl.BlockSpec(shape, lambda b, pt: (0,) * len(shape))
    in_specs = ([const((4, HD)), const((1, 2 * HD)), _row_spec(512), _row_spec(256), _row_spec(256)]
                + _page_specs(n_pages, PAGE_SIZE * A_KV, layer) + _page_specs(n_pages, PAGE_SIZE * A_KV, layer))
    grid_spec = pltpu.PrefetchScalarGridSpec(num_scalar_prefetch=1, grid=(db,), in_specs=in_specs,
                                             out_specs=_row_spec(512))
    return pl.pallas_call(
        functools.partial(_diff_s_body, n_pages, n_new, lam_init),
        grid_spec=grid_spec,
        out_shape=jax.ShapeDtypeStruct((db, SROWS, 512), F32),
        compiler_params=_params("parallel"),
        name=name,
    )(page_table, lam_p, subln, aq, ak_new, av_new, *([cache_k] * n_pages), *([cache_v] * n_pages))


def _moba_s_body(n_pages, n_new, pt_ref, q_ref, kn_ref, vn_ref, *rest):
    k_refs = rest[:n_pages]
    v_refs = rest[n_pages:2 * n_pages]
    o_ref = rest[2 * n_pages]
    lane = lax.broadcasted_iota(jnp.int32, (1, LANES), 1)
    halfi = lane >> 6
    rep = C_HEADS // C_KV
    new_mask = _new_mask(rep * SROWS, n_new)
    knt = _pad_rows(kn_ref[...]).T
    vnt = _pad_rows(vn_ref[...]).T
    ppb = MOBA_BLOCK // PAGE_SIZE
    n_blocks = n_pages // ppb
    for g in range(C_KV):
        rows = slice(g * HD, (g + 1) * HD)
        qb = q_ref[:, g * LANES:(g + 1) * LANES] * (HD ** -0.5)
        q = jnp.concatenate([jnp.where(halfi == r, qb, 0.0) for r in range(rep)], axis=0).astype(BF16)
        ss = _scores_kt(q, [k_refs[p][rows, :] for p in range(n_pages)] + [knt[rows, :]])
        bs = []
        for j in range(n_blocks):
            acc = ss[j * ppb]
            for u in range(1, ppb):
                acc = acc + ss[j * ppb + u]
            bs.append(jnp.sum(acc, axis=1, keepdims=True))
        pieces = []
        for j in range(n_blocks):
            rank = jnp.zeros(bs[j].shape, F32)
            for i in range(n_blocks):
                if i < j:
                    rank = rank + jnp.where(bs[i] >= bs[j], 1.0, 0.0)
                elif i > j:
                    rank = rank + jnp.where(bs[i] > bs[j], 1.0, 0.0)
            keep = rank < float(MOBA_TOPK)
            for u in range(ppb):
                pieces.append((ss[j * ppb + u], keep))
        pieces.append((ss[n_pages], new_mask))
        ps, tot = _softmax_pieces(pieces)
        o = _values_kt(ps, [v_refs[p][rows, :] for p in range(n_pages)] + [vnt[rows, :]]) / tot
        o_ref[:, g * LANES:(g + 1) * LANES] = jnp.where(halfi == 0, o[0:SROWS], o[SROWS:2 * SROWS])


def _moba_sample(page_table, layer, n_new, cq, ck_new, cv_new, cache_k, cache_v, name):
    db, n_pages = page_table.shape
    in_specs = ([_row_spec(512), _row_spec(256), _row_spec(256)]
                + _page_specs(n_pages, C_KV * HD, layer) + _page_specs(n_pages, C_KV * HD, layer))
    grid_spec = pltpu.PrefetchScalarGridSpec(num_scalar_prefetch=1, grid=(db,), in_specs=in_specs,
                                             out_specs=_row_spec(512))
    return pl.pallas_call(
        functools.partial(_moba_s_body, n_pages, n_new),
        grid_spec=grid_spec,
        out_shape=jax.ShapeDtypeStruct((db, SROWS, 512), F32),
        compiler_params=_params("parallel"),
        name=name,
    )(page_table, cq, ck_new, cv_new, *([cache_k] * n_pages), *([cache_v] * n_pages))


def _compress_s_body(n_pages, pt_ref, pek_ref, pev_ref, w1k_ref, w1v_ref, w2k_ref, w2v_ref, *rest):
    k_refs = rest[:n_pages]
    v_refs = rest[n_pages:2 * n_pages]
    ok_ref, ov_ref, xk_ref, xv_ref = rest[2 * n_pages:]
    for p in range(n_pages):
        xk_ref[p * PAGE_SIZE:(p + 1) * PAGE_SIZE, :] = k_refs[p][...].T
        xv_ref[p * PAGE_SIZE:(p + 1) * PAGE_SIZE, :] = v_refs[p][...].T
    _compress_one(xk_ref, pek_ref, w1k_ref, w2k_ref, ok_ref)
    _compress_one(xv_ref, pev_ref, w1v_ref, w2v_ref, ov_ref)


def _compress_sample(page_table, layer, cache_k, cache_v, cw, name):
    db, n_pages = page_table.shape
    const = lambda shape: pl.BlockSpec(shape, lambda b, pt: (0,) * len(shape))
    in_specs = ([const((CMP_BLOCK, 1, LANES))] * 2 + [const((CMP_BLOCK, LANES, 2 * CMP_HIDDEN))] * 2
                + [const((2 * CMP_HIDDEN, LANES))] * 2
                + _page_specs(n_pages, B_KV * HD, layer) + _page_specs(n_pages, B_KV * HD, layer))
    o_spec = pl.BlockSpec((None, LANES, LANES), lambda b, pt: (b, 0, 0))
    grid_spec = pltpu.PrefetchScalarGridSpec(
        num_scalar_prefetch=1, grid=(db,), in_specs=in_specs, out_specs=[o_spec, o_spec],
        scratch_shapes=[pltpu.VMEM((n_pages * PAGE_SIZE, LANES), F32)] * 2)
    return pl.pallas_call(
        functools.partial(_compress_s_body, n_pages),
        grid_spec=grid_spec,
        out_shape=[jax.ShapeDtypeStruct((db, LANES, LANES), F32)] * 2,
        compiler_params=_params("parallel"),
        name=name,
    )(page_table, cw["pek"], cw["pev"], cw["w1k"], cw["w1v"], cw["w2k"], cw["w2v"],
      *([cache_k] * n_pages), *([cache_v] * n_pages))


def _nsa_s_body(n_pages, n_new, q_pos0, n_win, pt_ref, qc_ref, qr_ref, bg_ref, kcmp_ref, vcmp_ref,
                ksn_ref, vsn_ref, kwn_ref, vwn_ref, wk_ref, wv_ref, *rest):
    k_refs = rest[:n_pages]
    v_refs = rest[n_pages:2 * n_pages]
    o_ref = rest[2 * n_pages]
    lane = lax.broadcasted_iota(jnp.int32, (1, LANES), 1)
    halfi = lane >> 6
    rep = B_HEADS // B_KV
    m_rows = rep * SROWS
    new_mask = _new_mask(m_rows, n_new)
    trow8 = lax.broadcasted_iota(jnp.int32, (SROWS, 1), 0)
    trow = lax.broadcasted_iota(jnp.int32, (m_rows, 1), 0) & (SROWS - 1)
    qpos8 = q_pos0 + trow8
    ns = -(-(q_pos0 + n_new) // SEL_BLOCK)
    bpp = PAGE_SIZE // SEL_BLOCK
    ksnt = _pad_rows(ksn_ref[...]).T
    vsnt = _pad_rows(vsn_ref[...]).T
    kwnt = _pad_rows(kwn_ref[...]).T
    vwnt = _pad_rows(vwn_ref[...]).T
    kcmp = kcmp_ref[...].astype(BF16)
    vcmp = vcmp_ref[...].astype(BF16)
    ci = lax.broadcasted_iota(jnp.int32, (LANES, LANES), 0)
    cj = lax.broadcasted_iota(jnp.int32, (LANES, LANES), 1)
    onehot = jnp.where((ci >> CMP_PER_SEL_SHIFT) == cj, 1.0, 0.0).astype(BF16)
    sig = jax.nn.sigmoid(bg_ref[...])
    tile_rows = lambda x: jnp.concatenate([x] * rep, axis=0)

    for g in range(B_KV):
        rows = slice(g * HD, (g + 1) * HD)
        own_c, own_r = [], []
        for r in range(rep):
            blk = slice((g * rep + r) // 2 * LANES, ((g * rep + r) // 2 + 1) * LANES)
            own_c.append(jnp.where(halfi == (r % 2), qc_ref[:, blk], 0.0) * (HD ** -0.5))
            own_r.append(jnp.where(halfi == (r % 2), qr_ref[:, blk], 0.0) * (HD ** -0.5))
        qc = jnp.concatenate(own_c, axis=0)
        qc = qc + pltpu.roll(qc, HD, 1)
        qc = jnp.where(halfi == g, qc, 0.0).astype(BF16)
        qr = jnp.concatenate(own_r, axis=0).astype(BF16)

        cmask = tile_rows(jnp.where((lane * CMP_STRIDE + (CMP_BLOCK - 1)) <= qpos8, 1.0, 0.0)) > 0.5
        s = jnp.where(cmask, _dot_nt(qc, kcmp), NEG)
        m = jnp.max(s, axis=1, keepdims=True)
        p = jnp.where(cmask, jnp.exp(s - m), 0.0)
        p = p / jnp.maximum(jnp.sum(p, axis=1, keepdims=True), 1e-30)
        o_cmp = jnp.where(halfi == g, _dot(p.astype(BF16), vcmp), 0.0)
        o_cmp = o_cmp + pltpu.roll(o_cmp, HD, 1)
        psum = p[0:SROWS]
        for r in range(1, rep):
            psum = psum + p[r * SROWS:(r + 1) * SROWS]
        hi, mid, lo = _split3(psum)
        imp = _dot(hi, onehot) + _dot(mid, onehot) + _dot(lo, onehot)
        cur = qpos8 >> 6
        impm = jnp.where(lane == cur, jnp.inf, jnp.where(lane < cur, imp, -jnp.inf))
        rank = _rank_desc(impm, ns, lane)
        sel = jnp.where(rank < float(SEL_TOPK), jnp.where(lane <= cur, 1.0, 0.0), 0.0)

        ss = _scores_kt(qr, [k_refs[pg][rows, :] for pg in range(n_pages)] + [ksnt[rows, :]])
        pieces = []
        for pg in range(n_pages):
            mk = sel[:, pg * bpp:pg * bpp + 1]
            for u in range(1, bpp):
                mk = jnp.where(halfi >= u, sel[:, pg * bpp + u:pg * bpp + u + 1], mk)
            pieces.append((ss[pg], tile_rows(mk) > 0.5))
        pieces.append((ss[n_pages], new_mask))
        ps, tot = _softmax_pieces(pieces)
        o_slc = _values_kt(ps, [v_refs[pg][rows, :] for pg in range(n_pages)] + [vsnt[rows, :]]) / tot

        wcols = [slice(u * LANES, (u + 1) * LANES) for u in range(n_win // LANES)]
        ss = _scores_kt(qr, [wk_ref[rows, c] for c in wcols] + [kwnt[rows, :]])
        pieces = []
        for u in range(len(wcols)):
            dist = (n_win - u * LANES) + trow - lane
            pieces.append((ss[u], dist <= WINDOW))
        pieces.append((ss[len(wcols)], new_mask))
        ps, tot = _softmax_pieces(pieces)
        o_win = _values_kt(ps, [wv_ref[rows, c] for c in wcols] + [vwnt[rows, :]]) / tot

        outs = []
        for r in range(rep):
            rs = slice(r * SROWS, (r + 1) * SROWS)
            base = (g * rep + r) * 3
            outs.append(sig[:, base:base + 1] * o_cmp[rs] + sig[:, base + 1:base + 2] * o_slc[rs]
                        + sig[:, base + 2:base + 3] * o_win[rs])
        for pr in range(rep // 2):
            col = (g * rep // 2 + pr) * LANES
            o_ref[:, col:col + LANES] = jnp.where(halfi == 0, outs[2 * pr], outs[2 * pr + 1])


def _nsa_sample(page_table, layer, n_new, q_pos0, bqc, bqr, bg, kcmp, vcmp, ks_new, vs_new, kw_new, vw_new,
                state_k, state_v, cache_k, cache_v, name):
    db, n_pages = page_table.shape
    n_win = state_k.shape[-1]
    w_spec = pl.BlockSpec((None, None, B_KV * HD, n_win), lambda b, pt: (layer, b, 0, 0))
    c_spec = pl.BlockSpec((None, LANES, LANES), lambda b, pt: (b, 0, 0))
    in_specs = ([_row_spec(512), _row_spec(512), _row_spec(LANES), c_spec, c_spec]
                + [_row_spec(LANES)] * 4 + [w_spec, w_spec]
                + _page_specs(n_pages, B_KV * HD, layer) + _page_specs(n_pages, B_KV * HD, layer))
    grid_spec = pltpu.PrefetchScalarGridSpec(num_scalar_prefetch=1, grid=(db,), in_specs=in_specs,
                                             out_specs=_row_spec(512))
    return pl.pallas_call(
        functools.partial(_nsa_s_body, n_pages, n_new, q_pos0, n_win),
        grid_spec=grid_spec,
        out_shape=jax.ShapeDtypeStruct((db, SROWS, 512), F32),
        compiler_params=_params("parallel"),
        name=name,
    )(page_table, bqc, bqr, bg, kcmp, vcmp, ks_new, vs_new, kw_new, vw_new, state_k, state_v,
      *([cache_k] * n_pages), *([cache_v] * n_pages))


def _layer_weights(l, norm_mix, norm_ffn, w_in, a_lambda, a_subln, b_cmp_pe, b_cmp_w1, b_cmp_w2,
                   w_branch, w_out, peer_wq, peer_subkeys, peer_u, peer_v):
    w = w_in[l]
    bg_w = jnp.pad(w[:, 2304:2328], ((0, 0), (0, LANES - 3 * B_HEADS)))
    return {
        "norm_mix": norm_mix[l][None, :],
        "norm_ffn": norm_ffn[l][None, :],
        "w_a": w[:, 0:1024].astype(BF16),
        "w_b": jnp.concatenate([w[:, 1024:2304], bg_w], axis=1).astype(BF16),
        "w_c": w[:, 2328:3352].astype(BF16),
        "w_g": w[:, 3352:6424].astype(BF16),
        "lam": a_lambda[l],
        "subln": a_subln[l][None, :],
        "cmp": _compress_weights(b_cmp_pe[l], b_cmp_w1[l], b_cmp_w2[l]),
        "w_branch": w_branch[l].astype(BF16),
        "w_out": w_out[l].astype(BF16),
        "wq": peer_wq[l].astype(BF16),
        "sk": peer_subkeys[l].reshape(2 * PEER_HEADS, PEER_NKEYS, LANES).astype(BF16),
        "u": peer_u[l].astype(BF16),
        "vt": peer_v[l].T.astype(BF16),
        "lam_init": 0.8 - 0.6 * math.exp(-0.3 * l),
    }


def _project_all(x, lw, cos_t, sin_t, tag):
    aq, ak, av = _project(x, lw["norm_mix"], lw["w_a"], cos_t, sin_t, PIECES_A, WIDTHS_A, "proj_a_" + tag)
    b_out = _project(x, lw["norm_mix"], lw["w_b"], cos_t, sin_t, PIECES_B, WIDTHS_B, "proj_b_" + tag)
    cq, ck, cv = _project(x, lw["norm_mix"], lw["w_c"], cos_t, sin_t, PIECES_C, WIDTHS_C, "proj_c_" + tag)
    (gate,) = _project(x, lw["norm_mix"], lw["w_g"], cos_t, sin_t, PIECES_G, WIDTHS_G, "proj_g_" + tag)
    return (aq, ak, av), b_out, (cq, ck, cv), gate


def _ffn(x, lw, tag):
    st, stat = _peer_stats(x, lw["norm_ffn"], lw["wq"], lw["sk"], "peer_stats_" + tag)
    return _peer_dense(x, lw["norm_ffn"], st, stat, lw["u"], lw["vt"], "peer_dense_" + tag)


def _prompt_layer(x, lw, cos_t, sin_t, bn, t):
    (aq, ak, av), (bqc, bqr, bkc, bvc, bks, bvs, bkw, bvw, bg), (cq, ck, cv), gate = _project_all(
        x, lw, cos_t, sin_t, "p")
    r3 = lambda a: a.reshape(bn, t, a.shape[-1])
    o_a = _diff_attention(r3(aq), r3(ak), r3(av), lw["lam"], lw["subln"], lw["lam_init"], 0, "diff_p")
    kcmp, vcmp = _compress(r3(bkc), r3(bvc), lw["cmp"], "cmp_p")
    o_b = _nsa_attention(r3(bqc), r3(bqr), r3(bg), kcmp, vcmp, r3(bks), r3(bvs), r3(bkw), r3(bvw),
                         0, t, 0, t, "nsa_p")
    o_c = _moba_attention(r3(cq), r3(ck), r3(cv), 0, "moba_p")
    n = bn * t
    x = _merge(x, gate, o_a.reshape(n, 512), o_b.reshape(n, 512), o_c.reshape(n, 512),
               lw["w_branch"], lw["w_out"], "merge_p")
    x = _ffn(x, lw, "p")
    wp = min(WINDOW, t)
    rows = (r3(ak).reshape(bn, t, A_KV, 2 * HD), r3(av).reshape(bn, t, A_KV, 2 * HD),
            r3(bkc).reshape(bn, t, B_KV, HD), r3(bvc).reshape(bn, t, B_KV, HD),
            r3(bks).reshape(bn, t, B_KV, HD), r3(bvs).reshape(bn, t, B_KV, HD),
            r3(ck).reshape(bn, t, C_KV, HD), r3(cv).reshape(bn, t, C_KV, HD),
            r3(bkw)[:, t - wp:].reshape(bn, wp, B_KV, HD), r3(bvw)[:, t - wp:].reshape(bn, wp, B_KV, HD))
    return x, rows


def _sample_layer(x, lw, l, cos_t, sin_t, db, t, past, caches, state_wk, state_wv, page_table):
    (aq, ak, av), (bqc, bqr, bkc, bvc, bks, bvs, bkw, bvw, bg), (cq, ck, cv), gate = _project_all(
        x, lw, cos_t, sin_t, "s")
    r3 = lambda a: a.reshape(db, SROWS, a.shape[-1])
    c_ak, c_av, c_bkc, c_bvc, c_bks, c_bvs, c_ck, c_cv = caches
    o_a = _diff_sample(page_table, l, t, r3(aq), r3(ak), r3(av), c_ak, c_av, lw["lam"], lw["subln"],
                       lw["lam_init"], "diff_s")
    kcmp, vcmp = _compress_sample(page_table, l, c_bkc, c_bvc, lw["cmp"], "cmp_s")
    o_b = _nsa_sample(page_table, l, t, past, r3(bqc), r3(bqr), r3(bg), kcmp, vcmp, r3(bks), r3(bvs),
                      r3(bkw), r3(bvw), state_wk, state_wv, c_bks, c_bvs, "nsa_s")
    o_c = _moba_sample(page_table, l, t, r3(cq), r3(ck), r3(cv), c_ck, c_cv, "moba_s")
    n = db * SROWS
    x = _merge(x, gate, o_a.reshape(n, 512), o_b.reshape(n, 512), o_c.reshape(n, 512),
               lw["w_branch"], lw["w_out"], "merge_s")
    x = _ffn(x, lw, "s")
    new = lambda a, kv, w: r3(a)[:, :t].reshape(db, t, kv, w)

    def rolled(state_t, a):
        new_t = jnp.transpose(r3(a)[:, :t], (0, 2, 1))
        out_t = jnp.concatenate([state_t[l][:, :, t:], new_t], axis=2)
        return jnp.transpose(out_t, (0, 2, 1)).reshape(db, out_t.shape[2], B_KV, HD)

    rows = (new(ak, A_KV, 2 * HD), new(av, A_KV, 2 * HD), new(bkc, B_KV, HD), new(bvc, B_KV, HD),
            new(bks, B_KV, HD), new(bvs, B_KV, HD), new(ck, C_KV, HD), new(cv, C_KV, HD),
            rolled(state_wk, bkw), rolled(state_wv, bvw))
    return x, rows


def kernel(x_prompt, x_sample, cache_a_k, cache_a_v, cache_b_kc, cache_b_vc, cache_b_ks, cache_b_vs,
           cache_c_k, cache_c_v, state_b_wk, state_b_wv, page_table, norm_mix, norm_ffn, norm_final,
           w_in, a_lambda, a_subln, b_cmp_pe, b_cmp_w1, b_cmp_w2, w_branch, w_out,
           peer_wq, peer_subkeys, peer_u, peer_v):
    bn, t, _ = x_prompt.shape
    db, ts, _ = x_sample.shape
    depth = w_in.shape[0]
    past = page_table.shape[1] * PAGE_SIZE
    wb = state_b_wk.shape[2]
    assert past % MOBA_BLOCK == 0 and ts <= SROWS and wb == WINDOW and wb % LANES == 0
    rows_view = lambda c: c.reshape(c.shape[0], c.shape[1], c.shape[2] * c.shape[3], c.shape[4])
    lanes_view = lambda c: jnp.transpose(c, (0, 1, 3, 4, 2)).reshape(
        c.shape[0], c.shape[1], c.shape[3] * c.shape[4], c.shape[2])
    caches = ([rows_view(c) for c in (cache_a_k, cache_a_v)]
              + [lanes_view(c) for c in (cache_b_kc, cache_b_vc, cache_b_ks, cache_b_vs, cache_c_k, cache_c_v)])
    swk = lanes_view(state_b_wk)
    swv = lanes_view(state_b_wv)

    cos_p, sin_p = _rope_tables(jnp.arange(t, dtype=jnp.int32))
    tm_s = min(ROW_TILE, db * SROWS)
    cos_s, sin_s = _rope_tables(past + (jnp.arange(tm_s, dtype=jnp.int32) % SROWS))

    xp = x_prompt.reshape(bn * t, D_MODEL)
    xs = jnp.pad(x_sample, ((0, 0), (0, SROWS - ts), (0, 0))).reshape(db * SROWS, D_MODEL)
    rows_p, rows_s = [], []
    for l in range(depth):
        lw = _layer_weights(l, norm_mix, norm_ffn, w_in, a_lambda, a_subln, b_cmp_pe, b_cmp_w1, b_cmp_w2,
                            w_branch, w_out, peer_wq, peer_subkeys, peer_u, peer_v)
        xp, rp = _prompt_layer(xp, lw, cos_p, sin_p, bn, t)
        xs, rs = _sample_layer(xs, lw, l, cos_s, sin_s, db, ts, past, caches, swk, swv, page_table)
        rows_p.append(rp)
        rows_s.append(rs)
    y_prompt = _final_norm(xp, norm_final[None, :], "final_p").reshape(bn, t, D_MODEL)
    y_sample = _final_norm(xs, norm_final[None, :], "final_s").reshape(db, SROWS, D_MODEL)[:, :ts]
    outs_p = [jnp.stack(r, axis=0) for r in zip(*rows_p)]
    outs_s = [jnp.stack(r, axis=0) for r in zip(*rows_s)]
    return (y_prompt, y_sample, *outs_p, *outs_s)
```

```python
import functools
import math

import jax
import jax.numpy as jnp
import numpy as np
from jax import lax
from jax.experimental import pallas as pl
from jax.experimental.pallas import tpu as pltpu

F32 = jnp.float32
BF16 = jnp.bfloat16

D_MODEL = 1024
HD = 64
A_HEADS, A_KV = 4, 2
B_HEADS, B_KV = 8, 2
C_HEADS, C_KV = 8, 4
CMP_BLOCK, CMP_STRIDE, CMP_HIDDEN = 32, 16, 128
SEL_BLOCK, SEL_TOPK = 64, 8
CMP_PER_SEL_SHIFT = 2
WINDOW = 512
MOBA_BLOCK, MOBA_TOPK = 256, 3
PEER_HEADS, PEER_NKEYS, PEER_TOPK = 8, 128, 16
ROPE_THETA = 10000.0
EPS = 1e-6
PAGE_SIZE = 128

LANES = 128
NEG = -1e30
QK_SCALE_LOG2 = math.log2(math.e) * HD ** -0.5
VMEM_LIMIT = 48 * 1024 * 1024
ROW_TILE = 256
KV_CHUNK = 256
EXPERT_TILE = 2048
EXPERT_SUB = 256


def _params(*sem):
    return pltpu.CompilerParams(dimension_semantics=sem, vmem_limit_bytes=VMEM_LIMIT)


def _dot(a, b):
    return jnp.dot(a, b, preferred_element_type=F32)


def _dot_nt(a, b):
    return lax.dot_general(a, b, (((1,), (1,)), ((), ())), preferred_element_type=F32)


def _split3(a):
    hi = a.astype(BF16)
    r1 = a - hi.astype(F32)
    mid = r1.astype(BF16)
    lo = (r1 - mid.astype(F32)).astype(BF16)
    return hi, mid, lo


def _rms(x, w):
    return x * lax.rsqrt(jnp.mean(x * x, axis=-1, keepdims=True) + EPS) * w


def _proj_body(pieces, chunk, x_ref, nw_ref, w_ref, cos_ref, sin_ref, *outs):
    h = _rms(x_ref[...], nw_ref[...]).astype(BF16)
    cos = cos_ref[...]
    sin = sin_ref[...]
    lane = lax.broadcasted_iota(jnp.int32, (1, LANES), 1)
    first = (lane & (HD - 1)) < (HD // 2)
    n_cols = len(pieces) * LANES
    for c0 in range(0, n_cols, chunk):
        w = min(chunk, n_cols - c0)
        z = _dot(h, w_ref[:, c0:c0 + w])
        for p in range(w // LANES):
            zp = z[:, p * LANES:(p + 1) * LANES]
            for (oi, oc, rope) in pieces[c0 // LANES + p]:
                if rope:
                    rot = jnp.where(first, pltpu.roll(zp, LANES - HD // 2, 1), pltpu.roll(zp, HD // 2, 1))
                    outs[oi][:, oc:oc + LANES] = zp * cos + rot * sin
                else:
                    outs[oi][:, oc:oc + LANES] = zp


def _project(x, nw, w, cos_t, sin_t, pieces, out_widths, name):
    n = x.shape[0]
    tm = min(ROW_TILE, n)
    ntab = cos_t.shape[0] // tm
    ncols = w.shape[1]
    body = functools.partial(_proj_body, pieces, 512)
    return pl.pallas_call(
        body,
        grid=(n // tm,),
        in_specs=[
            pl.BlockSpec((tm, D_MODEL), lambda i: (i, 0)),
            pl.BlockSpec((1, D_MODEL), lambda i: (0, 0)),
            pl.BlockSpec((D_MODEL, ncols), lambda i: (0, 0)),
            pl.BlockSpec((tm, LANES), lambda i: (i % ntab, 0)),
            pl.BlockSpec((tm, LANES), lambda i: (i % ntab, 0)),
        ],
        out_specs=[pl.BlockSpec((tm, ow), lambda i: (i, 0)) for ow in out_widths],
        out_shape=[jax.ShapeDtypeStruct((n, ow), F32) for ow in out_widths],
        compiler_params=_params("parallel"),
        name=name,
    )(x, nw, w, cos_t, sin_t)


def _plain(oi, width):
    return [[(oi, c, False)] for c in range(0, width, LANES)]


def _roped(oi, width):
    return [[(oi, c, True)] for c in range(0, width, LANES)]


PIECES_A = _roped(0, 512) + _roped(1, 256) + _plain(2, 256)
WIDTHS_A = (512, 256, 256)
PIECES_B = ([[(0, c, False), (1, c, True)] for c in range(0, 512, LANES)]
            + _plain(2, 128) + _plain(3, 128) + _roped(4, 128) + _plain(5, 128)
            + _roped(6, 128) + _plain(7, 128) + _plain(8, 128))
WIDTHS_B = (512, 512, 128, 128, 128, 128, 128, 128, 128)
PIECES_C = _roped(0, 512) + _roped(1, 256) + _plain(2, 256)
WIDTHS_C = (512, 256, 256)
PIECES_G = _plain(0, 3072)
WIDTHS_G = (3072,)
SPREAD_B = B_HEADS * B_KV * HD
SPREAD_C = C_HEADS * C_KV * HD
PIECES_B_S = [[(0, c, False), (1, c, True)] for c in range(0, SPREAD_B, LANES)] + PIECES_B[512 // LANES:]
WIDTHS_B_S = (SPREAD_B, SPREAD_B) + WIDTHS_B[2:]
PIECES_C_S = _roped(0, SPREAD_C) + _roped(1, 256) + _plain(2, 256)
WIDTHS_C_S = (SPREAD_C, 256, 256)


def _spread_heads(wq, n_heads, n_groups):
    rep = n_heads // n_groups
    onehot = (np.arange(n_heads)[:, None] // rep == np.arange(n_groups)[None, :]).astype(np.float32)
    w = wq.reshape(wq.shape[0], n_heads, HD)
    return jnp.einsum("dhk,hg->dhgk", w, jnp.asarray(onehot)).reshape(wq.shape[0], n_heads * n_groups * HD)


def _rope_tables(pos):
    half = HD // 2
    freqs = jnp.power(ROPE_THETA, -jnp.arange(half, dtype=F32) / half)
    ang = pos.astype(F32)[:, None] * freqs[None, :]
    cos = jnp.cos(ang)
    sin = jnp.sin(ang)
    cos_t = jnp.concatenate([cos, cos, cos, cos], axis=1)
    sin_t = jnp.concatenate([-sin, sin, -sin, sin], axis=1)
    return cos_t, sin_t


def _flash_init(m_ref, l_ref, acc_ref):
    m_ref[...] = jnp.full(m_ref.shape, NEG, F32)
    l_ref[...] = jnp.zeros(l_ref.shape, F32)
    acc_ref[...] = jnp.zeros(acc_ref.shape, F32)


def _lane_blocks(x):
    return [x[:, c:c + LANES] for c in range(0, x.shape[1], LANES)]


def _flash_step(hh, q_emb, kc, vc, biases, m_ref, l_ref, acc_ref):
    blocks = _lane_blocks(_dot_nt(q_emb, kc))
    if biases is not None:
        blocks = [b if bi is None else b + bi for b, bi in zip(blocks, biases)]
    mx = blocks[0]
    for b in blocks[1:]:
        mx = jnp.maximum(mx, b)
    m_old = m_ref[hh]
    m_new = jnp.maximum(m_old, jnp.max(mx, axis=1, keepdims=True))
    alpha = jnp.exp2(m_old - m_new)
    ps = [jnp.exp2(b - m_new) for b in blocks]
    tot = ps[0]
    for p in ps[1:]:
        tot = tot + p
    l_ref[hh] = alpha * l_ref[hh] + jnp.sum(tot, axis=1, keepdims=True)
    acc_ref[hh] = alpha * acc_ref[hh] + _dot(jnp.concatenate(ps, axis=1).astype(BF16), vc)
    m_ref[hh] = m_new


def _causal_sweep(q0, tq, body):
    n_full = (q0 + 1) // KV_CHUNK
    n_pair = n_full // 2
    lax.fori_loop(0, n_pair, body(2 * KV_CHUNK, False), 0)
    lax.fori_loop(2 * n_pair, n_full, body(KV_CHUNK, False), 0)
    lax.fori_loop(n_full, (q0 + tq - 1) // KV_CHUNK + 1, body(KV_CHUNK, True), 0)


def _causal_bias(j, tk, qpos, inside=0.0):
    kpos = j * tk + lax.broadcasted_iota(jnp.int32, (1, tk), 1)
    return jnp.where(kpos <= qpos, inside, NEG)


def _flash_out(hh, l_ref, acc_ref):
    return acc_ref[hh] / jnp.maximum(l_ref[hh], 1e-30)


def _chunk(ref, j, tk):
    return ref[pl.ds(pl.multiple_of(j * tk, tk), tk), :].astype(BF16)


def _diff_body(tq, tk, q_pos0, lam_init, lam_ref, subln_ref, q_ref, k_ref, v_ref, o_ref,
               m_ref, l_ref, acc_ref):
    qi = pl.program_id(2)
    lane = lax.broadcasted_iota(jnp.int32, (1, LANES), 1)
    upper = lane >= HD
    q0 = q_pos0 + qi * tq
    qpos = q0 + lax.broadcasted_iota(jnp.int32, (tq, 1), 0)
    _flash_init(m_ref, l_ref, acc_ref)
    qs = []
    for r in range(2):
        qb = q_ref[:, r * LANES:(r + 1) * LANES] * QK_SCALE_LOG2
        qs.append(jnp.where(upper, 0.0, qb).astype(BF16))
        qs.append(jnp.where(upper, qb, 0.0).astype(BF16))
    def sweep(tk, diag):
        def body(j, carry):
            kc = _chunk(k_ref, j, tk)
            vc = _chunk(v_ref, j, tk)
            biases = _lane_blocks(_causal_bias(j, tk, qpos)) if diag else None
            for hh in range(4):
                _flash_step(hh, qs[hh], kc, vc, biases, m_ref, l_ref, acc_ref)
            return carry
        return body

    _causal_sweep(q0, tq, sweep)
    lp = lam_ref[...]
    lam = (jnp.exp(jnp.sum(lp[0:1] * lp[1:2], axis=1, keepdims=True))
           - jnp.exp(jnp.sum(lp[2:3] * lp[3:4], axis=1, keepdims=True)) + lam_init)
    for r in range(2):
        o = _flash_out(2 * r, l_ref, acc_ref) - lam * _flash_out(2 * r + 1, l_ref, acc_ref)
        o_ref[:, r * LANES:(r + 1) * LANES] = _rms(o, subln_ref[...]) * (1.0 - lam_init)


def _diff_attention(aq, ak, av, lam_p, subln, lam_init, q_pos0, name):
    bn, tq_all, _ = aq.shape
    lk = ak.shape[1]
    tq = min(ROW_TILE, tq_all)
    tk = KV_CHUNK
    body = functools.partial(_diff_body, tq, tk, q_pos0, lam_init)
    return pl.pallas_call(
        body,
        grid=(bn, A_KV, tq_all // tq),
        in_specs=[
            pl.BlockSpec((4, HD), lambda b, g, i: (0, 0)),
            pl.BlockSpec((1, 2 * HD), lambda b, g, i: (0, 0)),
            pl.BlockSpec((None, tq, 256), lambda b, g, i: (b, i, g)),
            pl.BlockSpec((None, lk, LANES), lambda b, g, i: (b, 0, g)),
            pl.BlockSpec((None, lk, LANES), lambda b, g, i: (b, 0, g)),
        ],
        out_specs=pl.BlockSpec((None, tq, 256), lambda b, g, i: (b, i, g)),
        out_shape=jax.ShapeDtypeStruct((bn, tq_all, 512), F32),
        scratch_shapes=[pltpu.VMEM((4, tq, LANES), F32), pltpu.VMEM((4, tq, LANES), F32),
                        pltpu.VMEM((4, tq, LANES), F32)],
        compiler_params=_params("parallel", "parallel", "arbitrary"),
        name=name,
    )(lam_p, subln, aq, ak, av)


def _compress_one(x_ref, pe_ref, w1_ref, w2_ref, o_ref):
    n_half = CMP_STRIDE
    xs = [x_ref[pl.ds(p, LANES, stride=CMP_STRIDE), :] for p in range(n_half)]
    xa = jnp.concatenate([(x + pe_ref[p]).astype(BF16) for p, x in enumerate(xs)], axis=1)
    xb = jnp.concatenate([(x + pe_ref[n_half + p]).astype(BF16) for p, x in enumerate(xs)], axis=1)
    w1 = w1_ref[...].reshape(2, n_half * LANES, 2 * CMP_HIDDEN)
    h = _dot(xa, w1[0]) + pltpu.roll(_dot(xb, w1[1]), LANES - 1, 0)
    y = _dot(jax.nn.gelu(h).astype(BF16), w2_ref[...])
    row = lax.broadcasted_iota(jnp.int32, (LANES, 1), 0)
    o_ref[...] = jnp.where(row < LANES - 1, y, 0.0)


def _compress_body(xk_ref, xv_ref, pek_ref, pev_ref, w1k_ref, w1v_ref, w2k_ref, w2v_ref, ok_ref, ov_ref):
    _compress_one(xk_ref, pek_ref, w1k_ref, w2k_ref, ok_ref)
    _compress_one(xv_ref, pev_ref, w1v_ref, w2v_ref, ov_ref)


def _compress(xk, xv, cw, name):
    bn = xk.shape[0]
    n_tok = LANES * CMP_STRIDE
    x_spec = pl.BlockSpec((None, n_tok, LANES), lambda b: (b, 0, 0))
    pe_spec = pl.BlockSpec((CMP_BLOCK, 1, LANES), lambda b: (0, 0, 0))
    w1_spec = pl.BlockSpec((CMP_BLOCK, LANES, 2 * CMP_HIDDEN), lambda b: (0, 0, 0))
    w2_spec = pl.BlockSpec((2 * CMP_HIDDEN, LANES), lambda b: (0, 0))
    o_spec = pl.BlockSpec((None, LANES, LANES), lambda b: (b, 0, 0))
    return pl.pallas_call(
        _compress_body,
        grid=(bn,),
        in_specs=[x_spec, x_spec, pe_spec, pe_spec, w1_spec, w1_spec, w2_spec, w2_spec],
        out_specs=[o_spec, o_spec],
        out_shape=[jax.ShapeDtypeStruct((bn, LANES, LANES), F32)] * 2,
        compiler_params=_params("parallel"),
        name=name,
    )(xk, xv, cw["pek"], cw["pev"], cw["w1k"], cw["w1v"], cw["w2k"], cw["w2v"])


def _compress_weights(pe, w1, w2):
    out = {}
    for idx, tag in ((0, "k"), (1, "v")):
        w1p = w1[idx].reshape(CMP_BLOCK, HD, CMP_HIDDEN)
        z = jnp.zeros_like(w1p)
        w1b = jnp.concatenate([jnp.concatenate([w1p, z], axis=2), jnp.concatenate([z, w1p], axis=2)], axis=1)
        z2 = jnp.zeros_like(w2[idx])
        w2b = jnp.concatenate([jnp.concatenate([w2[idx], z2], axis=1), jnp.concatenate([z2, w2[idx]], axis=1)], axis=0)
        out["w1" + tag] = w1b.astype(BF16)
        out["w2" + tag] = w2b.astype(BF16)
        out["pe" + tag] = jnp.concatenate([pe[idx], pe[idx]], axis=1)[:, None, :]
    return out


def _rank_desc(vals, n, lane):
    rank = jnp.zeros(vals.shape, F32)
    for i in range(n):
        col = vals[:, i:i + 1]
        tie = jnp.where(lane > i, 1.0, 0.0)
        rank = rank + jnp.where(col > vals, 1.0, 0.0) + jnp.where(col == vals, tie, 0.0)
    return rank


def _nsa_body(tq, tk, q_pos0, ns, w_pos0, w_valid, lw,
              qc_ref, qr_ref, bg_ref, kcmp_ref, vcmp_ref, ks_ref, vs_ref, kw_ref, vw_ref, o_ref,
              m_ref, l_ref, acc_ref):
    g = pl.program_id(1)
    qi = pl.program_id(2)
    lane = lax.broadcasted_iota(jnp.int32, (1, LANES), 1)
    halfi = lane >> 6
    q0 = q_pos0 + qi * tq
    qpos = q0 + lax.broadcasted_iota(jnp.int32, (tq, 1), 0)
    rep = B_HEADS // B_KV

    def embed_all(ref, scale):
        out = []
        for pr in range(rep // 2):
            v = ref[:, pr * LANES:(pr + 1) * LANES] * scale
            swapped = pltpu.roll(v, HD, 1)
            for u in range(2):
                out.append(jnp.where(halfi == g, jnp.where(g == u, v, swapped), 0.0).astype(BF16))
        return out

    def pair_out(o_even, o_odd):
        aligned = jnp.where(g == 0, o_even, o_odd)
        other = jnp.where(g == 0, o_odd, o_even)
        return jnp.where(halfi == g, aligned, pltpu.roll(other, HD, 1))

    kcmp = kcmp_ref[...].astype(BF16)
    vcmp = vcmp_ref[...].astype(BF16)
    cmask = (lane * CMP_STRIDE + (CMP_BLOCK - 1)) <= qpos
    psum = jnp.zeros((tq, LANES), F32)
    o_cmp = []
    qsc = embed_all(qc_ref, HD ** -0.5)
    for r in range(rep):
        s = jnp.where(cmask, _dot_nt(qsc[r], kcmp), NEG)
        m = jnp.max(s, axis=1, keepdims=True)
        p = jnp.where(cmask, jnp.exp(s - m), 0.0)
        p = p / jnp.maximum(jnp.sum(p, axis=1, keepdims=True), 1e-30)
        psum = psum + p
        o_cmp.append(_dot(p.astype(BF16), vcmp))

    ci = lax.broadcasted_iota(jnp.int32, (LANES, LANES), 0)
    cj = lax.broadcasted_iota(jnp.int32, (LANES, LANES), 1)
    onehot = jnp.where((ci >> CMP_PER_SEL_SHIFT) == cj, 1.0, 0.0).astype(BF16)
    hi, mid, lo = _split3(psum)
    imp = _dot(hi, onehot) + _dot(mid, onehot) + _dot(lo, onehot)
    cur = qpos >> 6
    impm = jnp.where(lane == cur, jnp.inf, jnp.where(lane < cur, imp, -jnp.inf))
    rank = _rank_desc(impm, ns, lane)
    sel = jnp.where(rank < float(SEL_TOPK), jnp.where(lane <= cur, 1.0, 0.0), 0.0).astype(BF16)

    qsr = embed_all(qr_ref, QK_SCALE_LOG2)
    _flash_init(m_ref, l_ref, acc_ref)
    def sel_sweep(ck, diag):
        bpc = ck // SEL_BLOCK

        def body(j, carry):
            kc = _chunk(ks_ref, j, ck)
            vc = _chunk(vs_ref, j, ck)
            ei = lax.broadcasted_iota(jnp.int32, (LANES, ck), 0)
            el = lax.broadcasted_iota(jnp.int32, (LANES, ck), 1)
            expand = jnp.where(ei == j * bpc + (el >> 6), 1.0, 0.0).astype(BF16)
            bias = (_dot(sel, expand) - 1.0) * (-NEG)
            if diag:
                bias = _causal_bias(j, ck, qpos, bias)
            biases = _lane_blocks(bias)
            for r in range(rep):
                _flash_step(r, qsr[r], kc, vc, biases, m_ref, l_ref, acc_ref)
            return carry
        return body

    _causal_sweep(q0, tq, sel_sweep)
    o_slc = [_flash_out(r, l_ref, acc_ref) for r in range(rep)]

    _flash_init(m_ref, l_ref, acc_ref)
    w_lo = jnp.maximum(q0 - WINDOW - w_pos0, 0) // tk
    w_hi = jnp.minimum(q0 + tq - 1 - w_pos0, lw - 1) // tk + 1

    def win_body(j, carry):
        kc = _chunk(kw_ref, j, tk)
        vc = _chunk(vw_ref, j, tk)
        kidx = j * tk + lax.broadcasted_iota(jnp.int32, (1, tk), 1)
        dist = qpos - (kidx + w_pos0)
        inwin = jnp.where(dist >= 0, jnp.where(dist <= WINDOW, 0.0, NEG), NEG)
        biases = _lane_blocks(jnp.where(kidx < w_valid, inwin, NEG))
        for r in range(rep):
            _flash_step(r, qsr[r], kc, vc, biases, m_ref, l_ref, acc_ref)
        return carry

    lax.fori_loop(w_lo, w_hi, win_body, 0)

    sig = jax.nn.sigmoid(bg_ref[...])
    outs = []
    for r in range(rep):
        o_win = _flash_out(r, l_ref, acc_ref)
        base = (g * rep + r) * 3
        gates = [jnp.sum(jnp.where(lane == base + c, sig, 0.0), axis=1, keepdims=True) for c in range(3)]
        outs.append(gates[0] * o_cmp[r] + gates[1] * o_slc[r] + gates[2] * o_win)
    for pr in range(rep // 2):
        o_ref[:, pr * LANES:(pr + 1) * LANES] = pair_out(outs[2 * pr], outs[2 * pr + 1])


def _nsa_attention(bqc, bqr, bg, kcmp, vcmp, ks, vs, kw, vw, q_pos0, k_valid, w_pos0, w_valid, name):
    bn, tq_all, _ = bqc.shape
    lk = ks.shape[1]
    lw = kw.shape[1]
    tq = min(ROW_TILE, tq_all)
    tk = KV_CHUNK
    ns = -(-k_valid // SEL_BLOCK)
    body = functools.partial(_nsa_body, tq, tk, q_pos0, ns, w_pos0, w_valid, lw)
    q_spec = pl.BlockSpec((None, tq, 256), lambda b, g, i: (b, i, g))
    full = lambda rows: pl.BlockSpec((None, rows, LANES), lambda b, g, i: (b, 0, 0))
    return pl.pallas_call(
        body,
        grid=(bn, B_KV, tq_all // tq),
        in_specs=[q_spec, q_spec, pl.BlockSpec((None, tq, LANES), lambda b, g, i: (b, i, 0)),
                  full(LANES), full(LANES), full(lk), full(lk), full(lw), full(lw)],
        out_specs=pl.BlockSpec((None, tq, 256), lambda b, g, i: (b, i, g)),
        out_shape=jax.ShapeDtypeStruct((bn, tq_all, 512), F32),
        scratch_shapes=[pltpu.VMEM((4, tq, LANES), F32), pltpu.VMEM((4, tq, LANES), F32),
                        pltpu.VMEM((4, tq, LANES), F32)],
        compiler_params=_params("parallel", "parallel", "arbitrary"),
        name=name,
    )(bqc, bqr, bg, kcmp, vcmp, ks, vs, kw, vw)


def _moba_body(tq, tk, q_pos0, nb, q_ref, k_ref, v_ref, o_ref, km_ref, m_ref, l_ref, acc_ref):
    g = pl.program_id(1)
    qi = pl.program_id(2)
    gh = g % 2
    lane = lax.broadcasted_iota(jnp.int32, (1, LANES), 1)
    halfi = lane >> 6
    q0 = q_pos0 + qi * tq
    qpos = q0 + lax.broadcasted_iota(jnp.int32, (tq, 1), 0)
    cur = qpos >> 8
    rep = C_HEADS // C_KV

    @pl.when(qi == 0)
    def _():
        km_ref[...] = jnp.zeros(km_ref.shape, F32)
        km_ref[0:nb, :] = jnp.sum(k_ref[...].reshape(nb, MOBA_BLOCK, LANES), axis=1) * (1.0 / MOBA_BLOCK)

    km_hi, km_mid, _ = _split3(km_ref[...])

    qv = q_ref[...]
    qs = []
    sels = []
    swapped = pltpu.roll(qv, HD, 1)
    for r in range(rep):
        qe = jnp.where(halfi == gh, jnp.where(gh == r, qv, swapped), 0.0)
        q_hi, q_mid, _ = _split3(qe)
        s_blk = _dot_nt(q_hi, km_hi) + _dot_nt(q_hi, km_mid) + _dot_nt(q_mid, km_hi)
        sm = jnp.where(lane < cur, s_blk, -jnp.inf)
        rank = _rank_desc(sm, nb, lane)
        past = jnp.where(rank < float(MOBA_TOPK), jnp.where(lane < cur, 1.0, 0.0), 0.0)
        sels.append(jnp.where(lane == cur, 1.0, past))
        qs.append((qe * QK_SCALE_LOG2).astype(BF16))

    _flash_init(m_ref, l_ref, acc_ref)
    def sweep(ck, diag):
        bpc = ck // MOBA_BLOCK
        lpb = MOBA_BLOCK // LANES

        def body(j, carry):
            kc = _chunk(k_ref, j, ck)
            vc = _chunk(v_ref, j, ck)
            diag_biases = _lane_blocks(_causal_bias(j, ck, qpos)) if diag else None
            for r in range(rep):
                if diag:
                    biases = diag_biases
                else:
                    biases = []
                    for u in range(bpc):
                        col = jnp.sum(jnp.where(lane == j * bpc + u, sels[r], 0.0), axis=1, keepdims=True)
                        biases += [(col - 1.0) * (-NEG)] * lpb
                _flash_step(r, qs[r], kc, vc, biases, m_ref, l_ref, acc_ref)
            return carry
        return body

    _causal_sweep(q0, tq, sweep)
    o0 = _flash_out(0, l_ref, acc_ref)
    o1 = _flash_out(1, l_ref, acc_ref)
    aligned = jnp.where(gh == 0, o0, o1)
    other = jnp.where(gh == 0, o1, o0)
    o_ref[...] = jnp.where(halfi == gh, aligned, pltpu.roll(other, HD, 1))


def _moba_attention(cq, ck, cv, q_pos0, name):
    bn, tq_all, _ = cq.shape
    lk = ck.shape[1]
    tq = min(ROW_TILE, tq_all)
    tk = MOBA_BLOCK
    nb = lk // MOBA_BLOCK
    assert tq == MOBA_BLOCK == KV_CHUNK and q_pos0 % MOBA_BLOCK == 0
    body = functools.partial(_moba_body, tq, tk, q_pos0, nb)
    return pl.pallas_call(
        body,
        grid=(bn, C_KV, tq_all // tq),
        in_specs=[
            pl.BlockSpec((None, tq, LANES), lambda b, g, i: (b, i, g)),
            pl.BlockSpec((None, lk, LANES), lambda b, g, i: (b, 0, g // 2)),
            pl.BlockSpec((None, lk, LANES), lambda b, g, i: (b, 0, g // 2)),
        ],
        out_specs=pl.BlockSpec((None, tq, LANES), lambda b, g, i: (b, i, g)),
        out_shape=jax.ShapeDtypeStruct((bn, tq_all, 512), F32),
        scratch_shapes=[pltpu.VMEM((LANES, LANES), F32),
                        pltpu.VMEM((2, tq, LANES), F32), pltpu.VMEM((2, tq, LANES), F32),
                        pltpu.VMEM((2, tq, LANES), F32)],
        compiler_params=_params("parallel", "parallel", "arbitrary"),
        name=name,
    )(cq, ck, cv)


def _merge_body(x_ref, gate_ref, oa_ref, ob_ref, oc_ref, wb_ref, wo_ref, y_ref):
    y = jnp.zeros((x_ref.shape[0], D_MODEL), F32)
    for c, o_ref in enumerate((oa_ref, ob_ref, oc_ref)):
        br = _dot(o_ref[...].astype(BF16), wb_ref[c])
        y = y + jax.nn.sigmoid(gate_ref[:, c * D_MODEL:(c + 1) * D_MODEL]) * br
    y_ref[...] = x_ref[...] + _dot(y.astype(BF16), wo_ref[...])


def _merge(x, gate, o_a, o_b, o_c, wb, wo, name):
    n = x.shape[0]
    tm = min(ROW_TILE, n)
    row = lambda w: pl.BlockSpec((tm, w), lambda i: (i, 0))
    return pl.pallas_call(
        _merge_body,
        grid=(n // tm,),
        in_specs=[row(D_MODEL), row(3 * D_MODEL), row(512), row(512), row(512),
                  pl.BlockSpec((3, 512, D_MODEL), lambda i: (0, 0, 0)),
                  pl.BlockSpec((D_MODEL, D_MODEL), lambda i: (0, 0))],
        out_specs=row(D_MODEL),
        out_shape=jax.ShapeDtypeStruct((n, D_MODEL), F32),
        compiler_params=_params("parallel"),
        name=name,
    )(x, gate, o_a, o_b, o_c, wb, wo)


def _bitonic_desc(vs):
    vs = list(vs)
    n = len(vs)
    k = 2
    while k <= n:
        j = k // 2
        while j >= 1:
            for i in range(n):
                partner = i ^ j
                if partner > i:
                    hi = jnp.maximum(vs[i], vs[partner])
                    lo = jnp.minimum(vs[i], vs[partner])
                    vs[i], vs[partner] = (hi, lo) if (i & k) == 0 else (lo, hi)
            j //= 2
        k *= 2
    return vs


def _pop_heads(cols, n_out):
    cols = list(cols)
    for k in range(n_out):
        m = jnp.max(cols[0], axis=0, keepdims=True)
        eq = cols[0] == m
        yield m, eq
        depth = min(len(cols), n_out - k)
        for d in range(depth - 1):
            cols[d] = jnp.where(eq, cols[d + 1], cols[d])
        cols[depth - 1] = jnp.where(eq, -jnp.inf, cols[depth - 1])


def _peer_stats_body(tm, x_ref, nw_ref, wq_ref, sk_ref, st_ref, stat_ref, top_ref, nxt_ref):
    h = _rms(x_ref[...], nw_ref[...]).astype(BF16)
    q = _dot(h, wq_ref[...]).astype(BF16)
    n_hc = 2 * PEER_HEADS
    for hc in range(n_hc):
        st_ref[hc] = _dot_nt(sk_ref[hc], q[:, hc * LANES:(hc + 1) * LANES])

    sub = 8
    n_out = PEER_TOPK + 1

    def top_body(hc, carry):
        for hf in range(tm // LANES):
            s = st_ref[hc, :, hf * LANES:(hf + 1) * LANES]
            cols = _bitonic_desc([s[d * sub:(d + 1) * sub] for d in range(PEER_NKEYS // sub)])
            vals = [m for m, _ in _pop_heads(cols, n_out)]
            top_ref[hc, :, hf * LANES:(hf + 1) * LANES] = jnp.concatenate(vals[:PEER_TOPK], axis=0)
            nxt_ref[hc, :, hf * LANES:(hf + 1) * LANES] = vals[PEER_TOPK]
        return carry

    lax.fori_loop(0, n_hc, top_body, 0)

    row8 = lax.broadcasted_iota(jnp.int32, (sub, LANES), 0)
    fill = jnp.full((sub, LANES), -jnp.inf, F32)

    def head_body(hd, carry):
        for hf in range(tm // LANES):
            sl = slice(hf * LANES, (hf + 1) * LANES)
            a = top_ref[2 * hd, :, sl]
            b = top_ref[2 * hd + 1, :, sl]
            cands = [a[0:1] + b[0:sub], a[0:1] + b[sub:2 * sub]]
            for i in range(1, sub):
                cands.append(jnp.where(row8 < PEER_TOPK // (i + 1), a[i:i + 1] + b[0:sub], -jnp.inf))
            cands.append(a[sub:2 * sub] + b[0:1])
            mx = a[0:1] + b[0:1]
            cum = jnp.zeros((1, LANES), F32)
            tau = mx
            nxt = mx
            cols = _bitonic_desc(cands + [fill] * (16 - len(cands)))
            for m, eq in _pop_heads(cols, n_out):
                tau = jnp.where(cum < float(PEER_TOPK), m, tau)
                nxt = jnp.where(cum < float(n_out), m, nxt)
                cum = cum + jnp.sum(jnp.where(eq, 1.0, 0.0), axis=0, keepdims=True)
            nxt = jnp.maximum(nxt, jnp.maximum(a[0:1] + nxt_ref[2 * hd + 1, :, sl], nxt_ref[2 * hd, :, sl] + b[0:1]))
            z = None
            for c in cands:
                t = jnp.sum(jnp.where(c >= tau, jnp.exp(c - mx), 0.0), axis=0, keepdims=True)
                z = t if z is None else z + t
            stat_ref[hd, :, sl] = 0.5 * (tau + nxt)
            stat_ref[PEER_HEADS + hd, :, sl] = a[0:1]
            stat_ref[2 * PEER_HEADS + hd, :, sl] = b[0:1]
            stat_ref[3 * PEER_HEADS + hd, :, sl] = 1.0 / z
        return carry

    lax.fori_loop(0, PEER_HEADS, head_body, 0)


def _peer_stats(x, nw, wq, sk, name):
    n = x.shape[0]
    tm = min(ROW_TILE, n)
    n_hc = 2 * PEER_HEADS
    return pl.pallas_call(
        functools.partial(_peer_stats_body, tm),
        grid=(n // tm,),
        in_specs=[pl.BlockSpec((tm, D_MODEL), lambda i: (i, 0)),
                  pl.BlockSpec((1, D_MODEL), lambda i: (0, 0)),
                  pl.BlockSpec((D_MODEL, n_hc * LANES), lambda i: (0, 0)),
                  pl.BlockSpec((n_hc, LANES, LANES), lambda i: (0, 0, 0))],
        out_specs=[pl.BlockSpec((n_hc, LANES, tm), lambda i: (0, 0, i)),
                   pl.BlockSpec((4 * PEER_HEADS, 1, tm), lambda i: (0, 0, i))],
        out_shape=[jax.ShapeDtypeStruct((n_hc, LANES, n), F32),
                   jax.ShapeDtypeStruct((4 * PEER_HEADS, 1, n), F32)],
        scratch_shapes=[pltpu.VMEM((n_hc, PEER_TOPK, tm), F32), pltpu.VMEM((n_hc, 1, tm), F32)],
        compiler_params=_params("parallel"),
        name=name,
    )(x, nw, wq, sk)


def _peer_dense_body(tm, te, x_ref, nw_ref, st_ref, stat_ref, u_ref, vt_ref, o_ref,
                     h_ref, thr_ref, w1_ref, s2c_ref, e2_ref, acc_ref):
    e = pl.program_id(1)

    @pl.when(e == 0)
    def _():
        h_ref[...] = _rms(x_ref[...], nw_ref[...]).astype(BF16)
        acc_ref[...] = jnp.zeros(acc_ref.shape, F32)
        for hd in range(PEER_HEADS):
            tau = stat_ref[hd]
            m1 = stat_ref[PEER_HEADS + hd]
            m2 = stat_ref[2 * PEER_HEADS + hd]
            inv_z = stat_ref[3 * PEER_HEADS + hd]
            thr_ref[hd] = m1 - st_ref[2 * hd]
            w1_ref[hd] = jnp.exp(st_ref[2 * hd] - m1) * inv_z
            s2c_ref[hd] = (st_ref[2 * hd + 1] - (tau - m1)).astype(BF16)
            e2_ref[hd] = jnp.exp(st_ref[2 * hd + 1] - m2).astype(BF16)

    n_i = te // PEER_NKEYS
    zero = jnp.zeros((), BF16)
    h = h_ref[...]
    total = None
    for k in range(te // EXPERT_SUB):
        rows = slice(k * EXPERT_SUB, (k + 1) * EXPERT_SUB)
        act = jax.nn.gelu(_dot_nt(u_ref[rows, :], h)).astype(BF16)
        parts = []
        for ii in range(EXPERT_SUB // PEER_NKEYS):
            i = e * n_i + k * (EXPERT_SUB // PEER_NKEYS) + ii
            gmat = jnp.zeros((PEER_NKEYS, tm), BF16)
            for hd in range(PEER_HEADS):
                thr = thr_ref[hd, pl.ds(i, 1), :].astype(BF16)
                w1 = w1_ref[hd, pl.ds(i, 1), :].astype(BF16)
                gmat = gmat + jnp.where(s2c_ref[hd] > thr, e2_ref[hd] * w1, zero)
            parts.append(gmat * act[ii * PEER_NKEYS:(ii + 1) * PEER_NKEYS])
        t = _dot(vt_ref[:, rows], jnp.concatenate(parts, axis=0))
        total = t if total is None else total + t
    acc_ref[...] += total

    @pl.when(e == pl.num_programs(1) - 1)
    def _():
        o_ref[...] = x_ref[...] + acc_ref[...].T


def _peer_dense(x, nw, st, stat, u_bf, vt_bf, name):
    n = x.shape[0]
    tm = min(ROW_TILE, n)
    te = EXPERT_TILE
    n_exp = u_bf.shape[0]
    n_hc = 2 * PEER_HEADS
    return pl.pallas_call(
        functools.partial(_peer_dense_body, tm, te),
        grid=(n // tm, n_exp // te),
        in_specs=[pl.BlockSpec((tm, D_MODEL), lambda i, e: (i, 0)),
                  pl.BlockSpec((1, D_MODEL), lambda i, e: (0, 0)),
                  pl.BlockSpec((n_hc, LANES, tm), lambda i, e: (0, 0, i)),
                  pl.BlockSpec((4 * PEER_HEADS, 1, tm), lambda i, e: (0, 0, i)),
                  pl.BlockSpec((te, D_MODEL), lambda i, e: (e, 0)),
                  pl.BlockSpec((D_MODEL, te), lambda i, e: (0, e))],
        out_specs=pl.BlockSpec((tm, D_MODEL), lambda i, e: (i, 0)),
        out_shape=jax.ShapeDtypeStruct((n, D_MODEL), F32),
        scratch_shapes=[pltpu.VMEM((tm, D_MODEL), BF16),
                        pltpu.VMEM((PEER_HEADS, PEER_NKEYS, tm), F32),
                        pltpu.VMEM((PEER_HEADS, PEER_NKEYS, tm), F32),
                        pltpu.VMEM((PEER_HEADS, PEER_NKEYS, tm), BF16),
                        pltpu.VMEM((PEER_HEADS, PEER_NKEYS, tm), BF16),
                        pltpu.VMEM((D_MODEL, tm), F32)],
        compiler_params=_params("parallel", "arbitrary"),
        name=name,
    )(x, nw, st, stat, u_bf, vt_bf)


def _norm_body(x_ref, w_ref, o_ref):
    o_ref[...] = _rms(x_ref[...], w_ref[...])


def _final_norm(x, w, name):
    n = x.shape[0]
    tm = min(ROW_TILE, n)
    return pl.pallas_call(
        _norm_body,
        grid=(n // tm,),
        in_specs=[pl.BlockSpec((tm, D_MODEL), lambda i: (i, 0)), pl.BlockSpec((1, D_MODEL), lambda i: (0, 0))],
        out_specs=pl.BlockSpec((tm, D_MODEL), lambda i: (i, 0)),
        out_shape=jax.ShapeDtypeStruct((n, D_MODEL), F32),
        compiler_params=_params("parallel"),
        name=name,
    )(x, w)


SROWS = 8
SEQS = 2


def _page_specs(n_pages, rows, layer):
    def spec(s, p):
        return pl.BlockSpec((None, None, rows, LANES), lambda b, pt: (layer, pt[b * SEQS + s, p], 0, 0))
    return [spec(s, p) for s in range(SEQS) for p in range(n_pages)]


def _row_spec(width):
    return pl.BlockSpec((SEQS, SROWS, width), lambda b, pt: (b, 0, 0))


def _seq_spec(rows, width):
    return pl.BlockSpec((SEQS, rows, width), lambda b, pt: (b, 0, 0))


def _per_sequence(body, n_const, n_seq, n_pages):
    def wrapped(pt_ref, *refs):
        consts = refs[:n_const]
        seqs = refs[n_const:n_const + n_seq]
        pages = refs[n_const + n_seq:n_const + n_seq + 2 * SEQS * n_pages]
        tail = refs[n_const + n_seq + 2 * SEQS * n_pages:]
        kp, vp = pages[:SEQS * n_pages], pages[SEQS * n_pages:]
        for s in range(SEQS):
            body(pt_ref, *consts, *[r.at[s] for r in seqs], *kp[s * n_pages:(s + 1) * n_pages],
                 *vp[s * n_pages:(s + 1) * n_pages], *[r.at[s] for r in tail])
    return wrapped


def _dup(x):
    return jnp.concatenate([x, x], axis=0)


def _pad_rows(x):
    return jnp.concatenate([x, jnp.zeros((PAGE_SIZE - x.shape[0], x.shape[1]), F32)], axis=0)


def _new_mask(m_rows, n_new):
    lane = lax.broadcasted_iota(jnp.int32, (1, LANES), 1)
    trow = lax.broadcasted_iota(jnp.int32, (m_rows, 1), 0) & (SROWS - 1)
    return jnp.where(lane < n_new, jnp.where(lane <= trow, 1.0, 0.0), 0.0) > 0.5


def _softmax_pieces(pieces):
    mx = None
    for s, mk in pieces:
        sm = s if mk is None else jnp.where(mk, s, NEG)
        mx = sm if mx is None else jnp.maximum(mx, sm)
    m = jnp.max(mx, axis=1, keepdims=True)
    ps = []
    tot = None
    for s, mk in pieces:
        p = jnp.exp(s - m)
        if mk is not None:
            p = jnp.where(mk, p, 0.0)
        tot = p if tot is None else tot + p
        ps.append(p.astype(BF16))
    return ps, jnp.sum(tot, axis=1, keepdims=True)


def _scores_kt(q, kts):
    return [_dot(q, _dup(kt.astype(BF16))) for kt in kts]


def _values_kt(ps, vts):
    o = None
    for p, vt in zip(ps, vts):
        t = _dot_nt(p, _dup(vt.astype(BF16)))
        o = t if o is None else o + t
    return o


def _diff_s_body(n_pages, n_new, lam_init, pt_ref, lam_ref, subln_ref, q_ref, kn_ref, vn_ref, *rest):
    k_refs = rest[:n_pages]
    v_refs = rest[n_pages:2 * n_pages]
    o_ref = rest[2 * n_pages]
    lane = lax.broadcasted_iota(jnp.int32, (1, LANES), 1)
    upper = lane >= HD
    new_mask = _new_mask(4 * SROWS, n_new)
    lp = lam_ref[...]
    lam = (jnp.exp(jnp.sum(lp[0:1] * lp[1:2], axis=1, keepdims=True))
           - jnp.exp(jnp.sum(lp[2:3] * lp[3:4], axis=1, keepdims=True)) + lam_init)
    for g in range(A_KV):
        parts = []
        for r in range(A_HEADS // A_KV):
            qb = q_ref[:, (2 * g + r) * LANES:(2 * g + r + 1) * LANES] * (HD ** -0.5)
            parts += [jnp.where(upper, 0.0, qb), jnp.where(upper, qb, 0.0)]
        q = jnp.concatenate(parts, axis=0).astype(BF16)
        pieces = [(_dot_nt(q, k_refs[p][pl.ds(g, PAGE_SIZE, stride=A_KV), :].astype(BF16)), None)
                  for p in range(n_pages)]
        knew = _pad_rows(kn_ref[:, g * LANES:(g + 1) * LANES]).astype(BF16)
        vnew = _pad_rows(vn_ref[:, g * LANES:(g + 1) * LANES]).astype(BF16)
        pieces.append((_dot_nt(q, knew), new_mask))
        ps, tot = _softmax_pieces(pieces)
        o = _dot(ps[n_pages], vnew)
        for p in range(n_pages):
            o = o + _dot(ps[p], v_refs[p][pl.ds(g, PAGE_SIZE, stride=A_KV), :].astype(BF16))
        o = o / tot
        for r in range(A_HEADS // A_KV):
            d = o[2 * r * SROWS:(2 * r + 1) * SROWS] - lam * o[(2 * r + 1) * SROWS:(2 * r + 2) * SROWS]
            o_ref[:, (2 * g + r) * LANES:(2 * g + r + 1) * LANES] = _rms(d, subln_ref[...]) * (1.0 - lam_init)


def _diff_sample(page_table, layer, n_new, aq, ak_new, av_new, cache_k, cache_v, lam_p, subln, lam_init, name):
    db, n_pages = page_table.shape
    const = lambda shape: pl.BlockSpec(shape, lambda b, pt: (0,) * len(shape))
    in_specs = ([const((4, HD)), const((1, 2 * HD)), _row_spec(512), _row_spec(256), _row_spec(256)]
                + _page_specs(n_pages, PAGE_SIZE * A_KV, layer) + _page_specs(n_pages, PAGE_SIZE * A_KV, layer))
    grid_spec = pltpu.PrefetchScalarGridSpec(num_scalar_prefetch=1, grid=(db // SEQS,), in_specs=in_specs,
                                             out_specs=_row_spec(512))
    return pl.pallas_call(
        _per_sequence(functools.partial(_diff_s_body, n_pages, n_new, lam_init), 2, 3, n_pages),
        grid_spec=grid_spec,
        out_shape=jax.ShapeDtypeStruct((db, SROWS, 512), F32),
        compiler_params=_params("parallel"),
        name=name,
    )(page_table, lam_p, subln, aq, ak_new, av_new, *([cache_k] * (SEQS * n_pages)), *([cache_v] * (SEQS * n_pages)))


def _stack_heads(q_ref, n_heads, width, scale):
    return (jnp.concatenate([q_ref[:, h * width:(h + 1) * width] for h in range(n_heads)], axis=0)
            * scale).astype(BF16)


def _place_heads(o, place_ref, n_heads):
    out = None
    for h in range(n_heads):
        t = _dot(o[h * SROWS:(h + 1) * SROWS].astype(BF16), place_ref[h])
        out = t if out is None else out + t
    return out


def _place_matrices(n_heads, n_groups):
    rep = n_heads // n_groups
    p = np.zeros((n_heads, n_groups * HD, n_heads * HD), np.float32)
    for h in range(n_heads):
        for d in range(HD):
            p[h, (h // rep) * HD + d, h * HD + d] = 1.0
    return jnp.asarray(p, BF16)


def _moba_s_body(n_pages, n_new, pt_ref, place_ref, q_ref, kn_ref, vn_ref, *rest):
    k_refs = rest[:n_pages]
    v_refs = rest[n_pages:2 * n_pages]
    o_ref = rest[2 * n_pages]
    new_mask = _new_mask(C_HEADS * SROWS, n_new)
    knt = _pad_rows(kn_ref[...]).T.astype(BF16)
    vnt = _pad_rows(vn_ref[...]).T.astype(BF16)
    ppb = MOBA_BLOCK // PAGE_SIZE
    n_blocks = n_pages // ppb
    q = _stack_heads(q_ref, C_HEADS, C_KV * HD, HD ** -0.5)
    ss = [_dot(q, k_refs[p][...].astype(BF16)) for p in range(n_pages)] + [_dot(q, knt)]
    bs = []
    for j in range(n_blocks):
        acc = ss[j * ppb]
        for u in range(1, ppb):
            acc = acc + ss[j * ppb + u]
        bs.append(jnp.sum(acc, axis=1, keepdims=True))
    pieces = []
    for j in range(n_blocks):
        rank = jnp.zeros(bs[j].shape, F32)
        for i in range(n_blocks):
            if i < j:
                rank = rank + jnp.where(bs[i] >= bs[j], 1.0, 0.0)
            elif i > j:
                rank = rank + jnp.where(bs[i] > bs[j], 1.0, 0.0)
        keep = rank < float(MOBA_TOPK)
        for u in range(ppb):
            pieces.append((ss[j * ppb + u], keep))
    pieces.append((ss[n_pages], new_mask))
    ps, tot = _softmax_pieces(pieces)
    o = _dot_nt(ps[n_pages], vnt)
    for p in range(n_pages):
        o = o + _dot_nt(ps[p], v_refs[p][...].astype(BF16))
    o_ref[...] = _place_heads(o / tot, place_ref, C_HEADS)


def _moba_sample(page_table, layer, n_new, cq, ck_new, cv_new, cache_k, cache_v, name):
    db, n_pages = page_table.shape
    place = _place_matrices(C_HEADS, C_KV)
    in_specs = ([pl.BlockSpec(place.shape, lambda b, pt: (0, 0, 0)),
                 _row_spec(C_HEADS * C_KV * HD), _row_spec(256), _row_spec(256)]
                + _page_specs(n_pages, C_KV * HD, layer) + _page_specs(n_pages, C_KV * HD, layer))
    grid_spec = pltpu.PrefetchScalarGridSpec(num_scalar_prefetch=1, grid=(db // SEQS,), in_specs=in_specs,
                                             out_specs=_row_spec(512))
    return pl.pallas_call(
        _per_sequence(functools.partial(_moba_s_body, n_pages, n_new), 1, 3, n_pages),
        grid_spec=grid_spec,
        out_shape=jax.ShapeDtypeStruct((db, SROWS, 512), F32),
        compiler_params=_params("parallel"),
        name=name,
    )(page_table, place, cq, ck_new, cv_new,*([cache_k] * (SEQS * n_pages)), *([cache_v] * (SEQS * n_pages)))


def _compress_s_body(n_pages, pt_ref, pek_ref, pev_ref, w1k_ref, w1v_ref, w2k_ref, w2v_ref, *rest):
    k_refs = rest[:n_pages]
    v_refs = rest[n_pages:2 * n_pages]
    ok_ref, ov_ref, xk_ref, xv_ref = rest[2 * n_pages:]
    for p in range(n_pages):
        xk_ref[p * PAGE_SIZE:(p + 1) * PAGE_SIZE, :] = k_refs[p][...].T
        xv_ref[p * PAGE_SIZE:(p + 1) * PAGE_SIZE, :] = v_refs[p][...].T
    _compress_one(xk_ref, pek_ref, w1k_ref, w2k_ref, ok_ref)
    _compress_one(xv_ref, pev_ref, w1v_ref, w2v_ref, ov_ref)


def _compress_sample(page_table, layer, cache_k, cache_v, cw, name):
    db, n_pages = page_table.shape
    const = lambda shape: pl.BlockSpec(shape, lambda b, pt: (0,) * len(shape))
    in_specs = ([const((CMP_BLOCK, 1, LANES))] * 2 + [const((CMP_BLOCK, LANES, 2 * CMP_HIDDEN))] * 2
                + [const((2 * CMP_HIDDEN, LANES))] * 2
                + _page_specs(n_pages, B_KV * HD, layer) + _page_specs(n_pages, B_KV * HD, layer))
    o_spec = _seq_spec(LANES, LANES)
    grid_spec = pltpu.PrefetchScalarGridSpec(
        num_scalar_prefetch=1, grid=(db // SEQS,), in_specs=in_specs, out_specs=[o_spec, o_spec],
        scratch_shapes=[pltpu.VMEM((SEQS, n_pages * PAGE_SIZE, LANES), F32)] * 2)
    return pl.pallas_call(
        _per_sequence(functools.partial(_compress_s_body, n_pages), 6, 0, n_pages),
        grid_spec=grid_spec,
        out_shape=[jax.ShapeDtypeStruct((db, LANES, LANES), F32)] * 2,
        compiler_params=_params("parallel"),
        name=name,
    )(page_table, cw["pek"], cw["pev"], cw["w1k"], cw["w1v"], cw["w2k"], cw["w2v"],
      *([cache_k] * (SEQS * n_pages)), *([cache_v] * (SEQS * n_pages)))


def _nsa_s_body(n_pages, n_new, q_pos0, n_win, pt_ref, place_ref, qc_ref, qr_ref, bg_ref, kcmp_ref, vcmp_ref,
                ksn_ref, vsn_ref, kwn_ref, vwn_ref, wk_ref, wv_ref, *rest):
    k_refs = rest[:n_pages]
    v_refs = rest[n_pages:2 * n_pages]
    o_ref = rest[2 * n_pages]
    lane = lax.broadcasted_iota(jnp.int32, (1, LANES), 1)
    halfi = lane >> 6
    rep = B_HEADS // B_KV
    m_rows = B_HEADS * SROWS
    new_mask = _new_mask(m_rows, n_new)
    trow8 = lax.broadcasted_iota(jnp.int32, (SROWS, 1), 0)
    trow = lax.broadcasted_iota(jnp.int32, (m_rows, 1), 0) & (SROWS - 1)
    qpos8 = q_pos0 + trow8
    ns = -(-(q_pos0 + n_new) // SEL_BLOCK)
    bpp = PAGE_SIZE // SEL_BLOCK
    transposed = lambda ref: _pad_rows(ref[...]).T.astype(BF16)
    ksnt, vsnt, kwnt, vwnt = transposed(ksn_ref), transposed(vsn_ref), transposed(kwn_ref), transposed(vwn_ref)
    kcmp = kcmp_ref[...].astype(BF16)
    vcmp = vcmp_ref[...].astype(BF16)
    ci = lax.broadcasted_iota(jnp.int32, (LANES, LANES), 0)
    cj = lax.broadcasted_iota(jnp.int32, (LANES, LANES), 1)
    onehot = jnp.where((ci >> CMP_PER_SEL_SHIFT) == cj, 1.0, 0.0).astype(BF16)
    sig = jax.nn.sigmoid(bg_ref[...])
    per_group = lambda xs: jnp.concatenate([x for x in xs for _ in range(rep)], axis=0)
    qc = _stack_heads(qc_ref, B_HEADS, B_KV * HD, HD ** -0.5)
    qr = _stack_heads(qr_ref, B_HEADS, B_KV * HD, HD ** -0.5)

    cm8 = jnp.where((lane * CMP_STRIDE + (CMP_BLOCK - 1)) <= qpos8, 1.0, 0.0)
    cmask = jnp.concatenate([cm8] * B_HEADS, axis=0) > 0.5
    s = jnp.where(cmask, _dot_nt(qc, kcmp), NEG)
    m = jnp.max(s, axis=1, keepdims=True)
    p = jnp.where(cmask, jnp.exp(s - m), 0.0)
    p = p / jnp.maximum(jnp.sum(p, axis=1, keepdims=True), 1e-30)
    o_cmp = _dot(p.astype(BF16), vcmp)
    cur = qpos8 >> 6
    sels = []
    for g in range(B_KV):
        psum = p[g * rep * SROWS:(g * rep + 1) * SROWS]
        for r in range(1, rep):
            psum = psum + p[(g * rep + r) * SROWS:(g * rep + r + 1) * SROWS]
        hi, mid, lo = _split3(psum)
        imp = _dot(hi, onehot) + _dot(mid, onehot) + _dot(lo, onehot)
        impm = jnp.where(lane == cur, jnp.inf, jnp.where(lane < cur, imp, -jnp.inf))
        rank = _rank_desc(impm, ns, lane)
        sels.append(jnp.where(rank < float(SEL_TOPK), jnp.where(lane <= cur, 1.0, 0.0), 0.0))

    ss = [_dot(qr, k_refs[pg][...].astype(BF16)) for pg in range(n_pages)] + [_dot(qr, ksnt)]
    pieces = []
    for pg in range(n_pages):
        mks = []
        for sel in sels:
            mk = sel[:, pg * bpp:pg * bpp + 1]
            for u in range(1, bpp):
                mk = jnp.where(halfi >= u, sel[:, pg * bpp + u:pg * bpp + u + 1], mk)
            mks.append(mk)
        pieces.append((ss[pg], per_group(mks) > 0.5))
    pieces.append((ss[n_pages], new_mask))
    ps, tot = _softmax_pieces(pieces)
    o_slc = _dot_nt(ps[n_pages], vsnt)
    for pg in range(n_pages):
        o_slc = o_slc + _dot_nt(ps[pg], v_refs[pg][...].astype(BF16))
    o_slc = o_slc / tot

    wcols = [slice(u * LANES, (u + 1) * LANES) for u in range(n_win // LANES)]
    ss = [_dot(qr, wk_ref[:, c].astype(BF16)) for c in wcols] + [_dot(qr, kwnt)]
    pieces = []
    for u in range(len(wcols)):
        dist = (n_win - u * LANES) + trow - lane
        pieces.append((ss[u], dist <= WINDOW))
    pieces.append((ss[len(wcols)], new_mask))
    ps, tot = _softmax_pieces(pieces)
    o_win = _dot_nt(ps[len(wcols)], vwnt)
    for u, c in enumerate(wcols):
        o_win = o_win + _dot_nt(ps[u], wv_ref[:, c].astype(BF16))
    o_win = o_win / tot

    gate = lambda c: jnp.concatenate([sig[:, h * 3 + c:h * 3 + c + 1] for h in range(B_HEADS)], axis=0)
    o = gate(0) * o_cmp + gate(1) * o_slc + gate(2) * o_win
    o_ref[...] = _place_heads(o, place_ref, B_HEADS)


def _nsa_sample(page_table, layer, n_new, q_pos0, bqc, bqr, bg, kcmp, vcmp, ks_new, vs_new, kw_new, vw_new,
                state_k, state_v, cache_k, cache_v, name):
    db, n_pages = page_table.shape
    n_win = state_k.shape[-1]
    place = _place_matrices(B_HEADS, B_KV)
    w_spec = pl.BlockSpec((None, SEQS, B_KV * HD, n_win), lambda b, pt: (layer, b, 0, 0))
    c_spec = _seq_spec(LANES, LANES)
    q_width = B_HEADS * B_KV * HD
    in_specs = ([pl.BlockSpec(place.shape, lambda b, pt: (0, 0, 0)),
                 _row_spec(q_width), _row_spec(q_width), _row_spec(LANES), c_spec, c_spec]
                + [_row_spec(LANES)] * 4 + [w_spec, w_spec]
                + _page_specs(n_pages, B_KV * HD, layer) + _page_specs(n_pages, B_KV * HD, layer))
    grid_spec = pltpu.PrefetchScalarGridSpec(num_scalar_prefetch=1, grid=(db // SEQS,), in_specs=in_specs,
                                             out_specs=_row_spec(512))
    return pl.pallas_call(
        _per_sequence(functools.partial(_nsa_s_body, n_pages, n_new, q_pos0, n_win), 1, 11, n_pages),
        grid_spec=grid_spec,
        out_shape=jax.ShapeDtypeStruct((db, SROWS, 512), F32),
        compiler_params=_params("parallel"),
        name=name,
    )(page_table, place, bqc, bqr, bg, kcmp, vcmp, ks_new, vs_new, kw_new, vw_new, state_k, state_v,
      *([cache_k] * (SEQS * n_pages)), *([cache_v] * (SEQS * n_pages)))


def _layer_weights(l, norm_mix, norm_ffn, w_in, a_lambda, a_subln, b_cmp_pe, b_cmp_w1, b_cmp_w2,
                   w_branch, w_out, peer_wq, peer_subkeys, peer_u, peer_v):
    w = w_in[l]
    bg_w = jnp.pad(w[:, 2304:2328], ((0, 0), (0, LANES - 3 * B_HEADS)))
    return {
        "norm_mix": norm_mix[l][None, :],
        "norm_ffn": norm_ffn[l][None, :],
        "w_a": w[:, 0:1024].astype(BF16),
        "w_b": jnp.concatenate([w[:, 1024:2304], bg_w], axis=1).astype(BF16),
        "w_c": w[:, 2328:3352].astype(BF16),
        "w_b_s": jnp.concatenate([_spread_heads(w[:, 1024:1536], B_HEADS, B_KV), w[:, 1536:2304], bg_w],
                                 axis=1).astype(BF16),
        "w_c_s": jnp.concatenate([_spread_heads(w[:, 2328:2840], C_HEADS, C_KV), w[:, 2840:3352]],
                                 axis=1).astype(BF16),
        "w_g": w[:, 3352:6424].astype(BF16),
        "lam": a_lambda[l],
        "subln": a_subln[l][None, :],
        "cmp": _compress_weights(b_cmp_pe[l], b_cmp_w1[l], b_cmp_w2[l]),
        "w_branch": w_branch[l].astype(BF16),
        "w_out": w_out[l].astype(BF16),
        "wq": peer_wq[l].astype(BF16),
        "sk": peer_subkeys[l].reshape(2 * PEER_HEADS, PEER_NKEYS, LANES).astype(BF16),
        "u": peer_u[l].astype(BF16),
        "vt": peer_v[l].T.astype(BF16),
        "lam_init": 0.8 - 0.6 * math.exp(-0.3 * l),
    }


def _project_all(x, lw, cos_t, sin_t, tag):
    aq, ak, av = _project(x, lw["norm_mix"], lw["w_a"], cos_t, sin_t, PIECES_A, WIDTHS_A, "proj_a_" + tag)
    if tag == "s":
        b_out = _project(x, lw["norm_mix"], lw["w_b_s"], cos_t, sin_t, PIECES_B_S, WIDTHS_B_S, "proj_b_s")
        cq, ck, cv = _project(x, lw["norm_mix"], lw["w_c_s"], cos_t, sin_t, PIECES_C_S, WIDTHS_C_S, "proj_c_s")
    else:
        b_out = _project(x, lw["norm_mix"], lw["w_b"], cos_t, sin_t, PIECES_B, WIDTHS_B, "proj_b_p")
        cq, ck, cv = _project(x, lw["norm_mix"], lw["w_c"], cos_t, sin_t, PIECES_C, WIDTHS_C, "proj_c_p")
    (gate,) = _project(x, lw["norm_mix"], lw["w_g"], cos_t, sin_t, PIECES_G, WIDTHS_G, "proj_g_" + tag)
    return (aq, ak, av), b_out, (cq, ck, cv), gate


def _ffn(x, lw, tag):
    st, stat = _peer_stats(x, lw["norm_ffn"], lw["wq"], lw["sk"], "peer_stats_" + tag)
    return _peer_dense(x, lw["norm_ffn"], st, stat, lw["u"], lw["vt"], "peer_dense_" + tag)


def _prompt_layer(x, lw, cos_t, sin_t, bn, t):
    (aq, ak, av), (bqc, bqr, bkc, bvc, bks, bvs, bkw, bvw, bg), (cq, ck, cv), gate = _project_all(
        x, lw, cos_t, sin_t, "p")
    r3 = lambda a: a.reshape(bn, t, a.shape[-1])
    o_a = _diff_attention(r3(aq), r3(ak), r3(av), lw["lam"], lw["subln"], lw["lam_init"], 0, "diff_p")
    kcmp, vcmp = _compress(r3(bkc), r3(bvc), lw["cmp"], "cmp_p")
    o_b = _nsa_attention(r3(bqc), r3(bqr), r3(bg), kcmp, vcmp, r3(bks), r3(bvs), r3(bkw), r3(bvw),
                         0, t, 0, t, "nsa_p")
    o_c = _moba_attention(r3(cq), r3(ck), r3(cv), 0, "moba_p")
    n = bn * t
    x = _merge(x, gate, o_a.reshape(n, 512), o_b.reshape(n, 512), o_c.reshape(n, 512),
               lw["w_branch"], lw["w_out"], "merge_p")
    x = _ffn(x, lw, "p")
    wp = min(WINDOW, t)
    rows = (r3(ak).reshape(bn, t, A_KV, 2 * HD), r3(av).reshape(bn, t, A_KV, 2 * HD),
            r3(bkc).reshape(bn, t, B_KV, HD), r3(bvc).reshape(bn, t, B_KV, HD),
            r3(bks).reshape(bn, t, B_KV, HD), r3(bvs).reshape(bn, t, B_KV, HD),
            r3(ck).reshape(bn, t, C_KV, HD), r3(cv).reshape(bn, t, C_KV, HD),
            r3(bkw)[:, t - wp:].reshape(bn, wp, B_KV, HD), r3(bvw)[:, t - wp:].reshape(bn, wp, B_KV, HD))
    return x, rows


def _sample_layer(x, lw, l, cos_t, sin_t, db, t, past, caches, state_wk, state_wv, page_table):
    (aq, ak, av), (bqc, bqr, bkc, bvc, bks, bvs, bkw, bvw, bg), (cq, ck, cv), gate = _project_all(
        x, lw, cos_t, sin_t, "s")
    r3 = lambda a: a.reshape(db, SROWS, a.shape[-1])
    c_ak, c_av, c_bkc, c_bvc, c_bks, c_bvs, c_ck, c_cv = caches
    o_a = _diff_sample(page_table, l, t, r3(aq), r3(ak), r3(av), c_ak, c_av, lw["lam"], lw["subln"],
                       lw["lam_init"], "diff_s")
    kcmp, vcmp = _compress_sample(page_table, l, c_bkc, c_bvc, lw["cmp"], "cmp_s")
    o_b = _nsa_sample(page_table, l, t, past, r3(bqc), r3(bqr), r3(bg), kcmp, vcmp, r3(bks), r3(bvs),
                      r3(bkw), r3(bvw), state_wk, state_wv, c_bks, c_bvs, "nsa_s")
    o_c = _moba_sample(page_table, l, t, r3(cq), r3(ck), r3(cv), c_ck, c_cv, "moba_s")
    n = db * SROWS
    x = _merge(x, gate, o_a.reshape(n, 512), o_b.reshape(n, 512), o_c.reshape(n, 512),
               lw["w_branch"], lw["w_out"], "merge_s")
    x = _ffn(x, lw, "s")
    new = lambda a, kv, w: r3(a)[:, :t].reshape(db, t, kv, w)

    def rolled(state_t, a):
        new_t = jnp.transpose(r3(a)[:, :t], (0, 2, 1))
        out_t = jnp.concatenate([state_t[l][:, :, t:], new_t], axis=2)
        return jnp.transpose(out_t, (0, 2, 1)).reshape(db, out_t.shape[2], B_KV, HD)

    rows = (new(ak, A_KV, 2 * HD), new(av, A_KV, 2 * HD), new(bkc, B_KV, HD), new(bvc, B_KV, HD),
            new(bks, B_KV, HD), new(bvs, B_KV, HD), new(ck, C_KV, HD), new(cv, C_KV, HD),
            rolled(state_wk, bkw), rolled(state_wv, bvw))
    return x, rows


def kernel(x_prompt, x_sample, cache_a_k, cache_a_v, cache_b_kc, cache_b_vc, cache_b_ks, cache_b_vs,
           cache_c_k, cache_c_v, state_b_wk, state_b_wv, page_table, norm_mix, norm_ffn, norm_final,
           w_in, a_lambda, a_subln, b_cmp_pe, b_cmp_w1, b_cmp_w2, w_branch, w_out,
           peer_wq, peer_subkeys, peer_u, peer_v):
    bn, t, _ = x_prompt.shape
    db, ts, _ = x_sample.shape
    depth = w_in.shape[0]
    past = page_table.shape[1] * PAGE_SIZE
    wb = state_b_wk.shape[2]
    assert past % MOBA_BLOCK == 0 and ts <= SROWS and wb == WINDOW and wb % LANES == 0 and db % SEQS == 0
    rows_view = lambda c: c.reshape(c.shape[0], c.shape[1], c.shape[2] * c.shape[3], c.shape[4])
    lanes_view = lambda c: jnp.transpose(c, (0, 1, 3, 4, 2)).reshape(
        c.shape[0], c.shape[1], c.shape[3] * c.shape[4], c.shape[2])
    caches = ([rows_view(c) for c in (cache_a_k, cache_a_v)]
              + [lanes_view(c) for c in (cache_b_kc, cache_b_vc, cache_b_ks, cache_b_vs, cache_c_k, cache_c_v)])
    swk = lanes_view(state_b_wk)
    swv = lanes_view(state_b_wv)

    cos_p, sin_p = _rope_tables(jnp.arange(t, dtype=jnp.int32))
    tm_s = min(ROW_TILE, db * SROWS)
    cos_s, sin_s = _rope_tables(past + (jnp.arange(tm_s, dtype=jnp.int32) % SROWS))

    xp = x_prompt.reshape(bn * t, D_MODEL)
    xs = jnp.pad(x_sample, ((0, 0), (0, SROWS - ts), (0, 0))).reshape(db * SROWS, D_MODEL)
    rows_p, rows_s = [], []
    for l in range(depth):
        lw = _layer_weights(l, norm_mix, norm_ffn, w_in, a_lambda, a_subln, b_cmp_pe, b_cmp_w1, b_cmp_w2,
                            w_branch, w_out, peer_wq, peer_subkeys, peer_u, peer_v)
        xp, rp = _prompt_layer(xp, lw, cos_p, sin_p, bn, t)
        xs, rs = _sample_layer(xs, lw, l, cos_s, sin_s, db, ts, past, caches, swk, swv, page_table)
        rows_p.append(rp)
        rows_s.append(rs)
    y_prompt = _final_norm(xp, norm_final[None, :], "final_p").reshape(bn, t, D_MODEL)
    y_sample = _final_norm(xs, norm_final[None, :], "final_s").reshape(db, SROWS, D_MODEL)[:, :ts]
    outs_p = [jnp.stack(r, axis=0) for r in zip(*rows_p)]
    outs_s = [jnp.stack(r, axis=0) for r in zip(*rows_s)]
    return (y_prompt, y_sample, *outs_p, *outs_s)
```

```python
import functools
import math

import jax
import jax.numpy as jnp
import numpy as np
from jax import lax
from jax.experimental import pallas as pl
from jax.experimental.pallas import tpu as pltpu

F32 = jnp.float32
BF16 = jnp.bfloat16

D_MODEL = 1024
HD = 64
A_HEADS, A_KV = 4, 2
B_HEADS, B_KV = 8, 2
C_HEADS, C_KV = 8, 4
CMP_BLOCK, CMP_STRIDE, CMP_HIDDEN = 32, 16, 128
SEL_BLOCK, SEL_TOPK = 64, 8
CMP_PER_SEL_SHIFT = 2
WINDOW = 512
MOBA_BLOCK, MOBA_TOPK = 256, 3
PEER_HEADS, PEER_NKEYS, PEER_TOPK = 8, 128, 16
ROPE_THETA = 10000.0
EPS = 1e-6
PAGE_SIZE = 128

LANES = 128
NEG = -1e30
QK_SCALE_LOG2 = math.log2(math.e) * HD ** -0.5
VMEM_LIMIT = 48 * 1024 * 1024
ROW_TILE = 256
KV_CHUNK = 256
EXPERT_TILE = 2048
EXPERT_SUB = 256


def _params(*sem):
    return pltpu.CompilerParams(dimension_semantics=sem, vmem_limit_bytes=VMEM_LIMIT)


def _dot(a, b):
    return jnp.dot(a, b, preferred_element_type=F32)


def _dot_nt(a, b):
    return lax.dot_general(a, b, (((1,), (1,)), ((), ())), preferred_element_type=F32)


def _split3(a):
    hi = a.astype(BF16)
    r1 = a - hi.astype(F32)
    mid = r1.astype(BF16)
    lo = (r1 - mid.astype(F32)).astype(BF16)
    return hi, mid, lo


def _rms(x, w):
    return x * lax.rsqrt(jnp.mean(x * x, axis=-1, keepdims=True) + EPS) * w


def _proj_body(pieces, chunk, x_ref, nw_ref, w_ref, cos_ref, sin_ref, *outs):
    h = _rms(x_ref[...], nw_ref[...]).astype(BF16)
    cos = cos_ref[...]
    sin = sin_ref[...]
    lane = lax.broadcasted_iota(jnp.int32, (1, LANES), 1)
    first = (lane & (HD - 1)) < (HD // 2)
    n_cols = len(pieces) * LANES
    for c0 in range(0, n_cols, chunk):
        w = min(chunk, n_cols - c0)
        z = _dot(h, w_ref[:, c0:c0 + w])
        for p in range(w // LANES):
            zp = z[:, p * LANES:(p + 1) * LANES]
            for (oi, oc, rope) in pieces[c0 // LANES + p]:
                if rope:
                    rot = jnp.where(first, pltpu.roll(zp, LANES - HD // 2, 1), pltpu.roll(zp, HD // 2, 1))
                    outs[oi][:, oc:oc + LANES] = zp * cos + rot * sin
                else:
                    outs[oi][:, oc:oc + LANES] = zp


def _project(x, nw, w, cos_t, sin_t, pieces, out_widths, name):
    n = x.shape[0]
    tm = min(ROW_TILE, n)
    ntab = cos_t.shape[0] // tm
    ncols = w.shape[1]
    body = functools.partial(_proj_body, pieces, 512)
    return pl.pallas_call(
        body,
        grid=(n // tm,),
        in_specs=[
            pl.BlockSpec((tm, D_MODEL), lambda i: (i, 0)),
            pl.BlockSpec((1, D_MODEL), lambda i: (0, 0)),
            pl.BlockSpec((D_MODEL, ncols), lambda i: (0, 0)),
            pl.BlockSpec((tm, LANES), lambda i: (i % ntab, 0)),
            pl.BlockSpec((tm, LANES), lambda i: (i % ntab, 0)),
        ],
        out_specs=[pl.BlockSpec((tm, ow), lambda i: (i, 0)) for ow in out_widths],
        out_shape=[jax.ShapeDtypeStruct((n, ow), F32) for ow in out_widths],
        compiler_params=_params("parallel"),
        name=name,
    )(x, nw, w, cos_t, sin_t)


def _plain(oi, width):
    return [[(oi, c, False)] for c in range(0, width, LANES)]


def _roped(oi, width):
    return [[(oi, c, True)] for c in range(0, width, LANES)]


PIECES_A = _roped(0, 512) + _roped(1, 256) + _plain(2, 256)
WIDTHS_A = (512, 256, 256)
PIECES_B = ([[(0, c, False), (1, c, True)] for c in range(0, 512, LANES)]
            + _plain(2, 128) + _plain(3, 128) + _roped(4, 128) + _plain(5, 128)
            + _roped(6, 128) + _plain(7, 128) + _plain(8, 128))
WIDTHS_B = (512, 512, 128, 128, 128, 128, 128, 128, 128)
PIECES_C = _roped(0, 512) + _roped(1, 256) + _plain(2, 256)
WIDTHS_C = (512, 256, 256)
PIECES_G = _plain(0, 3072)
WIDTHS_G = (3072,)
SPREAD_B = B_HEADS * B_KV * HD
SPREAD_C = C_HEADS * C_KV * HD
PIECES_B_S = [[(0, c, False), (1, c, True)] for c in range(0, SPREAD_B, LANES)] + PIECES_B[512 // LANES:]
WIDTHS_B_S = (SPREAD_B, SPREAD_B) + WIDTHS_B[2:]
PIECES_C_S = _roped(0, SPREAD_C) + _roped(1, 256) + _plain(2, 256)
WIDTHS_C_S = (SPREAD_C, 256, 256)


def _spread_heads(wq, n_heads, n_groups):
    rep = n_heads // n_groups
    onehot = (np.arange(n_heads)[:, None] // rep == np.arange(n_groups)[None, :]).astype(np.float32)
    w = wq.reshape(wq.shape[0], n_heads, HD)
    return jnp.einsum("dhk,hg->dhgk", w, jnp.asarray(onehot)).reshape(wq.shape[0], n_heads * n_groups * HD)


def _rope_tables(pos):
    half = HD // 2
    freqs = jnp.power(ROPE_THETA, -jnp.arange(half, dtype=F32) / half)
    ang = pos.astype(F32)[:, None] * freqs[None, :]
    cos = jnp.cos(ang)
    sin = jnp.sin(ang)
    cos_t = jnp.concatenate([cos, cos, cos, cos], axis=1)
    sin_t = jnp.concatenate([-sin, sin, -sin, sin], axis=1)
    return cos_t, sin_t


def _flash_init(m_ref, l_ref, acc_ref):
    m_ref[...] = jnp.full(m_ref.shape, NEG, F32)
    l_ref[...] = jnp.zeros(l_ref.shape, F32)
    acc_ref[...] = jnp.zeros(acc_ref.shape, F32)


def _lane_blocks(x):
    return [x[:, c:c + LANES] for c in range(0, x.shape[1], LANES)]


def _flash_step(hh, q_emb, kc, vc, biases, m_ref, l_ref, acc_ref):
    blocks = _lane_blocks(_dot_nt(q_emb, kc))
    if biases is not None:
        blocks = [b if bi is None else b + bi for b, bi in zip(blocks, biases)]
    mx = blocks[0]
    for b in blocks[1:]:
        mx = jnp.maximum(mx, b)
    m_old = m_ref[hh]
    m_new = jnp.maximum(m_old, jnp.max(mx, axis=1, keepdims=True))
    alpha = jnp.exp2(m_old - m_new)
    ps = [jnp.exp2(b - m_new) for b in blocks]
    tot = ps[0]
    for p in ps[1:]:
        tot = tot + p
    l_ref[hh] = alpha * l_ref[hh] + jnp.sum(tot, axis=1, keepdims=True)
    acc_ref[hh] = alpha * acc_ref[hh] + _dot(jnp.concatenate(ps, axis=1).astype(BF16), vc)
    m_ref[hh] = m_new


def _causal_sweep(q0, tq, body):
    n_full = (q0 + 1) // KV_CHUNK
    n_pair = n_full // 2
    lax.fori_loop(0, n_pair, body(2 * KV_CHUNK, False), 0)
    lax.fori_loop(2 * n_pair, n_full, body(KV_CHUNK, False), 0)
    lax.fori_loop(n_full, (q0 + tq - 1) // KV_CHUNK + 1, body(KV_CHUNK, True), 0)


def _causal_bias(j, tk, qpos, inside=0.0):
    kpos = j * tk + lax.broadcasted_iota(jnp.int32, (1, tk), 1)
    return jnp.where(kpos <= qpos, inside, NEG)


def _flash_out(hh, l_ref, acc_ref):
    return acc_ref[hh] / jnp.maximum(l_ref[hh], 1e-30)


def _chunk(ref, j, tk):
    return ref[pl.ds(pl.multiple_of(j * tk, tk), tk), :].astype(BF16)


def _diff_body(tq, tk, q_pos0, lam_init, lam_ref, subln_ref, q_ref, k_ref, v_ref, o_ref,
               m_ref, l_ref, acc_ref):
    qi = pl.program_id(2)
    lane = lax.broadcasted_iota(jnp.int32, (1, LANES), 1)
    upper = lane >= HD
    q0 = q_pos0 + qi * tq
    qpos = q0 + lax.broadcasted_iota(jnp.int32, (tq, 1), 0)
    _flash_init(m_ref, l_ref, acc_ref)
    qs = []
    for r in range(2):
        qb = q_ref[:, r * LANES:(r + 1) * LANES] * QK_SCALE_LOG2
        qs.append(jnp.where(upper, 0.0, qb).astype(BF16))
        qs.append(jnp.where(upper, qb, 0.0).astype(BF16))
    def sweep(tk, diag):
        def body(j, carry):
            kc = _chunk(k_ref, j, tk)
            vc = _chunk(v_ref, j, tk)
            biases = _lane_blocks(_causal_bias(j, tk, qpos)) if diag else None
            for hh in range(4):
                _flash_step(hh, qs[hh], kc, vc, biases, m_ref, l_ref, acc_ref)
            return carry
        return body

    _causal_sweep(q0, tq, sweep)
    lp = lam_ref[...]
    lam = (jnp.exp(jnp.sum(lp[0:1] * lp[1:2], axis=1, keepdims=True))
           - jnp.exp(jnp.sum(lp[2:3] * lp[3:4], axis=1, keepdims=True)) + lam_init)
    for r in range(2):
        o = _flash_out(2 * r, l_ref, acc_ref) - lam * _flash_out(2 * r + 1, l_ref, acc_ref)
        o_ref[:, r * LANES:(r + 1) * LANES] = _rms(o, subln_ref[...]) * (1.0 - lam_init)


def _diff_attention(aq, ak, av, lam_p, subln, lam_init, q_pos0, name):
    bn, tq_all, _ = aq.shape
    lk = ak.shape[1]
    tq = min(ROW_TILE, tq_all)
    tk = KV_CHUNK
    body = functools.partial(_diff_body, tq, tk, q_pos0, lam_init)
    return pl.pallas_call(
        body,
        grid=(bn, A_KV, tq_all // tq),
        in_specs=[
            pl.BlockSpec((4, HD), lambda b, g, i: (0, 0)),
            pl.BlockSpec((1, 2 * HD), lambda b, g, i: (0, 0)),
            pl.BlockSpec((None, tq, 256), lambda b, g, i: (b, i, g)),
            pl.BlockSpec((None, lk, LANES), lambda b, g, i: (b, 0, g)),
            pl.BlockSpec((None, lk, LANES), lambda b, g, i: (b, 0, g)),
        ],
        out_specs=pl.BlockSpec((None, tq, 256), lambda b, g, i: (b, i, g)),
        out_shape=jax.ShapeDtypeStruct((bn, tq_all, 512), F32),
        scratch_shapes=[pltpu.VMEM((4, tq, LANES), F32), pltpu.VMEM((4, tq, LANES), F32),
                        pltpu.VMEM((4, tq, LANES), F32)],
        compiler_params=_params("parallel", "parallel", "arbitrary"),
        name=name,
    )(lam_p, subln, aq, ak, av)


def _compress_one(x_ref, pe_ref, w1_ref, w2_ref, o_ref):
    n_half = CMP_STRIDE
    xs = [x_ref[pl.ds(p, LANES, stride=CMP_STRIDE), :] for p in range(n_half)]
    xa = jnp.concatenate([(x + pe_ref[p]).astype(BF16) for p, x in enumerate(xs)], axis=1)
    xb = jnp.concatenate([(x + pe_ref[n_half + p]).astype(BF16) for p, x in enumerate(xs)], axis=1)
    w1 = w1_ref[...].reshape(2, n_half * LANES, 2 * CMP_HIDDEN)
    h = _dot(xa, w1[0]) + pltpu.roll(_dot(xb, w1[1]), LANES - 1, 0)
    y = _dot(jax.nn.gelu(h).astype(BF16), w2_ref[...])
    row = lax.broadcasted_iota(jnp.int32, (LANES, 1), 0)
    o_ref[...] = jnp.where(row < LANES - 1, y, 0.0)


def _compress_body(xk_ref, xv_ref, pek_ref, pev_ref, w1k_ref, w1v_ref, w2k_ref, w2v_ref, ok_ref, ov_ref):
    _compress_one(xk_ref, pek_ref, w1k_ref, w2k_ref, ok_ref)
    _compress_one(xv_ref, pev_ref, w1v_ref, w2v_ref, ov_ref)


def _compress(xk, xv, cw, name):
    bn = xk.shape[0]
    n_tok = LANES * CMP_STRIDE
    x_spec = pl.BlockSpec((None, n_tok, LANES), lambda b: (b, 0, 0))
    pe_spec = pl.BlockSpec((CMP_BLOCK, 1, LANES), lambda b: (0, 0, 0))
    w1_spec = pl.BlockSpec((CMP_BLOCK, LANES, 2 * CMP_HIDDEN), lambda b: (0, 0, 0))
    w2_spec = pl.BlockSpec((2 * CMP_HIDDEN, LANES), lambda b: (0, 0))
    o_spec = pl.BlockSpec((None, LANES, LANES), lambda b: (b, 0, 0))
    return pl.pallas_call(
        _compress_body,
        grid=(bn,),
        in_specs=[x_spec, x_spec, pe_spec, pe_spec, w1_spec, w1_spec, w2_spec, w2_spec],
        out_specs=[o_spec, o_spec],
        out_shape=[jax.ShapeDtypeStruct((bn, LANES, LANES), F32)] * 2,
        compiler_params=_params("parallel"),
        name=name,
    )(xk, xv, cw["pek"], cw["pev"], cw["w1k"], cw["w1v"], cw["w2k"], cw["w2v"])


def _compress_weights(pe, w1, w2):
    out = {}
    for idx, tag in ((0, "k"), (1, "v")):
        w1p = w1[idx].reshape(CMP_BLOCK, HD, CMP_HIDDEN)
        z = jnp.zeros_like(w1p)
        w1b = jnp.concatenate([jnp.concatenate([w1p, z], axis=2), jnp.concatenate([z, w1p], axis=2)], axis=1)
        z2 = jnp.zeros_like(w2[idx])
        w2b = jnp.concatenate([jnp.concatenate([w2[idx], z2], axis=1), jnp.concatenate([z2, w2[idx]], axis=1)], axis=0)
        out["w1" + tag] = w1b.astype(BF16)
        out["w2" + tag] = w2b.astype(BF16)
        out["pe" + tag] = jnp.concatenate([pe[idx], pe[idx]], axis=1)[:, None, :]
    return out


def _rank_desc(vals, n, lane):
    rank = jnp.zeros(vals.shape, F32)
    for i in range(n):
        col = vals[:, i:i + 1]
        tie = jnp.where(lane > i, 1.0, 0.0)
        rank = rank + jnp.where(col > vals, 1.0, 0.0) + jnp.where(col == vals, tie, 0.0)
    return rank


def _nsa_body(tq, tk, q_pos0, ns, w_pos0, w_valid, lw,
              qc_ref, qr_ref, bg_ref, kcmp_ref, vcmp_ref, ks_ref, vs_ref, kw_ref, vw_ref, o_ref,
              m_ref, l_ref, acc_ref):
    g = pl.program_id(1)
    qi = pl.program_id(2)
    lane = lax.broadcasted_iota(jnp.int32, (1, LANES), 1)
    halfi = lane >> 6
    q0 = q_pos0 + qi * tq
    qpos = q0 + lax.broadcasted_iota(jnp.int32, (tq, 1), 0)
    rep = B_HEADS // B_KV

    def embed_all(ref, scale):
        out = []
        for pr in range(rep // 2):
            v = ref[:, pr * LANES:(pr + 1) * LANES] * scale
            swapped = pltpu.roll(v, HD, 1)
            for u in range(2):
                out.append(jnp.where(halfi == g, jnp.where(g == u, v, swapped), 0.0).astype(BF16))
        return out

    def pair_out(o_even, o_odd):
        aligned = jnp.where(g == 0, o_even, o_odd)
        other = jnp.where(g == 0, o_odd, o_even)
        return jnp.where(halfi == g, aligned, pltpu.roll(other, HD, 1))

    kcmp = kcmp_ref[...].astype(BF16)
    vcmp = vcmp_ref[...].astype(BF16)
    cmask = (lane * CMP_STRIDE + (CMP_BLOCK - 1)) <= qpos
    psum = jnp.zeros((tq, LANES), F32)
    o_cmp = []
    qsc = embed_all(qc_ref, HD ** -0.5)
    for r in range(rep):
        s = jnp.where(cmask, _dot_nt(qsc[r], kcmp), NEG)
        m = jnp.max(s, axis=1, keepdims=True)
        p = jnp.where(cmask, jnp.exp(s - m), 0.0)
        p = p / jnp.maximum(jnp.sum(p, axis=1, keepdims=True), 1e-30)
        psum = psum + p
        o_cmp.append(_dot(p.astype(BF16), vcmp))

    ci = lax.broadcasted_iota(jnp.int32, (LANES, LANES), 0)
    cj = lax.broadcasted_iota(jnp.int32, (LANES, LANES), 1)
    onehot = jnp.where((ci >> CMP_PER_SEL_SHIFT) == cj, 1.0, 0.0).astype(BF16)
    hi, mid, lo = _split3(psum)
    imp = _dot(hi, onehot) + _dot(mid, onehot) + _dot(lo, onehot)
    cur = qpos >> 6
    impm = jnp.where(lane == cur, jnp.inf, jnp.where(lane < cur, imp, -jnp.inf))
    rank = _rank_desc(impm, ns, lane)
    sel = jnp.where(rank < float(SEL_TOPK), jnp.where(lane <= cur, 1.0, 0.0), 0.0).astype(BF16)

    qsr = embed_all(qr_ref, QK_SCALE_LOG2)
    _flash_init(m_ref, l_ref, acc_ref)
    def sel_sweep(ck, diag):
        bpc = ck // SEL_BLOCK

        def body(j, carry):
            kc = _chunk(ks_ref, j, ck)
            vc = _chunk(vs_ref, j, ck)
            ei = lax.broadcasted_iota(jnp.int32, (LANES, ck), 0)
            el = lax.broadcasted_iota(jnp.int32, (LANES, ck), 1)
            expand = jnp.where(ei == j * bpc + (el >> 6), 1.0, 0.0).astype(BF16)
            bias = (_dot(sel, expand) - 1.0) * (-NEG)
            if diag:
                bias = _causal_bias(j, ck, qpos, bias)
            biases = _lane_blocks(bias)
            for r in range(rep):
                _flash_step(r, qsr[r], kc, vc, biases, m_ref, l_ref, acc_ref)
            return carry
        return body

    _causal_sweep(q0, tq, sel_sweep)
    o_slc = [_flash_out(r, l_ref, acc_ref) for r in range(rep)]

    _flash_init(m_ref, l_ref, acc_ref)
    w_lo = jnp.maximum(q0 - WINDOW - w_pos0, 0) // tk
    w_hi = jnp.minimum(q0 + tq - 1 - w_pos0, lw - 1) // tk + 1

    def win_body(j, carry):
        kc = _chunk(kw_ref, j, tk)
        vc = _chunk(vw_ref, j, tk)
        kidx = j * tk + lax.broadcasted_iota(jnp.int32, (1, tk), 1)
        dist = qpos - (kidx + w_pos0)
        inwin = jnp.where(dist >= 0, jnp.where(dist <= WINDOW, 0.0, NEG), NEG)
        biases = _lane_blocks(jnp.where(kidx < w_valid, inwin, NEG))
        for r in range(rep):
            _flash_step(r, qsr[r], kc, vc, biases, m_ref, l_ref, acc_ref)
        return carry

    lax.fori_loop(w_lo, w_hi, win_body, 0)

    sig = jax.nn.sigmoid(bg_ref[...])
    outs = []
    for r in range(rep):
        o_win = _flash_out(r, l_ref, acc_ref)
        base = (g * rep + r) * 3
        gates = [jnp.sum(jnp.where(lane == base + c, sig, 0.0), axis=1, keepdims=True) for c in range(3)]
        outs.append(gates[0] * o_cmp[r] + gates[1] * o_slc[r] + gates[2] * o_win)
    for pr in range(rep // 2):
        o_ref[:, pr * LANES:(pr + 1) * LANES] = pair_out(outs[2 * pr], outs[2 * pr + 1])


def _nsa_attention(bqc, bqr, bg, kcmp, vcmp, ks, vs, kw, vw, q_pos0, k_valid, w_pos0, w_valid, name):
    bn, tq_all, _ = bqc.shape
    lk = ks.shape[1]
    lw = kw.shape[1]
    tq = min(ROW_TILE, tq_all)
    tk = KV_CHUNK
    ns = -(-k_valid // SEL_BLOCK)
    body = functools.partial(_nsa_body, tq, tk, q_pos0, ns, w_pos0, w_valid, lw)
    q_spec = pl.BlockSpec((None, tq, 256), lambda b, g, i: (b, i, g))
    full = lambda rows: pl.BlockSpec((None, rows, LANES), lambda b, g, i: (b, 0, 0))
    return pl.pallas_call(
        body,
        grid=(bn, B_KV, tq_all // tq),
        in_specs=[q_spec, q_spec, pl.BlockSpec((None, tq, LANES), lambda b, g, i: (b, i, 0)),
                  full(LANES), full(LANES), full(lk), full(lk), full(lw), full(lw)],
        out_specs=pl.BlockSpec((None, tq, 256), lambda b, g, i: (b, i, g)),
        out_shape=jax.ShapeDtypeStruct((bn, tq_all, 512), F32),
        scratch_shapes=[pltpu.VMEM((4, tq, LANES), F32), pltpu.VMEM((4, tq, LANES), F32),
                        pltpu.VMEM((4, tq, LANES), F32)],
        compiler_params=_params("parallel", "parallel", "arbitrary"),
        name=name,
    )(bqc, bqr, bg, kcmp, vcmp, ks, vs, kw, vw)


def _moba_body(tq, tk, q_pos0, nb, q_ref, k_ref, v_ref, o_ref, km_ref, m_ref, l_ref, acc_ref):
    g = pl.program_id(1)
    qi = pl.program_id(2)
    gh = g % 2
    lane = lax.broadcasted_iota(jnp.int32, (1, LANES), 1)
    halfi = lane >> 6
    q0 = q_pos0 + qi * tq
    qpos = q0 + lax.broadcasted_iota(jnp.int32, (tq, 1), 0)
    cur = qpos >> 8
    rep = C_HEADS // C_KV

    @pl.when(qi == 0)
    def _():
        km_ref[...] = jnp.zeros(km_ref.shape, F32)
        km_ref[0:nb, :] = jnp.sum(k_ref[...].reshape(nb, MOBA_BLOCK, LANES), axis=1) * (1.0 / MOBA_BLOCK)

    km_hi, km_mid, _ = _split3(km_ref[...])

    qv = q_ref[...]
    qs = []
    sels = []
    swapped = pltpu.roll(qv, HD, 1)
    for r in range(rep):
        qe = jnp.where(halfi == gh, jnp.where(gh == r, qv, swapped), 0.0)
        q_hi, q_mid, _ = _split3(qe)
        s_blk = _dot_nt(q_hi, km_hi) + _dot_nt(q_hi, km_mid) + _dot_nt(q_mid, km_hi)
        sm = jnp.where(lane < cur, s_blk, -jnp.inf)
        rank = _rank_desc(sm, nb, lane)
        past = jnp.where(rank < float(MOBA_TOPK), jnp.where(lane < cur, 1.0, 0.0), 0.0)
        sels.append(jnp.where(lane == cur, 1.0, past))
        qs.append((qe * QK_SCALE_LOG2).astype(BF16))

    _flash_init(m_ref, l_ref, acc_ref)
    def sweep(ck, diag):
        bpc = ck // MOBA_BLOCK
        lpb = MOBA_BLOCK // LANES

        def body(j, carry):
            kc = _chunk(k_ref, j, ck)
            vc = _chunk(v_ref, j, ck)
            diag_biases = _lane_blocks(_causal_bias(j, ck, qpos)) if diag else None
            for r in range(rep):
                if diag:
                    biases = diag_biases
                else:
                    biases = []
                    for u in range(bpc):
                        col = jnp.sum(jnp.where(lane == j * bpc + u, sels[r], 0.0), axis=1, keepdims=True)
                        biases += [(col - 1.0) * (-NEG)] * lpb
                _flash_step(r, qs[r], kc, vc, biases, m_ref, l_ref, acc_ref)
            return carry
        return body

    _causal_sweep(q0, tq, sweep)
    o0 = _flash_out(0, l_ref, acc_ref)
    o1 = _flash_out(1, l_ref, acc_ref)
    aligned = jnp.where(gh == 0, o0, o1)
    other = jnp.where(gh == 0, o1, o0)
    o_ref[...] = jnp.where(halfi == gh, aligned, pltpu.roll(other, HD, 1))


def _moba_attention(cq, ck, cv, q_pos0, name):
    bn, tq_all, _ = cq.shape
    lk = ck.shape[1]
    tq = min(ROW_TILE, tq_all)
    tk = MOBA_BLOCK
    nb = lk // MOBA_BLOCK
    assert tq == MOBA_BLOCK == KV_CHUNK and q_pos0 % MOBA_BLOCK == 0
    body = functools.partial(_moba_body, tq, tk, q_pos0, nb)
    return pl.pallas_call(
        body,
        grid=(bn, C_KV, tq_all // tq),
        in_specs=[
            pl.BlockSpec((None, tq, LANES), lambda b, g, i: (b, i, g)),
            pl.BlockSpec((None, lk, LANES), lambda b, g, i: (b, 0, g // 2)),
            pl.BlockSpec((None, lk, LANES), lambda b, g, i: (b, 0, g // 2)),
        ],
        out_specs=pl.BlockSpec((None, tq, LANES), lambda b, g, i: (b, i, g)),
        out_shape=jax.ShapeDtypeStruct((bn, tq_all, 512), F32),
        scratch_shapes=[pltpu.VMEM((LANES, LANES), F32),
                        pltpu.VMEM((2, tq, LANES), F32), pltpu.VMEM((2, tq, LANES), F32),
                        pltpu.VMEM((2, tq, LANES), F32)],
        compiler_params=_params("parallel", "parallel", "arbitrary"),
        name=name,
    )(cq, ck, cv)


def _merge_body(x_ref, gate_ref, oa_ref, ob_ref, oc_ref, wb_ref, wo_ref, y_ref):
    y = jnp.zeros((x_ref.shape[0], D_MODEL), F32)
    for c, o_ref in enumerate((oa_ref, ob_ref, oc_ref)):
        br = _dot(o_ref[...].astype(BF16), wb_ref[c])
        y = y + jax.nn.sigmoid(gate_ref[:, c * D_MODEL:(c + 1) * D_MODEL]) * br
    y_ref[...] = x_ref[...] + _dot(y.astype(BF16), wo_ref[...])


def _merge(x, gate, o_a, o_b, o_c, wb, wo, name):
    n = x.shape[0]
    tm = min(ROW_TILE, n)
    row = lambda w: pl.BlockSpec((tm, w), lambda i: (i, 0))
    return pl.pallas_call(
        _merge_body,
        grid=(n // tm,),
        in_specs=[row(D_MODEL), row(3 * D_MODEL), row(512), row(512), row(512),
                  pl.BlockSpec((3, 512, D_MODEL), lambda i: (0, 0, 0)),
                  pl.BlockSpec((D_MODEL, D_MODEL), lambda i: (0, 0))],
        out_specs=row(D_MODEL),
        out_shape=jax.ShapeDtypeStruct((n, D_MODEL), F32),
        compiler_params=_params("parallel"),
        name=name,
    )(x, gate, o_a, o_b, o_c, wb, wo)


def _bitonic_desc(vs):
    vs = list(vs)
    n = len(vs)
    k = 2
    while k <= n:
        j = k // 2
        while j >= 1:
            for i in range(n):
                partner = i ^ j
                if partner > i:
                    hi = jnp.maximum(vs[i], vs[partner])
                    lo = jnp.minimum(vs[i], vs[partner])
                    vs[i], vs[partner] = (hi, lo) if (i & k) == 0 else (lo, hi)
            j //= 2
        k *= 2
    return vs


def _pop_heads(cols, n_out):
    cols = list(cols)
    for k in range(n_out):
        m = jnp.max(cols[0], axis=0, keepdims=True)
        eq = cols[0] == m
        yield m, eq
        depth = min(len(cols), n_out - k)
        for d in range(depth - 1):
            cols[d] = jnp.where(eq, cols[d + 1], cols[d])
        cols[depth - 1] = jnp.where(eq, -jnp.inf, cols[depth - 1])


def _peer_stats_body(tm, x_ref, nw_ref, wq_ref, sk_ref, st_ref, stat_ref, top_ref, nxt_ref):
    h = _rms(x_ref[...], nw_ref[...]).astype(BF16)
    q = _dot(h, wq_ref[...]).astype(BF16)
    n_hc = 2 * PEER_HEADS
    for hc in range(n_hc):
        st_ref[hc] = _dot_nt(sk_ref[hc], q[:, hc * LANES:(hc + 1) * LANES])

    sub = 8
    n_out = PEER_TOPK + 1

    def top_body(hc, carry):
        for hf in range(tm // LANES):
            s = st_ref[hc, :, hf * LANES:(hf + 1) * LANES]
            cols = _bitonic_desc([s[d * sub:(d + 1) * sub] for d in range(PEER_NKEYS // sub)])
            vals = [m for m, _ in _pop_heads(cols, n_out)]
            top_ref[hc, :, hf * LANES:(hf + 1) * LANES] = jnp.concatenate(vals[:PEER_TOPK], axis=0)
            nxt_ref[hc, :, hf * LANES:(hf + 1) * LANES] = vals[PEER_TOPK]
        return carry

    lax.fori_loop(0, n_hc, top_body, 0)

    row8 = lax.broadcasted_iota(jnp.int32, (sub, LANES), 0)
    fill = jnp.full((sub, LANES), -jnp.inf, F32)

    def head_body(hd, carry):
        for hf in range(tm // LANES):
            sl = slice(hf * LANES, (hf + 1) * LANES)
            a = top_ref[2 * hd, :, sl]
            b = top_ref[2 * hd + 1, :, sl]
            cands = [a[0:1] + b[0:sub], a[0:1] + b[sub:2 * sub]]
            for i in range(1, sub):
                cands.append(jnp.where(row8 < PEER_TOPK // (i + 1), a[i:i + 1] + b[0:sub], -jnp.inf))
            cands.append(a[sub:2 * sub] + b[0:1])
            mx = a[0:1] + b[0:1]
            cum = jnp.zeros((1, LANES), F32)
            tau = mx
            nxt = mx
            cols = _bitonic_desc(cands + [fill] * (16 - len(cands)))
            for m, eq in _pop_heads(cols, n_out):
                tau = jnp.where(cum < float(PEER_TOPK), m, tau)
                nxt = jnp.where(cum < float(n_out), m, nxt)
                cum = cum + jnp.sum(jnp.where(eq, 1.0, 0.0), axis=0, keepdims=True)
            nxt = jnp.maximum(nxt, jnp.maximum(a[0:1] + nxt_ref[2 * hd + 1, :, sl], nxt_ref[2 * hd, :, sl] + b[0:1]))
            z = None
            for c in cands:
                t = jnp.sum(jnp.where(c >= tau, jnp.exp(c - mx), 0.0), axis=0, keepdims=True)
                z = t if z is None else z + t
            stat_ref[hd, :, sl] = 0.5 * (tau + nxt)
            stat_ref[PEER_HEADS + hd, :, sl] = a[0:1]
            stat_ref[2 * PEER_HEADS + hd, :, sl] = b[0:1]
            stat_ref[3 * PEER_HEADS + hd, :, sl] = 1.0 / z
        return carry

    lax.fori_loop(0, PEER_HEADS, head_body, 0)


def _peer_stats(x, nw, wq, sk, name):
    n = x.shape[0]
    tm = min(ROW_TILE, n)
    n_hc = 2 * PEER_HEADS
    return pl.pallas_call(
        functools.partial(_peer_stats_body, tm),
        grid=(n // tm,),
        in_specs=[pl.BlockSpec((tm, D_MODEL), lambda i: (i, 0)),
                  pl.BlockSpec((1, D_MODEL), lambda i: (0, 0)),
                  pl.BlockSpec((D_MODEL, n_hc * LANES), lambda i: (0, 0)),
                  pl.BlockSpec((n_hc, LANES, LANES), lambda i: (0, 0, 0))],
        out_specs=[pl.BlockSpec((n_hc, LANES, tm), lambda i: (0, 0, i)),
                   pl.BlockSpec((4 * PEER_HEADS, 1, tm), lambda i: (0, 0, i))],
        out_shape=[jax.ShapeDtypeStruct((n_hc, LANES, n), F32),
                   jax.ShapeDtypeStruct((4 * PEER_HEADS, 1, n), F32)],
        scratch_shapes=[pltpu.VMEM((n_hc, PEER_TOPK, tm), F32), pltpu.VMEM((n_hc, 1, tm), F32)],
        compiler_params=_params("parallel"),
        name=name,
    )(x, nw, wq, sk)


def _peer_dense_body(tm, te, last_layer, x_ref, nw_ref, fw_ref, st_ref, stat_ref, u_ref, vt_ref, o_ref,
                     h_ref, thr_ref, w1_ref, s2c_ref, e2_ref, acc_ref):
    e = pl.program_id(1)

    @pl.when(e == 0)
    def _():
        h_ref[...] = _rms(x_ref[...], nw_ref[...]).astype(BF16)
        acc_ref[...] = jnp.zeros(acc_ref.shape, F32)
        for hd in range(PEER_HEADS):
            tau = stat_ref[hd]
            m1 = stat_ref[PEER_HEADS + hd]
            m2 = stat_ref[2 * PEER_HEADS + hd]
            inv_z = stat_ref[3 * PEER_HEADS + hd]
            thr_ref[hd] = m1 - st_ref[2 * hd]
            w1_ref[hd] = jnp.exp(st_ref[2 * hd] - m1) * inv_z
            s2c_ref[hd] = (st_ref[2 * hd + 1] - (tau - m1)).astype(BF16)
            e2_ref[hd] = jnp.exp(st_ref[2 * hd + 1] - m2).astype(BF16)

    n_i = te // PEER_NKEYS
    zero = jnp.zeros((), BF16)
    h = h_ref[...]
    total = None
    for k in range(te // EXPERT_SUB):
        rows = slice(k * EXPERT_SUB, (k + 1) * EXPERT_SUB)
        act = jax.nn.gelu(_dot_nt(u_ref[rows, :], h)).astype(BF16)
        parts = []
        for ii in range(EXPERT_SUB // PEER_NKEYS):
            i = e * n_i + k * (EXPERT_SUB // PEER_NKEYS) + ii
            gmat = jnp.zeros((PEER_NKEYS, tm), BF16)
            for hd in range(PEER_HEADS):
                thr = thr_ref[hd, pl.ds(i, 1), :].astype(BF16)
                w1 = w1_ref[hd, pl.ds(i, 1), :].astype(BF16)
                gmat = gmat + jnp.where(s2c_ref[hd] > thr, e2_ref[hd] * w1, zero)
            parts.append(gmat * act[ii * PEER_NKEYS:(ii + 1) * PEER_NKEYS])
        t = _dot(vt_ref[:, rows], jnp.concatenate(parts, axis=0))
        total = t if total is None else total + t
    acc_ref[...] += total

    @pl.when(e == pl.num_programs(1) - 1)
    def _():
        y = x_ref[...] + acc_ref[...].T
        o_ref[...] = _rms(y, fw_ref[...]) if last_layer else y


def _peer_dense(x, nw, final_w, last_layer, st, stat, u_bf, vt_bf, name):
    n = x.shape[0]
    tm = min(ROW_TILE, n)
    te = EXPERT_TILE
    n_exp = u_bf.shape[0]
    n_hc = 2 * PEER_HEADS
    return pl.pallas_call(
        functools.partial(_peer_dense_body, tm, te, last_layer),
        grid=(n // tm, n_exp // te),
        in_specs=[pl.BlockSpec((tm, D_MODEL), lambda i, e: (i, 0)),
                  pl.BlockSpec((1, D_MODEL), lambda i, e: (0, 0)),
                  pl.BlockSpec((1, D_MODEL), lambda i, e: (0, 0)),
                  pl.BlockSpec((n_hc, LANES, tm), lambda i, e: (0, 0, i)),
                  pl.BlockSpec((4 * PEER_HEADS, 1, tm), lambda i, e: (0, 0, i)),
                  pl.BlockSpec((te, D_MODEL), lambda i, e: (e, 0)),
                  pl.BlockSpec((D_MODEL, te), lambda i, e: (0, e))],
        out_specs=pl.BlockSpec((tm, D_MODEL), lambda i, e: (i, 0)),
        out_shape=jax.ShapeDtypeStruct((n, D_MODEL), F32),
        scratch_shapes=[pltpu.VMEM((tm, D_MODEL), BF16),
                        pltpu.VMEM((PEER_HEADS, PEER_NKEYS, tm), F32),
                        pltpu.VMEM((PEER_HEADS, PEER_NKEYS, tm), F32),
                        pltpu.VMEM((PEER_HEADS, PEER_NKEYS, tm), BF16),
                        pltpu.VMEM((PEER_HEADS, PEER_NKEYS, tm), BF16),
                        pltpu.VMEM((D_MODEL, tm), F32)],
        compiler_params=_params("parallel", "arbitrary"),
        name=name,
    )(x, nw, final_w, st, stat, u_bf, vt_bf)


SROWS = 8
SEQS = 2


def _page_specs(n_pages, rows, layer):
    def spec(s, p):
        return pl.BlockSpec((None, None, rows, LANES), lambda b, pt: (layer, pt[b * SEQS + s, p], 0, 0))
    return [spec(s, p) for s in range(SEQS) for p in range(n_pages)]


def _row_spec(width):
    return pl.BlockSpec((SEQS, SROWS, width), lambda b, pt: (b, 0, 0))


def _seq_spec(rows, width):
    return pl.BlockSpec((SEQS, rows, width), lambda b, pt: (b, 0, 0))


def _per_sequence(body, n_const, n_seq, n_pages):
    def wrapped(pt_ref, *refs):
        consts = refs[:n_const]
        seqs = refs[n_const:n_const + n_seq]
        pages = refs[n_const + n_seq:n_const + n_seq + 2 * SEQS * n_pages]
        tail = refs[n_const + n_seq + 2 * SEQS * n_pages:]
        kp, vp = pages[:SEQS * n_pages], pages[SEQS * n_pages:]
        for s in range(SEQS):
            body(pt_ref, *consts, *[r.at[s] for r in seqs], *kp[s * n_pages:(s + 1) * n_pages],
                 *vp[s * n_pages:(s + 1) * n_pages], *[r.at[s] for r in tail])
    return wrapped


def _dup(x):
    return jnp.concatenate([x, x], axis=0)


def _pad_rows(x):
    return jnp.concatenate([x, jnp.zeros((PAGE_SIZE - x.shape[0], x.shape[1]), F32)], axis=0)


def _new_mask(m_rows, n_new):
    lane = lax.broadcasted_iota(jnp.int32, (1, LANES), 1)
    trow = lax.broadcasted_iota(jnp.int32, (m_rows, 1), 0) & (SROWS - 1)
    return jnp.where(lane < n_new, jnp.where(lane <= trow, 1.0, 0.0), 0.0) > 0.5


def _softmax_pieces(pieces):
    mx = None
    for s, mk in pieces:
        sm = s if mk is None else jnp.where(mk, s, NEG)
        mx = sm if mx is None else jnp.maximum(mx, sm)
    m = jnp.max(mx, axis=1, keepdims=True)
    ps = []
    tot = None
    for s, mk in pieces:
        p = jnp.exp(s - m)
        if mk is not None:
            p = jnp.where(mk, p, 0.0)
        tot = p if tot is None else tot + p
        ps.append(p.astype(BF16))
    return ps, jnp.sum(tot, axis=1, keepdims=True)


def _scores_kt(q, kts):
    return [_dot(q, _dup(kt.astype(BF16))) for kt in kts]


def _values_kt(ps, vts):
    o = None
    for p, vt in zip(ps, vts):
        t = _dot_nt(p, _dup(vt.astype(BF16)))
        o = t if o is None else o + t
    return o


def _diff_s_body(n_pages, n_new, lam_init, pt_ref, lam_ref, subln_ref, q_ref, kn_ref, vn_ref, *rest):
    k_refs = rest[:n_pages]
    v_refs = rest[n_pages:2 * n_pages]
    o_ref = rest[2 * n_pages]
    lane = lax.broadcasted_iota(jnp.int32, (1, LANES), 1)
    upper = lane >= HD
    new_mask = _new_mask(4 * SROWS, n_new)
    lp = lam_ref[...]
    lam = (jnp.exp(jnp.sum(lp[0:1] * lp[1:2], axis=1, keepdims=True))
           - jnp.exp(jnp.sum(lp[2:3] * lp[3:4], axis=1, keepdims=True)) + lam_init)
    for g in range(A_KV):
        parts = []
        for r in range(A_HEADS // A_KV):
            qb = q_ref[:, (2 * g + r) * LANES:(2 * g + r + 1) * LANES] * (HD ** -0.5)
            parts += [jnp.where(upper, 0.0, qb), jnp.where(upper, qb, 0.0)]
        q = jnp.concatenate(parts, axis=0).astype(BF16)
        pieces = [(_dot_nt(q, k_refs[p][pl.ds(g, PAGE_SIZE, stride=A_KV), :].astype(BF16)), None)
                  for p in range(n_pages)]
        knew = _pad_rows(kn_ref[:, g * LANES:(g + 1) * LANES]).astype(BF16)
        vnew = _pad_rows(vn_ref[:, g * LANES:(g + 1) * LANES]).astype(BF16)
        pieces.append((_dot_nt(q, knew), new_mask))
        ps, tot = _softmax_pieces(pieces)
        o = _dot(ps[n_pages], vnew)
        for p in range(n_pages):
            o = o + _dot(ps[p], v_refs[p][pl.ds(g, PAGE_SIZE, stride=A_KV), :].astype(BF16))
        o = o / tot
        for r in range(A_HEADS // A_KV):
            d = o[2 * r * SROWS:(2 * r + 1) * SROWS] - lam * o[(2 * r + 1) * SROWS:(2 * r + 2) * SROWS]
            o_ref[:, (2 * g + r) * LANES:(2 * g + r + 1) * LANES] = _rms(d, subln_ref[...]) * (1.0 - lam_init)


def _diff_sample(page_table, layer, n_new, aq, ak_new, av_new, cache_k, cache_v, lam_p, subln, lam_init, name):
    db, n_pages = page_table.shape
    const = lambda shape: pl.BlockSpec(shape, lambda b, pt: (0,) * len(shape))
    in_specs = ([const((4, HD)), const((1, 2 * HD)), _row_spec(512), _row_spec(256), _row_spec(256)]
                + _page_specs(n_pages, PAGE_SIZE * A_KV, layer) + _page_specs(n_pages, PAGE_SIZE * A_KV, layer))
    grid_spec = pltpu.PrefetchScalarGridSpec(num_scalar_prefetch=1, grid=(db // SEQS,), in_specs=in_specs,
                                             out_specs=_row_spec(512))
    return pl.pallas_call(
        _per_sequence(functools.partial(_diff_s_body, n_pages, n_new, lam_init), 2, 3, n_pages),
        grid_spec=grid_spec,
        out_shape=jax.ShapeDtypeStruct((db, SROWS, 512), F32),
        compiler_params=_params("parallel"),
        name=name,
    )(page_table, lam_p, subln, aq, ak_new, av_new, *([cache_k] * (SEQS * n_pages)), *([cache_v] * (SEQS * n_pages)))


def _stack_heads(q_ref, n_heads, width, scale):
    return (jnp.concatenate([q_ref[:, h * width:(h + 1) * width] for h in range(n_heads)], axis=0)
            * scale).astype(BF16)


def _place_heads(o, place_ref, n_heads):
    out = None
    for h in range(n_heads):
        t = _dot(o[h * SROWS:(h + 1) * SROWS].astype(BF16), place_ref[h])
        out = t if out is None else out + t
    return out


def _place_matrices(n_heads, n_groups):
    rep = n_heads // n_groups
    p = np.zeros((n_heads, n_groups * HD, n_heads * HD), np.float32)
    for h in range(n_heads):
        for d in range(HD):
            p[h, (h // rep) * HD + d, h * HD + d] = 1.0
    return jnp.asarray(p, BF16)


def _moba_s_body(n_pages, n_new, pt_ref, place_ref, q_ref, kn_ref, vn_ref, *rest):
    k_refs = rest[:n_pages]
    v_refs = rest[n_pages:2 * n_pages]
    o_ref = rest[2 * n_pages]
    new_mask = _new_mask(C_HEADS * SROWS, n_new)
    knt = _pad_rows(kn_ref[...]).T.astype(BF16)
    vnt = _pad_rows(vn_ref[...]).T.astype(BF16)
    ppb = MOBA_BLOCK // PAGE_SIZE
    n_blocks = n_pages // ppb
    q = _stack_heads(q_ref, C_HEADS, C_KV * HD, HD ** -0.5)
    ss = [_dot(q, k_refs[p][...].astype(BF16)) for p in range(n_pages)] + [_dot(q, knt)]
    bs = []
    for j in range(n_blocks):
        acc = ss[j * ppb]
        for u in range(1, ppb):
            acc = acc + ss[j * ppb + u]
        bs.append(jnp.sum(acc, axis=1, keepdims=True))
    pieces = []
    for j in range(n_blocks):
        rank = jnp.zeros(bs[j].shape, F32)
        for i in range(n_blocks):
            if i < j:
                rank = rank + jnp.where(bs[i] >= bs[j], 1.0, 0.0)
            elif i > j:
                rank = rank + jnp.where(bs[i] > bs[j], 1.0, 0.0)
        keep = rank < float(MOBA_TOPK)
        for u in range(ppb):
            pieces.append((ss[j * ppb + u], keep))
    pieces.append((ss[n_pages], new_mask))
    ps, tot = _softmax_pieces(pieces)
    o = _dot_nt(ps[n_pages], vnt)
    for p in range(n_pages):
        o = o + _dot_nt(ps[p], v_refs[p][...].astype(BF16))
    o_ref[...] = _place_heads(o / tot, place_ref, C_HEADS)


def _moba_sample(page_table, layer, n_new, cq, ck_new, cv_new, cache_k, cache_v, name):
    db, n_pages = page_table.shape
    place = _place_matrices(C_HEADS, C_KV)
    in_specs = ([pl.BlockSpec(place.shape, lambda b, pt: (0, 0, 0)),
                 _row_spec(C_HEADS * C_KV * HD), _row_spec(256), _row_spec(256)]
                + _page_specs(n_pages, C_KV * HD, layer) + _page_specs(n_pages, C_KV * HD, layer))
    grid_spec = pltpu.PrefetchScalarGridSpec(num_scalar_prefetch=1, grid=(db // SEQS,), in_specs=in_specs,
                                             out_specs=_row_spec(512))
    return pl.pallas_call(
        _per_sequence(functools.partial(_moba_s_body, n_pages, n_new), 1, 3, n_pages),
        grid_spec=grid_spec,
        out_shape=jax.ShapeDtypeStruct((db, SROWS, 512), F32),
        compiler_params=_params("parallel"),
        name=name,
    )(page_table, place, cq, ck_new, cv_new,*([cache_k] * (SEQS * n_pages)), *([cache_v] * (SEQS * n_pages)))


def _compress_s_body(n_pages, pt_ref, pek_ref, pev_ref, w1k_ref, w1v_ref, w2k_ref, w2v_ref, *rest):
    k_refs = rest[:n_pages]
    v_refs = rest[n_pages:2 * n_pages]
    ok_ref, ov_ref, xk_ref, xv_ref = rest[2 * n_pages:]
    for p in range(n_pages):
        xk_ref[p * PAGE_SIZE:(p + 1) * PAGE_SIZE, :] = k_refs[p][...].T
        xv_ref[p * PAGE_SIZE:(p + 1) * PAGE_SIZE, :] = v_refs[p][...].T
    _compress_one(xk_ref, pek_ref, w1k_ref, w2k_ref, ok_ref)
    _compress_one(xv_ref, pev_ref, w1v_ref, w2v_ref, ov_ref)


def _compress_sample(page_table, layer, cache_k, cache_v, cw, name):
    db, n_pages = page_table.shape
    const = lambda shape: pl.BlockSpec(shape, lambda b, pt: (0,) * len(shape))
    in_specs = ([const((CMP_BLOCK, 1, LANES))] * 2 + [const((CMP_BLOCK, LANES, 2 * CMP_HIDDEN))] * 2
                + [const((2 * CMP_HIDDEN, LANES))] * 2
                + _page_specs(n_pages, B_KV * HD, layer) + _page_specs(n_pages, B_KV * HD, layer))
    o_spec = _seq_spec(LANES, LANES)
    grid_spec = pltpu.PrefetchScalarGridSpec(
        num_scalar_prefetch=1, grid=(db // SEQS,), in_specs=in_specs, out_specs=[o_spec, o_spec],
        scratch_shapes=[pltpu.VMEM((SEQS, n_pages * PAGE_SIZE, LANES), F32)] * 2)
    return pl.pallas_call(
        _per_sequence(functools.partial(_compress_s_body, n_pages), 6, 0, n_pages),
        grid_spec=grid_spec,
        out_shape=[jax.ShapeDtypeStruct((db, LANES, LANES), F32)] * 2,
        compiler_params=_params("parallel"),
        name=name,
    )(page_table, cw["pek"], cw["pev"], cw["w1k"], cw["w1v"], cw["w2k"], cw["w2v"],
      *([cache_k] * (SEQS * n_pages)), *([cache_v] * (SEQS * n_pages)))


def _nsa_s_body(n_pages, n_new, q_pos0, n_win, pt_ref, place_ref, qc_ref, qr_ref, bg_ref, kcmp_ref, vcmp_ref,
                ksn_ref, vsn_ref, kwn_ref, vwn_ref, wk_ref, wv_ref, *rest):
    k_refs = rest[:n_pages]
    v_refs = rest[n_pages:2 * n_pages]
    o_ref = rest[2 * n_pages]
    lane = lax.broadcasted_iota(jnp.int32, (1, LANES), 1)
    halfi = lane >> 6
    rep = B_HEADS // B_KV
    m_rows = B_HEADS * SROWS
    new_mask = _new_mask(m_rows, n_new)
    trow8 = lax.broadcasted_iota(jnp.int32, (SROWS, 1), 0)
    trow = lax.broadcasted_iota(jnp.int32, (m_rows, 1), 0) & (SROWS - 1)
    qpos8 = q_pos0 + trow8
    ns = -(-(q_pos0 + n_new) // SEL_BLOCK)
    bpp = PAGE_SIZE // SEL_BLOCK
    transposed = lambda ref: _pad_rows(ref[...]).T.astype(BF16)
    ksnt, vsnt, kwnt, vwnt = transposed(ksn_ref), transposed(vsn_ref), transposed(kwn_ref), transposed(vwn_ref)
    kcmp = kcmp_ref[...].astype(BF16)
    vcmp = vcmp_ref[...].astype(BF16)
    ci = lax.broadcasted_iota(jnp.int32, (LANES, LANES), 0)
    cj = lax.broadcasted_iota(jnp.int32, (LANES, LANES), 1)
    onehot = jnp.where((ci >> CMP_PER_SEL_SHIFT) == cj, 1.0, 0.0).astype(BF16)
    sig = jax.nn.sigmoid(bg_ref[...])
    per_group = lambda xs: jnp.concatenate([x for x in xs for _ in range(rep)], axis=0)
    qc = _stack_heads(qc_ref, B_HEADS, B_KV * HD, HD ** -0.5)
    qr = _stack_heads(qr_ref, B_HEADS, B_KV * HD, HD ** -0.5)

    cm8 = jnp.where((lane * CMP_STRIDE + (CMP_BLOCK - 1)) <= qpos8, 1.0, 0.0)
    cmask = jnp.concatenate([cm8] * B_HEADS, axis=0) > 0.5
    s = jnp.where(cmask, _dot_nt(qc, kcmp), NEG)
    m = jnp.max(s, axis=1, keepdims=True)
    p = jnp.where(cmask, jnp.exp(s - m), 0.0)
    p = p / jnp.maximum(jnp.sum(p, axis=1, keepdims=True), 1e-30)
    o_cmp = _dot(p.astype(BF16), vcmp)
    cur = qpos8 >> 6
    sels = []
    for g in range(B_KV):
        psum = p[g * rep * SROWS:(g * rep + 1) * SROWS]
        for r in range(1, rep):
            psum = psum + p[(g * rep + r) * SROWS:(g * rep + r + 1) * SROWS]
        hi, mid, lo = _split3(psum)
        imp = _dot(hi, onehot) + _dot(mid, onehot) + _dot(lo, onehot)
        impm = jnp.where(lane == cur, jnp.inf, jnp.where(lane < cur, imp, -jnp.inf))
        rank = _rank_desc(impm, ns, lane)
        sels.append(jnp.where(rank < float(SEL_TOPK), jnp.where(lane <= cur, 1.0, 0.0), 0.0))

    ss = [_dot(qr, k_refs[pg][...].astype(BF16)) for pg in range(n_pages)] + [_dot(qr, ksnt)]
    pieces = []
    for pg in range(n_pages):
        mks = []
        for sel in sels:
            mk = sel[:, pg * bpp:pg * bpp + 1]
            for u in range(1, bpp):
                mk = jnp.where(halfi >= u, sel[:, pg * bpp + u:pg * bpp + u + 1], mk)
            mks.append(mk)
        pieces.append((ss[pg], per_group(mks) > 0.5))
    pieces.append((ss[n_pages], new_mask))
    ps, tot = _softmax_pieces(pieces)
    o_slc = _dot_nt(ps[n_pages], vsnt)
    for pg in range(n_pages):
        o_slc = o_slc + _dot_nt(ps[pg], v_refs[pg][...].astype(BF16))
    o_slc = o_slc / tot

    wcols = [slice(u * LANES, (u + 1) * LANES) for u in range(n_win // LANES)]
    ss = [_dot(qr, wk_ref[:, c].astype(BF16)) for c in wcols] + [_dot(qr, kwnt)]
    pieces = []
    for u in range(len(wcols)):
        dist = (n_win - u * LANES) + trow - lane
        pieces.append((ss[u], dist <= WINDOW))
    pieces.append((ss[len(wcols)], new_mask))
    ps, tot = _softmax_pieces(pieces)
    o_win = _dot_nt(ps[len(wcols)], vwnt)
    for u, c in enumerate(wcols):
        o_win = o_win + _dot_nt(ps[u], wv_ref[:, c].astype(BF16))
    o_win = o_win / tot

    gate = lambda c: jnp.concatenate([sig[:, h * 3 + c:h * 3 + c + 1] for h in range(B_HEADS)], axis=0)
    o = gate(0) * o_cmp + gate(1) * o_slc + gate(2) * o_win
    o_ref[...] = _place_heads(o, place_ref, B_HEADS)


def _nsa_sample(page_table, layer, n_new, q_pos0, bqc, bqr, bg, kcmp, vcmp, ks_new, vs_new, kw_new, vw_new,
                state_k, state_v, cache_k, cache_v, name):
    db, n_pages = page_table.shape
    n_win = state_k.shape[-1]
    place = _place_matrices(B_HEADS, B_KV)
    w_spec = pl.BlockSpec((None, SEQS, B_KV * HD, n_win), lambda b, pt: (layer, b, 0, 0))
    c_spec = _seq_spec(LANES, LANES)
    q_width = B_HEADS * B_KV * HD
    in_specs = ([pl.BlockSpec(place.shape, lambda b, pt: (0, 0, 0)),
                 _row_spec(q_width), _row_spec(q_width), _row_spec(LANES), c_spec, c_spec]
                + [_row_spec(LANES)] * 4 + [w_spec, w_spec]
                + _page_specs(n_pages, B_KV * HD, layer) + _page_specs(n_pages, B_KV * HD, layer))
    grid_spec = pltpu.PrefetchScalarGridSpec(num_scalar_prefetch=1, grid=(db // SEQS,), in_specs=in_specs,
                                             out_specs=_row_spec(512))
    return pl.pallas_call(
        _per_sequence(functools.partial(_nsa_s_body, n_pages, n_new, q_pos0, n_win), 1, 11, n_pages),
        grid_spec=grid_spec,
        out_shape=jax.ShapeDtypeStruct((db, SROWS, 512), F32),
        compiler_params=_params("parallel"),
        name=name,
    )(page_table, place, bqc, bqr, bg, kcmp, vcmp, ks_new, vs_new, kw_new, vw_new, state_k, state_v,
      *([cache_k] * (SEQS * n_pages)), *([cache_v] * (SEQS * n_pages)))


def _layer_weights(l, norm_mix, norm_ffn, norm_final, w_in, a_lambda, a_subln, b_cmp_pe, b_cmp_w1, b_cmp_w2,
                   w_branch, w_out, peer_wq, peer_subkeys, peer_u, peer_v):
    w = w_in[l]
    bg_w = jnp.pad(w[:, 2304:2328], ((0, 0), (0, LANES - 3 * B_HEADS)))
    return {
        "norm_mix": norm_mix[l][None, :],
        "norm_ffn": norm_ffn[l][None, :],
        "norm_final": norm_final[None, :],
        "last_layer": l == w_in.shape[0] - 1,
        "w_a": w[:, 0:1024].astype(BF16),
        "w_b": jnp.concatenate([w[:, 1024:2304], bg_w], axis=1).astype(BF16),
        "w_c": w[:, 2328:3352].astype(BF16),
        "w_b_s": jnp.concatenate([_spread_heads(w[:, 1024:1536], B_HEADS, B_KV), w[:, 1536:2304], bg_w],
                                 axis=1).astype(BF16),
        "w_c_s": jnp.concatenate([_spread_heads(w[:, 2328:2840], C_HEADS, C_KV), w[:, 2840:3352]],
                                 axis=1).astype(BF16),
        "w_g": w[:, 3352:6424].astype(BF16),
        "lam": a_lambda[l],
        "subln": a_subln[l][None, :],
        "cmp": _compress_weights(b_cmp_pe[l], b_cmp_w1[l], b_cmp_w2[l]),
        "w_branch": w_branch[l].astype(BF16),
        "w_out": w_out[l].astype(BF16),
        "wq": peer_wq[l].astype(BF16),
        "sk": peer_subkeys[l].reshape(2 * PEER_HEADS, PEER_NKEYS, LANES).astype(BF16),
        "u": peer_u[l].astype(BF16),
        "vt": peer_v[l].T.astype(BF16),
        "lam_init": 0.8 - 0.6 * math.exp(-0.3 * l),
    }


def _project_all(x, lw, cos_t, sin_t, tag):
    aq, ak, av = _project(x, lw["norm_mix"], lw["w_a"], cos_t, sin_t, PIECES_A, WIDTHS_A, "proj_a_" + tag)
    if tag == "s":
        b_out = _project(x, lw["norm_mix"], lw["w_b_s"], cos_t, sin_t, PIECES_B_S, WIDTHS_B_S, "proj_b_s")
        cq, ck, cv = _project(x, lw["norm_mix"], lw["w_c_s"], cos_t, sin_t, PIECES_C_S, WIDTHS_C_S, "proj_c_s")
    else:
        b_out = _project(x, lw["norm_mix"], lw["w_b"], cos_t, sin_t, PIECES_B, WIDTHS_B, "proj_b_p")
        cq, ck, cv = _project(x, lw["norm_mix"], lw["w_c"], cos_t, sin_t, PIECES_C, WIDTHS_C, "proj_c_p")
    (gate,) = _project(x, lw["norm_mix"], lw["w_g"], cos_t, sin_t, PIECES_G, WIDTHS_G, "proj_g_" + tag)
    return (aq, ak, av), b_out, (cq, ck, cv), gate


def _ffn(x, lw, tag):
    st, stat = _peer_stats(x, lw["norm_ffn"], lw["wq"], lw["sk"], "peer_stats_" + tag)
    return _peer_dense(x, lw["norm_ffn"], lw["norm_final"], lw["last_layer"], st, stat, lw["u"], lw["vt"],
                       "peer_dense_" + tag)


def _prompt_layer(x, lw, cos_t, sin_t, bn, t):
    (aq, ak, av), (bqc, bqr, bkc, bvc, bks, bvs, bkw, bvw, bg), (cq, ck, cv), gate = _project_all(
        x, lw, cos_t, sin_t, "p")
    r3 = lambda a: a.reshape(bn, t, a.shape[-1])
    o_a = _diff_attention(r3(aq), r3(ak), r3(av), lw["lam"], lw["subln"], lw["lam_init"], 0, "diff_p")
    kcmp, vcmp = _compress(r3(bkc), r3(bvc), lw["cmp"], "cmp_p")
    o_b = _nsa_attention(r3(bqc), r3(bqr), r3(bg), kcmp, vcmp, r3(bks), r3(bvs), r3(bkw), r3(bvw),
                         0, t, 0, t, "nsa_p")
    o_c = _moba_attention(r3(cq), r3(ck), r3(cv), 0, "moba_p")
    n = bn * t
    x = _merge(x, gate, o_a.reshape(n, 512), o_b.reshape(n, 512), o_c.reshape(n, 512),
               lw["w_branch"], lw["w_out"], "merge_p")
    x = _ffn(x, lw, "p")
    wp = min(WINDOW, t)
    rows = (r3(ak).reshape(bn, t, A_KV, 2 * HD), r3(av).reshape(bn, t, A_KV, 2 * HD),
            r3(bkc).reshape(bn, t, B_KV, HD), r3(bvc).reshape(bn, t, B_KV, HD),
            r3(bks).reshape(bn, t, B_KV, HD), r3(bvs).reshape(bn, t, B_KV, HD),
            r3(ck).reshape(bn, t, C_KV, HD), r3(cv).reshape(bn, t, C_KV, HD),
            r3(bkw)[:, t - wp:].reshape(bn, wp, B_KV, HD), r3(bvw)[:, t - wp:].reshape(bn, wp, B_KV, HD))
    return x, rows


def _sample_layer(x, lw, l, cos_t, sin_t, db, t, past, caches, state_wk, state_wv, page_table):
    x = jnp.pad(x.reshape(db, t, D_MODEL), ((0, 0), (0, SROWS - t), (0, 0))).reshape(db * SROWS, D_MODEL)
    (aq, ak, av), (bqc, bqr, bkc, bvc, bks, bvs, bkw, bvw, bg), (cq, ck, cv), gate = _project_all(
        x, lw, cos_t, sin_t, "s")
    r3 = lambda a: a.reshape(db, SROWS, a.shape[-1])
    c_ak, c_av, c_bkc, c_bvc, c_bks, c_bvs, c_ck, c_cv = caches
    o_a = _diff_sample(page_table, l, t, r3(aq), r3(ak), r3(av), c_ak, c_av, lw["lam"], lw["subln"],
                       lw["lam_init"], "diff_s")
    kcmp, vcmp = _compress_sample(page_table, l, c_bkc, c_bvc, lw["cmp"], "cmp_s")
    o_b = _nsa_sample(page_table, l, t, past, r3(bqc), r3(bqr), r3(bg), kcmp, vcmp, r3(bks), r3(bvs),
                      r3(bkw), r3(bvw), state_wk, state_wv, c_bks, c_bvs, "nsa_s")
    o_c = _moba_sample(page_table, l, t, r3(cq), r3(ck), r3(cv), c_ck, c_cv, "moba_s")
    n = db * SROWS
    x = _merge(x, gate, o_a.reshape(n, 512), o_b.reshape(n, 512), o_c.reshape(n, 512),
               lw["w_branch"], lw["w_out"], "merge_s")
    x = _ffn(x.reshape(db, SROWS, D_MODEL)[:, :t].reshape(db * t, D_MODEL), lw, "s")
    new = lambda a, kv, w: r3(a)[:, :t].reshape(db, t, kv, w)

    def rolled(state_t, a):
        new_t = jnp.transpose(r3(a)[:, :t], (0, 2, 1))
        out_t = jnp.concatenate([state_t[l][:, :, t:], new_t], axis=2)
        return jnp.transpose(out_t, (0, 2, 1)).reshape(db, out_t.shape[2], B_KV, HD)

    rows = (new(ak, A_KV, 2 * HD), new(av, A_KV, 2 * HD), new(bkc, B_KV, HD), new(bvc, B_KV, HD),
            new(bks, B_KV, HD), new(bvs, B_KV, HD), new(ck, C_KV, HD), new(cv, C_KV, HD),
            rolled(state_wk, bkw), rolled(state_wv, bvw))
    return x, rows


def kernel(x_prompt, x_sample, cache_a_k, cache_a_v, cache_b_kc, cache_b_vc, cache_b_ks, cache_b_vs,
           cache_c_k, cache_c_v, state_b_wk, state_b_wv, page_table, norm_mix, norm_ffn, norm_final,
           w_in, a_lambda, a_subln, b_cmp_pe, b_cmp_w1, b_cmp_w2, w_branch, w_out,
           peer_wq, peer_subkeys, peer_u, peer_v):
    bn, t, _ = x_prompt.shape
    db, ts, _ = x_sample.shape
    depth = w_in.shape[0]
    past = page_table.shape[1] * PAGE_SIZE
    wb = state_b_wk.shape[2]
    assert past % MOBA_BLOCK == 0 and ts <= SROWS and wb == WINDOW and wb % LANES == 0 and db % SEQS == 0
    rows_view = lambda c: c.reshape(c.shape[0], c.shape[1], c.shape[2] * c.shape[3], c.shape[4])
    lanes_view = lambda c: jnp.transpose(c, (0, 1, 3, 4, 2)).reshape(
        c.shape[0], c.shape[1], c.shape[3] * c.shape[4], c.shape[2])
    caches = ([rows_view(c) for c in (cache_a_k, cache_a_v)]
              + [lanes_view(c) for c in (cache_b_kc, cache_b_vc, cache_b_ks, cache_b_vs, cache_c_k, cache_c_v)])
    swk = lanes_view(state_b_wk)
    swv = lanes_view(state_b_wv)

    cos_p, sin_p = _rope_tables(jnp.arange(t, dtype=jnp.int32))
    tm_s = min(ROW_TILE, db * SROWS)
    cos_s, sin_s = _rope_tables(past + (jnp.arange(tm_s, dtype=jnp.int32) % SROWS))

    xp = x_prompt.reshape(bn * t, D_MODEL)
    xs = x_sample.reshape(db * ts, D_MODEL)
    rows_p, rows_s = [], []
    for l in range(depth):
        lw = _layer_weights(l, norm_mix, norm_ffn, norm_final, w_in, a_lambda, a_subln, b_cmp_pe, b_cmp_w1,
                            b_cmp_w2, w_branch, w_out, peer_wq, peer_subkeys, peer_u, peer_v)
        xp, rp = _prompt_layer(xp, lw, cos_p, sin_p, bn, t)
        xs, rs = _sample_layer(xs, lw, l, cos_s, sin_s, db, ts, past, caches, swk, swv, page_table)
        rows_p.append(rp)
        rows_s.append(rs)
    y_prompt = xp.reshape(bn, t, D_MODEL)
    y_sample = xs.reshape(db, ts, D_MODEL)
    outs_p = [jnp.stack(r, axis=0) for r in zip(*rows_p)]
    outs_s = [jnp.stack(r, axis=0) for r in zip(*rows_s)]
    return (y_prompt, y_sample, *outs_p, *outs_s)
```

```python
import functools
import math

import jax
import jax.numpy as jnp
import numpy as np
from jax import lax
from jax.experimental import pallas as pl
from jax.experimental.pallas import tpu as pltpu

F32 = jnp.float32
BF16 = jnp.bfloat16

D_MODEL = 1024
HD = 64
A_HEADS, A_KV = 4, 2
B_HEADS, B_KV = 8, 2
C_HEADS, C_KV = 8, 4
CMP_BLOCK, CMP_STRIDE, CMP_HIDDEN = 32, 16, 128
SEL_BLOCK, SEL_TOPK = 64, 8
CMP_PER_SEL_SHIFT = 2
WINDOW = 512
MOBA_BLOCK, MOBA_TOPK = 256, 3
PEER_HEADS, PEER_NKEYS, PEER_TOPK = 8, 128, 16
ROPE_THETA = 10000.0
EPS = 1e-6
PAGE_SIZE = 128

LANES = 128
NEG = -1e30
QK_SCALE_LOG2 = math.log2(math.e) * HD ** -0.5
VMEM_LIMIT = 48 * 1024 * 1024
ROW_TILE = 256
PROJ_TILE = 512
KV_CHUNK = 256
EXPERT_TILE = 4096
EXPERT_SUB = 256


def _params(*sem):
    return pltpu.CompilerParams(dimension_semantics=sem, vmem_limit_bytes=VMEM_LIMIT)


def _dot(a, b):
    return jnp.dot(a, b, preferred_element_type=F32)


def _dot_nt(a, b):
    return lax.dot_general(a, b, (((1,), (1,)), ((), ())), preferred_element_type=F32)


def _split3(a):
    hi = a.astype(BF16)
    r1 = a - hi.astype(F32)
    mid = r1.astype(BF16)
    lo = (r1 - mid.astype(F32)).astype(BF16)
    return hi, mid, lo


def _rms(x, w):
    return x * lax.rsqrt(jnp.mean(x * x, axis=-1, keepdims=True) + EPS) * w


def _proj_body(pieces, chunk, x_ref, nw_ref, w_ref, cos_ref, sin_ref, *outs):
    h = _rms(x_ref[...], nw_ref[...]).astype(BF16)
    cos = cos_ref[...]
    sin = sin_ref[...]
    lane = lax.broadcasted_iota(jnp.int32, (1, LANES), 1)
    first = (lane & (HD - 1)) < (HD // 2)
    n_cols = len(pieces) * LANES
    for c0 in range(0, n_cols, chunk):
        w = min(chunk, n_cols - c0)
        z = _dot(h, w_ref[:, c0:c0 + w])
        for p in range(w // LANES):
            zp = z[:, p * LANES:(p + 1) * LANES]
            for (oi, oc, rope) in pieces[c0 // LANES + p]:
                if rope:
                    rot = jnp.where(first, pltpu.roll(zp, LANES - HD // 2, 1), pltpu.roll(zp, HD // 2, 1))
                    outs[oi][:, oc:oc + LANES] = zp * cos + rot * sin
                else:
                    outs[oi][:, oc:oc + LANES] = zp


def _project(x, nw, w, cos_t, sin_t, pieces, out_widths, name):
    n = x.shape[0]
    tm = min(PROJ_TILE, n)
    ntab = cos_t.shape[0] // tm
    ncols = w.shape[1]
    body = functools.partial(_proj_body, pieces, 512)
    return pl.pallas_call(
        body,
        grid=(n // tm,),
        in_specs=[
            pl.BlockSpec((tm, D_MODEL), lambda i: (i, 0)),
            pl.BlockSpec((1, D_MODEL), lambda i: (0, 0)),
            pl.BlockSpec((D_MODEL, ncols), lambda i: (0, 0)),
            pl.BlockSpec((tm, LANES), lambda i: (i % ntab, 0)),
            pl.BlockSpec((tm, LANES), lambda i: (i % ntab, 0)),
        ],
        out_specs=[pl.BlockSpec((tm, ow), lambda i: (i, 0)) for ow in out_widths],
        out_shape=[jax.ShapeDtypeStruct((n, ow), F32) for ow in out_widths],
        compiler_params=_params("parallel"),
        name=name,
    )(x, nw, w, cos_t, sin_t)


def _plain(oi, width):
    return [[(oi, c, False)] for c in range(0, width, LANES)]


def _roped(oi, width):
    return [[(oi, c, True)] for c in range(0, width, LANES)]


PIECES_A = _roped(0, 512) + _roped(1, 256) + _plain(2, 256)
WIDTHS_A = (512, 256, 256)
PIECES_B = ([[(0, c, False), (1, c, True)] for c in range(0, 512, LANES)]
            + _plain(2, 128) + _plain(3, 128) + _roped(4, 128) + _plain(5, 128)
            + _roped(6, 128) + _plain(7, 128) + _plain(8, 128))
WIDTHS_B = (512, 512, 128, 128, 128, 128, 128, 128, 128)
PIECES_C = _roped(0, 512) + _roped(1, 256) + _plain(2, 256)
WIDTHS_C = (512, 256, 256)
PIECES_G = _plain(0, 3072)
WIDTHS_G = (3072,)
SPREAD_B = B_HEADS * B_KV * HD
SPREAD_C = C_HEADS * C_KV * HD
PIECES_B_S = [[(0, c, False), (1, c, True)] for c in range(0, SPREAD_B, LANES)] + PIECES_B[512 // LANES:]
WIDTHS_B_S = (SPREAD_B, SPREAD_B) + WIDTHS_B[2:]
PIECES_C_S = _roped(0, SPREAD_C) + _roped(1, 256) + _plain(2, 256)
WIDTHS_C_S = (SPREAD_C, 256, 256)


def _spread_heads(wq, n_heads, n_groups):
    rep = n_heads // n_groups
    onehot = (np.arange(n_heads)[:, None] // rep == np.arange(n_groups)[None, :]).astype(np.float32)
    w = wq.reshape(wq.shape[0], n_heads, HD)
    return jnp.einsum("dhk,hg->dhgk", w, jnp.asarray(onehot)).reshape(wq.shape[0], n_heads * n_groups * HD)


def _rope_tables(pos):
    half = HD // 2
    freqs = jnp.power(ROPE_THETA, -jnp.arange(half, dtype=F32) / half)
    ang = pos.astype(F32)[:, None] * freqs[None, :]
    cos = jnp.cos(ang)
    sin = jnp.sin(ang)
    cos_t = jnp.concatenate([cos, cos, cos, cos], axis=1)
    sin_t = jnp.concatenate([-sin, sin, -sin, sin], axis=1)
    return cos_t, sin_t


def _flash_init(m_ref, l_ref, acc_ref):
    m_ref[...] = jnp.full(m_ref.shape, NEG, F32)
    l_ref[...] = jnp.zeros(l_ref.shape, F32)
    acc_ref[...] = jnp.zeros(acc_ref.shape, F32)


def _lane_blocks(x):
    return [x[:, c:c + LANES] for c in range(0, x.shape[1], LANES)]


def _flash_step(hh, q_emb, kc, vc, biases, m_ref, l_ref, acc_ref):
    blocks = _lane_blocks(_dot_nt(q_emb, kc))
    if biases is not None:
        blocks = [b if bi is None else b + bi for b, bi in zip(blocks, biases)]
    mx = blocks[0]
    for b in blocks[1:]:
        mx = jnp.maximum(mx, b)
    m_old = m_ref[hh]
    m_new = jnp.maximum(m_old, jnp.max(mx, axis=1, keepdims=True))
    alpha = jnp.exp2(m_old - m_new)
    ps = [jnp.exp2(b - m_new) for b in blocks]
    tot = ps[0]
    for p in ps[1:]:
        tot = tot + p
    l_ref[hh] = alpha * l_ref[hh] + jnp.sum(tot, axis=1, keepdims=True)
    acc_ref[hh] = alpha * acc_ref[hh] + _dot(jnp.concatenate(ps, axis=1).astype(BF16), vc)
    m_ref[hh] = m_new


def _causal_sweep(q0, tq, body):
    n_full = (q0 + 1) // KV_CHUNK
    n_pair = n_full // 2
    lax.fori_loop(0, n_pair, body(2 * KV_CHUNK, False), 0)
    lax.fori_loop(2 * n_pair, n_full, body(KV_CHUNK, False), 0)
    lax.fori_loop(n_full, (q0 + tq - 1) // KV_CHUNK + 1, body(KV_CHUNK, True), 0)


def _causal_bias(j, tk, qpos, inside=0.0):
    kpos = j * tk + lax.broadcasted_iota(jnp.int32, (1, tk), 1)
    return jnp.where(kpos <= qpos, inside, NEG)


def _flash_out(hh, l_ref, acc_ref):
    return acc_ref[hh] / jnp.maximum(l_ref[hh], 1e-30)


def _chunk(ref, j, tk):
    return ref[pl.ds(pl.multiple_of(j * tk, tk), tk), :].astype(BF16)


def _diff_body(tq, tk, q_pos0, lam_init, lam_ref, subln_ref, q_ref, k_ref, v_ref, o_ref,
               m_ref, l_ref, acc_ref):
    qi = pl.program_id(2)
    lane = lax.broadcasted_iota(jnp.int32, (1, LANES), 1)
    upper = lane >= HD
    q0 = q_pos0 + qi * tq
    qpos = q0 + lax.broadcasted_iota(jnp.int32, (tq, 1), 0)
    _flash_init(m_ref, l_ref, acc_ref)
    qs = []
    for r in range(2):
        qb = q_ref[:, r * LANES:(r + 1) * LANES] * QK_SCALE_LOG2
        qs.append(jnp.where(upper, 0.0, qb).astype(BF16))
        qs.append(jnp.where(upper, qb, 0.0).astype(BF16))
    def sweep(tk, diag):
        def body(j, carry):
            kc = _chunk(k_ref, j, tk)
            vc = _chunk(v_ref, j, tk)
            biases = _lane_blocks(_causal_bias(j, tk, qpos)) if diag else None
            for hh in range(4):
                _flash_step(hh, qs[hh], kc, vc, biases, m_ref, l_ref, acc_ref)
            return carry
        return body

    _causal_sweep(q0, tq, sweep)
    lp = lam_ref[...]
    lam = (jnp.exp(jnp.sum(lp[0:1] * lp[1:2], axis=1, keepdims=True))
           - jnp.exp(jnp.sum(lp[2:3] * lp[3:4], axis=1, keepdims=True)) + lam_init)
    for r in range(2):
        o = _flash_out(2 * r, l_ref, acc_ref) - lam * _flash_out(2 * r + 1, l_ref, acc_ref)
        o_ref[:, r * LANES:(r + 1) * LANES] = _rms(o, subln_ref[...]) * (1.0 - lam_init)


def _diff_attention(aq, ak, av, lam_p, subln, lam_init, q_pos0, name):
    bn, tq_all, _ = aq.shape
    lk = ak.shape[1]
    tq = min(ROW_TILE, tq_all)
    tk = KV_CHUNK
    body = functools.partial(_diff_body, tq, tk, q_pos0, lam_init)
    return pl.pallas_call(
        body,
        grid=(bn, A_KV, tq_all // tq),
        in_specs=[
            pl.BlockSpec((4, HD), lambda b, g, i: (0, 0)),
            pl.BlockSpec((1, 2 * HD), lambda b, g, i: (0, 0)),
            pl.BlockSpec((None, tq, 256), lambda b, g, i: (b, i, g)),
            pl.BlockSpec((None, lk, LANES), lambda b, g, i: (b, 0, g)),
            pl.BlockSpec((None, lk, LANES), lambda b, g, i: (b, 0, g)),
        ],
        out_specs=pl.BlockSpec((None, tq, 256), lambda b, g, i: (b, i, g)),
        out_shape=jax.ShapeDtypeStruct((bn, tq_all, 512), F32),
        scratch_shapes=[pltpu.VMEM((4, tq, LANES), F32), pltpu.VMEM((4, tq, LANES), F32),
                        pltpu.VMEM((4, tq, LANES), F32)],
        compiler_params=_params("parallel", "parallel", "arbitrary"),
        name=name,
    )(lam_p, subln, aq, ak, av)


def _compress_one(x_ref, pe_ref, w1_ref, w2_ref, o_ref):
    n_half = CMP_STRIDE
    xs = [x_ref[pl.ds(p, LANES, stride=CMP_STRIDE), :] for p in range(n_half)]
    xa = jnp.concatenate([(x + pe_ref[p]).astype(BF16) for p, x in enumerate(xs)], axis=1)
    xb = jnp.concatenate([(x + pe_ref[n_half + p]).astype(BF16) for p, x in enumerate(xs)], axis=1)
    w1 = w1_ref[...].reshape(2, n_half * LANES, 2 * CMP_HIDDEN)
    h = _dot(xa, w1[0]) + pltpu.roll(_dot(xb, w1[1]), LANES - 1, 0)
    y = _dot(jax.nn.gelu(h).astype(BF16), w2_ref[...])
    row = lax.broadcasted_iota(jnp.int32, (LANES, 1), 0)
    o_ref[...] = jnp.where(row < LANES - 1, y, 0.0)


def _compress_body(xk_ref, xv_ref, pek_ref, pev_ref, w1k_ref, w1v_ref, w2k_ref, w2v_ref, ok_ref, ov_ref):
    _compress_one(xk_ref, pek_ref, w1k_ref, w2k_ref, ok_ref)
    _compress_one(xv_ref, pev_ref, w1v_ref, w2v_ref, ov_ref)


def _compress(xk, xv, cw, name):
    bn = xk.shape[0]
    n_tok = LANES * CMP_STRIDE
    x_spec = pl.BlockSpec((None, n_tok, LANES), lambda b: (b, 0, 0))
    pe_spec = pl.BlockSpec((CMP_BLOCK, 1, LANES), lambda b: (0, 0, 0))
    w1_spec = pl.BlockSpec((CMP_BLOCK, LANES, 2 * CMP_HIDDEN), lambda b: (0, 0, 0))
    w2_spec = pl.BlockSpec((2 * CMP_HIDDEN, LANES), lambda b: (0, 0))
    o_spec = pl.BlockSpec((None, LANES, LANES), lambda b: (b, 0, 0))
    return pl.pallas_call(
        _compress_body,
        grid=(bn,),
        in_specs=[x_spec, x_spec, pe_spec, pe_spec, w1_spec, w1_spec, w2_spec, w2_spec],
        out_specs=[o_spec, o_spec],
        out_shape=[jax.ShapeDtypeStruct((bn, LANES, LANES), F32)] * 2,
        compiler_params=_params("parallel"),
        name=name,
    )(xk, xv, cw["pek"], cw["pev"], cw["w1k"], cw["w1v"], cw["w2k"], cw["w2v"])


def _compress_weights(pe, w1, w2):
    out = {}
    for idx, tag in ((0, "k"), (1, "v")):
        w1p = w1[idx].reshape(CMP_BLOCK, HD, CMP_HIDDEN)
        z = jnp.zeros_like(w1p)
        w1b = jnp.concatenate([jnp.concatenate([w1p, z], axis=2), jnp.concatenate([z, w1p], axis=2)], axis=1)
        z2 = jnp.zeros_like(w2[idx])
        w2b = jnp.concatenate([jnp.concatenate([w2[idx], z2], axis=1), jnp.concatenate([z2, w2[idx]], axis=1)], axis=0)
        out["w1" + tag] = w1b.astype(BF16)
        out["w2" + tag] = w2b.astype(BF16)
        out["pe" + tag] = jnp.concatenate([pe[idx], pe[idx]], axis=1)[:, None, :]
    return out


def _rank_desc(vals, n, lane):
    rank = jnp.zeros(vals.shape, F32)
    for i in range(n):
        col = vals[:, i:i + 1]
        tie = jnp.where(lane > i, 1.0, 0.0)
        rank = rank + jnp.where(col > vals, 1.0, 0.0) + jnp.where(col == vals, tie, 0.0)
    return rank


def _nsa_body(tq, tk, q_pos0, ns, w_pos0, w_valid, lw,
              qc_ref, qr_ref, bg_ref, kcmp_ref, vcmp_ref, ks_ref, vs_ref, kw_ref, vw_ref, o_ref,
              m_ref, l_ref, acc_ref):
    g = pl.program_id(1)
    qi = pl.program_id(2)
    lane = lax.broadcasted_iota(jnp.int32, (1, LANES), 1)
    halfi = lane >> 6
    q0 = q_pos0 + qi * tq
    qpos = q0 + lax.broadcasted_iota(jnp.int32, (tq, 1), 0)
    rep = B_HEADS // B_KV

    def embed_all(ref, scale):
        out = []
        for pr in range(rep // 2):
            v = ref[:, pr * LANES:(pr + 1) * LANES] * scale
            swapped = pltpu.roll(v, HD, 1)
            for u in range(2):
                out.append(jnp.where(halfi == g, jnp.where(g == u, v, swapped), 0.0).astype(BF16))
        return out

    def pair_out(o_even, o_odd):
        aligned = jnp.where(g == 0, o_even, o_odd)
        other = jnp.where(g == 0, o_odd, o_even)
        return jnp.where(halfi == g, aligned, pltpu.roll(other, HD, 1))

    kcmp = kcmp_ref[...].astype(BF16)
    vcmp = vcmp_ref[...].astype(BF16)
    cmask = (lane * CMP_STRIDE + (CMP_BLOCK - 1)) <= qpos
    psum = jnp.zeros((tq, LANES), F32)
    o_cmp = []
    qsc = embed_all(qc_ref, HD ** -0.5)
    for r in range(rep):
        s = jnp.where(cmask, _dot_nt(qsc[r], kcmp), NEG)
        m = jnp.max(s, axis=1, keepdims=True)
        p = jnp.where(cmask, jnp.exp(s - m), 0.0)
        p = p / jnp.maximum(jnp.sum(p, axis=1, keepdims=True), 1e-30)
        psum = psum + p
        o_cmp.append(_dot(p.astype(BF16), vcmp))

    ci = lax.broadcasted_iota(jnp.int32, (LANES, LANES), 0)
    cj = lax.broadcasted_iota(jnp.int32, (LANES, LANES), 1)
    onehot = jnp.where((ci >> CMP_PER_SEL_SHIFT) == cj, 1.0, 0.0).astype(BF16)
    hi, mid, lo = _split3(psum)
    imp = _dot(hi, onehot) + _dot(mid, onehot) + _dot(lo, onehot)
    cur = qpos >> 6
    impm = jnp.where(lane == cur, jnp.inf, jnp.where(lane < cur, imp, -jnp.inf))
    rank = _rank_desc(impm, ns, lane)
    sel = jnp.where(rank < float(SEL_TOPK), jnp.where(lane <= cur, 1.0, 0.0), 0.0).astype(BF16)

    qsr = embed_all(qr_ref, QK_SCALE_LOG2)
    _flash_init(m_ref, l_ref, acc_ref)
    def sel_sweep(ck, diag):
        bpc = ck // SEL_BLOCK

        def body(j, carry):
            kc = _chunk(ks_ref, j, ck)
            vc = _chunk(vs_ref, j, ck)
            ei = lax.broadcasted_iota(jnp.int32, (LANES, ck), 0)
            el = lax.broadcasted_iota(jnp.int32, (LANES, ck), 1)
            expand = jnp.where(ei == j * bpc + (el >> 6), 1.0, 0.0).astype(BF16)
            bias = (_dot(sel, expand) - 1.0) * (-NEG)
            if diag:
                bias = _causal_bias(j, ck, qpos, bias)
            biases = _lane_blocks(bias)
            for r in range(rep):
                _flash_step(r, qsr[r], kc, vc, biases, m_ref, l_ref, acc_ref)
            return carry
        return body

    _causal_sweep(q0, tq, sel_sweep)
    o_slc = [_flash_out(r, l_ref, acc_ref) for r in range(rep)]

    _flash_init(m_ref, l_ref, acc_ref)
    w_lo = jnp.maximum(q0 - WINDOW - w_pos0, 0) // tk
    w_hi = jnp.minimum(q0 + tq - 1 - w_pos0, lw - 1) // tk + 1

    def win_body(j, carry):
        kc = _chunk(kw_ref, j, tk)
        vc = _chunk(vw_ref, j, tk)
        kidx = j * tk + lax.broadcasted_iota(jnp.int32, (1, tk), 1)
        dist = qpos - (kidx + w_pos0)
        inwin = jnp.where(dist >= 0, jnp.where(dist <= WINDOW, 0.0, NEG), NEG)
        biases = _lane_blocks(jnp.where(kidx < w_valid, inwin, NEG))
        for r in range(rep):
            _flash_step(r, qsr[r], kc, vc, biases, m_ref, l_ref, acc_ref)
        return carry

    lax.fori_loop(w_lo, w_hi, win_body, 0)

    sig = jax.nn.sigmoid(bg_ref[...])
    outs = []
    for r in range(rep):
        o_win = _flash_out(r, l_ref, acc_ref)
        base = (g * rep + r) * 3
        gates = [jnp.sum(jnp.where(lane == base + c, sig, 0.0), axis=1, keepdims=True) for c in range(3)]
        outs.append(gates[0] * o_cmp[r] + gates[1] * o_slc[r] + gates[2] * o_win)
    for pr in range(rep // 2):
        o_ref[:, pr * LANES:(pr + 1) * LANES] = pair_out(outs[2 * pr], outs[2 * pr + 1])


def _nsa_attention(bqc, bqr, bg, kcmp, vcmp, ks, vs, kw, vw, q_pos0, k_valid, w_pos0, w_valid, name):
    bn, tq_all, _ = bqc.shape
    lk = ks.shape[1]
    lw = kw.shape[1]
    tq = min(ROW_TILE, tq_all)
    tk = KV_CHUNK
    ns = -(-k_valid // SEL_BLOCK)
    body = functools.partial(_nsa_body, tq, tk, q_pos0, ns, w_pos0, w_valid, lw)
    q_spec = pl.BlockSpec((None, tq, 256), lambda b, g, i: (b, i, g))
    full = lambda rows: pl.BlockSpec((None, rows, LANES), lambda b, g, i: (b, 0, 0))
    return pl.pallas_call(
        body,
        grid=(bn, B_KV, tq_all // tq),
        in_specs=[q_spec, q_spec, pl.BlockSpec((None, tq, LANES), lambda b, g, i: (b, i, 0)),
                  full(LANES), full(LANES), full(lk), full(lk), full(lw), full(lw)],
        out_specs=pl.BlockSpec((None, tq, 256), lambda b, g, i: (b, i, g)),
        out_shape=jax.ShapeDtypeStruct((bn, tq_all, 512), F32),
        scratch_shapes=[pltpu.VMEM((4, tq, LANES), F32), pltpu.VMEM((4, tq, LANES), F32),
                        pltpu.VMEM((4, tq, LANES), F32)],
        compiler_params=_params("parallel", "parallel", "arbitrary"),
        name=name,
    )(bqc, bqr, bg, kcmp, vcmp, ks, vs, kw, vw)


def _moba_body(tq, tk, q_pos0, nb, q_ref, k_ref, v_ref, o_ref, km_ref, m_ref, l_ref, acc_ref):
    g = pl.program_id(1)
    qi = pl.program_id(2)
    gh = g % 2
    lane = lax.broadcasted_iota(jnp.int32, (1, LANES), 1)
    halfi = lane >> 6
    q0 = q_pos0 + qi * tq
    qpos = q0 + lax.broadcasted_iota(jnp.int32, (tq, 1), 0)
    cur = qpos >> 8
    rep = C_HEADS // C_KV

    @pl.when(qi == 0)
    def _():
        km_ref[...] = jnp.zeros(km_ref.shape, F32)
        km_ref[0:nb, :] = jnp.sum(k_ref[...].reshape(nb, MOBA_BLOCK, LANES), axis=1) * (1.0 / MOBA_BLOCK)

    km_hi, km_mid, _ = _split3(km_ref[...])

    qv = q_ref[...]
    qs = []
    sels = []
    swapped = pltpu.roll(qv, HD, 1)
    for r in range(rep):
        qe = jnp.where(halfi == gh, jnp.where(gh == r, qv, swapped), 0.0)
        q_hi, q_mid, _ = _split3(qe)
        s_blk = _dot_nt(q_hi, km_hi) + _dot_nt(q_hi, km_mid) + _dot_nt(q_mid, km_hi)
        sm = jnp.where(lane < cur, s_blk, -jnp.inf)
        rank = _rank_desc(sm, nb, lane)
        past = jnp.where(rank < float(MOBA_TOPK), jnp.where(lane < cur, 1.0, 0.0), 0.0)
        sels.append(jnp.where(lane == cur, 1.0, past))
        qs.append((qe * QK_SCALE_LOG2).astype(BF16))

    _flash_init(m_ref, l_ref, acc_ref)
    def sweep(ck, diag):
        bpc = ck // MOBA_BLOCK
        lpb = MOBA_BLOCK // LANES

        def body(j, carry):
            kc = _chunk(k_ref, j, ck)
            vc = _chunk(v_ref, j, ck)
            diag_biases = _lane_blocks(_causal_bias(j, ck, qpos)) if diag else None
            for r in range(rep):
                if diag:
                    biases = diag_biases
                else:
                    biases = []
                    for u in range(bpc):
                        col = jnp.sum(jnp.where(lane == j * bpc + u, sels[r], 0.0), axis=1, keepdims=True)
                        biases += [(col - 1.0) * (-NEG)] * lpb
                _flash_step(r, qs[r], kc, vc, biases, m_ref, l_ref, acc_ref)
            return carry
        return body

    _causal_sweep(q0, tq, sweep)
    o0 = _flash_out(0, l_ref, acc_ref)
    o1 = _flash_out(1, l_ref, acc_ref)
    aligned = jnp.where(gh == 0, o0, o1)
    other = jnp.where(gh == 0, o1, o0)
    o_ref[...] = jnp.where(halfi == gh, aligned, pltpu.roll(other, HD, 1))


def _moba_attention(cq, ck, cv, q_pos0, name):
    bn, tq_all, _ = cq.shape
    lk = ck.shape[1]
    tq = min(ROW_TILE, tq_all)
    tk = MOBA_BLOCK
    nb = lk // MOBA_BLOCK
    assert tq == MOBA_BLOCK == KV_CHUNK and q_pos0 % MOBA_BLOCK == 0
    body = functools.partial(_moba_body, tq, tk, q_pos0, nb)
    return pl.pallas_call(
        body,
        grid=(bn, C_KV, tq_all // tq),
        in_specs=[
            pl.BlockSpec((None, tq, LANES), lambda b, g, i: (b, i, g)),
            pl.BlockSpec((None, lk, LANES), lambda b, g, i: (b, 0, g // 2)),
            pl.BlockSpec((None, lk, LANES), lambda b, g, i: (b, 0, g // 2)),
        ],
        out_specs=pl.BlockSpec((None, tq, LANES), lambda b, g, i: (b, i, g)),
        out_shape=jax.ShapeDtypeStruct((bn, tq_all, 512), F32),
        scratch_shapes=[pltpu.VMEM((LANES, LANES), F32),
                        pltpu.VMEM((2, tq, LANES), F32), pltpu.VMEM((2, tq, LANES), F32),
                        pltpu.VMEM((2, tq, LANES), F32)],
        compiler_params=_params("parallel", "parallel", "arbitrary"),
        name=name,
    )(cq, ck, cv)


def _merge_body(x_ref, gate_ref, oa_ref, ob_ref, oc_ref, wb_ref, wo_ref, y_ref):
    y = jnp.zeros((x_ref.shape[0], D_MODEL), F32)
    for c, o_ref in enumerate((oa_ref, ob_ref, oc_ref)):
        br = _dot(o_ref[...].astype(BF16), wb_ref[c])
        y = y + jax.nn.sigmoid(gate_ref[:, c * D_MODEL:(c + 1) * D_MODEL]) * br
    y_ref[...] = x_ref[...] + _dot(y.astype(BF16), wo_ref[...])


def _merge(x, gate, o_a, o_b, o_c, wb, wo, name):
    n = x.shape[0]
    tm = min(PROJ_TILE, n)
    row = lambda w: pl.BlockSpec((tm, w), lambda i: (i, 0))
    return pl.pallas_call(
        _merge_body,
        grid=(n // tm,),
        in_specs=[row(D_MODEL), row(3 * D_MODEL), row(512), row(512), row(512),
                  pl.BlockSpec((3, 512, D_MODEL), lambda i: (0, 0, 0)),
                  pl.BlockSpec((D_MODEL, D_MODEL), lambda i: (0, 0))],
        out_specs=row(D_MODEL),
        out_shape=jax.ShapeDtypeStruct((n, D_MODEL), F32),
        compiler_params=_params("parallel"),
        name=name,
    )(x, gate, o_a, o_b, o_c, wb, wo)


def _bitonic_desc(vs):
    vs = list(vs)
    n = len(vs)
    k = 2
    while k <= n:
        j = k // 2
        while j >= 1:
            for i in range(n):
                partner = i ^ j
                if partner > i:
                    hi = jnp.maximum(vs[i], vs[partner])
                    lo = jnp.minimum(vs[i], vs[partner])
                    vs[i], vs[partner] = (hi, lo) if (i & k) == 0 else (lo, hi)
            j //= 2
        k *= 2
    return vs


def _pop_heads(cols, n_out):
    cols = list(cols)
    for k in range(n_out):
        m = jnp.max(cols[0], axis=0, keepdims=True)
        eq = cols[0] == m
        yield m, eq
        depth = min(len(cols), n_out - k)
        for d in range(depth - 1):
            cols[d] = jnp.where(eq, cols[d + 1], cols[d])
        cols[depth - 1] = jnp.where(eq, -jnp.inf, cols[depth - 1])


def _peer_stats_body(tm, x_ref, nw_ref, wq_ref, sk_ref, st_ref, stat_ref, top_ref, nxt_ref):
    h = _rms(x_ref[...], nw_ref[...]).astype(BF16)
    q = _dot(h, wq_ref[...]).astype(BF16)
    n_hc = 2 * PEER_HEADS
    for hc in range(n_hc):
        st_ref[hc] = _dot_nt(sk_ref[hc], q[:, hc * LANES:(hc + 1) * LANES])

    sub = 8
    n_out = PEER_TOPK + 1

    def top_body(hc, carry):
        for hf in range(tm // LANES):
            s = st_ref[hc, :, hf * LANES:(hf + 1) * LANES]
            cols = _bitonic_desc([s[d * sub:(d + 1) * sub] for d in range(PEER_NKEYS // sub)])
            vals = [m for m, _ in _pop_heads(cols, n_out)]
            top_ref[hc, :, hf * LANES:(hf + 1) * LANES] = jnp.concatenate(vals[:PEER_TOPK], axis=0)
            nxt_ref[hc, :, hf * LANES:(hf + 1) * LANES] = vals[PEER_TOPK]
        return carry

    lax.fori_loop(0, n_hc, top_body, 0)

    row8 = lax.broadcasted_iota(jnp.int32, (sub, LANES), 0)
    fill = jnp.full((sub, LANES), -jnp.inf, F32)

    def head_body(hd, carry):
        for hf in range(tm // LANES):
            sl = slice(hf * LANES, (hf + 1) * LANES)
            a = top_ref[2 * hd, :, sl]
            b = top_ref[2 * hd + 1, :, sl]
            cands = [a[0:1] + b[0:sub], a[0:1] + b[sub:2 * sub]]
            for i in range(1, sub):
                cands.append(jnp.where(row8 < PEER_TOPK // (i + 1), a[i:i + 1] + b[0:sub], -jnp.inf))
            cands.append(a[sub:2 * sub] + b[0:1])
            mx = a[0:1] + b[0:1]
            cum = jnp.zeros((1, LANES), F32)
            tau = mx
            nxt = mx
            cols = _bitonic_desc(cands + [fill] * (16 - len(cands)))
            for m, eq in _pop_heads(cols, n_out):
                tau = jnp.where(cum < float(PEER_TOPK), m, tau)
                nxt = jnp.where(cum < float(n_out), m, nxt)
                cum = cum + jnp.sum(jnp.where(eq, 1.0, 0.0), axis=0, keepdims=True)
            nxt = jnp.maximum(nxt, jnp.maximum(a[0:1] + nxt_ref[2 * hd + 1, :, sl], nxt_ref[2 * hd, :, sl] + b[0:1]))
            z = None
            for c in cands:
                t = jnp.sum(jnp.where(c >= tau, jnp.exp(c - mx), 0.0), axis=0, keepdims=True)
                z = t if z is None else z + t
            stat_ref[hd, :, sl] = 0.5 * (tau + nxt)
            stat_ref[PEER_HEADS + hd, :, sl] = a[0:1]
            stat_ref[2 * PEER_HEADS + hd, :, sl] = b[0:1]
            stat_ref[3 * PEER_HEADS + hd, :, sl] = 1.0 / z
        return carry

    lax.fori_loop(0, PEER_HEADS, head_body, 0)


def _peer_stats(x, nw, wq, sk, name):
    n = x.shape[0]
    tm = min(ROW_TILE, n)
    n_hc = 2 * PEER_HEADS
    return pl.pallas_call(
        functools.partial(_peer_stats_body, tm),
        grid=(n // tm,),
        in_specs=[pl.BlockSpec((tm, D_MODEL), lambda i: (i, 0)),
                  pl.BlockSpec((1, D_MODEL), lambda i: (0, 0)),
                  pl.BlockSpec((D_MODEL, n_hc * LANES), lambda i: (0, 0)),
                  pl.BlockSpec((n_hc, LANES, LANES), lambda i: (0, 0, 0))],
        out_specs=[pl.BlockSpec((n_hc, LANES, tm), lambda i: (0, 0, i)),
                   pl.BlockSpec((4 * PEER_HEADS, 1, tm), lambda i: (0, 0, i))],
        out_shape=[jax.ShapeDtypeStruct((n_hc, LANES, n), F32),
                   jax.ShapeDtypeStruct((4 * PEER_HEADS, 1, n), F32)],
        scratch_shapes=[pltpu.VMEM((n_hc, PEER_TOPK, tm), F32), pltpu.VMEM((n_hc, 1, tm), F32)],
        compiler_params=_params("parallel"),
        name=name,
    )(x, nw, wq, sk)


def _peer_dense_body(tm, te, last_layer, x_ref, nw_ref, fw_ref, st_ref, stat_ref, u_ref, vt_ref, o_ref,
                     h_ref, thr_ref, w1_ref, s2c_ref, e2_ref, acc_ref):
    e = pl.program_id(1)

    @pl.when(e == 0)
    def _():
        h_ref[...] = _rms(x_ref[...], nw_ref[...]).astype(BF16)
        acc_ref[...] = jnp.zeros(acc_ref.shape, F32)
        for hd in range(PEER_HEADS):
            tau = stat_ref[hd]
            m1 = stat_ref[PEER_HEADS + hd]
            m2 = stat_ref[2 * PEER_HEADS + hd]
            inv_z = stat_ref[3 * PEER_HEADS + hd]
            thr_ref[hd] = m1 - st_ref[2 * hd]
            w1_ref[hd] = jnp.exp(st_ref[2 * hd] - m1) * inv_z
            s2c_ref[hd] = (st_ref[2 * hd + 1] - (tau - m1)).astype(BF16)
            e2_ref[hd] = jnp.exp(st_ref[2 * hd + 1] - m2).astype(BF16)

    n_i = te // PEER_NKEYS
    zero = jnp.zeros((), BF16)
    h = h_ref[...]
    total = None
    for k in range(te // EXPERT_SUB):
        rows = slice(k * EXPERT_SUB, (k + 1) * EXPERT_SUB)
        act = jax.nn.gelu(_dot_nt(u_ref[rows, :], h)).astype(BF16)
        parts = []
        for ii in range(EXPERT_SUB // PEER_NKEYS):
            i = e * n_i + k * (EXPERT_SUB // PEER_NKEYS) + ii
            gmat = jnp.zeros((PEER_NKEYS, tm), BF16)
            for hd in range(PEER_HEADS):
                thr = thr_ref[hd, pl.ds(i, 1), :].astype(BF16)
                w1 = w1_ref[hd, pl.ds(i, 1), :].astype(BF16)
                gmat = gmat + jnp.where(s2c_ref[hd] > thr, e2_ref[hd] * w1, zero)
            parts.append(gmat * act[ii * PEER_NKEYS:(ii + 1) * PEER_NKEYS])
        t = _dot(vt_ref[:, rows], jnp.concatenate(parts, axis=0))
        total = t if total is None else total + t
    acc_ref[...] += total

    @pl.when(e == pl.num_programs(1) - 1)
    def _():
        y = x_ref[...] + acc_ref[...].T
        o_ref[...] = _rms(y, fw_ref[...]) if last_layer else y


def _peer_dense(x, nw, final_w, last_layer, st, stat, u_bf, vt_bf, name):
    n = x.shape[0]
    tm = min(ROW_TILE, n)
    te = EXPERT_TILE
    n_exp = u_bf.shape[0]
    n_hc = 2 * PEER_HEADS
    return pl.pallas_call(
        functools.partial(_peer_dense_body, tm, te, last_layer),
        grid=(n // tm, n_exp // te),
        in_specs=[pl.BlockSpec((tm, D_MODEL), lambda i, e: (i, 0)),
                  pl.BlockSpec((1, D_MODEL), lambda i, e: (0, 0)),
                  pl.BlockSpec((1, D_MODEL), lambda i, e: (0, 0)),
                  pl.BlockSpec((n_hc, LANES, tm), lambda i, e: (0, 0, i)),
                  pl.BlockSpec((4 * PEER_HEADS, 1, tm), lambda i, e: (0, 0, i)),
                  pl.BlockSpec((te, D_MODEL), lambda i, e: (e, 0)),
                  pl.BlockSpec((D_MODEL, te), lambda i, e: (0, e))],
        out_specs=pl.BlockSpec((tm, D_MODEL), lambda i, e: (i, 0)),
        out_shape=jax.ShapeDtypeStruct((n, D_MODEL), F32),
        scratch_shapes=[pltpu.VMEM((tm, D_MODEL), BF16),
                        pltpu.VMEM((PEER_HEADS, PEER_NKEYS, tm), F32),
                        pltpu.VMEM((PEER_HEADS, PEER_NKEYS, tm), F32),
                        pltpu.VMEM((PEER_HEADS, PEER_NKEYS, tm), BF16),
                        pltpu.VMEM((PEER_HEADS, PEER_NKEYS, tm), BF16),
                        pltpu.VMEM((D_MODEL, tm), F32)],
        compiler_params=_params("parallel", "arbitrary"),
        name=name,
    )(x, nw, final_w, st, stat, u_bf, vt_bf)


SROWS = 8
SEQS = 2


def _page_specs(n_pages, rows, layer):
    def spec(s, p):
        return pl.BlockSpec((None, None, rows, LANES), lambda b, pt: (layer, pt[b * SEQS + s, p], 0, 0))
    return [spec(s, p) for s in range(SEQS) for p in range(n_pages)]


def _row_spec(width):
    return pl.BlockSpec((SEQS, SROWS, width), lambda b, pt: (b, 0, 0))


def _seq_spec(rows, width):
    return pl.BlockSpec((SEQS, rows, width), lambda b, pt: (b, 0, 0))


def _per_sequence(body, n_const, n_seq, n_pages):
    def wrapped(pt_ref, *refs):
        consts = refs[:n_const]
        seqs = refs[n_const:n_const + n_seq]
        pages = refs[n_const + n_seq:n_const + n_seq + 2 * SEQS * n_pages]
        tail = refs[n_const + n_seq + 2 * SEQS * n_pages:]
        kp, vp = pages[:SEQS * n_pages], pages[SEQS * n_pages:]
        for s in range(SEQS):
            body(pt_ref, *consts, *[r.at[s] for r in seqs], *kp[s * n_pages:(s + 1) * n_pages],
                 *vp[s * n_pages:(s + 1) * n_pages], *[r.at[s] for r in tail])
    return wrapped


def _dup(x):
    return jnp.concatenate([x, x], axis=0)


def _pad_rows(x):
    return jnp.concatenate([x, jnp.zeros((PAGE_SIZE - x.shape[0], x.shape[1]), F32)], axis=0)


def _new_mask(m_rows, n_new):
    lane = lax.broadcasted_iota(jnp.int32, (1, LANES), 1)
    trow = lax.broadcasted_iota(jnp.int32, (m_rows, 1), 0) & (SROWS - 1)
    return jnp.where(lane < n_new, jnp.where(lane <= trow, 1.0, 0.0), 0.0) > 0.5


def _softmax_pieces(pieces):
    mx = None
    for s, mk in pieces:
        sm = s if mk is None else jnp.where(mk, s, NEG)
        mx = sm if mx is None else jnp.maximum(mx, sm)
    m = jnp.max(mx, axis=1, keepdims=True)
    ps = []
    tot = None
    for s, mk in pieces:
        p = jnp.exp(s - m)
        if mk is not None:
            p = jnp.where(mk, p, 0.0)
        tot = p if tot is None else tot + p
        ps.append(p.astype(BF16))
    return ps, jnp.sum(tot, axis=1, keepdims=True)


def _scores_kt(q, kts):
    return [_dot(q, _dup(kt.astype(BF16))) for kt in kts]


def _values_kt(ps, vts):
    o = None
    for p, vt in zip(ps, vts):
        t = _dot_nt(p, _dup(vt.astype(BF16)))
        o = t if o is None else o + t
    return o


def _diff_s_body(n_pages, n_new, lam_init, pt_ref, lam_ref, subln_ref, q_ref, kn_ref, vn_ref, *rest):
    k_refs = rest[:n_pages]
    v_refs = rest[n_pages:2 * n_pages]
    o_ref = rest[2 * n_pages]
    lane = lax.broadcasted_iota(jnp.int32, (1, LANES), 1)
    upper = lane >= HD
    new_mask = _new_mask(4 * SROWS, n_new)
    lp = lam_ref[...]
    lam = (jnp.exp(jnp.sum(lp[0:1] * lp[1:2], axis=1, keepdims=True))
           - jnp.exp(jnp.sum(lp[2:3] * lp[3:4], axis=1, keepdims=True)) + lam_init)
    for g in range(A_KV):
        parts = []
        for r in range(A_HEADS // A_KV):
            qb = q_ref[:, (2 * g + r) * LANES:(2 * g + r + 1) * LANES] * (HD ** -0.5)
            parts += [jnp.where(upper, 0.0, qb), jnp.where(upper, qb, 0.0)]
        q = jnp.concatenate(parts, axis=0).astype(BF16)
        pieces = [(_dot_nt(q, k_refs[p][pl.ds(g, PAGE_SIZE, stride=A_KV), :].astype(BF16)), None)
                  for p in range(n_pages)]
        knew = _pad_rows(kn_ref[:, g * LANES:(g + 1) * LANES]).astype(BF16)
        vnew = _pad_rows(vn_ref[:, g * LANES:(g + 1) * LANES]).astype(BF16)
        pieces.append((_dot_nt(q, knew), new_mask))
        ps, tot = _softmax_pieces(pieces)
        o = _dot(ps[n_pages], vnew)
        for p in range(n_pages):
            o = o + _dot(ps[p], v_refs[p][pl.ds(g, PAGE_SIZE, stride=A_KV), :].astype(BF16))
        o = o / tot
        for r in range(A_HEADS // A_KV):
            d = o[2 * r * SROWS:(2 * r + 1) * SROWS] - lam * o[(2 * r + 1) * SROWS:(2 * r + 2) * SROWS]
            o_ref[:, (2 * g + r) * LANES:(2 * g + r + 1) * LANES] = _rms(d, subln_ref[...]) * (1.0 - lam_init)


def _diff_sample(page_table, layer, n_new, aq, ak_new, av_new, cache_k, cache_v, lam_p, subln, lam_init, name):
    db, n_pages = page_table.shape
    const = lambda shape: pl.BlockSpec(shape, lambda b, pt: (0,) * len(shape))
    in_specs = ([const((4, HD)), const((1, 2 * HD)), _row_spec(512), _row_spec(256), _row_spec(256)]
                + _page_specs(n_pages, PAGE_SIZE * A_KV, layer) + _page_specs(n_pages, PAGE_SIZE * A_KV, layer))
    grid_spec = pltpu.PrefetchScalarGridSpec(num_scalar_prefetch=1, grid=(db // SEQS,), in_specs=in_specs,
                                             out_specs=_row_spec(512))
    return pl.pallas_call(
        _per_sequence(functools.partial(_diff_s_body, n_pages, n_new, lam_init), 2, 3, n_pages),
        grid_spec=grid_spec,
        out_shape=jax.ShapeDtypeStruct((db, SROWS, 512), F32),
        compiler_params=_params("parallel"),
        name=name,
    )(page_table, lam_p, subln, aq, ak_new, av_new, *([cache_k] * (SEQS * n_pages)), *([cache_v] * (SEQS * n_pages)))


def _stack_heads(q_ref, n_heads, width, scale):
    return (jnp.concatenate([q_ref[:, h * width:(h + 1) * width] for h in range(n_heads)], axis=0)
            * scale).astype(BF16)


def _place_heads(o, place_ref, n_heads):
    out = None
    for h in range(n_heads):
        t = _dot(o[h * SROWS:(h + 1) * SROWS].astype(BF16), place_ref[h])
        out = t if out is None else out + t
    return out


def _place_matrices(n_heads, n_groups):
    rep = n_heads // n_groups
    p = np.zeros((n_heads, n_groups * HD, n_heads * HD), np.float32)
    for h in range(n_heads):
        for d in range(HD):
            p[h, (h // rep) * HD + d, h * HD + d] = 1.0
    return jnp.asarray(p, BF16)


def _moba_s_body(n_pages, n_new, pt_ref, place_ref, q_ref, kn_ref, vn_ref, *rest):
    k_refs = rest[:n_pages]
    v_refs = rest[n_pages:2 * n_pages]
    o_ref = rest[2 * n_pages]
    new_mask = _new_mask(C_HEADS * SROWS, n_new)
    knt = _pad_rows(kn_ref[...]).T.astype(BF16)
    vnt = _pad_rows(vn_ref[...]).T.astype(BF16)
    ppb = MOBA_BLOCK // PAGE_SIZE
    n_blocks = n_pages // ppb
    q = _stack_heads(q_ref, C_HEADS, C_KV * HD, HD ** -0.5)
    ss = [_dot(q, k_refs[p][...].astype(BF16)) for p in range(n_pages)] + [_dot(q, knt)]
    bs = []
    for j in range(n_blocks):
        acc = ss[j * ppb]
        for u in range(1, ppb):
            acc = acc + ss[j * ppb + u]
        bs.append(jnp.sum(acc, axis=1, keepdims=True))
    pieces = []
    for j in range(n_blocks):
        rank = jnp.zeros(bs[j].shape, F32)
        for i in range(n_blocks):
            if i < j:
                rank = rank + jnp.where(bs[i] >= bs[j], 1.0, 0.0)
            elif i > j:
                rank = rank + jnp.where(bs[i] > bs[j], 1.0, 0.0)
        keep = rank < float(MOBA_TOPK)
        for u in range(ppb):
            pieces.append((ss[j * ppb + u], keep))
    pieces.append((ss[n_pages], new_mask))
    ps, tot = _softmax_pieces(pieces)
    o = _dot_nt(ps[n_pages], vnt)
    for p in range(n_pages):
        o = o + _dot_nt(ps[p], v_refs[p][...].astype(BF16))
    o_ref[...] = _place_heads(o / tot, place_ref, C_HEADS)


def _moba_sample(page_table, layer, n_new, cq, ck_new, cv_new, cache_k, cache_v, name):
    db, n_pages = page_table.shape
    place = _place_matrices(C_HEADS, C_KV)
    in_specs = ([pl.BlockSpec(place.shape, lambda b, pt: (0, 0, 0)),
                 _row_spec(C_HEADS * C_KV * HD), _row_spec(256), _row_spec(256)]
                + _page_specs(n_pages, C_KV * HD, layer) + _page_specs(n_pages, C_KV * HD, layer))
    grid_spec = pltpu.PrefetchScalarGridSpec(num_scalar_prefetch=1, grid=(db // SEQS,), in_specs=in_specs,
                                             out_specs=_row_spec(512))
    return pl.pallas_call(
        _per_sequence(functools.partial(_moba_s_body, n_pages, n_new), 1, 3, n_pages),
        grid_spec=grid_spec,
        out_shape=jax.ShapeDtypeStruct((db, SROWS, 512), F32),
        compiler_params=_params("parallel"),
        name=name,
    )(page_table, place, cq, ck_new, cv_new,*([cache_k] * (SEQS * n_pages)), *([cache_v] * (SEQS * n_pages)))


def _compress_s_body(n_pages, pt_ref, pek_ref, pev_ref, w1k_ref, w1v_ref, w2k_ref, w2v_ref, *rest):
    k_refs = rest[:n_pages]
    v_refs = rest[n_pages:2 * n_pages]
    ok_ref, ov_ref, xk_ref, xv_ref = rest[2 * n_pages:]
    for p in range(n_pages):
        xk_ref[p * PAGE_SIZE:(p + 1) * PAGE_SIZE, :] = k_refs[p][...].T
        xv_ref[p * PAGE_SIZE:(p + 1) * PAGE_SIZE, :] = v_refs[p][...].T
    _compress_one(xk_ref, pek_ref, w1k_ref, w2k_ref, ok_ref)
    _compress_one(xv_ref, pev_ref, w1v_ref, w2v_ref, ov_ref)


def _compress_sample(page_table, layer, cache_k, cache_v, cw, name):
    db, n_pages = page_table.shape
    const = lambda shape: pl.BlockSpec(shape, lambda b, pt: (0,) * len(shape))
    in_specs = ([const((CMP_BLOCK, 1, LANES))] * 2 + [const((CMP_BLOCK, LANES, 2 * CMP_HIDDEN))] * 2
                + [const((2 * CMP_HIDDEN, LANES))] * 2
                + _page_specs(n_pages, B_KV * HD, layer) + _page_specs(n_pages, B_KV * HD, layer))
    o_spec = _seq_spec(LANES, LANES)
    grid_spec = pltpu.PrefetchScalarGridSpec(
        num_scalar_prefetch=1, grid=(db // SEQS,), in_specs=in_specs, out_specs=[o_spec, o_spec],
        scratch_shapes=[pltpu.VMEM((SEQS, n_pages * PAGE_SIZE, LANES), F32)] * 2)
    return pl.pallas_call(
        _per_sequence(functools.partial(_compress_s_body, n_pages), 6, 0, n_pages),
        grid_spec=grid_spec,
        out_shape=[jax.ShapeDtypeStruct((db, LANES, LANES), F32)] * 2,
        compiler_params=_params("parallel"),
        name=name,
    )(page_table, cw["pek"], cw["pev"], cw["w1k"], cw["w1v"], cw["w2k"], cw["w2v"],
      *([cache_k] * (SEQS * n_pages)), *([cache_v] * (SEQS * n_pages)))


def _nsa_s_body(n_pages, n_new, q_pos0, n_win, pt_ref, place_ref, qc_ref, qr_ref, bg_ref, kcmp_ref, vcmp_ref,
                ksn_ref, vsn_ref, kwn_ref, vwn_ref, wk_ref, wv_ref, *rest):
    k_refs = rest[:n_pages]
    v_refs = rest[n_pages:2 * n_pages]
    o_ref = rest[2 * n_pages]
    lane = lax.broadcasted_iota(jnp.int32, (1, LANES), 1)
    halfi = lane >> 6
    rep = B_HEADS // B_KV
    m_rows = B_HEADS * SROWS
    new_mask = _new_mask(m_rows, n_new)
    trow8 = lax.broadcasted_iota(jnp.int32, (SROWS, 1), 0)
    trow = lax.broadcasted_iota(jnp.int32, (m_rows, 1), 0) & (SROWS - 1)
    qpos8 = q_pos0 + trow8
    ns = -(-(q_pos0 + n_new) // SEL_BLOCK)
    bpp = PAGE_SIZE // SEL_BLOCK
    transposed = lambda ref: _pad_rows(ref[...]).T.astype(BF16)
    ksnt, vsnt, kwnt, vwnt = transposed(ksn_ref), transposed(vsn_ref), transposed(kwn_ref), transposed(vwn_ref)
    kcmp = kcmp_ref[...].astype(BF16)
    vcmp = vcmp_ref[...].astype(BF16)
    ci = lax.broadcasted_iota(jnp.int32, (LANES, LANES), 0)
    cj = lax.broadcasted_iota(jnp.int32, (LANES, LANES), 1)
    onehot = jnp.where((ci >> CMP_PER_SEL_SHIFT) == cj, 1.0, 0.0).astype(BF16)
    sig = jax.nn.sigmoid(bg_ref[...])
    per_group = lambda xs: jnp.concatenate([x for x in xs for _ in range(rep)], axis=0)
    qc = _stack_heads(qc_ref, B_HEADS, B_KV * HD, HD ** -0.5)
    qr = _stack_heads(qr_ref, B_HEADS, B_KV * HD, HD ** -0.5)

    cm8 = jnp.where((lane * CMP_STRIDE + (CMP_BLOCK - 1)) <= qpos8, 1.0, 0.0)
    cmask = jnp.concatenate([cm8] * B_HEADS, axis=0) > 0.5
    s = jnp.where(cmask, _dot_nt(qc, kcmp), NEG)
    m = jnp.max(s, axis=1, keepdims=True)
    p = jnp.where(cmask, jnp.exp(s - m), 0.0)
    p = p / jnp.maximum(jnp.sum(p, axis=1, keepdims=True), 1e-30)
    o_cmp = _dot(p.astype(BF16), vcmp)
    cur = qpos8 >> 6
    sels = []
    for g in range(B_KV):
        psum = p[g * rep * SROWS:(g * rep + 1) * SROWS]
        for r in range(1, rep):
            psum = psum + p[(g * rep + r) * SROWS:(g * rep + r + 1) * SROWS]
        hi, mid, lo = _split3(psum)
        imp = _dot(hi, onehot) + _dot(mid, onehot) + _dot(lo, onehot)
        impm = jnp.where(lane == cur, jnp.inf, jnp.where(lane < cur, imp, -jnp.inf))
        rank = _rank_desc(impm, ns, lane)
        sels.append(jnp.where(rank < float(SEL_TOPK), jnp.where(lane <= cur, 1.0, 0.0), 0.0))

    ss = [_dot(qr, k_refs[pg][...].astype(BF16)) for pg in range(n_pages)] + [_dot(qr, ksnt)]
    pieces = []
    for pg in range(n_pages):
        mks = []
        for sel in sels:
            mk = sel[:, pg * bpp:pg * bpp + 1]
            for u in range(1, bpp):
                mk = jnp.where(halfi >= u, sel[:, pg * bpp + u:pg * bpp + u + 1], mk)
            mks.append(mk)
        pieces.append((ss[pg], per_group(mks) > 0.5))
    pieces.append((ss[n_pages], new_mask))
    ps, tot = _softmax_pieces(pieces)
    o_slc = _dot_nt(ps[n_pages], vsnt)
    for pg in range(n_pages):
        o_slc = o_slc + _dot_nt(ps[pg], v_refs[pg][...].astype(BF16))
    o_slc = o_slc / tot

    wcols = [slice(u * LANES, (u + 1) * LANES) for u in range(n_win // LANES)]
    ss = [_dot(qr, wk_ref[:, c].astype(BF16)) for c in wcols] + [_dot(qr, kwnt)]
    pieces = []
    for u in range(len(wcols)):
        dist = (n_win - u * LANES) + trow - lane
        pieces.append((ss[u], dist <= WINDOW))
    pieces.append((ss[len(wcols)], new_mask))
    ps, tot = _softmax_pieces(pieces)
    o_win = _dot_nt(ps[len(wcols)], vwnt)
    for u, c in enumerate(wcols):
        o_win = o_win + _dot_nt(ps[u], wv_ref[:, c].astype(BF16))
    o_win = o_win / tot

    gate = lambda c: jnp.concatenate([sig[:, h * 3 + c:h * 3 + c + 1] for h in range(B_HEADS)], axis=0)
    o = gate(0) * o_cmp + gate(1) * o_slc + gate(2) * o_win
    o_ref[...] = _place_heads(o, place_ref, B_HEADS)


def _nsa_sample(page_table, layer, n_new, q_pos0, bqc, bqr, bg, kcmp, vcmp, ks_new, vs_new, kw_new, vw_new,
                state_k, state_v, cache_k, cache_v, name):
    db, n_pages = page_table.shape
    n_win = state_k.shape[-1]
    place = _place_matrices(B_HEADS, B_KV)
    w_spec = pl.BlockSpec((None, SEQS, B_KV * HD, n_win), lambda b, pt: (layer, b, 0, 0))
    c_spec = _seq_spec(LANES, LANES)
    q_width = B_HEADS * B_KV * HD
    in_specs = ([pl.BlockSpec(place.shape, lambda b, pt: (0, 0, 0)),
                 _row_spec(q_width), _row_spec(q_width), _row_spec(LANES), c_spec, c_spec]
                + [_row_spec(LANES)] * 4 + [w_spec, w_spec]
                + _page_specs(n_pages, B_KV * HD, layer) + _page_specs(n_pages, B_KV * HD, layer))
    grid_spec = pltpu.PrefetchScalarGridSpec(num_scalar_prefetch=1, grid=(db // SEQS,), in_specs=in_specs,
                                             out_specs=_row_spec(512))
    return pl.pallas_call(
        _per_sequence(functools.partial(_nsa_s_body, n_pages, n_new, q_pos0, n_win), 1, 11, n_pages),
        grid_spec=grid_spec,
        out_shape=jax.ShapeDtypeStruct((db, SROWS, 512), F32),
        compiler_params=_params("parallel"),
        name=name,
    )(page_table, place, bqc, bqr, bg, kcmp, vcmp, ks_new, vs_new, kw_new, vw_new, state_k, state_v,
      *([cache_k] * (SEQS * n_pages)), *([cache_v] * (SEQS * n_pages)))


def _layer_weights(l, norm_mix, norm_ffn, norm_final, w_in, a_lambda, a_subln, b_cmp_pe, b_cmp_w1, b_cmp_w2,
                   w_branch, w_out, peer_wq, peer_subkeys, peer_u, peer_v):
    w = w_in[l]
    bg_w = jnp.pad(w[:, 2304:2328], ((0, 0), (0, LANES - 3 * B_HEADS)))
    return {
        "norm_mix": norm_mix[l][None, :],
        "norm_ffn": norm_ffn[l][None, :],
        "norm_final": norm_final[None, :],
        "last_layer": l == w_in.shape[0] - 1,
        "w_a": w[:, 0:1024].astype(BF16),
        "w_b": jnp.concatenate([w[:, 1024:2304], bg_w], axis=1).astype(BF16),
        "w_c": w[:, 2328:3352].astype(BF16),
        "w_b_s": jnp.concatenate([_spread_heads(w[:, 1024:1536], B_HEADS, B_KV), w[:, 1536:2304], bg_w],
                                 axis=1).astype(BF16),
        "w_c_s": jnp.concatenate([_spread_heads(w[:, 2328:2840], C_HEADS, C_KV), w[:, 2840:3352]],
                                 axis=1).astype(BF16),
        "w_g": w[:, 3352:6424].astype(BF16),
        "lam": a_lambda[l],
        "subln": a_subln[l][None, :],
        "cmp": _compress_weights(b_cmp_pe[l], b_cmp_w1[l], b_cmp_w2[l]),
        "w_branch": w_branch[l].astype(BF16),
        "w_out": w_out[l].astype(BF16),
        "wq": peer_wq[l].astype(BF16),
        "sk": peer_subkeys[l].reshape(2 * PEER_HEADS, PEER_NKEYS, LANES).astype(BF16),
        "u": peer_u[l].astype(BF16),
        "vt": peer_v[l].T.astype(BF16),
        "lam_init": 0.8 - 0.6 * math.exp(-0.3 * l),
    }


def _project_all(x, lw, cos_t, sin_t, tag):
    aq, ak, av = _project(x, lw["norm_mix"], lw["w_a"], cos_t, sin_t, PIECES_A, WIDTHS_A, "proj_a_" + tag)
    if tag == "s":
        b_out = _project(x, lw["norm_mix"], lw["w_b_s"], cos_t, sin_t, PIECES_B_S, WIDTHS_B_S, "proj_b_s")
        cq, ck, cv = _project(x, lw["norm_mix"], lw["w_c_s"], cos_t, sin_t, PIECES_C_S, WIDTHS_C_S, "proj_c_s")
    else:
        b_out = _project(x, lw["norm_mix"], lw["w_b"], cos_t, sin_t, PIECES_B, WIDTHS_B, "proj_b_p")
        cq, ck, cv = _project(x, lw["norm_mix"], lw["w_c"], cos_t, sin_t, PIECES_C, WIDTHS_C, "proj_c_p")
    (gate,) = _project(x, lw["norm_mix"], lw["w_g"], cos_t, sin_t, PIECES_G, WIDTHS_G, "proj_g_" + tag)
    return (aq, ak, av), b_out, (cq, ck, cv), gate


def _ffn(x, lw, tag):
    st, stat = _peer_stats(x, lw["norm_ffn"], lw["wq"], lw["sk"], "peer_stats_" + tag)
    return _peer_dense(x, lw["norm_ffn"], lw["norm_final"], lw["last_layer"], st, stat, lw["u"], lw["vt"],
                       "peer_dense_" + tag)


def _prompt_layer(x, lw, cos_t, sin_t, bn, t):
    (aq, ak, av), (bqc, bqr, bkc, bvc, bks, bvs, bkw, bvw, bg), (cq, ck, cv), gate = _project_all(
        x, lw, cos_t, sin_t, "p")
    r3 = lambda a: a.reshape(bn, t, a.shape[-1])
    o_a = _diff_attention(r3(aq), r3(ak), r3(av), lw["lam"], lw["subln"], lw["lam_init"], 0, "diff_p")
    kcmp, vcmp = _compress(r3(bkc), r3(bvc), lw["cmp"], "cmp_p")
    o_b = _nsa_attention(r3(bqc), r3(bqr), r3(bg), kcmp, vcmp, r3(bks), r3(bvs), r3(bkw), r3(bvw),
                         0, t, 0, t, "nsa_p")
    o_c = _moba_attention(r3(cq), r3(ck), r3(cv), 0, "moba_p")
    n = bn * t
    x = _merge(x, gate, o_a.reshape(n, 512), o_b.reshape(n, 512), o_c.reshape(n, 512),
               lw["w_branch"], lw["w_out"], "merge_p")
    x = _ffn(x, lw, "p")
    wp = min(WINDOW, t)
    rows = (r3(ak).reshape(bn, t, A_KV, 2 * HD), r3(av).reshape(bn, t, A_KV, 2 * HD),
            r3(bkc).reshape(bn, t, B_KV, HD), r3(bvc).reshape(bn, t, B_KV, HD),
            r3(bks).reshape(bn, t, B_KV, HD), r3(bvs).reshape(bn, t, B_KV, HD),
            r3(ck).reshape(bn, t, C_KV, HD), r3(cv).reshape(bn, t, C_KV, HD),
            r3(bkw)[:, t - wp:].reshape(bn, wp, B_KV, HD), r3(bvw)[:, t - wp:].reshape(bn, wp, B_KV, HD))
    return x, rows


def _sample_layer(x, lw, l, cos_t, sin_t, db, t, past, caches, state_wk, state_wv, page_table):
    x = jnp.pad(x.reshape(db, t, D_MODEL), ((0, 0), (0, SROWS - t), (0, 0))).reshape(db * SROWS, D_MODEL)
    (aq, ak, av), (bqc, bqr, bkc, bvc, bks, bvs, bkw, bvw, bg), (cq, ck, cv), gate = _project_all(
        x, lw, cos_t, sin_t, "s")
    r3 = lambda a: a.reshape(db, SROWS, a.shape[-1])
    c_ak, c_av, c_bkc, c_bvc, c_bks, c_bvs, c_ck, c_cv = caches
    o_a = _diff_sample(page_table, l, t, r3(aq), r3(ak), r3(av), c_ak, c_av, lw["lam"], lw["subln"],
                       lw["lam_init"], "diff_s")
    kcmp, vcmp = _compress_sample(page_table, l, c_bkc, c_bvc, lw["cmp"], "cmp_s")
    o_b = _nsa_sample(page_table, l, t, past, r3(bqc), r3(bqr), r3(bg), kcmp, vcmp, r3(bks), r3(bvs),
                      r3(bkw), r3(bvw), state_wk, state_wv, c_bks, c_bvs, "nsa_s")
    o_c = _moba_sample(page_table, l, t, r3(cq), r3(ck), r3(cv), c_ck, c_cv, "moba_s")
    n = db * SROWS
    x = _merge(x, gate, o_a.reshape(n, 512), o_b.reshape(n, 512), o_c.reshape(n, 512),
               lw["w_branch"], lw["w_out"], "merge_s")
    x = _ffn(x.reshape(db, SROWS, D_MODEL)[:, :t].reshape(db * t, D_MODEL), lw, "s")
    new = lambda a, kv, w: r3(a)[:, :t].reshape(db, t, kv, w)

    def rolled(state_t, a):
        new_t = jnp.transpose(r3(a)[:, :t], (0, 2, 1))
        out_t = jnp.concatenate([state_t[l][:, :, t:], new_t], axis=2)
        return jnp.transpose(out_t, (0, 2, 1)).reshape(db, out_t.shape[2], B_KV, HD)

    rows = (new(ak, A_KV, 2 * HD), new(av, A_KV, 2 * HD), new(bkc, B_KV, HD), new(bvc, B_KV, HD),
            new(bks, B_KV, HD), new(bvs, B_KV, HD), new(ck, C_KV, HD), new(cv, C_KV, HD),
            rolled(state_wk, bkw), rolled(state_wv, bvw))
    return x, rows


def kernel(x_prompt, x_sample, cache_a_k, cache_a_v, cache_b_kc, cache_b_vc, cache_b_ks, cache_b_vs,
           cache_c_k, cache_c_v, state_b_wk, state_b_wv, page_table, norm_mix, norm_ffn, norm_final,
           w_in, a_lambda, a_subln, b_cmp_pe, b_cmp_w1, b_cmp_w2, w_branch, w_out,
           peer_wq, peer_subkeys, peer_u, peer_v):
    bn, t, _ = x_prompt.shape
    db, ts, _ = x_sample.shape
    depth = w_in.shape[0]
    past = page_table.shape[1] * PAGE_SIZE
    wb = state_b_wk.shape[2]
    assert past % MOBA_BLOCK == 0 and ts <= SROWS and wb == WINDOW and wb % LANES == 0 and db % SEQS == 0
    rows_view = lambda c: c.reshape(c.shape[0], c.shape[1], c.shape[2] * c.shape[3], c.shape[4])
    lanes_view = lambda c: jnp.transpose(c, (0, 1, 3, 4, 2)).reshape(
        c.shape[0], c.shape[1], c.shape[3] * c.shape[4], c.shape[2])
    caches = ([rows_view(c) for c in (cache_a_k, cache_a_v)]
              + [lanes_view(c) for c in (cache_b_kc, cache_b_vc, cache_b_ks, cache_b_vs, cache_c_k, cache_c_v)])
    swk = lanes_view(state_b_wk)
    swv = lanes_view(state_b_wv)

    cos_p, sin_p = _rope_tables(jnp.arange(t, dtype=jnp.int32))
    tm_s = min(PROJ_TILE, db * SROWS)
    cos_s, sin_s = _rope_tables(past + (jnp.arange(tm_s, dtype=jnp.int32) % SROWS))

    xp = x_prompt.reshape(bn * t, D_MODEL)
    xs = x_sample.reshape(db * ts, D_MODEL)
    rows_p, rows_s = [], []
    for l in range(depth):
        lw = _layer_weights(l, norm_mix, norm_ffn, norm_final, w_in, a_lambda, a_subln, b_cmp_pe, b_cmp_w1,
                            b_cmp_w2, w_branch, w_out, peer_wq, peer_subkeys, peer_u, peer_v)
        xp, rp = _prompt_layer(xp, lw, cos_p, sin_p, bn, t)
        xs, rs = _sample_layer(xs, lw, l, cos_s, sin_s, db, ts, past, caches, swk, swv, page_table)
        rows_p.append(rp)
        rows_s.append(rs)
    y_prompt = xp.reshape(bn, t, D_MODEL)
    y_sample = xs.reshape(db, ts, D_MODEL)
    outs_p = [jnp.stack(r, axis=0) for r in zip(*rows_p)]
    outs_s = [jnp.stack(r, axis=0) for r in zip(*rows_s)]
    return (y_prompt, y_sample, *outs_p, *outs_s)
```

```python
import functools
import math

import jax
import jax.numpy as jnp
import numpy as np
from jax import lax
from jax.experimental import pallas as pl
from jax.experimental.pallas import tpu as pltpu

F32 = jnp.float32
BF16 = jnp.bfloat16

D_MODEL = 1024
HD = 64
A_HEADS, A_KV = 4, 2
B_HEADS, B_KV = 8, 2
C_HEADS, C_KV = 8, 4
CMP_BLOCK, CMP_STRIDE, CMP_HIDDEN = 32, 16, 128
SEL_BLOCK, SEL_TOPK = 64, 8
CMP_PER_SEL_SHIFT = 2
WINDOW = 512
MOBA_BLOCK, MOBA_TOPK = 256, 3
PEER_HEADS, PEER_NKEYS, PEER_TOPK = 8, 128, 16
ROPE_THETA = 10000.0
EPS = 1e-6
PAGE_SIZE = 128

LANES = 128
NEG = -1e30
QK_SCALE_LOG2 = math.log2(math.e) * HD ** -0.5
VMEM_LIMIT = 48 * 1024 * 1024
ROW_TILE = 256
PROJ_TILE = 512
KV_CHUNK = 256
EXPERT_TILE = 4096
EXPERT_SUB = 256


def _params(*sem):
    return pltpu.CompilerParams(dimension_semantics=sem, vmem_limit_bytes=VMEM_LIMIT)


def _dot(a, b):
    return jnp.dot(a, b, preferred_element_type=F32)


def _dot_nt(a, b):
    return lax.dot_general(a, b, (((1,), (1,)), ((), ())), preferred_element_type=F32)


def _split3(a):
    hi = a.astype(BF16)
    r1 = a - hi.astype(F32)
    mid = r1.astype(BF16)
    lo = (r1 - mid.astype(F32)).astype(BF16)
    return hi, mid, lo


def _rms(x, w):
    return x * lax.rsqrt(jnp.mean(x * x, axis=-1, keepdims=True) + EPS) * w


def _proj_body(pieces, chunk, x_ref, nw_ref, w_ref, cos_ref, sin_ref, *outs):
    h = _rms(x_ref[...], nw_ref[...]).astype(BF16)
    cos = cos_ref[...]
    sin = sin_ref[...]
    lane = lax.broadcasted_iota(jnp.int32, (1, LANES), 1)
    first = (lane & (HD - 1)) < (HD // 2)
    n_cols = len(pieces) * LANES
    for c0 in range(0, n_cols, chunk):
        w = min(chunk, n_cols - c0)
        z = _dot(h, w_ref[:, c0:c0 + w])
        for p in range(w // LANES):
            zp = z[:, p * LANES:(p + 1) * LANES]
            for (oi, oc, rope) in pieces[c0 // LANES + p]:
                if rope:
                    rot = jnp.where(first, pltpu.roll(zp, LANES - HD // 2, 1), pltpu.roll(zp, HD // 2, 1))
                    outs[oi][:, oc:oc + LANES] = zp * cos + rot * sin
                else:
                    outs[oi][:, oc:oc + LANES] = zp


def _project(x, nw, w, cos_t, sin_t, pieces, out_widths, name):
    n = x.shape[0]
    tm = min(PROJ_TILE, n)
    ntab = cos_t.shape[0] // tm
    ncols = w.shape[1]
    body = functools.partial(_proj_body, pieces, 512)
    return pl.pallas_call(
        body,
        grid=(n // tm,),
        in_specs=[
            pl.BlockSpec((tm, D_MODEL), lambda i: (i, 0)),
            pl.BlockSpec((1, D_MODEL), lambda i: (0, 0)),
            pl.BlockSpec((D_MODEL, ncols), lambda i: (0, 0)),
            pl.BlockSpec((tm, LANES), lambda i: (i % ntab, 0)),
            pl.BlockSpec((tm, LANES), lambda i: (i % ntab, 0)),
        ],
        out_specs=[pl.BlockSpec((tm, ow), lambda i: (i, 0)) for ow in out_widths],
        out_shape=[jax.ShapeDtypeStruct((n, ow), F32) for ow in out_widths],
        compiler_params=_params("parallel"),
        name=name,
    )(x, nw, w, cos_t, sin_t)


def _plain(oi, width):
    return [[(oi, c, False)] for c in range(0, width, LANES)]


def _roped(oi, width):
    return [[(oi, c, True)] for c in range(0, width, LANES)]


PIECES_A = _roped(0, 512) + _roped(1, 256) + _plain(2, 256)
WIDTHS_A = (512, 256, 256)
PIECES_B = ([[(0, c, False), (1, c, True)] for c in range(0, 512, LANES)]
            + _plain(2, 128) + _plain(3, 128) + _roped(4, 128) + _plain(5, 128)
            + _roped(6, 128) + _plain(7, 128) + _plain(8, 128))
WIDTHS_B = (512, 512, 128, 128, 128, 128, 128, 128, 128)
PIECES_C = _roped(0, 512) + _roped(1, 256) + _plain(2, 256)
WIDTHS_C = (512, 256, 256)
PIECES_G = _plain(0, 3072)
WIDTHS_G = (3072,)
SPREAD_B = B_HEADS * B_KV * HD
SPREAD_C = C_HEADS * C_KV * HD
PIECES_B_S = [[(0, c, False), (1, c, True)] for c in range(0, SPREAD_B, LANES)] + PIECES_B[512 // LANES:]
WIDTHS_B_S = (SPREAD_B, SPREAD_B) + WIDTHS_B[2:]
PIECES_C_S = _roped(0, SPREAD_C) + _roped(1, 256) + _plain(2, 256)
WIDTHS_C_S = (SPREAD_C, 256, 256)


def _spread_heads(wq, n_heads, n_groups):
    rep = n_heads // n_groups
    onehot = (np.arange(n_heads)[:, None] // rep == np.arange(n_groups)[None, :]).astype(np.float32)
    w = wq.reshape(wq.shape[0], n_heads, HD)
    return jnp.einsum("dhk,hg->dhgk", w, jnp.asarray(onehot)).reshape(wq.shape[0], n_heads * n_groups * HD)


def _rope_tables(pos):
    half = HD // 2
    freqs = jnp.power(ROPE_THETA, -jnp.arange(half, dtype=F32) / half)
    ang = pos.astype(F32)[:, None] * freqs[None, :]
    cos = jnp.cos(ang)
    sin = jnp.sin(ang)
    cos_t = jnp.concatenate([cos, cos, cos, cos], axis=1)
    sin_t = jnp.concatenate([-sin, sin, -sin, sin], axis=1)
    return cos_t, sin_t


def _flash_init(m_ref, l_ref, acc_ref):
    m_ref[...] = jnp.full(m_ref.shape, NEG, F32)
    l_ref[...] = jnp.zeros(l_ref.shape, F32)
    acc_ref[...] = jnp.zeros(acc_ref.shape, F32)


def _lane_blocks(x):
    return [x[:, c:c + LANES] for c in range(0, x.shape[1], LANES)]


def _flash_step(hh, q_emb, kc, vc, biases, m_ref, l_ref, acc_ref):
    blocks = _lane_blocks(_dot_nt(q_emb, kc))
    if biases is not None:
        blocks = [b if bi is None else b + bi for b, bi in zip(blocks, biases)]
    mx = blocks[0]
    for b in blocks[1:]:
        mx = jnp.maximum(mx, b)
    m_old = m_ref[hh]
    m_new = jnp.maximum(m_old, jnp.max(mx, axis=1, keepdims=True))
    alpha = jnp.exp2(m_old - m_new)
    ps = [jnp.exp2(b - m_new) for b in blocks]
    tot = ps[0]
    for p in ps[1:]:
        tot = tot + p
    l_ref[hh] = alpha * l_ref[hh] + jnp.sum(tot, axis=1, keepdims=True)
    acc_ref[hh] = alpha * acc_ref[hh] + _dot(jnp.concatenate(ps, axis=1).astype(BF16), vc)
    m_ref[hh] = m_new


def _causal_sweep(q0, tq, body):
    n_full = (q0 + 1) // KV_CHUNK
    n_quad = n_full // 4
    n_pair = (n_full - 4 * n_quad) // 2
    lax.fori_loop(0, n_quad, body(4 * KV_CHUNK, False), 0)
    lax.fori_loop(2 * n_quad, 2 * n_quad + n_pair, body(2 * KV_CHUNK, False), 0)
    lax.fori_loop(4 * n_quad + 2 * n_pair, n_full, body(KV_CHUNK, False), 0)
    lax.fori_loop(n_full, (q0 + tq - 1) // KV_CHUNK + 1, body(KV_CHUNK, True), 0)


def _causal_bias(j, tk, qpos, inside=0.0):
    kpos = j * tk + lax.broadcasted_iota(jnp.int32, (1, tk), 1)
    return jnp.where(kpos <= qpos, inside, NEG)


def _flash_out(hh, l_ref, acc_ref):
    return acc_ref[hh] / jnp.maximum(l_ref[hh], 1e-30)


def _chunk(ref, j, tk):
    return ref[pl.ds(pl.multiple_of(j * tk, tk), tk), :].astype(BF16)


def _diff_body(tq, tk, q_pos0, lam_init, lam_ref, subln_ref, q_ref, k_ref, v_ref, o_ref,
               m_ref, l_ref, acc_ref):
    qi = pl.program_id(2)
    lane = lax.broadcasted_iota(jnp.int32, (1, LANES), 1)
    upper = lane >= HD
    q0 = q_pos0 + qi * tq
    qpos = q0 + lax.broadcasted_iota(jnp.int32, (tq, 1), 0)
    _flash_init(m_ref, l_ref, acc_ref)
    qs = []
    for r in range(2):
        qb = q_ref[:, r * LANES:(r + 1) * LANES] * QK_SCALE_LOG2
        qs.append(jnp.where(upper, 0.0, qb).astype(BF16))
        qs.append(jnp.where(upper, qb, 0.0).astype(BF16))
    def sweep(tk, diag):
        def body(j, carry):
            kc = _chunk(k_ref, j, tk)
            vc = _chunk(v_ref, j, tk)
            biases = _lane_blocks(_causal_bias(j, tk, qpos)) if diag else None
            for hh in range(4):
                _flash_step(hh, qs[hh], kc, vc, biases, m_ref, l_ref, acc_ref)
            return carry
        return body

    _causal_sweep(q0, tq, sweep)
    lp = lam_ref[...]
    lam = (jnp.exp(jnp.sum(lp[0:1] * lp[1:2], axis=1, keepdims=True))
           - jnp.exp(jnp.sum(lp[2:3] * lp[3:4], axis=1, keepdims=True)) + lam_init)
    for r in range(2):
        o = _flash_out(2 * r, l_ref, acc_ref) - lam * _flash_out(2 * r + 1, l_ref, acc_ref)
        o_ref[:, r * LANES:(r + 1) * LANES] = _rms(o, subln_ref[...]) * (1.0 - lam_init)


def _diff_attention(aq, ak, av, lam_p, subln, lam_init, q_pos0, name):
    bn, tq_all, _ = aq.shape
    lk = ak.shape[1]
    tq = min(ROW_TILE, tq_all)
    tk = KV_CHUNK
    body = functools.partial(_diff_body, tq, tk, q_pos0, lam_init)
    return pl.pallas_call(
        body,
        grid=(bn, A_KV, tq_all // tq),
        in_specs=[
            pl.BlockSpec((4, HD), lambda b, g, i: (0, 0)),
            pl.BlockSpec((1, 2 * HD), lambda b, g, i: (0, 0)),
            pl.BlockSpec((None, tq, 256), lambda b, g, i: (b, i, g)),
            pl.BlockSpec((None, lk, LANES), lambda b, g, i: (b, 0, g)),
            pl.BlockSpec((None, lk, LANES), lambda b, g, i: (b, 0, g)),
        ],
        out_specs=pl.BlockSpec((None, tq, 256), lambda b, g, i: (b, i, g)),
        out_shape=jax.ShapeDtypeStruct((bn, tq_all, 512), F32),
        scratch_shapes=[pltpu.VMEM((4, tq, LANES), F32), pltpu.VMEM((4, tq, LANES), F32),
                        pltpu.VMEM((4, tq, LANES), F32)],
        compiler_params=_params("parallel", "parallel", "arbitrary"),
        name=name,
    )(lam_p, subln, aq, ak, av)


def _compress_one(x_ref, pe_ref, w1_ref, w2_ref, o_ref):
    n_half = CMP_STRIDE
    xs = [x_ref[pl.ds(p, LANES, stride=CMP_STRIDE), :] for p in range(n_half)]
    xa = jnp.concatenate([(x + pe_ref[p]).astype(BF16) for p, x in enumerate(xs)], axis=1)
    xb = jnp.concatenate([(x + pe_ref[n_half + p]).astype(BF16) for p, x in enumerate(xs)], axis=1)
    w1 = w1_ref[...].reshape(2, n_half * LANES, 2 * CMP_HIDDEN)
    h = _dot(xa, w1[0]) + pltpu.roll(_dot(xb, w1[1]), LANES - 1, 0)
    y = _dot(jax.nn.gelu(h).astype(BF16), w2_ref[...])
    row = lax.broadcasted_iota(jnp.int32, (LANES, 1), 0)
    o_ref[...] = jnp.where(row < LANES - 1, y, 0.0)


def _compress_body(xk_ref, xv_ref, pek_ref, pev_ref, w1k_ref, w1v_ref, w2k_ref, w2v_ref, ok_ref, ov_ref):
    _compress_one(xk_ref, pek_ref, w1k_ref, w2k_ref, ok_ref)
    _compress_one(xv_ref, pev_ref, w1v_ref, w2v_ref, ov_ref)


def _compress(xk, xv, cw, name):
    bn = xk.shape[0]
    n_tok = LANES * CMP_STRIDE
    x_spec = pl.BlockSpec((None, n_tok, LANES), lambda b: (b, 0, 0))
    pe_spec = pl.BlockSpec((CMP_BLOCK, 1, LANES), lambda b: (0, 0, 0))
    w1_spec = pl.BlockSpec((CMP_BLOCK, LANES, 2 * CMP_HIDDEN), lambda b: (0, 0, 0))
    w2_spec = pl.BlockSpec((2 * CMP_HIDDEN, LANES), lambda b: (0, 0))
    o_spec = pl.BlockSpec((None, LANES, LANES), lambda b: (b, 0, 0))
    return pl.pallas_call(
        _compress_body,
        grid=(bn,),
        in_specs=[x_spec, x_spec, pe_spec, pe_spec, w1_spec, w1_spec, w2_spec, w2_spec],
        out_specs=[o_spec, o_spec],
        out_shape=[jax.ShapeDtypeStruct((bn, LANES, LANES), F32)] * 2,
        compiler_params=_params("parallel"),
        name=name,
    )(xk, xv, cw["pek"], cw["pev"], cw["w1k"], cw["w1v"], cw["w2k"], cw["w2v"])


def _compress_weights(pe, w1, w2):
    out = {}
    for idx, tag in ((0, "k"), (1, "v")):
        w1p = w1[idx].reshape(CMP_BLOCK, HD, CMP_HIDDEN)
        z = jnp.zeros_like(w1p)
        w1b = jnp.concatenate([jnp.concatenate([w1p, z], axis=2), jnp.concatenate([z, w1p], axis=2)], axis=1)
        z2 = jnp.zeros_like(w2[idx])
        w2b = jnp.concatenate([jnp.concatenate([w2[idx], z2], axis=1), jnp.concatenate([z2, w2[idx]], axis=1)], axis=0)
        out["w1" + tag] = w1b.astype(BF16)
        out["w2" + tag] = w2b.astype(BF16)
        out["pe" + tag] = jnp.concatenate([pe[idx], pe[idx]], axis=1)[:, None, :]
    return out


def _rank_desc(vals, n, lane):
    rank = jnp.zeros(vals.shape, F32)
    for i in range(n):
        col = vals[:, i:i + 1]
        tie = jnp.where(lane > i, 1.0, 0.0)
        rank = rank + jnp.where(col > vals, 1.0, 0.0) + jnp.where(col == vals, tie, 0.0)
    return rank


def _nsa_body(tq, tk, q_pos0, ns, w_pos0, w_valid, lw,
              qc_ref, qr_ref, bg_ref, kcmp_ref, vcmp_ref, ks_ref, vs_ref, kw_ref, vw_ref, o_ref,
              m_ref, l_ref, acc_ref):
    g = pl.program_id(1)
    qi = pl.program_id(2)
    lane = lax.broadcasted_iota(jnp.int32, (1, LANES), 1)
    halfi = lane >> 6
    q0 = q_pos0 + qi * tq
    qpos = q0 + lax.broadcasted_iota(jnp.int32, (tq, 1), 0)
    rep = B_HEADS // B_KV

    def embed_all(ref, scale):
        out = []
        for pr in range(rep // 2):
            v = ref[:, pr * LANES:(pr + 1) * LANES] * scale
            swapped = pltpu.roll(v, HD, 1)
            for u in range(2):
                out.append(jnp.where(halfi == g, jnp.where(g == u, v, swapped), 0.0).astype(BF16))
        return out

    def pair_out(o_even, o_odd):
        aligned = jnp.where(g == 0, o_even, o_odd)
        other = jnp.where(g == 0, o_odd, o_even)
        return jnp.where(halfi == g, aligned, pltpu.roll(other, HD, 1))

    kcmp = kcmp_ref[...].astype(BF16)
    vcmp = vcmp_ref[...].astype(BF16)
    cmask = (lane * CMP_STRIDE + (CMP_BLOCK - 1)) <= qpos
    psum = jnp.zeros((tq, LANES), F32)
    o_cmp = []
    qsc = embed_all(qc_ref, HD ** -0.5)
    for r in range(rep):
        s = jnp.where(cmask, _dot_nt(qsc[r], kcmp), NEG)
        m = jnp.max(s, axis=1, keepdims=True)
        p = jnp.where(cmask, jnp.exp(s - m), 0.0)
        p = p / jnp.maximum(jnp.sum(p, axis=1, keepdims=True), 1e-30)
        psum = psum + p
        o_cmp.append(_dot(p.astype(BF16), vcmp))

    ci = lax.broadcasted_iota(jnp.int32, (LANES, LANES), 0)
    cj = lax.broadcasted_iota(jnp.int32, (LANES, LANES), 1)
    onehot = jnp.where((ci >> CMP_PER_SEL_SHIFT) == cj, 1.0, 0.0).astype(BF16)
    hi, mid, lo = _split3(psum)
    imp = _dot(hi, onehot) + _dot(mid, onehot) + _dot(lo, onehot)
    cur = qpos >> 6
    impm = jnp.where(lane == cur, jnp.inf, jnp.where(lane < cur, imp, -jnp.inf))
    rank = _rank_desc(impm, ns, lane)
    sel = jnp.where(rank < float(SEL_TOPK), jnp.where(lane <= cur, 1.0, 0.0), 0.0).astype(BF16)

    qsr = embed_all(qr_ref, QK_SCALE_LOG2)
    _flash_init(m_ref, l_ref, acc_ref)
    def sel_sweep(ck, diag):
        bpc = ck // SEL_BLOCK

        def body(j, carry):
            kc = _chunk(ks_ref, j, ck)
            vc = _chunk(vs_ref, j, ck)
            ei = lax.broadcasted_iota(jnp.int32, (LANES, ck), 0)
            el = lax.broadcasted_iota(jnp.int32, (LANES, ck), 1)
            expand = jnp.where(ei == j * bpc + (el >> 6), 1.0, 0.0).astype(BF16)
            bias = (_dot(sel, expand) - 1.0) * (-NEG)
            if diag:
                bias = _causal_bias(j, ck, qpos, bias)
            biases = _lane_blocks(bias)
            for r in range(rep):
                _flash_step(r, qsr[r], kc, vc, biases, m_ref, l_ref, acc_ref)
            return carry
        return body

    _causal_sweep(q0, tq, sel_sweep)
    o_slc = [_flash_out(r, l_ref, acc_ref) for r in range(rep)]

    _flash_init(m_ref, l_ref, acc_ref)
    w_lo = jnp.maximum(q0 - WINDOW - w_pos0, 0) // tk
    w_hi = jnp.minimum(q0 + tq - 1 - w_pos0, lw - 1) // tk + 1

    def win_body(j, carry):
        kc = _chunk(kw_ref, j, tk)
        vc = _chunk(vw_ref, j, tk)
        kidx = j * tk + lax.broadcasted_iota(jnp.int32, (1, tk), 1)
        dist = qpos - (kidx + w_pos0)
        inwin = jnp.where(dist >= 0, jnp.where(dist <= WINDOW, 0.0, NEG), NEG)
        biases = _lane_blocks(jnp.where(kidx < w_valid, inwin, NEG))
        for r in range(rep):
            _flash_step(r, qsr[r], kc, vc, biases, m_ref, l_ref, acc_ref)
        return carry

    lax.fori_loop(w_lo, w_hi, win_body, 0)

    sig = jax.nn.sigmoid(bg_ref[...])
    outs = []
    for r in range(rep):
        o_win = _flash_out(r, l_ref, acc_ref)
        base = (g * rep + r) * 3
        gates = [jnp.sum(jnp.where(lane == base + c, sig, 0.0), axis=1, keepdims=True) for c in range(3)]
        outs.append(gates[0] * o_cmp[r] + gates[1] * o_slc[r] + gates[2] * o_win)
    for pr in range(rep // 2):
        o_ref[:, pr * LANES:(pr + 1) * LANES] = pair_out(outs[2 * pr], outs[2 * pr + 1])


def _nsa_attention(bqc, bqr, bg, kcmp, vcmp, ks, vs, kw, vw, q_pos0, k_valid, w_pos0, w_valid, name):
    bn, tq_all, _ = bqc.shape
    lk = ks.shape[1]
    lw = kw.shape[1]
    tq = min(ROW_TILE, tq_all)
    tk = KV_CHUNK
    ns = -(-k_valid // SEL_BLOCK)
    body = functools.partial(_nsa_body, tq, tk, q_pos0, ns, w_pos0, w_valid, lw)
    q_spec = pl.BlockSpec((None, tq, 256), lambda b, g, i: (b, i, g))
    full = lambda rows: pl.BlockSpec((None, rows, LANES), lambda b, g, i: (b, 0, 0))
    return pl.pallas_call(
        body,
        grid=(bn, B_KV, tq_all // tq),
        in_specs=[q_spec, q_spec, pl.BlockSpec((None, tq, LANES), lambda b, g, i: (b, i, 0)),
                  full(LANES), full(LANES), full(lk), full(lk), full(lw), full(lw)],
        out_specs=pl.BlockSpec((None, tq, 256), lambda b, g, i: (b, i, g)),
        out_shape=jax.ShapeDtypeStruct((bn, tq_all, 512), F32),
        scratch_shapes=[pltpu.VMEM((4, tq, LANES), F32), pltpu.VMEM((4, tq, LANES), F32),
                        pltpu.VMEM((4, tq, LANES), F32)],
        compiler_params=_params("parallel", "parallel", "arbitrary"),
        name=name,
    )(bqc, bqr, bg, kcmp, vcmp, ks, vs, kw, vw)


def _moba_body(tq, tk, q_pos0, nb, q_ref, k_ref, v_ref, o_ref, km_ref, m_ref, l_ref, acc_ref):
    g = pl.program_id(1)
    qi = pl.program_id(2)
    gh = g % 2
    lane = lax.broadcasted_iota(jnp.int32, (1, LANES), 1)
    halfi = lane >> 6
    q0 = q_pos0 + qi * tq
    qpos = q0 + lax.broadcasted_iota(jnp.int32, (tq, 1), 0)
    cur = qpos >> 8
    rep = C_HEADS // C_KV

    @pl.when(qi == 0)
    def _():
        km_ref[...] = jnp.zeros(km_ref.shape, F32)
        km_ref[0:nb, :] = jnp.sum(k_ref[...].reshape(nb, MOBA_BLOCK, LANES), axis=1) * (1.0 / MOBA_BLOCK)

    km_hi, km_mid, _ = _split3(km_ref[...])

    qv = q_ref[...]
    qs = []
    sels = []
    swapped = pltpu.roll(qv, HD, 1)
    for r in range(rep):
        qe = jnp.where(halfi == gh, jnp.where(gh == r, qv, swapped), 0.0)
        q_hi, q_mid, _ = _split3(qe)
        s_blk = _dot_nt(q_hi, km_hi) + _dot_nt(q_hi, km_mid) + _dot_nt(q_mid, km_hi)
        sm = jnp.where(lane < cur, s_blk, -jnp.inf)
        rank = _rank_desc(sm, nb, lane)
        past = jnp.where(rank < float(MOBA_TOPK), jnp.where(lane < cur, 1.0, 0.0), 0.0)
        sels.append(jnp.where(lane == cur, 1.0, past))
        qs.append((qe * QK_SCALE_LOG2).astype(BF16))

    _flash_init(m_ref, l_ref, acc_ref)
    def sweep(ck, diag):
        bpc = ck // MOBA_BLOCK
        lpb = MOBA_BLOCK // LANES

        def body(j, carry):
            kc = _chunk(k_ref, j, ck)
            vc = _chunk(v_ref, j, ck)
            diag_biases = _lane_blocks(_causal_bias(j, ck, qpos)) if diag else None
            for r in range(rep):
                if diag:
                    biases = diag_biases
                else:
                    biases = []
                    for u in range(bpc):
                        col = jnp.sum(jnp.where(lane == j * bpc + u, sels[r], 0.0), axis=1, keepdims=True)
                        biases += [(col - 1.0) * (-NEG)] * lpb
                _flash_step(r, qs[r], kc, vc, biases, m_ref, l_ref, acc_ref)
            return carry
        return body

    _causal_sweep(q0, tq, sweep)
    o0 = _flash_out(0, l_ref, acc_ref)
    o1 = _flash_out(1, l_ref, acc_ref)
    aligned = jnp.where(gh == 0, o0, o1)
    other = jnp.where(gh == 0, o1, o0)
    o_ref[...] = jnp.where(halfi == gh, aligned, pltpu.roll(other, HD, 1))


def _moba_attention(cq, ck, cv, q_pos0, name):
    bn, tq_all, _ = cq.shape
    lk = ck.shape[1]
    tq = min(ROW_TILE, tq_all)
    tk = MOBA_BLOCK
    nb = lk // MOBA_BLOCK
    assert tq == MOBA_BLOCK == KV_CHUNK and q_pos0 % MOBA_BLOCK == 0
    body = functools.partial(_moba_body, tq, tk, q_pos0, nb)
    return pl.pallas_call(
        body,
        grid=(bn, C_KV, tq_all // tq),
        in_specs=[
            pl.BlockSpec((None, tq, LANES), lambda b, g, i: (b, i, g)),
            pl.BlockSpec((None, lk, LANES), lambda b, g, i: (b, 0, g // 2)),
            pl.BlockSpec((None, lk, LANES), lambda b, g, i: (b, 0, g // 2)),
        ],
        out_specs=pl.BlockSpec((None, tq, LANES), lambda b, g, i: (b, i, g)),
        out_shape=jax.ShapeDtypeStruct((bn, tq_all, 512), F32),
        scratch_shapes=[pltpu.VMEM((LANES, LANES), F32),
                        pltpu.VMEM((2, tq, LANES), F32), pltpu.VMEM((2, tq, LANES), F32),
                        pltpu.VMEM((2, tq, LANES), F32)],
        compiler_params=_params("parallel", "parallel", "arbitrary"),
        name=name,
    )(cq, ck, cv)


def _merge_body(x_ref, gate_ref, oa_ref, ob_ref, oc_ref, wb_ref, wo_ref, y_ref):
    y = jnp.zeros((x_ref.shape[0], D_MODEL), F32)
    for c, o_ref in enumerate((oa_ref, ob_ref, oc_ref)):
        br = _dot(o_ref[...].astype(BF16), wb_ref[c])
        y = y + jax.nn.sigmoid(gate_ref[:, c * D_MODEL:(c + 1) * D_MODEL]) * br
    y_ref[...] = x_ref[...] + _dot(y.astype(BF16), wo_ref[...])


def _merge(x, gate, o_a, o_b, o_c, wb, wo, name):
    n = x.shape[0]
    tm = min(PROJ_TILE, n)
    row = lambda w: pl.BlockSpec((tm, w), lambda i: (i, 0))
    return pl.pallas_call(
        _merge_body,
        grid=(n // tm,),
        in_specs=[row(D_MODEL), row(3 * D_MODEL), row(512), row(512), row(512),
                  pl.BlockSpec((3, 512, D_MODEL), lambda i: (0, 0, 0)),
                  pl.BlockSpec((D_MODEL, D_MODEL), lambda i: (0, 0))],
        out_specs=row(D_MODEL),
        out_shape=jax.ShapeDtypeStruct((n, D_MODEL), F32),
        compiler_params=_params("parallel"),
        name=name,
    )(x, gate, o_a, o_b, o_c, wb, wo)


def _bitonic_desc(vs):
    vs = list(vs)
    n = len(vs)
    k = 2
    while k <= n:
        j = k // 2
        while j >= 1:
            for i in range(n):
                partner = i ^ j
                if partner > i:
                    hi = jnp.maximum(vs[i], vs[partner])
                    lo = jnp.minimum(vs[i], vs[partner])
                    vs[i], vs[partner] = (hi, lo) if (i & k) == 0 else (lo, hi)
            j //= 2
        k *= 2
    return vs


def _pop_heads(cols, n_out):
    cols = list(cols)
    for k in range(n_out):
        m = jnp.max(cols[0], axis=0, keepdims=True)
        eq = cols[0] == m
        yield m, eq
        depth = min(len(cols), n_out - k)
        for d in range(depth - 1):
            cols[d] = jnp.where(eq, cols[d + 1], cols[d])
        cols[depth - 1] = jnp.where(eq, -jnp.inf, cols[depth - 1])


def _peer_stats_body(tm, x_ref, nw_ref, wq_ref, sk_ref, st_ref, stat_ref, top_ref, nxt_ref):
    h = _rms(x_ref[...], nw_ref[...]).astype(BF16)
    q = _dot(h, wq_ref[...]).astype(BF16)
    n_hc = 2 * PEER_HEADS
    for hc in range(n_hc):
        st_ref[hc] = _dot_nt(sk_ref[hc], q[:, hc * LANES:(hc + 1) * LANES])

    sub = 8
    n_out = PEER_TOPK + 1

    def top_body(hc, carry):
        for hf in range(tm // LANES):
            s = st_ref[hc, :, hf * LANES:(hf + 1) * LANES]
            cols = _bitonic_desc([s[d * sub:(d + 1) * sub] for d in range(PEER_NKEYS // sub)])
            vals = [m for m, _ in _pop_heads(cols, n_out)]
            top_ref[hc, :, hf * LANES:(hf + 1) * LANES] = jnp.concatenate(vals[:PEER_TOPK], axis=0)
            nxt_ref[hc, :, hf * LANES:(hf + 1) * LANES] = vals[PEER_TOPK]
        return carry

    lax.fori_loop(0, n_hc, top_body, 0)

    row8 = lax.broadcasted_iota(jnp.int32, (sub, LANES), 0)
    fill = jnp.full((sub, LANES), -jnp.inf, F32)

    def head_body(hd, carry):
        for hf in range(tm // LANES):
            sl = slice(hf * LANES, (hf + 1) * LANES)
            a = top_ref[2 * hd, :, sl]
            b = top_ref[2 * hd + 1, :, sl]
            cands = [a[0:1] + b[0:sub], a[0:1] + b[sub:2 * sub]]
            for i in range(1, sub):
                cands.append(jnp.where(row8 < PEER_TOPK // (i + 1), a[i:i + 1] + b[0:sub], -jnp.inf))
            cands.append(a[sub:2 * sub] + b[0:1])
            mx = a[0:1] + b[0:1]
            cum = jnp.zeros((1, LANES), F32)
            tau = mx
            nxt = mx
            cols = _bitonic_desc(cands + [fill] * (16 - len(cands)))
            for m, eq in _pop_heads(cols, n_out):
                tau = jnp.where(cum < float(PEER_TOPK), m, tau)
                nxt = jnp.where(cum < float(n_out), m, nxt)
                cum = cum + jnp.sum(jnp.where(eq, 1.0, 0.0), axis=0, keepdims=True)
            nxt = jnp.maximum(nxt, jnp.maximum(a[0:1] + nxt_ref[2 * hd + 1, :, sl], nxt_ref[2 * hd, :, sl] + b[0:1]))
            z = None
            for c in cands:
                t = jnp.sum(jnp.where(c >= tau, jnp.exp(c - mx), 0.0), axis=0, keepdims=True)
                z = t if z is None else z + t
            stat_ref[hd, :, sl] = 0.5 * (tau + nxt)
            stat_ref[PEER_HEADS + hd, :, sl] = a[0:1]
            stat_ref[2 * PEER_HEADS + hd, :, sl] = b[0:1]
            stat_ref[3 * PEER_HEADS + hd, :, sl] = 1.0 / z
        return carry

    lax.fori_loop(0, PEER_HEADS, head_body, 0)


def _peer_stats(x, nw, wq, sk, name):
    n = x.shape[0]
    tm = min(ROW_TILE, n)
    n_hc = 2 * PEER_HEADS
    return pl.pallas_call(
        functools.partial(_peer_stats_body, tm),
        grid=(n // tm,),
        in_specs=[pl.BlockSpec((tm, D_MODEL), lambda i: (i, 0)),
                  pl.BlockSpec((1, D_MODEL), lambda i: (0, 0)),
                  pl.BlockSpec((D_MODEL, n_hc * LANES), lambda i: (0, 0)),
                  pl.BlockSpec((n_hc, LANES, LANES), lambda i: (0, 0, 0))],
        out_specs=[pl.BlockSpec((n_hc, LANES, tm), lambda i: (0, 0, i)),
                   pl.BlockSpec((4 * PEER_HEADS, 1, tm), lambda i: (0, 0, i))],
        out_shape=[jax.ShapeDtypeStruct((n_hc, LANES, n), F32),
                   jax.ShapeDtypeStruct((4 * PEER_HEADS, 1, n), F32)],
        scratch_shapes=[pltpu.VMEM((n_hc, PEER_TOPK, tm), F32), pltpu.VMEM((n_hc, 1, tm), F32)],
        compiler_params=_params("parallel"),
        name=name,
    )(x, nw, wq, sk)


def _peer_dense_body(tm, te, last_layer, x_ref, nw_ref, fw_ref, st_ref, stat_ref, u_ref, vt_ref, o_ref,
                     h_ref, thr_ref, w1_ref, s2c_ref, e2_ref, acc_ref):
    e = pl.program_id(1)

    @pl.when(e == 0)
    def _():
        h_ref[...] = _rms(x_ref[...], nw_ref[...]).astype(BF16)
        acc_ref[...] = jnp.zeros(acc_ref.shape, F32)
        for hd in range(PEER_HEADS):
            tau = stat_ref[hd]
            m1 = stat_ref[PEER_HEADS + hd]
            m2 = stat_ref[2 * PEER_HEADS + hd]
            inv_z = stat_ref[3 * PEER_HEADS + hd]
            thr_ref[hd] = m1 - st_ref[2 * hd]
            w1_ref[hd] = jnp.exp(st_ref[2 * hd] - m1) * inv_z
            s2c_ref[hd] = (st_ref[2 * hd + 1] - (tau - m1)).astype(BF16)
            e2_ref[hd] = jnp.exp(st_ref[2 * hd + 1] - m2).astype(BF16)

    n_i = te // PEER_NKEYS
    zero = jnp.zeros((), BF16)
    h = h_ref[...]
    total = None
    for k in range(te // EXPERT_SUB):
        rows = slice(k * EXPERT_SUB, (k + 1) * EXPERT_SUB)
        act = jax.nn.gelu(_dot_nt(u_ref[rows, :], h)).astype(BF16)
        parts = []
        for ii in range(EXPERT_SUB // PEER_NKEYS):
            i = e * n_i + k * (EXPERT_SUB // PEER_NKEYS) + ii
            gmat = jnp.zeros((PEER_NKEYS, tm), BF16)
            for hd in range(PEER_HEADS):
                thr = thr_ref[hd, pl.ds(i, 1), :].astype(BF16)
                w1 = w1_ref[hd, pl.ds(i, 1), :].astype(BF16)
                gmat = gmat + jnp.where(s2c_ref[hd] > thr, e2_ref[hd] * w1, zero)
            parts.append(gmat * act[ii * PEER_NKEYS:(ii + 1) * PEER_NKEYS])
        t = _dot(vt_ref[:, rows], jnp.concatenate(parts, axis=0))
        total = t if total is None else total + t
    acc_ref[...] += total

    @pl.when(e == pl.num_programs(1) - 1)
    def _():
        y = x_ref[...] + acc_ref[...].T
        o_ref[...] = _rms(y, fw_ref[...]) if last_layer else y


def _peer_dense(x, nw, final_w, last_layer, st, stat, u_bf, vt_bf, name):
    n = x.shape[0]
    tm = min(ROW_TILE, n)
    te = EXPERT_TILE
    n_exp = u_bf.shape[0]
    n_hc = 2 * PEER_HEADS
    return pl.pallas_call(
        functools.partial(_peer_dense_body, tm, te, last_layer),
        grid=(n // tm, n_exp // te),
        in_specs=[pl.BlockSpec((tm, D_MODEL), lambda i, e: (i, 0)),
                  pl.BlockSpec((1, D_MODEL), lambda i, e: (0, 0)),
                  pl.BlockSpec((1, D_MODEL), lambda i, e: (0, 0)),
                  pl.BlockSpec((n_hc, LANES, tm), lambda i, e: (0, 0, i)),
                  pl.BlockSpec((4 * PEER_HEADS, 1, tm), lambda i, e: (0, 0, i)),
                  pl.BlockSpec((te, D_MODEL), lambda i, e: (e, 0)),
                  pl.BlockSpec((D_MODEL, te), lambda i, e: (0, e))],
        out_specs=pl.BlockSpec((tm, D_MODEL), lambda i, e: (i, 0)),
        out_shape=jax.ShapeDtypeStruct((n, D_MODEL), F32),
        scratch_shapes=[pltpu.VMEM((tm, D_MODEL), BF16),
                        pltpu.VMEM((PEER_HEADS, PEER_NKEYS, tm), F32),
                        pltpu.VMEM((PEER_HEADS, PEER_NKEYS, tm), F32),
                        pltpu.VMEM((PEER_HEADS, PEER_NKEYS, tm), BF16),
                        pltpu.VMEM((PEER_HEADS, PEER_NKEYS, tm), BF16),
                        pltpu.VMEM((D_MODEL, tm), F32)],
        compiler_params=_params("parallel", "arbitrary"),
        name=name,
    )(x, nw, final_w, st, stat, u_bf, vt_bf)


SROWS = 8
SEQS = 2


def _page_specs(n_pages, rows, layer):
    def spec(s, p):
        return pl.BlockSpec((None, None, rows, LANES), lambda b, pt: (layer, pt[b * SEQS + s, p], 0, 0))
    return [spec(s, p) for s in range(SEQS) for p in range(n_pages)]


def _row_spec(width):
    return pl.BlockSpec((SEQS, SROWS, width), lambda b, pt: (b, 0, 0))


def _seq_spec(rows, width):
    return pl.BlockSpec((SEQS, rows, width), lambda b, pt: (b, 0, 0))


def _per_sequence(body, n_const, n_seq, n_pages):
    def wrapped(pt_ref, *refs):
        consts = refs[:n_const]
        seqs = refs[n_const:n_const + n_seq]
        pages = refs[n_const + n_seq:n_const + n_seq + 2 * SEQS * n_pages]
        tail = refs[n_const + n_seq + 2 * SEQS * n_pages:]
        kp, vp = pages[:SEQS * n_pages], pages[SEQS * n_pages:]
        for s in range(SEQS):
            body(pt_ref, *consts, *[r.at[s] for r in seqs], *kp[s * n_pages:(s + 1) * n_pages],
                 *vp[s * n_pages:(s + 1) * n_pages], *[r.at[s] for r in tail])
    return wrapped


def _dup(x):
    return jnp.concatenate([x, x], axis=0)


def _pad_rows(x):
    return jnp.concatenate([x, jnp.zeros((PAGE_SIZE - x.shape[0], x.shape[1]), F32)], axis=0)


def _new_mask(m_rows, n_new):
    lane = lax.broadcasted_iota(jnp.int32, (1, LANES), 1)
    trow = lax.broadcasted_iota(jnp.int32, (m_rows, 1), 0) & (SROWS - 1)
    return jnp.where(lane < n_new, jnp.where(lane <= trow, 1.0, 0.0), 0.0) > 0.5


def _softmax_pieces(pieces):
    mx = None
    for s, mk in pieces:
        sm = s if mk is None else jnp.where(mk, s, NEG)
        mx = sm if mx is None else jnp.maximum(mx, sm)
    m = jnp.max(mx, axis=1, keepdims=True)
    ps = []
    tot = None
    for s, mk in pieces:
        p = jnp.exp(s - m)
        if mk is not None:
            p = jnp.where(mk, p, 0.0)
        tot = p if tot is None else tot + p
        ps.append(p.astype(BF16))
    return ps, jnp.sum(tot, axis=1, keepdims=True)


def _scores_kt(q, kts):
    return [_dot(q, _dup(kt.astype(BF16))) for kt in kts]


def _values_kt(ps, vts):
    o = None
    for p, vt in zip(ps, vts):
        t = _dot_nt(p, _dup(vt.astype(BF16)))
        o = t if o is None else o + t
    return o


def _diff_s_body(n_pages, n_new, lam_init, pt_ref, lam_ref, subln_ref, q_ref, kn_ref, vn_ref, *rest):
    k_refs = rest[:n_pages]
    v_refs = rest[n_pages:2 * n_pages]
    o_ref = rest[2 * n_pages]
    lane = lax.broadcasted_iota(jnp.int32, (1, LANES), 1)
    upper = lane >= HD
    new_mask = _new_mask(4 * SROWS, n_new)
    lp = lam_ref[...]
    lam = (jnp.exp(jnp.sum(lp[0:1] * lp[1:2], axis=1, keepdims=True))
           - jnp.exp(jnp.sum(lp[2:3] * lp[3:4], axis=1, keepdims=True)) + lam_init)
    for g in range(A_KV):
        parts = []
        for r in range(A_HEADS // A_KV):
            qb = q_ref[:, (2 * g + r) * LANES:(2 * g + r + 1) * LANES] * (HD ** -0.5)
            parts += [jnp.where(upper, 0.0, qb), jnp.where(upper, qb, 0.0)]
        q = jnp.concatenate(parts, axis=0).astype(BF16)
        pieces = [(_dot_nt(q, k_refs[p][pl.ds(g, PAGE_SIZE, stride=A_KV), :].astype(BF16)), None)
                  for p in range(n_pages)]
        knew = _pad_rows(kn_ref[:, g * LANES:(g + 1) * LANES]).astype(BF16)
        vnew = _pad_rows(vn_ref[:, g * LANES:(g + 1) * LANES]).astype(BF16)
        pieces.append((_dot_nt(q, knew), new_mask))
        ps, tot = _softmax_pieces(pieces)
        o = _dot(ps[n_pages], vnew)
        for p in range(n_pages):
            o = o + _dot(ps[p], v_refs[p][pl.ds(g, PAGE_SIZE, stride=A_KV), :].astype(BF16))
        o = o / tot
        for r in range(A_HEADS // A_KV):
            d = o[2 * r * SROWS:(2 * r + 1) * SROWS] - lam * o[(2 * r + 1) * SROWS:(2 * r + 2) * SROWS]
            o_ref[:, (2 * g + r) * LANES:(2 * g + r + 1) * LANES] = _rms(d, subln_ref[...]) * (1.0 - lam_init)


def _diff_sample(page_table, layer, n_new, aq, ak_new, av_new, cache_k, cache_v, lam_p, subln, lam_init, name):
    db, n_pages = page_table.shape
    const = lambda shape: pl.BlockSpec(shape, lambda b, pt: (0,) * len(shape))
    in_specs = ([const((4, HD)), const((1, 2 * HD)), _row_spec(512), _row_spec(256), _row_spec(256)]
                + _page_specs(n_pages, PAGE_SIZE * A_KV, layer) + _page_specs(n_pages, PAGE_SIZE * A_KV, layer))
    grid_spec = pltpu.PrefetchScalarGridSpec(num_scalar_prefetch=1, grid=(db // SEQS,), in_specs=in_specs,
                                             out_specs=_row_spec(512))
    return pl.pallas_call(
        _per_sequence(functools.partial(_diff_s_body, n_pages, n_new, lam_init), 2, 3, n_pages),
        grid_spec=grid_spec,
        out_shape=jax.ShapeDtypeStruct((db, SROWS, 512), F32),
        compiler_params=_params("parallel"),
        name=name,
    )(page_table, lam_p, subln, aq, ak_new, av_new, *([cache_k] * (SEQS * n_pages)), *([cache_v] * (SEQS * n_pages)))


def _stack_heads(q_ref, n_heads, width, scale):
    return (jnp.concatenate([q_ref[:, h * width:(h + 1) * width] for h in range(n_heads)], axis=0)
            * scale).astype(BF16)


def _place_heads(o, place_ref, n_heads):
    out = None
    for h in range(n_heads):
        t = _dot(o[h * SROWS:(h + 1) * SROWS].astype(BF16), place_ref[h])
        out = t if out is None else out + t
    return out


def _place_matrices(n_heads, n_groups):
    rep = n_heads // n_groups
    p = np.zeros((n_heads, n_groups * HD, n_heads * HD), np.float32)
    for h in range(n_heads):
        for d in range(HD):
            p[h, (h // rep) * HD + d, h * HD + d] = 1.0
    return jnp.asarray(p, BF16)


def _moba_s_body(n_pages, n_new, pt_ref, place_ref, q_ref, kn_ref, vn_ref, *rest):
    k_refs = rest[:n_pages]
    v_refs = rest[n_pages:2 * n_pages]
    o_ref = rest[2 * n_pages]
    new_mask = _new_mask(C_HEADS * SROWS, n_new)
    knt = _pad_rows(kn_ref[...]).T.astype(BF16)
    vnt = _pad_rows(vn_ref[...]).T.astype(BF16)
    ppb = MOBA_BLOCK // PAGE_SIZE
    n_blocks = n_pages // ppb
    q = _stack_heads(q_ref, C_HEADS, C_KV * HD, HD ** -0.5)
    ss = [_dot(q, k_refs[p][...].astype(BF16)) for p in range(n_pages)] + [_dot(q, knt)]
    bs = []
    for j in range(n_blocks):
        acc = ss[j * ppb]
        for u in range(1, ppb):
            acc = acc + ss[j * ppb + u]
        bs.append(jnp.sum(acc, axis=1, keepdims=True))
    pieces = []
    for j in range(n_blocks):
        rank = jnp.zeros(bs[j].shape, F32)
        for i in range(n_blocks):
            if i < j:
                rank = rank + jnp.where(bs[i] >= bs[j], 1.0, 0.0)
            elif i > j:
                rank = rank + jnp.where(bs[i] > bs[j], 1.0, 0.0)
        keep = rank < float(MOBA_TOPK)
        for u in range(ppb):
            pieces.append((ss[j * ppb + u], keep))
    pieces.append((ss[n_pages], new_mask))
    ps, tot = _softmax_pieces(pieces)
    o = _dot_nt(ps[n_pages], vnt)
    for p in range(n_pages):
        o = o + _dot_nt(ps[p], v_refs[p][...].astype(BF16))
    o_ref[...] = _place_heads(o / tot, place_ref, C_HEADS)


def _moba_sample(page_table, layer, n_new, cq, ck_new, cv_new, cache_k, cache_v, name):
    db, n_pages = page_table.shape
    place = _place_matrices(C_HEADS, C_KV)
    in_specs = ([pl.BlockSpec(place.shape, lambda b, pt: (0, 0, 0)),
                 _row_spec(C_HEADS * C_KV * HD), _row_spec(256), _row_spec(256)]
                + _page_specs(n_pages, C_KV * HD, layer) + _page_specs(n_pages, C_KV * HD, layer))
    grid_spec = pltpu.PrefetchScalarGridSpec(num_scalar_prefetch=1, grid=(db // SEQS,), in_specs=in_specs,
                                             out_specs=_row_spec(512))
    return pl.pallas_call(
        _per_sequence(functools.partial(_moba_s_body, n_pages, n_new), 1, 3, n_pages),
        grid_spec=grid_spec,
        out_shape=jax.ShapeDtypeStruct((db, SROWS, 512), F32),
        compiler_params=_params("parallel"),
        name=name,
    )(page_table, place, cq, ck_new, cv_new,*([cache_k] * (SEQS * n_pages)), *([cache_v] * (SEQS * n_pages)))


def _compress_s_body(n_pages, pt_ref, pek_ref, pev_ref, w1k_ref, w1v_ref, w2k_ref, w2v_ref, *rest):
    k_refs = rest[:n_pages]
    v_refs = rest[n_pages:2 * n_pages]
    ok_ref, ov_ref, xk_ref, xv_ref = rest[2 * n_pages:]
    for p in range(n_pages):
        xk_ref[p * PAGE_SIZE:(p + 1) * PAGE_SIZE, :] = k_refs[p][...].T
        xv_ref[p * PAGE_SIZE:(p + 1) * PAGE_SIZE, :] = v_refs[p][...].T
    _compress_one(xk_ref, pek_ref, w1k_ref, w2k_ref, ok_ref)
    _compress_one(xv_ref, pev_ref, w1v_ref, w2v_ref, ov_ref)


def _compress_sample(page_table, layer, cache_k, cache_v, cw, name):
    db, n_pages = page_table.shape
    const = lambda shape: pl.BlockSpec(shape, lambda b, pt: (0,) * len(shape))
    in_specs = ([const((CMP_BLOCK, 1, LANES))] * 2 + [const((CMP_BLOCK, LANES, 2 * CMP_HIDDEN))] * 2
                + [const((2 * CMP_HIDDEN, LANES))] * 2
                + _page_specs(n_pages, B_KV * HD, layer) + _page_specs(n_pages, B_KV * HD, layer))
    o_spec = _seq_spec(LANES, LANES)
    grid_spec = pltpu.PrefetchScalarGridSpec(
        num_scalar_prefetch=1, grid=(db // SEQS,), in_specs=in_specs, out_specs=[o_spec, o_spec],
        scratch_shapes=[pltpu.VMEM((SEQS, n_pages * PAGE_SIZE, LANES), F32)] * 2)
    return pl.pallas_call(
        _per_sequence(functools.partial(_compress_s_body, n_pages), 6, 0, n_pages),
        grid_spec=grid_spec,
        out_shape=[jax.ShapeDtypeStruct((db, LANES, LANES), F32)] * 2,
        compiler_params=_params("parallel"),
        name=name,
    )(page_table, cw["pek"], cw["pev"], cw["w1k"], cw["w1v"], cw["w2k"], cw["w2v"],
      *([cache_k] * (SEQS * n_pages)), *([cache_v] * (SEQS * n_pages)))


def _nsa_s_body(n_pages, n_new, q_pos0, n_win, pt_ref, place_ref, qc_ref, qr_ref, bg_ref, kcmp_ref, vcmp_ref,
                ksn_ref, vsn_ref, kwn_ref, vwn_ref, wk_ref, wv_ref, *rest):
    k_refs = rest[:n_pages]
    v_refs = rest[n_pages:2 * n_pages]
    o_ref = rest[2 * n_pages]
    lane = lax.broadcasted_iota(jnp.int32, (1, LANES), 1)
    halfi = lane >> 6
    rep = B_HEADS // B_KV
    m_rows = B_HEADS * SROWS
    new_mask = _new_mask(m_rows, n_new)
    trow8 = lax.broadcasted_iota(jnp.int32, (SROWS, 1), 0)
    trow = lax.broadcasted_iota(jnp.int32, (m_rows, 1), 0) & (SROWS - 1)
    qpos8 = q_pos0 + trow8
    ns = -(-(q_pos0 + n_new) // SEL_BLOCK)
    bpp = PAGE_SIZE // SEL_BLOCK
    transposed = lambda ref: _pad_rows(ref[...]).T.astype(BF16)
    ksnt, vsnt, kwnt, vwnt = transposed(ksn_ref), transposed(vsn_ref), transposed(kwn_ref), transposed(vwn_ref)
    kcmp = kcmp_ref[...].astype(BF16)
    vcmp = vcmp_ref[...].astype(BF16)
    ci = lax.broadcasted_iota(jnp.int32, (LANES, LANES), 0)
    cj = lax.broadcasted_iota(jnp.int32, (LANES, LANES), 1)
    onehot = jnp.where((ci >> CMP_PER_SEL_SHIFT) == cj, 1.0, 0.0).astype(BF16)
    sig = jax.nn.sigmoid(bg_ref[...])
    per_group = lambda xs: jnp.concatenate([x for x in xs for _ in range(rep)], axis=0)
    qc = _stack_heads(qc_ref, B_HEADS, B_KV * HD, HD ** -0.5)
    qr = _stack_heads(qr_ref, B_HEADS, B_KV * HD, HD ** -0.5)

    cm8 = jnp.where((lane * CMP_STRIDE + (CMP_BLOCK - 1)) <= qpos8, 1.0, 0.0)
    cmask = jnp.concatenate([cm8] * B_HEADS, axis=0) > 0.5
    s = jnp.where(cmask, _dot_nt(qc, kcmp), NEG)
    m = jnp.max(s, axis=1, keepdims=True)
    p = jnp.where(cmask, jnp.exp(s - m), 0.0)
    p = p / jnp.maximum(jnp.sum(p, axis=1, keepdims=True), 1e-30)
    o_cmp = _dot(p.astype(BF16), vcmp)
    cur = qpos8 >> 6
    sels = []
    for g in range(B_KV):
        psum = p[g * rep * SROWS:(g * rep + 1) * SROWS]
        for r in range(1, rep):
            psum = psum + p[(g * rep + r) * SROWS:(g * rep + r + 1) * SROWS]
        hi, mid, lo = _split3(psum)
        imp = _dot(hi, onehot) + _dot(mid, onehot) + _dot(lo, onehot)
        impm = jnp.where(lane == cur, jnp.inf, jnp.where(lane < cur, imp, -jnp.inf))
        rank = _rank_desc(impm, ns, lane)
        sels.append(jnp.where(rank < float(SEL_TOPK), jnp.where(lane <= cur, 1.0, 0.0), 0.0))

    ss = [_dot(qr, k_refs[pg][...].astype(BF16)) for pg in range(n_pages)] + [_dot(qr, ksnt)]
    pieces = []
    for pg in range(n_pages):
        mks = []
        for sel in sels:
            mk = sel[:, pg * bpp:pg * bpp + 1]
            for u in range(1, bpp):
                mk = jnp.where(halfi >= u, sel[:, pg * bpp + u:pg * bpp + u + 1], mk)
            mks.append(mk)
        pieces.append((ss[pg], per_group(mks) > 0.5))
    pieces.append((ss[n_pages], new_mask))
    ps, tot = _softmax_pieces(pieces)
    o_slc = _dot_nt(ps[n_pages], vsnt)
    for pg in range(n_pages):
        o_slc = o_slc + _dot_nt(ps[pg], v_refs[pg][...].astype(BF16))
    o_slc = o_slc / tot

    wcols = [slice(u * LANES, (u + 1) * LANES) for u in range(n_win // LANES)]
    ss = [_dot(qr, wk_ref[:, c].astype(BF16)) for c in wcols] + [_dot(qr, kwnt)]
    pieces = []
    for u in range(len(wcols)):
        dist = (n_win - u * LANES) + trow - lane
        pieces.append((ss[u], dist <= WINDOW))
    pieces.append((ss[len(wcols)], new_mask))
    ps, tot = _softmax_pieces(pieces)
    o_win = _dot_nt(ps[len(wcols)], vwnt)
    for u, c in enumerate(wcols):
        o_win = o_win + _dot_nt(ps[u], wv_ref[:, c].astype(BF16))
    o_win = o_win / tot

    gate = lambda c: jnp.concatenate([sig[:, h * 3 + c:h * 3 + c + 1] for h in range(B_HEADS)], axis=0)
    o = gate(0) * o_cmp + gate(1) * o_slc + gate(2) * o_win
    o_ref[...] = _place_heads(o, place_ref, B_HEADS)


def _nsa_sample(page_table, layer, n_new, q_pos0, bqc, bqr, bg, kcmp, vcmp, ks_new, vs_new, kw_new, vw_new,
                state_k, state_v, cache_k, cache_v, name):
    db, n_pages = page_table.shape
    n_win = state_k.shape[-1]
    place = _place_matrices(B_HEADS, B_KV)
    w_spec = pl.BlockSpec((None, SEQS, B_KV * HD, n_win), lambda b, pt: (layer, b, 0, 0))
    c_spec = _seq_spec(LANES, LANES)
    q_width = B_HEADS * B_KV * HD
    in_specs = ([pl.BlockSpec(place.shape, lambda b, pt: (0, 0, 0)),
                 _row_spec(q_width), _row_spec(q_width), _row_spec(LANES), c_spec, c_spec]
                + [_row_spec(LANES)] * 4 + [w_spec, w_spec]
                + _page_specs(n_pages, B_KV * HD, layer) + _page_specs(n_pages, B_KV * HD, layer))
    grid_spec = pltpu.PrefetchScalarGridSpec(num_scalar_prefetch=1, grid=(db // SEQS,), in_specs=in_specs,
                                             out_specs=_row_spec(512))
    return pl.pallas_call(
        _per_sequence(functools.partial(_nsa_s_body, n_pages, n_new, q_pos0, n_win), 1, 11, n_pages),
        grid_spec=grid_spec,
        out_shape=jax.ShapeDtypeStruct((db, SROWS, 512), F32),
        compiler_params=_params("parallel"),
        name=name,
    )(page_table, place, bqc, bqr, bg, kcmp, vcmp, ks_new, vs_new, kw_new, vw_new, state_k, state_v,
      *([cache_k] * (SEQS * n_pages)), *([cache_v] * (SEQS * n_pages)))


def _layer_weights(l, norm_mix, norm_ffn, norm_final, w_in, a_lambda, a_subln, b_cmp_pe, b_cmp_w1, b_cmp_w2,
                   w_branch, w_out, peer_wq, peer_subkeys, peer_u, peer_v):
    w = w_in[l]
    bg_w = jnp.pad(w[:, 2304:2328], ((0, 0), (0, LANES - 3 * B_HEADS)))
    return {
        "norm_mix": norm_mix[l][None, :],
        "norm_ffn": norm_ffn[l][None, :],
        "norm_final": norm_final[None, :],
        "last_layer": l == w_in.shape[0] - 1,
        "w_a": w[:, 0:1024].astype(BF16),
        "w_b": jnp.concatenate([w[:, 1024:2304], bg_w], axis=1).astype(BF16),
        "w_c": w[:, 2328:3352].astype(BF16),
        "w_b_s": jnp.concatenate([_spread_heads(w[:, 1024:1536], B_HEADS, B_KV), w[:, 1536:2304], bg_w],
                                 axis=1).astype(BF16),
        "w_c_s": jnp.concatenate([_spread_heads(w[:, 2328:2840], C_HEADS, C_KV), w[:, 2840:3352]],
                                 axis=1).astype(BF16),
        "w_g": w[:, 3352:6424].astype(BF16),
        "lam": a_lambda[l],
        "subln": a_subln[l][None, :],
        "cmp": _compress_weights(b_cmp_pe[l], b_cmp_w1[l], b_cmp_w2[l]),
        "w_branch": w_branch[l].astype(BF16),
        "w_out": w_out[l].astype(BF16),
        "wq": peer_wq[l].astype(BF16),
        "sk": peer_subkeys[l].reshape(2 * PEER_HEADS, PEER_NKEYS, LANES).astype(BF16),
        "u": peer_u[l].astype(BF16),
        "vt": peer_v[l].T.astype(BF16),
        "lam_init": 0.8 - 0.6 * math.exp(-0.3 * l),
    }


def _project_all(x, lw, cos_t, sin_t, tag):
    aq, ak, av = _project(x, lw["norm_mix"], lw["w_a"], cos_t, sin_t, PIECES_A, WIDTHS_A, "proj_a_" + tag)
    if tag == "s":
        b_out = _project(x, lw["norm_mix"], lw["w_b_s"], cos_t, sin_t, PIECES_B_S, WIDTHS_B_S, "proj_b_s")
        cq, ck, cv = _project(x, lw["norm_mix"], lw["w_c_s"], cos_t, sin_t, PIECES_C_S, WIDTHS_C_S, "proj_c_s")
    else:
        b_out = _project(x, lw["norm_mix"], lw["w_b"], cos_t, sin_t, PIECES_B, WIDTHS_B, "proj_b_p")
        cq, ck, cv = _project(x, lw["norm_mix"], lw["w_c"], cos_t, sin_t, PIECES_C, WIDTHS_C, "proj_c_p")
    (gate,) = _project(x, lw["norm_mix"], lw["w_g"], cos_t, sin_t, PIECES_G, WIDTHS_G, "proj_g_" + tag)
    return (aq, ak, av), b_out, (cq, ck, cv), gate


def _ffn(x, lw, tag):
    st, stat = _peer_stats(x, lw["norm_ffn"], lw["wq"], lw["sk"], "peer_stats_" + tag)
    return _peer_dense(x, lw["norm_ffn"], lw["norm_final"], lw["last_layer"], st, stat, lw["u"], lw["vt"],
                       "peer_dense_" + tag)


def _prompt_layer(x, lw, cos_t, sin_t, bn, t):
    (aq, ak, av), (bqc, bqr, bkc, bvc, bks, bvs, bkw, bvw, bg), (cq, ck, cv), gate = _project_all(
        x, lw, cos_t, sin_t, "p")
    r3 = lambda a: a.reshape(bn, t, a.shape[-1])
    o_a = _diff_attention(r3(aq), r3(ak), r3(av), lw["lam"], lw["subln"], lw["lam_init"], 0, "diff_p")
    kcmp, vcmp = _compress(r3(bkc), r3(bvc), lw["cmp"], "cmp_p")
    o_b = _nsa_attention(r3(bqc), r3(bqr), r3(bg), kcmp, vcmp, r3(bks), r3(bvs), r3(bkw), r3(bvw),
                         0, t, 0, t, "nsa_p")
    o_c = _moba_attention(r3(cq), r3(ck), r3(cv), 0, "moba_p")
    n = bn * t
    x = _merge(x, gate, o_a.reshape(n, 512), o_b.reshape(n, 512), o_c.reshape(n, 512),
               lw["w_branch"], lw["w_out"], "merge_p")
    x = _ffn(x, lw, "p")
    wp = min(WINDOW, t)
    rows = (r3(ak).reshape(bn, t, A_KV, 2 * HD), r3(av).reshape(bn, t, A_KV, 2 * HD),
            r3(bkc).reshape(bn, t, B_KV, HD), r3(bvc).reshape(bn, t, B_KV, HD),
            r3(bks).reshape(bn, t, B_KV, HD), r3(bvs).reshape(bn, t, B_KV, HD),
            r3(ck).reshape(bn, t, C_KV, HD), r3(cv).reshape(bn, t, C_KV, HD),
            r3(bkw)[:, t - wp:].reshape(bn, wp, B_KV, HD), r3(bvw)[:, t - wp:].reshape(bn, wp, B_KV, HD))
    return x, rows


def _sample_layer(x, lw, l, cos_t, sin_t, db, t, past, caches, state_wk, state_wv, page_table):
    x = jnp.pad(x.reshape(db, t, D_MODEL), ((0, 0), (0, SROWS - t), (0, 0))).reshape(db * SROWS, D_MODEL)
    (aq, ak, av), (bqc, bqr, bkc, bvc, bks, bvs, bkw, bvw, bg), (cq, ck, cv), gate = _project_all(
        x, lw, cos_t, sin_t, "s")
    r3 = lambda a: a.reshape(db, SROWS, a.shape[-1])
    c_ak, c_av, c_bkc, c_bvc, c_bks, c_bvs, c_ck, c_cv = caches
    o_a = _diff_sample(page_table, l, t, r3(aq), r3(ak), r3(av), c_ak, c_av, lw["lam"], lw["subln"],
                       lw["lam_init"], "diff_s")
    kcmp, vcmp = _compress_sample(page_table, l, c_bkc, c_bvc, lw["cmp"], "cmp_s")
    o_b = _nsa_sample(page_table, l, t, past, r3(bqc), r3(bqr), r3(bg), kcmp, vcmp, r3(bks), r3(bvs),
                      r3(bkw), r3(bvw), state_wk, state_wv, c_bks, c_bvs, "nsa_s")
    o_c = _moba_sample(page_table, l, t, r3(cq), r3(ck), r3(cv), c_ck, c_cv, "moba_s")
    n = db * SROWS
    x = _merge(x, gate, o_a.reshape(n, 512), o_b.reshape(n, 512), o_c.reshape(n, 512),
               lw["w_branch"], lw["w_out"], "merge_s")
    x = _ffn(x.reshape(db, SROWS, D_MODEL)[:, :t].reshape(db * t, D_MODEL), lw, "s")
    new = lambda a, kv, w: r3(a)[:, :t].reshape(db, t, kv, w)

    def rolled(state_t, a):
        new_t = jnp.transpose(r3(a)[:, :t], (0, 2, 1))
        out_t = jnp.concatenate([state_t[l][:, :, t:], new_t], axis=2)
        return jnp.transpose(out_t, (0, 2, 1)).reshape(db, out_t.shape[2], B_KV, HD)

    rows = (new(ak, A_KV, 2 * HD), new(av, A_KV, 2 * HD), new(bkc, B_KV, HD), new(bvc, B_KV, HD),
            new(bks, B_KV, HD), new(bvs, B_KV, HD), new(ck, C_KV, HD), new(cv, C_KV, HD),
            rolled(state_wk, bkw), rolled(state_wv, bvw))
    return x, rows


def kernel(x_prompt, x_sample, cache_a_k, cache_a_v, cache_b_kc, cache_b_vc, cache_b_ks, cache_b_vs,
           cache_c_k, cache_c_v, state_b_wk, state_b_wv, page_table, norm_mix, norm_ffn, norm_final,
           w_in, a_lambda, a_subln, b_cmp_pe, b_cmp_w1, b_cmp_w2, w_branch, w_out,
           peer_wq, peer_subkeys, peer_u, peer_v):
    bn, t, _ = x_prompt.shape
    db, ts, _ = x_sample.shape
    depth = w_in.shape[0]
    past = page_table.shape[1] * PAGE_SIZE
    wb = state_b_wk.shape[2]
    assert past % MOBA_BLOCK == 0 and ts <= SROWS and wb == WINDOW and wb % LANES == 0 and db % SEQS == 0
    rows_view = lambda c: c.reshape(c.shape[0], c.shape[1], c.shape[2] * c.shape[3], c.shape[4])
    lanes_view = lambda c: jnp.transpose(c, (0, 1, 3, 4, 2)).reshape(
        c.shape[0], c.shape[1], c.shape[3] * c.shape[4], c.shape[2])
    caches = ([rows_view(c) for c in (cache_a_k, cache_a_v)]
              + [lanes_view(c) for c in (cache_b_kc, cache_b_vc, cache_b_ks, cache_b_vs, cache_c_k, cache_c_v)])
    swk = lanes_view(state_b_wk)
    swv = lanes_view(state_b_wv)

    cos_p, sin_p = _rope_tables(jnp.arange(t, dtype=jnp.int32))
    tm_s = min(PROJ_TILE, db * SROWS)
    cos_s, sin_s = _rope_tables(past + (jnp.arange(tm_s, dtype=jnp.int32) % SROWS))

    xp = x_prompt.reshape(bn * t, D_MODEL)
    xs = x_sample.reshape(db * ts, D_MODEL)
    rows_p, rows_s = [], []
    for l in range(depth):
        lw = _layer_weights(l, norm_mix, norm_ffn, norm_final, w_in, a_lambda, a_subln, b_cmp_pe, b_cmp_w1,
                            b_cmp_w2, w_branch, w_out, peer_wq, peer_subkeys, peer_u, peer_v)
        xp, rp = _prompt_layer(xp, lw, cos_p, sin_p, bn, t)
        xs, rs = _sample_layer(xs, lw, l, cos_s, sin_s, db, ts, past, caches, swk, swv, page_table)
        rows_p.append(rp)
        rows_s.append(rs)
    y_prompt = xp.reshape(bn, t, D_MODEL)
    y_sample = xs.reshape(db, ts, D_MODEL)
    outs_p = [jnp.stack(r, axis=0) for r in zip(*rows_p)]
    outs_s = [jnp.stack(r, axis=0) for r in zip(*rows_s)]
    return (y_prompt, y_sample, *outs_p, *outs_s)
```
